```python
import math
import jax, jax.numpy as jnp
from jax import lax
import numpy as np

D_MODEL = 1024
BATCH = 16
SEQ = 2048
DEPTH = 1

MLA_HEADS = 8
MLA_Q_LORA = 256
MLA_KV_LORA = 256
MLA_NOPE = 64
MLA_ROPE = 32
MLA_V = 64
ROPE_THETA = 10000.0
Q_BLOCK = 128
SWA_HEADS = 8
SWA_KV_HEADS = 2
SWA_GROUP = SWA_HEADS // SWA_KV_HEADS
SWA_HEAD_DIM = 64
SWA_WINDOW = 128
N_EXPERTS = 256
TOP_K = 8
N_GROUPS = 8
TOPK_GROUPS = 4
EXPERT_HIDDEN = 256
SHARED_HIDDEN = 256
ROUTED_SCALE = 2.5
EXPERT_BLOCK = 128
DEEPNORM_ALPHA = (2.0 * DEPTH) ** 0.25
DEEPNORM_BETA = (8.0 * DEPTH) ** -0.25
LN_EPS = 1e-5
RMS_EPS = 1e-6

MLA_OUT = MLA_HEADS * MLA_V
SWA_OUT = SWA_HEADS * SWA_HEAD_DIM
IN_SPLIT_SIZES = [MLA_Q_LORA, MLA_KV_LORA, MLA_ROPE,
                  SWA_HEADS * SWA_HEAD_DIM, SWA_KV_HEADS * SWA_HEAD_DIM, SWA_KV_HEADS * SWA_HEAD_DIM,
                  D_MODEL, D_MODEL]
IN_PROJ_DIM = sum(IN_SPLIT_SIZES)
IN_SPLIT_POINTS = [int(v) for v in np.cumsum(IN_SPLIT_SIZES)[:-1]]

kernel_name = 'hybrid_mla_swa_gated_moe_deepnorm'


def layer_norm(x, g, b):
    xf = x.astype(jnp.float32)
    mu = jnp.mean(xf, axis=-1, keepdims=True)
    var = jnp.mean(jnp.square(xf - mu), axis=-1, keepdims=True)
    y = (xf - mu) * lax.rsqrt(var + LN_EPS)
    return (y * g.astype(jnp.float32) + b.astype(jnp.float32)).astype(x.dtype)


def rms_norm(x, g):
    xf = x.astype(jnp.float32)
    y = xf * lax.rsqrt(jnp.mean(jnp.square(xf), axis=-1, keepdims=True) + RMS_EPS)
    return (y * g.astype(jnp.float32)).astype(x.dtype)


def rope_tables(seq_len):
    inv_freq = ROPE_THETA ** (-jnp.arange(0, MLA_ROPE, 2, dtype=jnp.float32) / MLA_ROPE)
    ang = jnp.arange(seq_len, dtype=jnp.float32)[:, None] * inv_freq[None, :]
    return jnp.cos(ang), jnp.sin(ang)


def apply_rope(x, cos, sin):
    half = x.shape[-1] // 2
    x1, x2 = x[..., :half], x[..., half:]
    out = jnp.concatenate([x1 * cos - x2 * sin, x1 * sin + x2 * cos], axis=-1)
    return out.astype(x.dtype)


def alibi_slopes(n_heads):
    return 2.0 ** (-8.0 * jnp.arange(1, n_heads + 1, dtype=jnp.float32) / n_heads)


def mla_branch(c_q, c_kv, k_rope_raw, q_norm_g, kv_norm_g, w_uq, w_ukv):
    B, S, _ = c_q.shape
    q = (rms_norm(c_q, q_norm_g) @ w_uq).reshape(B, S, MLA_HEADS, MLA_NOPE + MLA_ROPE)
    q_nope, q_rope = q[..., :MLA_NOPE], q[..., MLA_NOPE:]
    kv = (rms_norm(c_kv, kv_norm_g) @ w_ukv).reshape(B, S, MLA_HEADS, MLA_NOPE + MLA_V)
    k_nope, v = kv[..., :MLA_NOPE], kv[..., MLA_NOPE:]
    cos, sin = rope_tables(S)
    q_rope = apply_rope(q_rope, cos[:, None, :], sin[:, None, :])
    k_rope = apply_rope(k_rope_raw, cos, sin)
    scale = 1.0 / math.sqrt(MLA_NOPE + MLA_ROPE)
    nb = S // Q_BLOCK
    qn_b = jnp.moveaxis(q_nope.reshape(B, nb, Q_BLOCK, MLA_HEADS, MLA_NOPE), 1, 0)
    qr_b = jnp.moveaxis(q_rope.reshape(B, nb, Q_BLOCK, MLA_HEADS, MLA_ROPE), 1, 0)
    key_pos = jnp.arange(S)

    def attend(args):
        qn, qr, blk = args
        s = (jnp.einsum('bqhd,bkhd->bhqk', qn, k_nope, preferred_element_type=jnp.float32)
             + jnp.einsum('bqhr,bkr->bhqk', qr, k_rope, preferred_element_type=jnp.float32)) * scale
        q_pos = blk * Q_BLOCK + jnp.arange(Q_BLOCK)
        s = jnp.where(key_pos[None, :] <= q_pos[:, None], s, -jnp.inf)
        p = jax.nn.softmax(s, axis=-1).astype(v.dtype)
        return jnp.einsum('bhqk,bkhd->bqhd', p, v)

    out = lax.map(attend, (qn_b, qr_b, jnp.arange(nb)))
    return jnp.moveaxis(out, 0, 1).reshape(B, S, MLA_OUT)


def swa_branch(q, k, v, attn_sinks):
    B, S, _ = q.shape
    W = SWA_WINDOW
    nb = S // W
    q = q.reshape(B, nb, W, SWA_KV_HEADS, SWA_GROUP, SWA_HEAD_DIM)
    k = k.reshape(B, S, SWA_KV_HEADS, SWA_HEAD_DIM)
    v = v.reshape(B, S, SWA_KV_HEADS, SWA_HEAD_DIM)

    def band(t):
        pad = jnp.zeros((B, W) + t.shape[2:], t.dtype)
        prev = jnp.concatenate([pad, t[:, :S - W]], axis=1).reshape(B, nb, W, SWA_KV_HEADS, SWA_HEAD_DIM)
        cur = t.reshape(B, nb, W, SWA_KV_HEADS, SWA_HEAD_DIM)
        return jnp.concatenate([prev, cur], axis=2)

    k_band, v_band = band(k), band(v)
    scale = 1.0 / math.sqrt(SWA_HEAD_DIM)
    s = jnp.einsum('bnikgd,bnjkd->bnkgij', q, k_band, preferred_element_type=jnp.float32) * scale
    i = jnp.arange(W)[:, None]
    j = jnp.arange(2 * W)[None, :]
    dist = i + W - j
    key_pos = jnp.arange(nb)[:, None] * W + jnp.arange(2 * W)[None, :] - W
    valid = ((dist >= 0) & (dist < W))[None, :, :] & (key_pos >= 0)[:, None, :]
    slopes = alibi_slopes(SWA_HEADS).reshape(SWA_KV_HEADS, SWA_GROUP)[:, :, None, None]
    s = s - slopes * dist.astype(jnp.float32)
    s = jnp.where(valid[None, :, None, None], s, -jnp.inf)
    sink = attn_sinks.astype(jnp.float32).reshape(SWA_KV_HEADS, SWA_GROUP)[:, :, None, None]
    m = jnp.maximum(jnp.max(s, axis=-1, keepdims=True), sink)
    p = jnp.exp(s - m)
    denom = jnp.sum(p, axis=-1, keepdims=True) + jnp.exp(sink - m)
    p = (p / denom).astype(v.dtype)
    out = jnp.einsum('bnkgij,bnjkd->bnikgd', p, v_band)
    return out.reshape(B, S, SWA_OUT)


def token_mixer(x, w_in, mla_q_norm, mla_kv_norm, w_uq, w_ukv, attn_sinks, w_o_mla, w_o_swa, w_out):
    proj = x @ w_in
    c_q, c_kv, k_rope, q_s, k_s, v_s, gate_a, gate_b = jnp.split(proj, IN_SPLIT_POINTS, axis=-1)
    y_a = mla_branch(c_q, c_kv, k_rope, mla_q_norm, mla_kv_norm, w_uq, w_ukv) @ w_o_mla
    y_b = swa_branch(q_s, k_s, v_s, attn_sinks) @ w_o_swa
    merged = jax.nn.sigmoid(gate_a) * y_a + jax.nn.sigmoid(gate_b) * y_b
    return merged @ w_out


def route(h, w_router, router_bias):
    T = h.shape[0]
    scores = jax.nn.sigmoid((h @ w_router).astype(jnp.float32))
    choice = scores + router_bias.astype(jnp.float32)
    grp = choice.reshape(T, N_GROUPS, N_EXPERTS // N_GROUPS)
    grp_score = jnp.sum(lax.top_k(grp, 2)[0], axis=-1)
    _, grp_idx = lax.top_k(grp_score, TOPK_GROUPS)
    grp_mask = jnp.any(grp_idx[:, :, None] == jnp.arange(N_GROUPS)[None, None, :], axis=1)
    expert_mask = jnp.repeat(grp_mask, N_EXPERTS // N_GROUPS, axis=1)
    _, top_idx = lax.top_k(jnp.where(expert_mask, choice, -jnp.inf), TOP_K)
    w = jnp.take_along_axis(scores, top_idx, axis=-1)
    w = w / jnp.sum(w, axis=-1, keepdims=True) * ROUTED_SCALE
    return top_idx, w


def routed_experts(h, top_idx, top_w, w_exp_up, w_exp_down):
    T = h.shape[0]
    TK = T * TOP_K
    e_flat = top_idx.reshape(-1).astype(jnp.int32)
    tok_flat = jnp.repeat(jnp.arange(T, dtype=jnp.int32), TOP_K)
    w_flat = top_w.reshape(-1).astype(h.dtype)
    order = jnp.argsort(e_flat)
    sorted_e = e_flat[order]
    sizes = jax.ops.segment_sum(jnp.ones((TK,), jnp.int32), e_flat, num_segments=N_EXPERTS)
    start = jnp.cumsum(sizes) - sizes
    pad_sizes = (sizes + EXPERT_BLOCK - 1) // EXPERT_BLOCK * EXPERT_BLOCK
    pad_end = jnp.cumsum(pad_sizes)
    pad_start = pad_end - pad_sizes
    dest = pad_start[sorted_e] + (jnp.arange(TK, dtype=jnp.int32) - start[sorted_e])
    n_rows = (TK + EXPERT_BLOCK - 1) // EXPERT_BLOCK * EXPERT_BLOCK + N_EXPERTS * EXPERT_BLOCK
    n_blocks = n_rows // EXPERT_BLOCK
    row_tok = jnp.zeros((n_rows,), jnp.int32).at[dest].set(tok_flat[order])
    row_w = jnp.zeros((n_rows,), h.dtype).at[dest].set(w_flat[order])
    blk_start = jnp.arange(n_blocks, dtype=jnp.int32) * EXPERT_BLOCK
    blk_expert = jnp.clip(jnp.searchsorted(pad_end, blk_start, side='right'), 0, N_EXPERTS - 1)

    def step(acc, blk):
        tok, wt, e = blk
        xb = h[tok]
        gu = xb @ w_exp_up[e]
        hid = jax.nn.silu(gu[:, :EXPERT_HIDDEN]) * gu[:, EXPERT_HIDDEN:]
        yb = (hid @ w_exp_down[e]) * wt[:, None]
        return acc.at[tok].add(yb), None

    acc, _ = lax.scan(step, jnp.zeros_like(h),
                      (row_tok.reshape(n_blocks, EXPERT_BLOCK), row_w.reshape(n_blocks, EXPERT_BLOCK), blk_expert))
    return acc


def moe_ffn(x, w_router, router_bias, w_exp_up, w_exp_down, w_sh_up, w_sh_down):
    B, S, D = x.shape
    h = x.reshape(B * S, D)
    top_idx, top_w = route(h, w_router, router_bias)
    routed = routed_experts(h, top_idx, top_w, w_exp_up, w_exp_down)
    gu = h @ w_sh_up
    shared = (jax.nn.silu(gu[:, :SHARED_HIDDEN]) * gu[:, SHARED_HIDDEN:]) @ w_sh_down
    return (routed + shared).reshape(B, S, D)


def setup_inputs(seed: int = 0) -> dict:
    key = jax.random.key(seed)
    ks = jax.random.split(key, 20)
    f32 = jnp.float32

    def nrm(k, shape, scale):
        return jax.random.normal(k, shape, f32) * scale

    return {
        'x': nrm(ks[0], (BATCH, SEQ, D_MODEL), 1.0),
        'w_in': nrm(ks[1], (D_MODEL, IN_PROJ_DIM), D_MODEL ** -0.5),
        'mla_q_norm': 1.0 + nrm(ks[2], (MLA_Q_LORA,), 0.02),
        'mla_kv_norm': 1.0 + nrm(ks[3], (MLA_KV_LORA,), 0.02),
        'w_uq': nrm(ks[4], (MLA_Q_LORA, MLA_HEADS * (MLA_NOPE + MLA_ROPE)), MLA_Q_LORA ** -0.5),
        'w_ukv': nrm(ks[5], (MLA_KV_LORA, MLA_HEADS * (MLA_NOPE + MLA_V)), MLA_KV_LORA ** -0.5),
        'attn_sinks': nrm(ks[6], (SWA_HEADS,), 0.5),
        'w_o_mla': nrm(ks[7], (MLA_OUT, D_MODEL), DEEPNORM_BETA * MLA_OUT ** -0.5),
        'w_o_swa': nrm(ks[8], (SWA_OUT, D_MODEL), DEEPNORM_BETA * SWA_OUT ** -0.5),
        'w_out': nrm(ks[9], (D_MODEL, D_MODEL), DEEPNORM_BETA * D_MODEL ** -0.5),
        'ln1_g': 1.0 + nrm(ks[10], (D_MODEL,), 0.02),
        'ln1_b': nrm(ks[11], (D_MODEL,), 0.02),
        'w_router': nrm(ks[12], (D_MODEL, N_EXPERTS), D_MODEL ** -0.5),
        'router_bias': nrm(ks[13], (N_EXPERTS,), 0.01),
        'w_exp_up': nrm(ks[14], (N_EXPERTS, D_MODEL, 2 * EXPERT_HIDDEN), D_MODEL ** -0.5),
        'w_exp_down': nrm(ks[15], (N_EXPERTS, EXPERT_HIDDEN, D_MODEL), DEEPNORM_BETA * EXPERT_HIDDEN ** -0.5),
        'w_sh_up': nrm(ks[16], (D_MODEL, 2 * SHARED_HIDDEN), D_MODEL ** -0.5),
        'w_sh_down': nrm(ks[17], (SHARED_HIDDEN, D_MODEL), DEEPNORM_BETA * SHARED_HIDDEN ** -0.5),
        'ln2_g': 1.0 + nrm(ks[18], (D_MODEL,), 0.02),
        'ln2_b': nrm(ks[19], (D_MODEL,), 0.02),
    }


def reference(x, w_in, mla_q_norm, mla_kv_norm, w_uq, w_ukv, attn_sinks, w_o_mla, w_o_swa, w_out,
              ln1_g, ln1_b, w_router, router_bias, w_exp_up, w_exp_down, w_sh_up, w_sh_down,
              ln2_g, ln2_b):
    h = x
    for _ in range(DEPTH):
        mix = token_mixer(h, w_in, mla_q_norm, mla_kv_norm, w_uq, w_ukv, attn_sinks, w_o_mla, w_o_swa, w_out)
        h = layer_norm(DEEPNORM_ALPHA * h + mix, ln1_g, ln1_b)
        ffn = moe_ffn(h, w_router, router_bias, w_exp_up, w_exp_down, w_sh_up, w_sh_down)
        h = layer_norm(DEEPNORM_ALPHA * h + ffn, ln2_g, ln2_b)
    return h
```

```python
import functools
import math

import jax
import jax.numpy as jnp
from jax import lax
from jax.experimental import pallas as pl
from jax.experimental.pallas import tpu as pltpu

D_MODEL = 1024
MLA_HEADS = 8
MLA_Q_LORA = 256
MLA_KV_LORA = 256
MLA_NOPE = 64
MLA_ROPE = 32
MLA_V = 64
ROPE_THETA = 10000.0
SWA_HEADS = 8
SWA_KV_HEADS = 2
SWA_HEAD_DIM = 64
SWA_WINDOW = 128
N_EXPERTS = 256
TOP_K = 8
N_GROUPS = 8
GROUP_SIZE = N_EXPERTS // N_GROUPS
TOPK_GROUPS = 4
EXPERT_HIDDEN = 256
SHARED_HIDDEN = 256
ROUTED_SCALE = 2.5
DEEPNORM_ALPHA = 2.0 ** 0.25
LN_EPS = 1e-5
RMS_EPS = 1e-6

LANES = 128
ROW_WORDS = D_MODEL // 2
ROW_SUB = ROW_WORDS // LANES
VMEM_LIMIT = 48 * 1024 * 1024

PROJ_TM = 512
MLA_TQ = 256
POST_TM = 256
ROW_BLK = 256
MOVE_TM = 256

BF16 = jnp.bfloat16
F32 = jnp.float32
NEG_INF = float("-inf")


def _sigmoid(v):
    return 1.0 / (1.0 + jnp.exp(-v))


def _dot(a, b):
    return jnp.dot(a, b, preferred_element_type=F32)


def _dot_nt(a, b):
    return lax.dot_general(a, b, (((1,), (1,)), ((), ())), preferred_element_type=F32)


def _pack_rows(y, out_ref):
    rows = y.shape[0]
    for j in range(ROW_SUB):
        a = y[:, j * LANES:(j + 1) * LANES].astype(BF16).astype(F32)
        b = y[:, ROW_WORDS + j * LANES:ROW_WORDS + (j + 1) * LANES].astype(BF16).astype(F32)
        ua = pltpu.bitcast(a, jnp.uint32) >> 16
        ub = pltpu.bitcast(b, jnp.uint32)
        out_ref[pl.ds(j, rows, stride=ROW_SUB), :] = ua | ub


def _unpack_rows(ref, rows):
    lo, hi = [], []
    for j in range(ROW_SUB):
        u = ref[pl.ds(j, rows, stride=ROW_SUB), :]
        lo.append(pltpu.bitcast(u << 16, F32))
        hi.append(pltpu.bitcast(u & jnp.uint32(0xFFFF0000), F32))
    return lo + hi


def _proj_kernel(x_ref, w1_ref, wq_ref, wkv_ref, gq_ref, gkv_ref, cq_ref, sq_ref, ck_ref, sk_ref,
                 q_ref, k_ref, v_ref, qs_ref, ks_ref, vs_ref):
    xb = x_ref[...].astype(BF16)
    p = _dot(xb, w1_ref[...])

    def rms(c, g):
        return c * lax.rsqrt(jnp.mean(c * c, axis=-1, keepdims=True) + RMS_EPS) * g

    cqn = rms(p[:, 0:256], gq_ref[...]).astype(BF16)
    ckvn = rms(p[:, 256:512], gkv_ref[...]).astype(BF16)
    qs_ref[...] = p[:, 512:1024].astype(BF16)
    ks_ref[...] = p[:, 1024:1536].astype(BF16)
    vs_ref[...] = p[:, 1536:1792].astype(BF16)
    kr = p[:, 1792:1920] * ck_ref[...] + p[:, 1920:2048] * sk_ref[...]
    qq = _dot(cqn, wq_ref[...])
    kv = _dot(ckvn, wkv_ref[...])
    cq = cq_ref[...]
    sq = sq_ref[...]
    for h in range(MLA_HEADS):
        a = qq[:, h * LANES:(h + 1) * LANES]
        b = qq[:, 1024 + h * LANES:1024 + (h + 1) * LANES]
        q_ref[:, h * LANES:(h + 1) * LANES] = (a * cq + b * sq).astype(BF16)
        k_ref[:, h * LANES:(h + 1) * LANES] = (kv[:, h * LANES:(h + 1) * LANES] + kr).astype(BF16)
    v_ref[...] = kv[:, 1024:1536].astype(BF16)


def _proj(x2, w1, wq2, wkv, gq, gkv, tabs, seq):
    T = x2.shape[0]
    tm = PROJ_TM
    nper = seq // tm
    full = lambda shape: pl.BlockSpec(shape, lambda i: (0, 0))
    tab = pl.BlockSpec((tm, LANES), lambda i: (i % nper, 0))
    row = lambda n: pl.BlockSpec((tm, n), lambda i: (i, 0))
    return pl.pallas_call(
        _proj_kernel,
        grid=(T // tm,),
        in_specs=[row(D_MODEL), full(w1.shape), full(wq2.shape), full(wkv.shape),
                  full(gq.shape), full(gkv.shape), tab, tab, tab, tab],
        out_specs=[row(1024), row(1024), row(512), row(512), row(512), row(256)],
        out_shape=[jax.ShapeDtypeStruct((T, n), BF16) for n in (1024, 1024, 512, 512, 512, 256)],
        compiler_params=pltpu.CompilerParams(dimension_semantics=("parallel",),
                                             vmem_limit_bytes=VMEM_LIMIT),
        name="proj",
    )(x2, w1, wq2, wkv, gq, gkv, *tabs)


def _mla_kernel(q_ref, k_ref, v_ref, o_ref, m_scr, l_scr, acc_scr):
    tq = q_ref.shape[0]
    qi = pl.program_id(2)
    lane_lo = lax.broadcasted_iota(jnp.int32, (tq, LANES), 1) < 64
    m_scr[...] = jnp.full(m_scr.shape, NEG_INF, F32)
    l_scr[...] = jnp.zeros(l_scr.shape, F32)
    acc_scr[...] = jnp.zeros(acc_scr.shape, F32)

    def step(kc, masked):
        ks = pl.multiple_of(kc * tq, tq)
        kblk = k_ref[pl.ds(ks, tq), :]
        vblk = v_ref[pl.ds(ks, tq), :]
        pvs, alphas = [], []
        for h in range(2):
            s = _dot_nt(q_ref[:, h * LANES:(h + 1) * LANES], kblk[:, h * LANES:(h + 1) * LANES])
            if masked:
                r = lax.broadcasted_iota(jnp.int32, s.shape, 0)
                c = lax.broadcasted_iota(jnp.int32, s.shape, 1)
                s = jnp.where(c <= r, s, NEG_INF)
            m_prev = m_scr[h]
            m_new = jnp.maximum(m_prev, jnp.max(s, axis=1, keepdims=True))
            alpha = jnp.exp(m_prev - m_new)
            p = jnp.exp(s - m_new[:, 0:1])
            l_scr[h] = alpha * l_scr[h] + jnp.sum(p, axis=1, keepdims=True)
            m_scr[h] = m_new
            pvs.append(_dot(p.astype(BF16), vblk))
            alphas.append(alpha)
        acc_scr[...] = (acc_scr[...] * jnp.where(lane_lo, alphas[0], alphas[1])
                        + jnp.where(lane_lo, pvs[0], pvs[1]))

    def body(kc, carry):
        step(kc, False)
        return carry

    lax.fori_loop(0, qi, body, 0)
    step(qi, True)
    l = jnp.where(lane_lo, l_scr[0], l_scr[1])
    o_ref[...] = (acc_scr[...] / l).astype(BF16)


def _mla(q, k, v, batch, seq):
    T = q.shape[0]
    tq = MLA_TQ
    nq = seq // tq
    return pl.pallas_call(
        _mla_kernel,
        grid=(batch, MLA_HEADS // 2, nq),
        in_specs=[pl.BlockSpec((tq, 2 * LANES), lambda b, j, i: (b * nq + i, j)),
                  pl.BlockSpec((seq, 2 * LANES), lambda b, j, i: (b, j)),
                  pl.BlockSpec((seq, LANES), lambda b, j, i: (b, j))],
        out_specs=pl.BlockSpec((tq, LANES), lambda b, j, i: (b * nq + i, j)),
        out_shape=jax.ShapeDtypeStruct((T, MLA_HEADS * MLA_V), BF16),
        scratch_shapes=[pltpu.VMEM((2, tq, LANES), F32), pltpu.VMEM((2, tq, LANES), F32),
                        pltpu.VMEM((tq, LANES), F32)],
        compiler_params=pltpu.CompilerParams(
            dimension_semantics=("parallel", "parallel", "arbitrary"), vmem_limit_bytes=VMEM_LIMIT),
        name="mla",
    )(q, k, v)


def _swa_kernel(sink_ref, q_ref, kc_ref, kp_ref, vc_ref, vp_ref, o_ref):
    W = SWA_WINDOW
    n = pl.program_id(1)
    i = lax.broadcasted_iota(jnp.int32, (W, W), 0)
    j = lax.broadcasted_iota(jnp.int32, (W, W), 1)
    cur_ok = j <= i
    prev_ok = (j > i) & (n > 0)
    dist_c = (i - j).astype(F32)
    dist_p = (i + W - j).astype(F32)
    lane_lo = lax.broadcasted_iota(jnp.int32, (W, LANES), 1) < 64
    for pair in range(SWA_HEADS // 2):
        kvh = pair // 2
        qp = q_ref[:, pair * LANES:(pair + 1) * LANES]
        vc = vc_ref[:, kvh * LANES:(kvh + 1) * LANES]
        vp = vp_ref[:, kvh * LANES:(kvh + 1) * LANES]
        outs = []
        for g in range(2):
            head = 2 * pair + g
            slope = 2.0 ** (-8.0 * (head + 1) / SWA_HEADS)
            col = (2 * kvh + g) * LANES
            s_c = _dot_nt(qp, kc_ref[:, col:col + LANES])
            s_p = _dot_nt(qp, kp_ref[:, col:col + LANES])
            s_c = jnp.where(cur_ok, s_c - slope * dist_c, NEG_INF)
            s_p = jnp.where(prev_ok, s_p - slope * dist_p, NEG_INF)
            sink = sink_ref[head]
            m = jnp.maximum(jnp.maximum(jnp.max(s_c, axis=1, keepdims=True),
                                        jnp.max(s_p, axis=1, keepdims=True)), sink)
            p_c = jnp.exp(s_c - m)
            p_p = jnp.exp(s_p - m)
            denom = (jnp.sum(p_c, axis=1, keepdims=True) + jnp.sum(p_p, axis=1, keepdims=True)
                     + jnp.exp(sink - m))
            outs.append((_dot(p_c.astype(BF16), vc) + _dot(p_p.astype(BF16), vp)) / denom)
        o_ref[:, pair * LANES:(pair + 1) * LANES] = jnp.where(lane_lo, outs[0], outs[1]).astype(BF16)


def _swa(sinks, qs, ks, vs, batch, seq):
    T = qs.shape[0]
    W = SWA_WINDOW
    nb = seq // W
    cur = lambda n: pl.BlockSpec((W, n), lambda b, i: (b * nb + i, 0))
    prev = lambda n: pl.BlockSpec((W, n), lambda b, i: (b * nb + jnp.maximum(i - 1, 0), 0))
    return pl.pallas_call(
        _swa_kernel,
        grid=(batch, nb),
        in_specs=[pl.BlockSpec(memory_space=pltpu.SMEM), cur(512), cur(512), prev(512), cur(256), prev(256)],
        out_specs=cur(512),
        out_shape=jax.ShapeDtypeStruct((T, 512), BF16),
        compiler_params=pltpu.CompilerParams(dimension_semantics=("parallel", "parallel"),
                                             vmem_limit_bytes=VMEM_LIMIT),
        name="swa",
    )(sinks, qs, ks, ks, vs, vs)


def _post_kernel(x_ref, om_ref, os_ref, wom_ref, wos_ref, wg_ref, wout_ref, g1_ref, b1_ref,
                 wr_ref, rb_ref, tri_ref,
                 h1_ref, h1p_ref, e_ref, w_ref, r_ref, cnt_ref, carry_scr):
    tm = x_ref.shape[0]
    step = pl.program_id(0)

    @pl.when(step == 0)
    def _():
        carry_scr[...] = jnp.zeros(carry_scr.shape, F32)

    x = x_ref[...]
    ya = _dot(om_ref[...], wom_ref[...])
    yb = _dot(os_ref[...], wos_ref[...])
    gates = _dot(x.astype(BF16), wg_ref[...])
    merged = _sigmoid(gates[:, :D_MODEL]) * ya + _sigmoid(gates[:, D_MODEL:]) * yb
    mix = _dot(merged.astype(BF16), wout_ref[...])
    z = DEEPNORM_ALPHA * x + mix
    mu = jnp.mean(z, axis=-1, keepdims=True)
    zc = z - mu
    var = jnp.mean(zc * zc, axis=-1, keepdims=True)
    h1 = zc * lax.rsqrt(var + LN_EPS) * g1_ref[...] + b1_ref[...]
    h1_ref[...] = h1
    _pack_rows(h1, h1p_ref)

    scores = _sigmoid(_dot_nt(wr_ref[...], h1.astype(BF16)))
    choice = scores + rb_ref[...]
    row = lax.broadcasted_iota(jnp.int32, (N_EXPERTS, tm), 0)
    grow = lax.broadcasted_iota(jnp.int32, (GROUP_SIZE, tm), 0)
    gscore = []
    for g in range(N_GROUPS):
        blk = choice[g * GROUP_SIZE:(g + 1) * GROUP_SIZE, :]
        m1 = jnp.max(blk, axis=0, keepdims=True)
        i1 = jnp.min(jnp.where(blk == m1, grow, GROUP_SIZE), axis=0, keepdims=True)
        m2 = jnp.max(jnp.where(grow == i1, NEG_INF, blk), axis=0, keepdims=True)
        gscore.append(m1 + m2)
    gsc = jnp.concatenate(gscore, axis=0)
    gidx = lax.broadcasted_iota(jnp.int32, (N_GROUPS, tm), 0)
    grank = jnp.zeros((N_GROUPS, tm), jnp.int32)
    for g in range(N_GROUPS):
        sg = gsc[g:g + 1, :]
        beats = (sg > gsc) | ((sg == gsc) & (gidx > g))
        grank = grank + beats.astype(jnp.int32)
    gsel = (grank < TOPK_GROUPS).astype(F32)
    emask = jnp.concatenate(
        [jnp.broadcast_to(gsel[g:g + 1, :], (GROUP_SIZE, tm)) for g in range(N_GROUPS)], axis=0)
    work = jnp.where(emask > 0.0, choice, NEG_INF)
    sel = jnp.zeros((N_EXPERTS, tm), F32)
    idxs, svals = [], []
    for _k in range(TOP_K):
        m = jnp.max(work, axis=0, keepdims=True)
        idx = jnp.min(jnp.where(work == m, row, N_EXPERTS), axis=0, keepdims=True)
        hit = row == idx
        svals.append(jnp.sum(jnp.where(hit, scores, 0.0), axis=0, keepdims=True))
        work = jnp.where(hit, NEG_INF, work)
        sel = jnp.where(hit, 1.0, sel)
        idxs.append(idx)
    ssum = svals[0]
    for sv in svals[1:]:
        ssum = ssum + sv
    e_ref[...] = jnp.concatenate(idxs, axis=0)
    w_ref[...] = jnp.concatenate([sv / ssum * ROUTED_SCALE for sv in svals], axis=0)

    carry = carry_scr[...]
    rank = _dot(sel.astype(BF16), tri_ref[...]) + carry[:, 0:1]
    r_ref[...] = jnp.concatenate(
        [jnp.sum(jnp.where(row == idx, rank, 0.0), axis=0, keepdims=True) for idx in idxs],
        axis=0).astype(jnp.int32)
    carry = carry + jnp.sum(sel, axis=1, keepdims=True)
    carry_scr[...] = carry
    cnt_ref[...] = carry.astype(jnp.int32)


def _post(x2, o_mla, o_swa, wom, wos, wg, wout, g1, b1, wr_t, rbias, tri):
    T = x2.shape[0]
    tm = POST_TM
    full = lambda a: pl.BlockSpec(a.shape, lambda i: (0, 0))
    row = lambda n: pl.BlockSpec((tm, n), lambda i: (i, 0))
    col = pl.BlockSpec((TOP_K, tm), lambda i: (0, i))
    return pl.pallas_call(
        _post_kernel,
        grid=(T // tm,),
        in_specs=[row(D_MODEL), row(512), row(512), full(wom), full(wos), full(wg), full(wout),
                  full(g1), full(b1), full(wr_t), full(rbias), full(tri)],
        out_specs=[row(D_MODEL), pl.BlockSpec((tm * ROW_SUB, LANES), lambda i: (i, 0)), col, col, col,
                   pl.BlockSpec((N_EXPERTS, LANES), lambda i: (0, 0))],
        out_shape=[jax.ShapeDtypeStruct((T, D_MODEL), F32),
                   jax.ShapeDtypeStruct((T * ROW_SUB, LANES), jnp.uint32),
                   jax.ShapeDtypeStruct((TOP_K, T), jnp.int32),
                   jax.ShapeDtypeStruct((TOP_K, T), F32),
                   jax.ShapeDtypeStruct((TOP_K, T), jnp.int32),
                   jax.ShapeDtypeStruct((N_EXPERTS, LANES), jnp.int32)],
        scratch_shapes=[pltpu.VMEM((N_EXPERTS, LANES), F32)],
        compiler_params=pltpu.CompilerParams(dimension_semantics=("arbitrary",),
                                             vmem_limit_bytes=VMEM_LIMIT),
        name="post",
    )(x2, o_mla, o_swa, wom, wos, wg, wout, g1, b1, wr_t, rbias, tri)


def _row_copy(src, dst, sem):
    return pltpu.make_async_copy(src, dst, sem)


def _dispatch_kernel(dest_ref, h_ref, xs_in_ref, xs_ref, sem):
    del xs_in_ref
    tm = h_ref.shape[0] // ROW_SUB

    def issue(t, carry):
        src = h_ref.at[pl.ds(pl.multiple_of(t * ROW_SUB, ROW_SUB), ROW_SUB)]
        for k in range(TOP_K):
            d = dest_ref[0, 0, t * TOP_K + k]
            _row_copy(src, xs_ref.at[pl.ds(pl.multiple_of(d * ROW_SUB, ROW_SUB), ROW_SUB)], sem).start()
        return carry

    lax.fori_loop(0, tm, issue, 0)

    def drain(t, carry):
        _row_copy(h_ref.at[pl.ds(0, ROW_SUB)], xs_ref.at[pl.ds(0, ROW_SUB)], sem).wait()
        return carry

    lax.fori_loop(0, tm * TOP_K, drain, 0)


def _dispatch(dest3, h1p, n_rows):
    T = h1p.shape[0] // ROW_SUB
    tm = MOVE_TM
    xs0 = jnp.zeros((n_rows * ROW_SUB, LANES), jnp.uint32)
    return pl.pallas_call(
        _dispatch_kernel,
        grid=(T // tm,),
        in_specs=[pl.BlockSpec((1, 1, tm * TOP_K), lambda i: (i, 0, 0), memory_space=pltpu.SMEM),
                  pl.BlockSpec((tm * ROW_SUB, LANES), lambda i: (i, 0)),
                  pl.BlockSpec(memory_space=pl.ANY)],
        out_specs=pl.BlockSpec(memory_space=pl.ANY),
        out_shape=jax.ShapeDtypeStruct(xs0.shape, jnp.uint32),
        scratch_shapes=[pltpu.SemaphoreType.DMA],
        input_output_aliases={2: 0},
        compiler_params=pltpu.CompilerParams(dimension_semantics=("arbitrary",),
                                             vmem_limit_bytes=VMEM_LIMIT),
        name="dispatch",
    )(dest3, h1p, xs0)


def _experts_kernel(be_ref, nused_ref, xs_ref, wup_ref, wdn_ref, ys_ref, wup_bf, wdn_bf):
    i = pl.program_id(0)
    rows = xs_ref.shape[0] // ROW_SUB
    e = be_ref[i]
    e_prev = be_ref[jnp.maximum(i - 1, 0)]

    @pl.when((i == 0) | (e != e_prev))
    def _():
        wup_bf[...] = wup_ref[0].astype(BF16)
        wdn_bf[...] = wdn_ref[0].astype(BF16)

    @pl.when(i < nused_ref[0])
    def _():
        xb = jnp.concatenate([c.astype(BF16) for c in _unpack_rows(xs_ref, rows)], axis=1)
        gu = _dot(xb, wup_bf[...])
        g = gu[:, :EXPERT_HIDDEN]
        hid = g * _sigmoid(g) * gu[:, EXPERT_HIDDEN:]
        _pack_rows(_dot(hid.astype(BF16), wdn_bf[...]), ys_ref)

    @pl.when(i >= nused_ref[0])
    def _():
        ys_ref[...] = jnp.zeros(ys_ref.shape, jnp.uint32)


def _experts(blk_expert, n_used, xs, w_exp_up, w_exp_down):
    n_blocks = blk_expert.shape[0]
    rb = ROW_BLK * ROW_SUB
    grid_spec = pltpu.PrefetchScalarGridSpec(
        num_scalar_prefetch=2,
        grid=(n_blocks,),
        in_specs=[pl.BlockSpec((rb, LANES), lambda i, be, nu: (i, 0)),
                  pl.BlockSpec((1, D_MODEL, 2 * EXPERT_HIDDEN), lambda i, be, nu: (be[i], 0, 0)),
                  pl.BlockSpec((1, EXPERT_HIDDEN, D_MODEL), lambda i, be, nu: (be[i], 0, 0))],
        out_specs=pl.BlockSpec((rb, LANES), lambda i, be, nu: (i, 0)),
        scratch_shapes=[pltpu.VMEM((D_MODEL, 2 * EXPERT_HIDDEN), BF16),
                        pltpu.VMEM((EXPERT_HIDDEN, D_MODEL), BF16)],
    )
    return pl.pallas_call(
        _experts_kernel,
        grid_spec=grid_spec,
        out_shape=jax.ShapeDtypeStruct(xs.shape, jnp.uint32),
        compiler_params=pltpu.CompilerParams(dimension_semantics=("arbitrary",),
                                             vmem_limit_bytes=VMEM_LIMIT),
        name="experts",
    )(blk_expert, n_used, xs, w_exp_up, w_exp_down)


def _combine_kernel(dest_ref, w_ref, h1_ref, ys_ref, wsu_ref, wsd_ref, g2_ref, b2_ref,
                    o_ref, buf, sem):
    tm = h1_ref.shape[0]

    def issue(t, carry):
        for k in range(TOP_K):
            d = dest_ref[0, 0, t * TOP_K + k]
            _row_copy(ys_ref.at[pl.ds(pl.multiple_of(d * ROW_SUB, ROW_SUB), ROW_SUB)],
                      buf.at[k, pl.ds(pl.multiple_of(t * ROW_SUB, ROW_SUB), ROW_SUB)], sem).start()
        return carry

    lax.fori_loop(0, tm, issue, 0)

    h1 = h1_ref[...]
    gu = _dot(h1.astype(BF16), wsu_ref[...])
    g = gu[:, :SHARED_HIDDEN]
    hid = g * _sigmoid(g) * gu[:, SHARED_HIDDEN:]
    ffn = _dot(hid.astype(BF16), wsd_ref[...])

    def drain(t, carry):
        _row_copy(ys_ref.at[pl.ds(0, ROW_SUB)], buf.at[0, pl.ds(0, ROW_SUB)], sem).wait()
        return carry

    lax.fori_loop(0, tm * TOP_K, drain, 0)

    w = w_ref[...]
    acc = [None] * (2 * ROW_SUB)
    for k in range(TOP_K):
        wk = w[:, k:k + 1]
        chunks = _unpack_rows(buf.at[k], tm)
        for c in range(2 * ROW_SUB):
            acc[c] = wk * chunks[c] if acc[c] is None else acc[c] + wk * chunks[c]
    routed = jnp.concatenate(acc, axis=1)
    z = DEEPNORM_ALPHA * h1 + (routed + ffn)
    mu = jnp.mean(z, axis=-1, keepdims=True)
    zc = z - mu
    var = jnp.mean(zc * zc, axis=-1, keepdims=True)
    o_ref[...] = zc * lax.rsqrt(var + LN_EPS) * g2_ref[...] + b2_ref[...]


def _combine(dest3, w_tok, h1, ys, wsu, wsd, g2, b2):
    T = h1.shape[0]
    tm = MOVE_TM
    full = lambda a: pl.BlockSpec(a.shape, lambda i: (0, 0))
    return pl.pallas_call(
        _combine_kernel,
        grid=(T // tm,),
        in_specs=[pl.BlockSpec((1, 1, tm * TOP_K), lambda i: (i, 0, 0), memory_space=pltpu.SMEM),
                  pl.BlockSpec((tm, TOP_K), lambda i: (i, 0)),
                  pl.BlockSpec((tm, D_MODEL), lambda i: (i, 0)),
                  pl.BlockSpec(memory_space=pl.ANY),
                  full(wsu), full(wsd), full(g2), full(b2)],
        out_specs=pl.BlockSpec((tm, D_MODEL), lambda i: (i, 0)),
        out_shape=jax.ShapeDtypeStruct((T, D_MODEL), F32),
        scratch_shapes=[pltpu.VMEM((TOP_K, tm * ROW_SUB, LANES), jnp.uint32), pltpu.SemaphoreType.DMA],
        compiler_params=pltpu.CompilerParams(dimension_semantics=("arbitrary",),
                                             vmem_limit_bytes=VMEM_LIMIT),
        name="combine",
    )(dest3, w_tok, h1, ys, wsu, wsd, g2, b2)


def _prep_weights(w_in, w_uq, w_ukv, seq):
    z = lambda n: jnp.zeros((D_MODEL, n), F32)
    kr = w_in[:, 512:544]
    qs = w_in[:, 544:1056] * (1.0 / math.sqrt(SWA_HEAD_DIM))
    ks0, ks1 = w_in[:, 1056:1120], w_in[:, 1120:1184]
    vs0, vs1 = w_in[:, 1184:1248], w_in[:, 1248:1312]
    half = MLA_ROPE // 2
    w1 = jnp.concatenate([
        w_in[:, 0:512], qs,
        ks0, z(64), z(64), ks0, ks1, z(64), z(64), ks1,
        vs0, vs0, vs1, vs1,
        z(64), kr, z(32),
        z(64), -kr[:, half:], kr[:, :half], z(32)], axis=1).astype(BF16)
    wg = w_in[:, 1312:3360].astype(BF16)

    zq = lambda n: jnp.zeros((MLA_Q_LORA, n), F32)
    qd = MLA_NOPE + MLA_ROPE
    q_cols, r_cols, k_cols, v_cols = [], [], [], []
    for h in range(MLA_HEADS):
        wq = w_uq[:, h * qd:(h + 1) * qd]
        q_cols += [wq, zq(32)]
        r_cols += [zq(64), -wq[:, MLA_NOPE + half:], wq[:, MLA_NOPE:MLA_NOPE + half], zq(32)]
        wk = w_ukv[:, h * 128:(h + 1) * 128]
        k_cols += [wk[:, :MLA_NOPE], zq(64)]
        v_cols += [wk[:, MLA_NOPE:]]
    wq2 = jnp.concatenate(q_cols + r_cols, axis=1).astype(BF16)
    wkv = jnp.concatenate(k_cols + v_cols, axis=1).astype(BF16)

    inv_freq = ROPE_THETA ** (-jnp.arange(0, MLA_ROPE, 2, dtype=F32) / MLA_ROPE)
    ang = jnp.arange(seq, dtype=F32)[:, None] * inv_freq[None, :]
    cos, sin = jnp.cos(ang), jnp.sin(ang)
    one, zero = jnp.ones((seq, 64), F32), jnp.zeros((seq, 64), F32)
    z32 = jnp.zeros((seq, 32), F32)
    scale = 1.0 / math.sqrt(MLA_NOPE + MLA_ROPE)
    tabs = (jnp.concatenate([one, cos, cos, z32], axis=1) * scale,
            jnp.concatenate([zero, sin, sin, z32], axis=1) * scale,
            jnp.concatenate([zero, cos, cos, z32], axis=1),
            jnp.concatenate([zero, sin, sin, z32], axis=1))
    return w1, wg, wq2, wkv, tabs


def kernel(x, w_in, mla_q_norm, mla_kv_norm, w_uq, w_ukv, attn_sinks, w_o_mla, w_o_swa, w_out,
           ln1_g, ln1_b, w_router, router_bias, w_exp_up, w_exp_down, w_sh_up, w_sh_down,
           ln2_g, ln2_b):
    batch, seq, _ = x.shape
    T = batch * seq
    x2 = x.reshape(T, D_MODEL)
    w1, wg, wq2, wkv, tabs = _prep_weights(w_in, w_uq, w_ukv, seq)
    q, k, v, qs, ks, vs = _proj(x2, w1, wq2, wkv, mla_q_norm.reshape(1, -1), mla_kv_norm.reshape(1, -1),
                                tabs, seq)
    o_mla = _mla(q, k, v, batch, seq)
    o_swa = _swa(attn_sinks.astype(F32), qs, ks, vs, batch, seq)

    tri = (lax.broadcasted_iota(jnp.int32, (POST_TM, POST_TM), 0)
           < lax.broadcasted_iota(jnp.int32, (POST_TM, POST_TM), 1)).astype(BF16)
    h1, h1p, e_t, w_t, r_t, cnt = _post(
        x2, o_mla, o_swa, w_o_mla.astype(BF16), w_o_swa.astype(BF16), wg, w_out.astype(BF16),
        ln1_g.reshape(1, -1), ln1_b.reshape(1, -1), w_router.T.astype(BF16),
        router_bias.reshape(-1, 1).astype(F32), tri)

    counts = cnt[:, 0]
    pad = (counts + ROW_BLK - 1) // ROW_BLK * ROW_BLK
    pad_end = jnp.cumsum(pad)
    pad_start = pad_end - pad
    n_rows = (T * TOP_K // ROW_BLK + N_EXPERTS) * ROW_BLK
    n_blocks = n_rows // ROW_BLK
    dest = pad_start[e_t] + r_t
    dest3 = dest.T.reshape(T // MOVE_TM, 1, MOVE_TM * TOP_K)
    blk_expert = jnp.clip(jnp.searchsorted(pad_end, jnp.arange(n_blocks, dtype=jnp.int32) * ROW_BLK,
                                           side="right"), 0, N_EXPERTS - 1).astype(jnp.int32)
    n_used = (pad_end[-1:] // ROW_BLK).astype(jnp.int32)

    xs = _dispatch(dest3, h1p, n_rows)
    ys = _experts(blk_expert, n_used, xs, w_exp_up, w_exp_down)
    out = _combine(dest3, w_t.T, h1, ys, w_sh_up.astype(BF16), w_sh_down.astype(BF16),
                   ln2_g.reshape(1, -1), ln2_b.reshape(1, -1))
    return out.reshape(batch, seq, D_MODEL)
```

```python
import functools
import math

import jax
import jax.numpy as jnp
from jax import lax
from jax.experimental import pallas as pl
from jax.experimental.pallas import tpu as pltpu

D_MODEL = 1024
MLA_HEADS = 8
MLA_Q_LORA = 256
MLA_KV_LORA = 256
MLA_NOPE = 64
MLA_ROPE = 32
MLA_V = 64
ROPE_THETA = 10000.0
SWA_HEADS = 8
SWA_KV_HEADS = 2
SWA_HEAD_DIM = 64
SWA_WINDOW = 128
N_EXPERTS = 256
TOP_K = 8
N_GROUPS = 8
GROUP_SIZE = N_EXPERTS // N_GROUPS
TOPK_GROUPS = 4
EXPERT_HIDDEN = 256
SHARED_HIDDEN = 256
ROUTED_SCALE = 2.5
DEEPNORM_ALPHA = 2.0 ** 0.25
LN_EPS = 1e-5
RMS_EPS = 1e-6

LANES = 128
ROW_WORDS = D_MODEL // 2
ROW_SUB = ROW_WORDS // LANES
VMEM_LIMIT = 48 * 1024 * 1024

PROJ_TM = 512
MLA_TQ = 256
MLA_HPS = 4
POST_TM = 256
ROW_BLK = 256
MOVE_TM = 256

BF16 = jnp.bfloat16
F32 = jnp.float32
NEG_INF = float("-inf")


def _sigmoid(v):
    return 1.0 / (1.0 + jnp.exp(-v))


def _dot(a, b):
    return jnp.dot(a, b, preferred_element_type=F32)


def _dot_nt(a, b):
    return lax.dot_general(a, b, (((1,), (1,)), ((), ())), preferred_element_type=F32)


def _pack_rows(y, out_ref):
    rows = y.shape[0]
    for j in range(ROW_SUB):
        a = y[:, j * LANES:(j + 1) * LANES].astype(BF16).astype(F32)
        b = y[:, ROW_WORDS + j * LANES:ROW_WORDS + (j + 1) * LANES].astype(BF16).astype(F32)
        ua = pltpu.bitcast(a, jnp.uint32) >> 16
        ub = pltpu.bitcast(b, jnp.uint32)
        out_ref[pl.ds(j, rows, stride=ROW_SUB), :] = ua | ub


def _unpack_rows(ref, rows):
    lo, hi = [], []
    for j in range(ROW_SUB):
        u = ref[pl.ds(j, rows, stride=ROW_SUB), :]
        lo.append(pltpu.bitcast(u << 16, F32))
        hi.append(pltpu.bitcast(u & jnp.uint32(0xFFFF0000), F32))
    return lo + hi


def _proj_kernel(x_ref, w1_ref, wq_ref, wk_ref, wvt_ref, gq_ref, gkv_ref, cq_ref, sq_ref, ck_ref, sk_ref,
                 q_ref, k_ref, vt_ref, qs_ref, ks_ref, vs_ref):
    xb = x_ref[...].astype(BF16)
    p = _dot(xb, w1_ref[...])

    def rms(c, g):
        return c * lax.rsqrt(jnp.mean(c * c, axis=-1, keepdims=True) + RMS_EPS) * g

    cqn = rms(p[:, 0:256], gq_ref[...]).astype(BF16)
    ckvn = rms(p[:, 256:512], gkv_ref[...]).astype(BF16)
    qs_ref[...] = p[:, 512:1024].astype(BF16)
    ks_ref[...] = p[:, 1024:1536].astype(BF16)
    vs_ref[...] = p[:, 1536:1792].astype(BF16)
    kr = p[:, 1792:1920] * ck_ref[...] + p[:, 1920:2048] * sk_ref[...]
    qq = _dot(cqn, wq_ref[...])
    kn = _dot(ckvn, wk_ref[...])
    cq = cq_ref[...]
    sq = sq_ref[...]
    for h in range(MLA_HEADS):
        a = qq[:, h * LANES:(h + 1) * LANES]
        b = qq[:, 1024 + h * LANES:1024 + (h + 1) * LANES]
        q_ref[:, h * LANES:(h + 1) * LANES] = (a * cq + b * sq).astype(BF16)
        k_ref[:, h * LANES:(h + 1) * LANES] = (kn[:, h * LANES:(h + 1) * LANES] + kr).astype(BF16)
    vt_ref[...] = _dot_nt(wvt_ref[...], ckvn).astype(BF16)


def _proj(x2, w1, wq2, wk, wvt, gq, gkv, tabs, seq):
    T = x2.shape[0]
    tm = PROJ_TM
    nper = seq // tm
    full = lambda shape: pl.BlockSpec(shape, lambda i: (0, 0))
    tab = pl.BlockSpec((tm, LANES), lambda i: (i % nper, 0))
    row = lambda n: pl.BlockSpec((tm, n), lambda i: (i, 0))
    return pl.pallas_call(
        _proj_kernel,
        grid=(T // tm,),
        in_specs=[row(D_MODEL), full(w1.shape), full(wq2.shape), full(wk.shape), full(wvt.shape),
                  full(gq.shape), full(gkv.shape), tab, tab, tab, tab],
        out_specs=[row(1024), row(1024), pl.BlockSpec((512, tm), lambda i: (0, i)),
                   row(512), row(512), row(256)],
        out_shape=[jax.ShapeDtypeStruct((T, 1024), BF16), jax.ShapeDtypeStruct((T, 1024), BF16),
                   jax.ShapeDtypeStruct((512, T), BF16), jax.ShapeDtypeStruct((T, 512), BF16),
                   jax.ShapeDtypeStruct((T, 512), BF16), jax.ShapeDtypeStruct((T, 256), BF16)],
        compiler_params=pltpu.CompilerParams(dimension_semantics=("parallel",),
                                             vmem_limit_bytes=VMEM_LIMIT),
        name="proj",
    )(x2, w1, wq2, wk, wvt, gq, gkv, *tabs)


def _mla_kernel(q_ref, k_ref, vt_ref, o_ref, *acc_scr):
    tq = q_ref.shape[0]
    qi = pl.program_id(2)
    for acc in acc_scr:
        acc[...] = jnp.zeros(acc.shape, F32)

    def step(kc, stats, masked):
        ks = pl.multiple_of(kc * tq, tq)
        new_stats = []
        for h in range(MLA_HPS):
            m_prev, l_prev = stats[h]
            s = _dot_nt(k_ref[pl.ds(ks, tq), h * LANES:(h + 1) * LANES],
                        q_ref[:, h * LANES:(h + 1) * LANES])
            if masked:
                key = lax.broadcasted_iota(jnp.int32, s.shape, 0)
                qry = lax.broadcasted_iota(jnp.int32, s.shape, 1)
                s = jnp.where(key <= qry, s, NEG_INF)
            m_new = jnp.maximum(m_prev, jnp.max(s, axis=0, keepdims=True))
            alpha = jnp.exp2(m_prev - m_new)
            p = jnp.exp2(s - m_new)
            l_new = alpha * l_prev + jnp.sum(p, axis=0, keepdims=True)
            pv = _dot(vt_ref[h * MLA_V:(h + 1) * MLA_V, pl.ds(ks, tq)], p.astype(BF16))
            acc_scr[h][...] = acc_scr[h][...] * alpha + pv
            new_stats.append((m_new, l_new))
        return tuple(new_stats)

    init = tuple((jnp.full((1, tq), NEG_INF, F32), jnp.zeros((1, tq), F32)) for _ in range(MLA_HPS))
    stats = lax.fori_loop(0, qi, lambda kc, st: step(kc, st, False), init)
    stats = step(qi, stats, True)
    for h2 in range(MLA_HPS // 2):
        out_t = jnp.concatenate([acc_scr[2 * h2 + g][...] / stats[2 * h2 + g][1] for g in range(2)],
                                axis=0)
        o_ref[:, h2 * LANES:(h2 + 1) * LANES] = out_t.T.astype(BF16)


def _mla(q, k, vt, batch, seq):
    T = q.shape[0]
    tq = MLA_TQ
    nq = seq // tq
    hps = MLA_HPS
    return pl.pallas_call(
        _mla_kernel,
        grid=(batch, MLA_HEADS // hps, nq),
        in_specs=[pl.BlockSpec((tq, hps * LANES), lambda b, j, i: (b * nq + i, j)),
                  pl.BlockSpec((seq, hps * LANES), lambda b, j, i: (b, j)),
                  pl.BlockSpec((hps * MLA_V, seq), lambda b, j, i: (j, b))],
        out_specs=pl.BlockSpec((tq, hps * MLA_V), lambda b, j, i: (b * nq + i, j)),
        out_shape=jax.ShapeDtypeStruct((T, MLA_HEADS * MLA_V), BF16),
        scratch_shapes=[pltpu.VMEM((MLA_V, tq), F32) for _ in range(hps)],
        compiler_params=pltpu.CompilerParams(
            dimension_semantics=("parallel", "parallel", "arbitrary"), vmem_limit_bytes=VMEM_LIMIT),
        name="mla",
    )(q, k, vt)


def _swa_kernel(sink_ref, q_ref, kc_ref, kp_ref, vc_ref, vp_ref, o_ref):
    W = SWA_WINDOW
    n = pl.program_id(1)
    i = lax.broadcasted_iota(jnp.int32, (W, W), 0)
    j = lax.broadcasted_iota(jnp.int32, (W, W), 1)
    cur_ok = j <= i
    prev_ok = (j > i) & (n > 0)
    dist_c = (i - j).astype(F32)
    dist_p = (i + W - j).astype(F32)
    lane_lo = lax.broadcasted_iota(jnp.int32, (W, LANES), 1) < 64
    for pair in range(SWA_HEADS // 2):
        kvh = pair // 2
        qp = q_ref[:, pair * LANES:(pair + 1) * LANES]
        vc = vc_ref[:, kvh * LANES:(kvh + 1) * LANES]
        vp = vp_ref[:, kvh * LANES:(kvh + 1) * LANES]
        outs = []
        for g in range(2):
            head = 2 * pair + g
            slope = 2.0 ** (-8.0 * (head + 1) / SWA_HEADS)
            col = (2 * kvh + g) * LANES
            s_c = _dot_nt(qp, kc_ref[:, col:col + LANES])
            s_p = _dot_nt(qp, kp_ref[:, col:col + LANES])
            s_c = jnp.where(cur_ok, s_c - slope * dist_c, NEG_INF)
            s_p = jnp.where(prev_ok, s_p - slope * dist_p, NEG_INF)
            sink = sink_ref[head]
            m = jnp.maximum(jnp.maximum(jnp.max(s_c, axis=1, keepdims=True),
                                        jnp.max(s_p, axis=1, keepdims=True)), sink)
            p_c = jnp.exp(s_c - m)
            p_p = jnp.exp(s_p - m)
            denom = (jnp.sum(p_c, axis=1, keepdims=True) + jnp.sum(p_p, axis=1, keepdims=True)
                     + jnp.exp(sink - m))
            outs.append((_dot(p_c.astype(BF16), vc) + _dot(p_p.astype(BF16), vp)) / denom)
        o_ref[:, pair * LANES:(pair + 1) * LANES] = jnp.where(lane_lo, outs[0], outs[1]).astype(BF16)


def _swa(sinks, qs, ks, vs, batch, seq):
    T = qs.shape[0]
    W = SWA_WINDOW
    nb = seq // W
    cur = lambda n: pl.BlockSpec((W, n), lambda b, i: (b * nb + i, 0))
    prev = lambda n: pl.BlockSpec((W, n), lambda b, i: (b * nb + jnp.maximum(i - 1, 0), 0))
    return pl.pallas_call(
        _swa_kernel,
        grid=(batch, nb),
        in_specs=[pl.BlockSpec(memory_space=pltpu.SMEM), cur(512), cur(512), prev(512), cur(256), prev(256)],
        out_specs=cur(512),
        out_shape=jax.ShapeDtypeStruct((T, 512), BF16),
        compiler_params=pltpu.CompilerParams(dimension_semantics=("parallel", "parallel"),
                                             vmem_limit_bytes=VMEM_LIMIT),
        name="swa",
    )(sinks, qs, ks, ks, vs, vs)


def _post_kernel(x_ref, om_ref, os_ref, wom_ref, wos_ref, wg_ref, wout_ref, g1_ref, b1_ref,
                 wr_ref, rb_ref, tri_ref,
                 h1_ref, h1p_ref, e_ref, w_ref, r_ref, cnt_ref, carry_scr):
    tm = x_ref.shape[0]
    step = pl.program_id(0)

    @pl.when(step == 0)
    def _():
        carry_scr[...] = jnp.zeros(carry_scr.shape, F32)

    x = x_ref[...]
    ya = _dot(om_ref[...], wom_ref[...])
    yb = _dot(os_ref[...], wos_ref[...])
    gates = _dot(x.astype(BF16), wg_ref[...])
    merged = _sigmoid(gates[:, :D_MODEL]) * ya + _sigmoid(gates[:, D_MODEL:]) * yb
    mix = _dot(merged.astype(BF16), wout_ref[...])
    z = DEEPNORM_ALPHA * x + mix
    mu = jnp.mean(z, axis=-1, keepdims=True)
    zc = z - mu
    var = jnp.mean(zc * zc, axis=-1, keepdims=True)
    h1 = zc * lax.rsqrt(var + LN_EPS) * g1_ref[...] + b1_ref[...]
    h1_ref[...] = h1
    _pack_rows(h1, h1p_ref)

    scores = _sigmoid(_dot_nt(wr_ref[...], h1.astype(BF16)))
    choice = scores + rb_ref[...]
    row = lax.broadcasted_iota(jnp.int32, (N_EXPERTS, tm), 0)
    grow = lax.broadcasted_iota(jnp.int32, (GROUP_SIZE, tm), 0)
    gscore = []
    for g in range(N_GROUPS):
        blk = choice[g * GROUP_SIZE:(g + 1) * GROUP_SIZE, :]
        m1 = jnp.max(blk, axis=0, keepdims=True)
        i1 = jnp.min(jnp.where(blk == m1, grow, GROUP_SIZE), axis=0, keepdims=True)
        m2 = jnp.max(jnp.where(grow == i1, NEG_INF, blk), axis=0, keepdims=True)
        gscore.append(m1 + m2)
    gsc = jnp.concatenate(gscore, axis=0)
    gidx = lax.broadcasted_iota(jnp.int32, (N_GROUPS, tm), 0)
    grank = jnp.zeros((N_GROUPS, tm), jnp.int32)
    for g in range(N_GROUPS):
        sg = gsc[g:g + 1, :]
        beats = (sg > gsc) | ((sg == gsc) & (gidx > g))
        grank = grank + beats.astype(jnp.int32)
    gsel = (grank < TOPK_GROUPS).astype(F32)
    emask = jnp.concatenate(
        [jnp.broadcast_to(gsel[g:g + 1, :], (GROUP_SIZE, tm)) for g in range(N_GROUPS)], axis=0)
    work = jnp.where(emask > 0.0, choice, NEG_INF)
    sel = jnp.zeros((N_EXPERTS, tm), F32)
    idxs, svals = [], []
    for _k in range(TOP_K):
        m = jnp.max(work, axis=0, keepdims=True)
        idx = jnp.min(jnp.where(work == m, row, N_EXPERTS), axis=0, keepdims=True)
        hit = row == idx
        svals.append(jnp.sum(jnp.where(hit, scores, 0.0), axis=0, keepdims=True))
        work = jnp.where(hit, NEG_INF, work)
        sel = jnp.where(hit, 1.0, sel)
        idxs.append(idx)
    ssum = svals[0]
    for sv in svals[1:]:
        ssum = ssum + sv
    e_ref[...] = jnp.concatenate(idxs, axis=0)
    w_ref[...] = jnp.concatenate([sv / ssum * ROUTED_SCALE for sv in svals], axis=0)

    carry = carry_scr[...]
    rank = _dot(sel.astype(BF16), tri_ref[...]) + carry[:, 0:1]
    r_ref[...] = jnp.concatenate(
        [jnp.sum(jnp.where(row == idx, rank, 0.0), axis=0, keepdims=True) for idx in idxs],
        axis=0).astype(jnp.int32)
    carry = carry + jnp.sum(sel, axis=1, keepdims=True)
    carry_scr[...] = carry
    cnt_ref[...] = carry.astype(jnp.int32)


def _post(x2, o_mla, o_swa, wom, wos, wg, wout, g1, b1, wr_t, rbias, tri):
    T = x2.shape[0]
    tm = POST_TM
    full = lambda a: pl.BlockSpec(a.shape, lambda i: (0, 0))
    row = lambda n: pl.BlockSpec((tm, n), lambda i: (i, 0))
    col = pl.BlockSpec((TOP_K, tm), lambda i: (0, i))
    return pl.pallas_call(
        _post_kernel,
        grid=(T // tm,),
        in_specs=[row(D_MODEL), row(512), row(512), full(wom), full(wos), full(wg), full(wout),
                  full(g1), full(b1), full(wr_t), full(rbias), full(tri)],
        out_specs=[row(D_MODEL), pl.BlockSpec((tm * ROW_SUB, LANES), lambda i: (i, 0)), col, col, col,
                   pl.BlockSpec((N_EXPERTS, LANES), lambda i: (0, 0))],
        out_shape=[jax.ShapeDtypeStruct((T, D_MODEL), F32),
                   jax.ShapeDtypeStruct((T * ROW_SUB, LANES), jnp.uint32),
                   jax.ShapeDtypeStruct((TOP_K, T), jnp.int32),
                   jax.ShapeDtypeStruct((TOP_K, T), F32),
                   jax.ShapeDtypeStruct((TOP_K, T), jnp.int32),
                   jax.ShapeDtypeStruct((N_EXPERTS, LANES), jnp.int32)],
        scratch_shapes=[pltpu.VMEM((N_EXPERTS, LANES), F32)],
        compiler_params=pltpu.CompilerParams(dimension_semantics=("arbitrary",),
                                             vmem_limit_bytes=VMEM_LIMIT),
        name="post",
    )(x2, o_mla, o_swa, wom, wos, wg, wout, g1, b1, wr_t, rbias, tri)


def _dest_kernel(start_ref, e_ref, r_ref, o_ref):
    e = e_ref[...]
    base = jnp.zeros(e.shape, jnp.int32)
    for j in range(N_EXPERTS):
        base = jnp.where(e == j, start_ref[j], base)
    o_ref[...] = base + r_ref[...]


def _dest(seg_start, e_t, r_t):
    T = e_t.shape[1]
    tn = min(T, 4096)
    col = pl.BlockSpec((TOP_K, tn), lambda i: (0, i))
    return pl.pallas_call(
        _dest_kernel,
        grid=(T // tn,),
        in_specs=[pl.BlockSpec(memory_space=pltpu.SMEM), col, col],
        out_specs=col,
        out_shape=jax.ShapeDtypeStruct((TOP_K, T), jnp.int32),
        compiler_params=pltpu.CompilerParams(dimension_semantics=("parallel",)),
        name="dest",
    )(seg_start, e_t, r_t)


def _row_copy(src, dst, sem):
    return pltpu.make_async_copy(src, dst, sem)


def _dispatch_kernel(dest_ref, h_ref, xs_in_ref, xs_ref, sem):
    del xs_in_ref
    tm = h_ref.shape[0] // ROW_SUB

    def issue(t, carry):
        src = h_ref.at[pl.ds(pl.multiple_of(t * ROW_SUB, ROW_SUB), ROW_SUB)]
        for k in range(TOP_K):
            d = dest_ref[0, 0, t * TOP_K + k]
            _row_copy(src, xs_ref.at[pl.ds(pl.multiple_of(d * ROW_SUB, ROW_SUB), ROW_SUB)], sem).start()
        return carry

    lax.fori_loop(0, tm, issue, 0)

    def drain(t, carry):
        _row_copy(h_ref.at[pl.ds(0, ROW_SUB)], xs_ref.at[pl.ds(0, ROW_SUB)], sem).wait()
        return carry

    lax.fori_loop(0, tm * TOP_K, drain, 0)


def _dispatch(dest3, h1p, n_rows):
    T = h1p.shape[0] // ROW_SUB
    tm = MOVE_TM
    xs0 = jnp.zeros((n_rows * ROW_SUB, LANES), jnp.uint32)
    return pl.pallas_call(
        _dispatch_kernel,
        grid=(T // tm,),
        in_specs=[pl.BlockSpec((1, 1, tm * TOP_K), lambda i: (i, 0, 0), memory_space=pltpu.SMEM),
                  pl.BlockSpec((tm * ROW_SUB, LANES), lambda i: (i, 0)),
                  pl.BlockSpec(memory_space=pl.ANY)],
        out_specs=pl.BlockSpec(memory_space=pl.ANY),
        out_shape=jax.ShapeDtypeStruct(xs0.shape, jnp.uint32),
        scratch_shapes=[pltpu.SemaphoreType.DMA],
        input_output_aliases={2: 0},
        compiler_params=pltpu.CompilerParams(dimension_semantics=("arbitrary",),
                                             vmem_limit_bytes=VMEM_LIMIT),
        name="dispatch",
    )(dest3, h1p, xs0)


def _experts_kernel(be_ref, nused_ref, xs_ref, wup_ref, wdn_ref, ys_ref, wup_bf, wdn_bf):
    i = pl.program_id(0)
    rows = xs_ref.shape[0] // ROW_SUB
    e = be_ref[i]
    e_prev = be_ref[jnp.maximum(i - 1, 0)]

    @pl.when((i == 0) | (e != e_prev))
    def _():
        wup_bf[...] = wup_ref[0].astype(BF16)
        wdn_bf[...] = wdn_ref[0].astype(BF16)

    @pl.when(i < nused_ref[0])
    def _():
        xb = jnp.concatenate([c.astype(BF16) for c in _unpack_rows(xs_ref, rows)], axis=1)
        gu = _dot(xb, wup_bf[...])
        g = gu[:, :EXPERT_HIDDEN]
        hid = g * _sigmoid(g) * gu[:, EXPERT_HIDDEN:]
        _pack_rows(_dot(hid.astype(BF16), wdn_bf[...]), ys_ref)

    @pl.when(i >= nused_ref[0])
    def _():
        ys_ref[...] = jnp.zeros(ys_ref.shape, jnp.uint32)


def _experts(blk_expert, n_used, xs, w_exp_up, w_exp_down):
    n_blocks = blk_expert.shape[0]
    rb = ROW_BLK * ROW_SUB
    grid_spec = pltpu.PrefetchScalarGridSpec(
        num_scalar_prefetch=2,
        grid=(n_blocks,),
        in_specs=[pl.BlockSpec((rb, LANES), lambda i, be, nu: (i, 0)),
                  pl.BlockSpec((1, D_MODEL, 2 * EXPERT_HIDDEN), lambda i, be, nu: (be[i], 0, 0)),
                  pl.BlockSpec((1, EXPERT_HIDDEN, D_MODEL), lambda i, be, nu: (be[i], 0, 0))],
        out_specs=pl.BlockSpec((rb, LANES), lambda i, be, nu: (i, 0)),
        scratch_shapes=[pltpu.VMEM((D_MODEL, 2 * EXPERT_HIDDEN), BF16),
                        pltpu.VMEM((EXPERT_HIDDEN, D_MODEL), BF16)],
    )
    return pl.pallas_call(
        _experts_kernel,
        grid_spec=grid_spec,
        out_shape=jax.ShapeDtypeStruct(xs.shape, jnp.uint32),
        compiler_params=pltpu.CompilerParams(dimension_semantics=("arbitrary",),
                                             vmem_limit_bytes=VMEM_LIMIT),
        name="experts",
    )(blk_expert, n_used, xs, w_exp_up, w_exp_down)


def _combine_kernel(dest_ref, w_ref, h1_ref, ys_ref, wsu_ref, wsd_ref, g2_ref, b2_ref,
                    o_ref, buf, sem):
    tm = h1_ref.shape[0]

    def issue(t, carry):
        for k in range(TOP_K):
            d = dest_ref[0, 0, t * TOP_K + k]
            _row_copy(ys_ref.at[pl.ds(pl.multiple_of(d * ROW_SUB, ROW_SUB), ROW_SUB)],
                      buf.at[k, pl.ds(pl.multiple_of(t * ROW_SUB, ROW_SUB), ROW_SUB)], sem).start()
        return carry

    lax.fori_loop(0, tm, issue, 0)

    h1 = h1_ref[...]
    gu = _dot(h1.astype(BF16), wsu_ref[...])
    g = gu[:, :SHARED_HIDDEN]
    hid = g * _sigmoid(g) * gu[:, SHARED_HIDDEN:]
    ffn = _dot(hid.astype(BF16), wsd_ref[...])

    def drain(t, carry):
        _row_copy(ys_ref.at[pl.ds(0, ROW_SUB)], buf.at[0, pl.ds(0, ROW_SUB)], sem).wait()
        return carry

    lax.fori_loop(0, tm * TOP_K, drain, 0)

    w = w_ref[...]
    acc = [None] * (2 * ROW_SUB)
    for k in range(TOP_K):
        wk = w[:, k:k + 1]
        chunks = _unpack_rows(buf.at[k], tm)
        for c in range(2 * ROW_SUB):
            acc[c] = wk * chunks[c] if acc[c] is None else acc[c] + wk * chunks[c]
    routed = jnp.concatenate(acc, axis=1)
    z = DEEPNORM_ALPHA * h1 + (routed + ffn)
    mu = jnp.mean(z, axis=-1, keepdims=True)
    zc = z - mu
    var = jnp.mean(zc * zc, axis=-1, keepdims=True)
    o_ref[...] = zc * lax.rsqrt(var + LN_EPS) * g2_ref[...] + b2_ref[...]


def _combine(dest3, w_tok, h1, ys, wsu, wsd, g2, b2):
    T = h1.shape[0]
    tm = MOVE_TM
    full = lambda a: pl.BlockSpec(a.shape, lambda i: (0, 0))
    return pl.pallas_call(
        _combine_kernel,
        grid=(T // tm,),
        in_specs=[pl.BlockSpec((1, 1, tm * TOP_K), lambda i: (i, 0, 0), memory_space=pltpu.SMEM),
                  pl.BlockSpec((tm, TOP_K), lambda i: (i, 0)),
                  pl.BlockSpec((tm, D_MODEL), lambda i: (i, 0)),
                  pl.BlockSpec(memory_space=pl.ANY),
                  full(wsu), full(wsd), full(g2), full(b2)],
        out_specs=pl.BlockSpec((tm, D_MODEL), lambda i: (i, 0)),
        out_shape=jax.ShapeDtypeStruct((T, D_MODEL), F32),
        scratch_shapes=[pltpu.VMEM((TOP_K, tm * ROW_SUB, LANES), jnp.uint32), pltpu.SemaphoreType.DMA],
        compiler_params=pltpu.CompilerParams(dimension_semantics=("arbitrary",),
                                             vmem_limit_bytes=VMEM_LIMIT),
        name="combine",
    )(dest3, w_tok, h1, ys, wsu, wsd, g2, b2)


def _prep_weights(w_in, w_uq, w_ukv, seq):
    z = lambda n: jnp.zeros((D_MODEL, n), F32)
    kr = w_in[:, 512:544]
    qs = w_in[:, 544:1056] * (1.0 / math.sqrt(SWA_HEAD_DIM))
    ks0, ks1 = w_in[:, 1056:1120], w_in[:, 1120:1184]
    vs0, vs1 = w_in[:, 1184:1248], w_in[:, 1248:1312]
    half = MLA_ROPE // 2
    w1 = jnp.concatenate([
        w_in[:, 0:512], qs,
        ks0, z(64), z(64), ks0, ks1, z(64), z(64), ks1,
        vs0, vs0, vs1, vs1,
        z(64), kr, z(32),
        z(64), -kr[:, half:], kr[:, :half], z(32)], axis=1).astype(BF16)
    wg = w_in[:, 1312:3360].astype(BF16)

    zq = lambda n: jnp.zeros((MLA_Q_LORA, n), F32)
    qd = MLA_NOPE + MLA_ROPE
    q_cols, r_cols, k_cols, v_cols = [], [], [], []
    for h in range(MLA_HEADS):
        wq = w_uq[:, h * qd:(h + 1) * qd]
        q_cols += [wq, zq(32)]
        r_cols += [zq(64), -wq[:, MLA_NOPE + half:], wq[:, MLA_NOPE:MLA_NOPE + half], zq(32)]
        wk = w_ukv[:, h * 128:(h + 1) * 128]
        k_cols += [wk[:, :MLA_NOPE], zq(64)]
        v_cols += [wk[:, MLA_NOPE:]]
    wq2 = jnp.concatenate(q_cols + r_cols, axis=1).astype(BF16)
    wk = jnp.concatenate(k_cols, axis=1).astype(BF16)
    wvt = jnp.concatenate(v_cols, axis=1).T.astype(BF16)

    inv_freq = ROPE_THETA ** (-jnp.arange(0, MLA_ROPE, 2, dtype=F32) / MLA_ROPE)
    ang = jnp.arange(seq, dtype=F32)[:, None] * inv_freq[None, :]
    cos, sin = jnp.cos(ang), jnp.sin(ang)
    one, zero = jnp.ones((seq, 64), F32), jnp.zeros((seq, 64), F32)
    z32 = jnp.zeros((seq, 32), F32)
    scale = math.log2(math.e) / math.sqrt(MLA_NOPE + MLA_ROPE)
    tabs = (jnp.concatenate([one, cos, cos, z32], axis=1) * scale,
            jnp.concatenate([zero, sin, sin, z32], axis=1) * scale,
            jnp.concatenate([zero, cos, cos, z32], axis=1),
            jnp.concatenate([zero, sin, sin, z32], axis=1))
    return w1, wg, wq2, wk, wvt, tabs


def kernel(x, w_in, mla_q_norm, mla_kv_norm, w_uq, w_ukv, attn_sinks, w_o_mla, w_o_swa, w_out,
           ln1_g, ln1_b, w_router, router_bias, w_exp_up, w_exp_down, w_sh_up, w_sh_down,
           ln2_g, ln2_b):
    batch, seq, _ = x.shape
    T = batch * seq
    x2 = x.reshape(T, D_MODEL)
    w1, wg, wq2, wk, wvt, tabs = _prep_weights(w_in, w_uq, w_ukv, seq)
    q, k, vt, qs, ks, vs = _proj(x2, w1, wq2, wk, wvt, mla_q_norm.reshape(1, -1),
                                 mla_kv_norm.reshape(1, -1), tabs, seq)
    o_mla = _mla(q, k, vt, batch, seq)
    o_swa = _swa(attn_sinks.astype(F32), qs, ks, vs, batch, seq)

    tri = (lax.broadcasted_iota(jnp.int32, (POST_TM, POST_TM), 0)
           < lax.broadcasted_iota(jnp.int32, (POST_TM, POST_TM), 1)).astype(BF16)
    h1, h1p, e_t, w_t, r_t, cnt = _post(
        x2, o_mla, o_swa, w_o_mla.astype(BF16), w_o_swa.astype(BF16), wg, w_out.astype(BF16),
        ln1_g.reshape(1, -1), ln1_b.reshape(1, -1), w_router.T.astype(BF16),
        router_bias.reshape(-1, 1).astype(F32), tri)

    counts = cnt[:, 0]
    pad = (counts + ROW_BLK - 1) // ROW_BLK * ROW_BLK
    pad_end = jnp.cumsum(pad)
    pad_start = pad_end - pad
    n_rows = (T * TOP_K // ROW_BLK + N_EXPERTS) * ROW_BLK
    n_blocks = n_rows // ROW_BLK
    dest = _dest(pad_start.astype(jnp.int32), e_t, r_t)
    dest3 = dest.T.reshape(T // MOVE_TM, 1, MOVE_TM * TOP_K)
    blk_expert = jnp.clip(jnp.searchsorted(pad_end, jnp.arange(n_blocks, dtype=jnp.int32) * ROW_BLK,
                                           side="right"), 0, N_EXPERTS - 1).astype(jnp.int32)
    n_used = (pad_end[-1:] // ROW_BLK).astype(jnp.int32)

    xs = _dispatch(dest3, h1p, n_rows)
    ys = _experts(blk_expert, n_used, xs, w_exp_up, w_exp_down)
    out = _combine(dest3, w_t.T, h1, ys, w_sh_up.astype(BF16), w_sh_down.astype(BF16),
                   ln2_g.reshape(1, -1), ln2_b.reshape(1, -1))
    return out.reshape(batch, seq, D_MODEL)
```

```python
import math

import jax
import jax.numpy as jnp
from jax import lax
from jax.experimental import pallas as pl
from jax.experimental.pallas import tpu as pltpu
from jax.experimental.pallas import tpu_sc as plsc

D_MODEL = 1024
MLA_HEADS = 8
MLA_Q_LORA = 256
MLA_KV_LORA = 256
MLA_NOPE = 64
MLA_ROPE = 32
MLA_V = 64
ROPE_THETA = 10000.0
SWA_HEADS = 8
SWA_KV_HEADS = 2
SWA_HEAD_DIM = 64
SWA_WINDOW = 128
N_EXPERTS = 256
TOP_K = 8
N_GROUPS = 8
GROUP_SIZE = N_EXPERTS // N_GROUPS
TOPK_GROUPS = 4
EXPERT_HIDDEN = 256
SHARED_HIDDEN = 256
ROUTED_SCALE = 2.5
DEEPNORM_ALPHA = 2.0 ** 0.25
LN_EPS = 1e-5
RMS_EPS = 1e-6

LANES = 128
ROW_WORDS = D_MODEL // 2
ROW_SUB = ROW_WORDS // LANES
VMEM_LIMIT = 48 * 1024 * 1024

PROJ_TM = 512
MLA_TQ = 256
MLA_HPS = 4
POST_TM = 256
ROW_BLK = 256
COMBINE_TM = 256
SC_CHUNK = 32

BF16 = jnp.bfloat16
F32 = jnp.float32
NEG_INF = float("-inf")


def _sigmoid(v):
    return 1.0 / (1.0 + jnp.exp(-v))


def _dot(a, b):
    return jnp.dot(a, b, preferred_element_type=F32)


def _dot_nt(a, b):
    return lax.dot_general(a, b, (((1,), (1,)), ((), ())), preferred_element_type=F32)


def _pack_rows(y, out_ref):
    rows = y.shape[0]
    for j in range(ROW_SUB):
        a = y[:, j * LANES:(j + 1) * LANES].astype(BF16).astype(F32)
        b = y[:, ROW_WORDS + j * LANES:ROW_WORDS + (j + 1) * LANES].astype(BF16).astype(F32)
        ua = pltpu.bitcast(a, jnp.uint32) >> 16
        ub = pltpu.bitcast(b, jnp.uint32)
        out_ref[pl.ds(j, rows, stride=ROW_SUB), :] = ua | ub


def _unpack_rows(ref, rows):
    lo, hi = [], []
    for j in range(ROW_SUB):
        u = ref[pl.ds(j, rows, stride=ROW_SUB), :]
        lo.append(pltpu.bitcast(u << 16, F32))
        hi.append(pltpu.bitcast(u & jnp.uint32(0xFFFF0000), F32))
    return lo + hi


def _proj_kernel(x_ref, w1_ref, wq_ref, wk_ref, wvt_ref, gq_ref, gkv_ref, cq_ref, sq_ref, ck_ref, sk_ref,
                 q_ref, k_ref, vt_ref, qs_ref, ks_ref, vs_ref):
    xb = x_ref[...].astype(BF16)
    p = _dot(xb, w1_ref[...])

    def rms(c, g):
        return c * lax.rsqrt(jnp.mean(c * c, axis=-1, keepdims=True) + RMS_EPS) * g

    cqn = rms(p[:, 0:256], gq_ref[...]).astype(BF16)
    ckvn = rms(p[:, 256:512], gkv_ref[...]).astype(BF16)
    qs_ref[...] = p[:, 512:1024].astype(BF16)
    ks_ref[...] = p[:, 1024:1536].astype(BF16)
    vs_ref[...] = p[:, 1536:1792].astype(BF16)
    kr = p[:, 1792:1920] * ck_ref[...] + p[:, 1920:2048] * sk_ref[...]
    qq = _dot(cqn, wq_ref[...])
    kn = _dot(ckvn, wk_ref[...])
    cq = cq_ref[...]
    sq = sq_ref[...]
    for h in range(MLA_HEADS):
        a = qq[:, h * LANES:(h + 1) * LANES]
        b = qq[:, 1024 + h * LANES:1024 + (h + 1) * LANES]
        q_ref[:, h * LANES:(h + 1) * LANES] = (a * cq + b * sq).astype(BF16)
        k_ref[:, h * LANES:(h + 1) * LANES] = (kn[:, h * LANES:(h + 1) * LANES] + kr).astype(BF16)
    vt_ref[...] = _dot_nt(wvt_ref[...], ckvn).astype(BF16)


def _proj(x2, w1, wq2, wk, wvt, gq, gkv, tabs, seq):
    T = x2.shape[0]
    tm = PROJ_TM
    nper = seq // tm
    full = lambda shape: pl.BlockSpec(shape, lambda i: (0, 0))
    tab = pl.BlockSpec((tm, LANES), lambda i: (i % nper, 0))
    row = lambda n: pl.BlockSpec((tm, n), lambda i: (i, 0))
    return pl.pallas_call(
        _proj_kernel,
        grid=(T // tm,),
        in_specs=[row(D_MODEL), full(w1.shape), full(wq2.shape), full(wk.shape), full(wvt.shape),
                  full(gq.shape), full(gkv.shape), tab, tab, tab, tab],
        out_specs=[row(1024), row(1024), pl.BlockSpec((512, tm), lambda i: (0, i)),
                   row(512), row(512), row(256)],
        out_shape=[jax.ShapeDtypeStruct((T, 1024), BF16), jax.ShapeDtypeStruct((T, 1024), BF16),
                   jax.ShapeDtypeStruct((512, T), BF16), jax.ShapeDtypeStruct((T, 512), BF16),
                   jax.ShapeDtypeStruct((T, 512), BF16), jax.ShapeDtypeStruct((T, 256), BF16)],
        compiler_params=pltpu.CompilerParams(dimension_semantics=("parallel",),
                                             vmem_limit_bytes=VMEM_LIMIT),
        name="proj",
    )(x2, w1, wq2, wk, wvt, gq, gkv, *tabs)


def _mla_kernel(q_ref, k_ref, vt_ref, o_ref, *acc_scr):
    tq = q_ref.shape[0]
    qi = pl.program_id(2)
    for acc in acc_scr:
        acc[...] = jnp.zeros(acc.shape, F32)

    def step(kc, stats, masked):
        ks = pl.multiple_of(kc * tq, tq)
        new_stats = []
        for h in range(MLA_HPS):
            m_prev, l_prev = stats[h]
            s = _dot_nt(k_ref[pl.ds(ks, tq), h * LANES:(h + 1) * LANES],
                        q_ref[:, h * LANES:(h + 1) * LANES])
            if masked:
                key = lax.broadcasted_iota(jnp.int32, s.shape, 0)
                qry = lax.broadcasted_iota(jnp.int32, s.shape, 1)
                s = jnp.where(key <= qry, s, NEG_INF)
            m_new = jnp.maximum(m_prev, jnp.max(s, axis=0, keepdims=True))
            alpha = jnp.exp2(m_prev - m_new)
            p = jnp.exp2(s - m_new)
            l_new = alpha * l_prev + jnp.sum(p, axis=0, keepdims=True)
            pv = _dot(vt_ref[h * MLA_V:(h + 1) * MLA_V, pl.ds(ks, tq)], p.astype(BF16))
            acc_scr[h][...] = acc_scr[h][...] * alpha + pv
            new_stats.append((m_new, l_new))
        return tuple(new_stats)

    init = tuple((jnp.full((1, tq), NEG_INF, F32), jnp.zeros((1, tq), F32)) for _ in range(MLA_HPS))
    stats = lax.fori_loop(0, qi, lambda kc, st: step(kc, st, False), init)
    stats = step(qi, stats, True)
    for h2 in range(MLA_HPS // 2):
        out_t = jnp.concatenate([acc_scr[2 * h2 + g][...] / stats[2 * h2 + g][1] for g in range(2)],
                                axis=0)
        o_ref[:, h2 * LANES:(h2 + 1) * LANES] = out_t.T.astype(BF16)


def _mla(q, k, vt, batch, seq):
    T = q.shape[0]
    tq = MLA_TQ
    nq = seq // tq
    hps = MLA_HPS
    return pl.pallas_call(
        _mla_kernel,
        grid=(batch, MLA_HEADS // hps, nq),
        in_specs=[pl.BlockSpec((tq, hps * LANES), lambda b, j, i: (b * nq + i, j)),
                  pl.BlockSpec((seq, hps * LANES), lambda b, j, i: (b, j)),
                  pl.BlockSpec((hps * MLA_V, seq), lambda b, j, i: (j, b))],
        out_specs=pl.BlockSpec((tq, hps * MLA_V), lambda b, j, i: (b * nq + i, j)),
        out_shape=jax.ShapeDtypeStruct((T, MLA_HEADS * MLA_V), BF16),
        scratch_shapes=[pltpu.VMEM((MLA_V, tq), F32) for _ in range(hps)],
        compiler_params=pltpu.CompilerParams(
            dimension_semantics=("parallel", "parallel", "arbitrary"), vmem_limit_bytes=VMEM_LIMIT),
        name="mla",
    )(q, k, vt)


def _swa_kernel(sink_ref, q_ref, kc_ref, kp_ref, vc_ref, vp_ref, o_ref):
    W = SWA_WINDOW
    n = pl.program_id(1)
    i = lax.broadcasted_iota(jnp.int32, (W, W), 0)
    j = lax.broadcasted_iota(jnp.int32, (W, W), 1)
    cur_ok = j <= i
    prev_ok = (j > i) & (n > 0)
    dist_c = (i - j).astype(F32)
    dist_p = (i + W - j).astype(F32)
    lane_lo = lax.broadcasted_iota(jnp.int32, (W, LANES), 1) < 64
    for pair in range(SWA_HEADS // 2):
        kvh = pair // 2
        qp = q_ref[:, pair * LANES:(pair + 1) * LANES]
        vc = vc_ref[:, kvh * LANES:(kvh + 1) * LANES]
        vp = vp_ref[:, kvh * LANES:(kvh + 1) * LANES]
        outs = []
        for g in range(2):
            head = 2 * pair + g
            slope = 2.0 ** (-8.0 * (head + 1) / SWA_HEADS)
            col = (2 * kvh + g) * LANES
            s_c = _dot_nt(qp, kc_ref[:, col:col + LANES])
            s_p = _dot_nt(qp, kp_ref[:, col:col + LANES])
            s_c = jnp.where(cur_ok, s_c - slope * dist_c, NEG_INF)
            s_p = jnp.where(prev_ok, s_p - slope * dist_p, NEG_INF)
            sink = sink_ref[head]
            m = jnp.maximum(jnp.maximum(jnp.max(s_c, axis=1, keepdims=True),
                                        jnp.max(s_p, axis=1, keepdims=True)), sink)
            p_c = jnp.exp(s_c - m)
            p_p = jnp.exp(s_p - m)
            denom = (jnp.sum(p_c, axis=1, keepdims=True) + jnp.sum(p_p, axis=1, keepdims=True)
                     + jnp.exp(sink - m))
            outs.append((_dot(p_c.astype(BF16), vc) + _dot(p_p.astype(BF16), vp)) / denom)
        o_ref[:, pair * LANES:(pair + 1) * LANES] = jnp.where(lane_lo, outs[0], outs[1]).astype(BF16)


def _swa(sinks, qs, ks, vs, batch, seq):
    T = qs.shape[0]
    W = SWA_WINDOW
    nb = seq // W
    cur = lambda n: pl.BlockSpec((W, n), lambda b, i: (b * nb + i, 0))
    prev = lambda n: pl.BlockSpec((W, n), lambda b, i: (b * nb + jnp.maximum(i - 1, 0), 0))
    return pl.pallas_call(
        _swa_kernel,
        grid=(batch, nb),
        in_specs=[pl.BlockSpec(memory_space=pltpu.SMEM), cur(512), cur(512), prev(512), cur(256), prev(256)],
        out_specs=cur(512),
        out_shape=jax.ShapeDtypeStruct((T, 512), BF16),
        compiler_params=pltpu.CompilerParams(dimension_semantics=("parallel", "parallel"),
                                             vmem_limit_bytes=VMEM_LIMIT),
        name="swa",
    )(sinks, qs, ks, ks, vs, vs)


def _post_kernel(x_ref, om_ref, os_ref, wom_ref, wos_ref, wg_ref, wout_ref, g1_ref, b1_ref,
                 wr_ref, rb_ref, tri_ref,
                 h1_ref, h1p_ref, e_ref, w_ref, r_ref, cnt_ref, carry_scr):
    tm = x_ref.shape[0]
    step = pl.program_id(0)

    @pl.when(step == 0)
    def _():
        carry_scr[...] = jnp.zeros(carry_scr.shape, F32)

    x = x_ref[...]
    ya = _dot(om_ref[...], wom_ref[...])
    yb = _dot(os_ref[...], wos_ref[...])
    gates = _dot(x.astype(BF16), wg_ref[...])
    merged = _sigmoid(gates[:, :D_MODEL]) * ya + _sigmoid(gates[:, D_MODEL:]) * yb
    mix = _dot(merged.astype(BF16), wout_ref[...])
    z = DEEPNORM_ALPHA * x + mix
    mu = jnp.mean(z, axis=-1, keepdims=True)
    zc = z - mu
    var = jnp.mean(zc * zc, axis=-1, keepdims=True)
    h1 = zc * lax.rsqrt(var + LN_EPS) * g1_ref[...] + b1_ref[...]
    h1_ref[...] = h1
    _pack_rows(h1, h1p_ref)

    scores = _sigmoid(_dot_nt(wr_ref[...], h1.astype(BF16)))
    choice = scores + rb_ref[...]
    row = lax.broadcasted_iota(jnp.int32, (N_EXPERTS, tm), 0)
    grow = lax.broadcasted_iota(jnp.int32, (GROUP_SIZE, tm), 0)
    gscore = []
    for g in range(N_GROUPS):
        blk = choice[g * GROUP_SIZE:(g + 1) * GROUP_SIZE, :]
        m1 = jnp.max(blk, axis=0, keepdims=True)
        i1 = jnp.min(jnp.where(blk == m1, grow, GROUP_SIZE), axis=0, keepdims=True)
        m2 = jnp.max(jnp.where(grow == i1, NEG_INF, blk), axis=0, keepdims=True)
        gscore.append(m1 + m2)
    gsc = jnp.concatenate(gscore, axis=0)
    gidx = lax.broadcasted_iota(jnp.int32, (N_GROUPS, tm), 0)
    grank = jnp.zeros((N_GROUPS, tm), jnp.int32)
    for g in range(N_GROUPS):
        sg = gsc[g:g + 1, :]
        beats = (sg > gsc) | ((sg == gsc) & (gidx > g))
        grank = grank + beats.astype(jnp.int32)
    gsel = (grank < TOPK_GROUPS).astype(F32)
    emask = jnp.concatenate(
        [jnp.broadcast_to(gsel[g:g + 1, :], (GROUP_SIZE, tm)) for g in range(N_GROUPS)], axis=0)
    work = jnp.where(emask > 0.0, choice, NEG_INF)
    sel = jnp.zeros((N_EXPERTS, tm), F32)
    idxs, svals = [], []
    for _k in range(TOP_K):
        m = jnp.max(work, axis=0, keepdims=True)
        idx = jnp.min(jnp.where(work == m, row, N_EXPERTS), axis=0, keepdims=True)
        hit = row == idx
        svals.append(jnp.sum(jnp.where(hit, scores, 0.0), axis=0, keepdims=True))
        work = jnp.where(hit, NEG_INF, work)
        sel = jnp.where(hit, 1.0, sel)
        idxs.append(idx)
    ssum = svals[0]
    for sv in svals[1:]:
        ssum = ssum + sv
    e_ref[...] = jnp.concatenate(idxs, axis=0)
    w_ref[...] = jnp.concatenate([sv / ssum * ROUTED_SCALE for sv in svals], axis=0)

    carry = carry_scr[...]
    rank = _dot(sel.astype(BF16), tri_ref[...]) + carry[:, 0:1]
    r_ref[...] = jnp.concatenate(
        [jnp.sum(jnp.where(row == idx, rank, 0.0), axis=0, keepdims=True) for idx in idxs],
        axis=0).astype(jnp.int32)
    carry = carry + jnp.sum(sel, axis=1, keepdims=True)
    carry_scr[...] = carry
    cnt_ref[...] = carry.astype(jnp.int32)


def _post(x2, o_mla, o_swa, wom, wos, wg, wout, g1, b1, wr_t, rbias, tri):
    T = x2.shape[0]
    tm = POST_TM
    full = lambda a: pl.BlockSpec(a.shape, lambda i: (0, 0))
    row = lambda n: pl.BlockSpec((tm, n), lambda i: (i, 0))
    col = pl.BlockSpec((TOP_K, tm), lambda i: (0, i))
    return pl.pallas_call(
        _post_kernel,
        grid=(T // tm,),
        in_specs=[row(D_MODEL), row(512), row(512), full(wom), full(wos), full(wg), full(wout),
                  full(g1), full(b1), full(wr_t), full(rbias), full(tri)],
        out_specs=[row(D_MODEL), pl.BlockSpec((tm * ROW_SUB, LANES), lambda i: (i, 0)), col, col, col,
                   pl.BlockSpec((N_EXPERTS, LANES), lambda i: (0, 0))],
        out_shape=[jax.ShapeDtypeStruct((T, D_MODEL), F32),
                   jax.ShapeDtypeStruct((T * ROW_SUB, LANES), jnp.uint32),
                   jax.ShapeDtypeStruct((TOP_K, T), jnp.int32),
                   jax.ShapeDtypeStruct((TOP_K, T), F32),
                   jax.ShapeDtypeStruct((TOP_K, T), jnp.int32),
                   jax.ShapeDtypeStruct((N_EXPERTS, LANES), jnp.int32)],
        scratch_shapes=[pltpu.VMEM((N_EXPERTS, LANES), F32)],
        compiler_params=pltpu.CompilerParams(dimension_semantics=("arbitrary",),
                                             vmem_limit_bytes=VMEM_LIMIT),
        name="post",
    )(x2, o_mla, o_swa, wom, wos, wg, wout, g1, b1, wr_t, rbias, tri)


def _dest_kernel(start_ref, e_ref, r_ref, o_ref):
    e = e_ref[...]
    base = jnp.zeros(e.shape, jnp.int32)
    for j in range(N_EXPERTS):
        base = jnp.where(e == j, start_ref[j], base)
    o_ref[...] = base + r_ref[...]


def _dest(seg_start, e_t, r_t):
    T = e_t.shape[1]
    tn = min(T, 4096)
    col = pl.BlockSpec((TOP_K, tn), lambda i: (0, i))
    return pl.pallas_call(
        _dest_kernel,
        grid=(T // tn,),
        in_specs=[pl.BlockSpec(memory_space=pltpu.SMEM), col, col],
        out_specs=col,
        out_shape=jax.ShapeDtypeStruct((TOP_K, T), jnp.int32),
        compiler_params=pltpu.CompilerParams(dimension_semantics=("parallel",)),
        name="dest",
    )(seg_start, e_t, r_t)


def _sc_worker_chunks(n_tokens):
    info = plsc.get_sparse_core_info()
    n_workers = info.num_cores * info.num_subcores
    per_worker = n_tokens // SC_CHUNK // n_workers
    assert per_worker * n_workers * SC_CHUNK == n_tokens
    first = (lax.axis_index("s") * info.num_cores + lax.axis_index("c")) * per_worker
    return first, per_worker


def _sc_dispatch(h1p, idx, n_rows):
    n_tokens = h1p.shape[0] // ROW_SUB
    R = SC_CHUNK * ROW_SUB

    def body(h_hbm, idx_hbm, xs_hbm, idx_v, rows_v, sem):
        first, per_worker = _sc_worker_chunks(n_tokens)

        @pl.loop(0, per_worker)
        def _(ci):
            c = first + ci
            pltpu.sync_copy(idx_hbm.at[c], idx_v)
            pltpu.sync_copy(h_hbm.at[pl.ds(c * R, R)], rows_v)
            copies = [pltpu.make_async_copy(rows_v, xs_hbm.at[idx_v.at[k]], sem) for k in range(TOP_K)]
            for cp in copies:
                cp.start()
            for cp in copies:
                cp.wait()

    run = pl.kernel(
        body, out_type=(),
        mesh=plsc.VectorSubcoreMesh(core_axis_name="c", subcore_axis_name="s"),
        scratch_types=[pltpu.VMEM((TOP_K, R), jnp.int32), pltpu.VMEM((R, LANES), jnp.uint32),
                       pltpu.SemaphoreType.DMA],
        name="sc_dispatch")
    xs_ref = jax.new_ref(jnp.zeros((n_rows * ROW_SUB, LANES), jnp.uint32))
    run(h1p, idx, xs_ref)
    return jax.freeze(xs_ref)


def _sc_gather(ys, idx, n_tokens):
    R = SC_CHUNK * ROW_SUB
    half = TOP_K // 2

    def body(ys_hbm, idx_hbm, yg_hbm, idx_v, buf, gsem, wsem):
        first, per_worker = _sc_worker_chunks(n_tokens)

        @pl.loop(0, per_worker)
        def _(ci):
            c = first + ci
            pltpu.sync_copy(idx_hbm.at[c], idx_v)
            for k0 in (0, half):
                gathers = [pltpu.make_async_copy(ys_hbm.at[idx_v.at[k0 + k]], buf.at[k], gsem)
                           for k in range(half)]
                for cp in gathers:
                    cp.start()
                for cp in gathers:
                    cp.wait()
                writes = [pltpu.make_async_copy(buf.at[k], yg_hbm.at[k0 + k, pl.ds(c * R, R)], wsem)
                          for k in range(half)]
                for cp in writes:
                    cp.start()
                for cp in writes:
                    cp.wait()

    run = pl.kernel(
        body, out_type=jax.ShapeDtypeStruct((TOP_K, n_tokens * ROW_SUB, LANES), jnp.uint32),
        mesh=plsc.VectorSubcoreMesh(core_axis_name="c", subcore_axis_name="s"),
        scratch_types=[pltpu.VMEM((TOP_K, R), jnp.int32), pltpu.VMEM((half, R, LANES), jnp.uint32),
                       pltpu.SemaphoreType.DMA, pltpu.SemaphoreType.DMA],
        name="sc_gather")
    return run(ys, idx)


def _experts_kernel(be_ref, nused_ref, xs_ref, wup_ref, wdn_ref, ys_ref, wup_bf, wdn_bf):
    i = pl.program_id(0)
    rows = xs_ref.shape[0] // ROW_SUB
    e = be_ref[i]
    e_prev = be_ref[jnp.maximum(i - 1, 0)]

    @pl.when((i == 0) | (e != e_prev))
    def _():
        wup_bf[...] = wup_ref[0].astype(BF16)
        wdn_bf[...] = wdn_ref[0].astype(BF16)

    @pl.when(i < nused_ref[0])
    def _():
        xb = jnp.concatenate([c.astype(BF16) for c in _unpack_rows(xs_ref, rows)], axis=1)
        gu = _dot(xb, wup_bf[...])
        g = gu[:, :EXPERT_HIDDEN]
        hid = g * _sigmoid(g) * gu[:, EXPERT_HIDDEN:]
        _pack_rows(_dot(hid.astype(BF16), wdn_bf[...]), ys_ref)

    @pl.when(i >= nused_ref[0])
    def _():
        ys_ref[...] = jnp.zeros(ys_ref.shape, jnp.uint32)


def _experts(blk_expert, n_used, xs, w_exp_up, w_exp_down):
    n_blocks = blk_expert.shape[0]
    rb = ROW_BLK * ROW_SUB
    grid_spec = pltpu.PrefetchScalarGridSpec(
        num_scalar_prefetch=2,
        grid=(n_blocks,),
        in_specs=[pl.BlockSpec((rb, LANES), lambda i, be, nu: (i, 0)),
                  pl.BlockSpec((1, D_MODEL, 2 * EXPERT_HIDDEN), lambda i, be, nu: (be[i], 0, 0)),
                  pl.BlockSpec((1, EXPERT_HIDDEN, D_MODEL), lambda i, be, nu: (be[i], 0, 0))],
        out_specs=pl.BlockSpec((rb, LANES), lambda i, be, nu: (i, 0)),
        scratch_shapes=[pltpu.VMEM((D_MODEL, 2 * EXPERT_HIDDEN), BF16),
                        pltpu.VMEM((EXPERT_HIDDEN, D_MODEL), BF16)],
    )
    return pl.pallas_call(
        _experts_kernel,
        grid_spec=grid_spec,
        out_shape=jax.ShapeDtypeStruct(xs.shape, jnp.uint32),
        compiler_params=pltpu.CompilerParams(dimension_semantics=("arbitrary",),
                                             vmem_limit_bytes=VMEM_LIMIT),
        name="experts",
    )(blk_expert, n_used, xs, w_exp_up, w_exp_down)


def _combine_kernel(w_ref, h1_ref, yg_ref, wsu_ref, wsd_ref, g2_ref, b2_ref, o_ref):
    tm = h1_ref.shape[0]
    h1 = h1_ref[...]
    gu = _dot(h1.astype(BF16), wsu_ref[...])
    g = gu[:, :SHARED_HIDDEN]
    hid = g * _sigmoid(g) * gu[:, SHARED_HIDDEN:]
    ffn = _dot(hid.astype(BF16), wsd_ref[...])

    w = w_ref[...]
    acc = [None] * (2 * ROW_SUB)
    for k in range(TOP_K):
        wk = w[:, k:k + 1]
        chunks = _unpack_rows(yg_ref.at[k], tm)
        for c in range(2 * ROW_SUB):
            acc[c] = wk * chunks[c] if acc[c] is None else acc[c] + wk * chunks[c]
    routed = jnp.concatenate(acc, axis=1)
    z = DEEPNORM_ALPHA * h1 + (routed + ffn)
    mu = jnp.mean(z, axis=-1, keepdims=True)
    zc = z - mu
    var = jnp.mean(zc * zc, axis=-1, keepdims=True)
    o_ref[...] = zc * lax.rsqrt(var + LN_EPS) * g2_ref[...] + b2_ref[...]


def _combine(w_tok, h1, yg, wsu, wsd, g2, b2):
    T = h1.shape[0]
    tm = COMBINE_TM
    full = lambda a: pl.BlockSpec(a.shape, lambda i: (0, 0))
    return pl.pallas_call(
        _combine_kernel,
        grid=(T // tm,),
        in_specs=[pl.BlockSpec((tm, TOP_K), lambda i: (i, 0)),
                  pl.BlockSpec((tm, D_MODEL), lambda i: (i, 0)),
                  pl.BlockSpec((TOP_K, tm * ROW_SUB, LANES), lambda i: (0, i, 0)),
                  full(wsu), full(wsd), full(g2), full(b2)],
        out_specs=pl.BlockSpec((tm, D_MODEL), lambda i: (i, 0)),
        out_shape=jax.ShapeDtypeStruct((T, D_MODEL), F32),
        compiler_params=pltpu.CompilerParams(dimension_semantics=("parallel",),
                                             vmem_limit_bytes=VMEM_LIMIT),
        name="combine",
    )(w_tok, h1, yg, wsu, wsd, g2, b2)


def _prep_weights(w_in, w_uq, w_ukv, seq):
    z = lambda n: jnp.zeros((D_MODEL, n), F32)
    kr = w_in[:, 512:544]
    qs = w_in[:, 544:1056] * (1.0 / math.sqrt(SWA_HEAD_DIM))
    ks0, ks1 = w_in[:, 1056:1120], w_in[:, 1120:1184]
    vs0, vs1 = w_in[:, 1184:1248], w_in[:, 1248:1312]
    half = MLA_ROPE // 2
    w1 = jnp.concatenate([
        w_in[:, 0:512], qs,
        ks0, z(64), z(64), ks0, ks1, z(64), z(64), ks1,
        vs0, vs0, vs1, vs1,
        z(64), kr, z(32),
        z(64), -kr[:, half:], kr[:, :half], z(32)], axis=1).astype(BF16)
    wg = w_in[:, 1312:3360].astype(BF16)

    zq = lambda n: jnp.zeros((MLA_Q_LORA, n), F32)
    qd = MLA_NOPE + MLA_ROPE
    q_cols, r_cols, k_cols, v_cols = [], [], [], []
    for h in range(MLA_HEADS):
        wq = w_uq[:, h * qd:(h + 1) * qd]
        q_cols += [wq, zq(32)]
        r_cols += [zq(64), -wq[:, MLA_NOPE + half:], wq[:, MLA_NOPE:MLA_NOPE + half], zq(32)]
        wk = w_ukv[:, h * 128:(h + 1) * 128]
        k_cols += [wk[:, :MLA_NOPE], zq(64)]
        v_cols += [wk[:, MLA_NOPE:]]
    wq2 = jnp.concatenate(q_cols + r_cols, axis=1).astype(BF16)
    wk = jnp.concatenate(k_cols, axis=1).astype(BF16)
    wvt = jnp.concatenate(v_cols, axis=1).T.astype(BF16)

    inv_freq = ROPE_THETA ** (-jnp.arange(0, MLA_ROPE, 2, dtype=F32) / MLA_ROPE)
    ang = jnp.arange(seq, dtype=F32)[:, None] * inv_freq[None, :]
    cos, sin = jnp.cos(ang), jnp.sin(ang)
    one, zero = jnp.ones((seq, 64), F32), jnp.zeros((seq, 64), F32)
    z32 = jnp.zeros((seq, 32), F32)
    scale = math.log2(math.e) / math.sqrt(MLA_NOPE + MLA_ROPE)
    tabs = (jnp.concatenate([one, cos, cos, z32], axis=1) * scale,
            jnp.concatenate([zero, sin, sin, z32], axis=1) * scale,
            jnp.concatenate([zero, cos, cos, z32], axis=1),
            jnp.concatenate([zero, sin, sin, z32], axis=1))
    return w1, wg, wq2, wk, wvt, tabs


def kernel(x, w_in, mla_q_norm, mla_kv_norm, w_uq, w_ukv, attn_sinks, w_o_mla, w_o_swa, w_out,
           ln1_g, ln1_b, w_router, router_bias, w_exp_up, w_exp_down, w_sh_up, w_sh_down,
           ln2_g, ln2_b):
    batch, seq, _ = x.shape
    T = batch * seq
    x2 = x.reshape(T, D_MODEL)
    w1, wg, wq2, wk, wvt, tabs = _prep_weights(w_in, w_uq, w_ukv, seq)
    q, k, vt, qs, ks, vs = _proj(x2, w1, wq2, wk, wvt, mla_q_norm.reshape(1, -1),
                                 mla_kv_norm.reshape(1, -1), tabs, seq)
    o_mla = _mla(q, k, vt, batch, seq)
    o_swa = _swa(attn_sinks.astype(F32), qs, ks, vs, batch, seq)

    tri = (lax.broadcasted_iota(jnp.int32, (POST_TM, POST_TM), 0)
           < lax.broadcasted_iota(jnp.int32, (POST_TM, POST_TM), 1)).astype(BF16)
    h1, h1p, e_t, w_t, r_t, cnt = _post(
        x2, o_mla, o_swa, w_o_mla.astype(BF16), w_o_swa.astype(BF16), wg, w_out.astype(BF16),
        ln1_g.reshape(1, -1), ln1_b.reshape(1, -1), w_router.T.astype(BF16),
        router_bias.reshape(-1, 1).astype(F32), tri)

    counts = cnt[:, 0]
    pad = (counts + ROW_BLK - 1) // ROW_BLK * ROW_BLK
    pad_end = jnp.cumsum(pad)
    pad_start = pad_end - pad
    n_rows = (T * TOP_K // ROW_BLK + N_EXPERTS) * ROW_BLK
    n_blocks = n_rows // ROW_BLK
    dest = _dest(pad_start.astype(jnp.int32), e_t, r_t)
    idx = (dest.reshape(TOP_K, T // SC_CHUNK, SC_CHUNK, 1) * ROW_SUB
           + jnp.arange(ROW_SUB, dtype=jnp.int32))
    idx = idx.transpose(1, 0, 2, 3).reshape(T // SC_CHUNK, TOP_K, SC_CHUNK * ROW_SUB)
    blk_expert = jnp.clip(jnp.searchsorted(pad_end, jnp.arange(n_blocks, dtype=jnp.int32) * ROW_BLK,
                                           side="right"), 0, N_EXPERTS - 1).astype(jnp.int32)
    n_used = (pad_end[-1:] // ROW_BLK).astype(jnp.int32)

    xs = _sc_dispatch(h1p, idx, n_rows)
    ys = _experts(blk_expert, n_used, xs, w_exp_up, w_exp_down)
    yg = _sc_gather(ys, idx, T)
    out = _combine(w_t.T, h1, yg, w_sh_up.astype(BF16), w_sh_down.astype(BF16),
                   ln2_g.reshape(1, -1), ln2_b.reshape(1, -1))
    return out.reshape(batch, seq, D_MODEL)
```

```python
import math

import jax
import jax.numpy as jnp
from jax import lax
from jax.experimental import pallas as pl
from jax.experimental.pallas import tpu as pltpu
from jax.experimental.pallas import tpu_sc as plsc

D_MODEL = 1024
MLA_HEADS = 8
MLA_Q_LORA = 256
MLA_KV_LORA = 256
MLA_NOPE = 64
MLA_ROPE = 32
MLA_V = 64
ROPE_THETA = 10000.0
SWA_HEADS = 8
SWA_KV_HEADS = 2
SWA_HEAD_DIM = 64
SWA_WINDOW = 128
N_EXPERTS = 256
TOP_K = 8
N_GROUPS = 8
GROUP_SIZE = N_EXPERTS // N_GROUPS
TOPK_GROUPS = 4
EXPERT_HIDDEN = 256
SHARED_HIDDEN = 256
ROUTED_SCALE = 2.5
DEEPNORM_ALPHA = 2.0 ** 0.25
LN_EPS = 1e-5
RMS_EPS = 1e-6

LANES = 128
ROW_WORDS = D_MODEL // 2
ROW_SUB = ROW_WORDS // LANES
VMEM_LIMIT = 48 * 1024 * 1024

PROJ_TM = 512
MLA_TQ = 256
MLA_HPS = 4
POST_TM = 256
ROW_BLK = 256
COMBINE_TM = 256
SC_CHUNK = 32

BF16 = jnp.bfloat16
F32 = jnp.float32
NEG_INF = float("-inf")


def _sigmoid(v):
    return 1.0 / (1.0 + jnp.exp(-v))


def _dot(a, b):
    return jnp.dot(a, b, preferred_element_type=F32)


def _dot_nt(a, b):
    return lax.dot_general(a, b, (((1,), (1,)), ((), ())), preferred_element_type=F32)


def _pack_rows(y, out_ref):
    rows = y.shape[0]
    for j in range(ROW_SUB):
        a = y[:, j * LANES:(j + 1) * LANES].astype(BF16).astype(F32)
        b = y[:, ROW_WORDS + j * LANES:ROW_WORDS + (j + 1) * LANES].astype(BF16).astype(F32)
        ua = pltpu.bitcast(a, jnp.uint32) >> 16
        ub = pltpu.bitcast(b, jnp.uint32)
        out_ref[pl.ds(j, rows, stride=ROW_SUB), :] = ua | ub


def _unpack_rows(ref, rows):
    lo, hi = [], []
    for j in range(ROW_SUB):
        u = ref[pl.ds(j, rows, stride=ROW_SUB), :]
        lo.append(pltpu.bitcast(u << 16, F32))
        hi.append(pltpu.bitcast(u & jnp.uint32(0xFFFF0000), F32))
    return lo + hi


def _proj_kernel(x_ref, w1_ref, wq_ref, wk_ref, wvt_ref, gq_ref, gkv_ref, cq_ref, sq_ref, ck_ref, sk_ref,
                 q_ref, k_ref, vt_ref, qs_ref, ks_ref, vs_ref):
    xb = x_ref[...].astype(BF16)
    p = _dot(xb, w1_ref[...])

    def rms(c, g):
        return c * lax.rsqrt(jnp.mean(c * c, axis=-1, keepdims=True) + RMS_EPS) * g

    cqn = rms(p[:, 0:256], gq_ref[...]).astype(BF16)
    ckvn = rms(p[:, 256:512], gkv_ref[...]).astype(BF16)
    qs_ref[...] = p[:, 512:1024].astype(BF16)
    ks_ref[...] = p[:, 1024:1536].astype(BF16)
    vs_ref[...] = p[:, 1536:1792].astype(BF16)
    kr = p[:, 1792:1920] * ck_ref[...] + p[:, 1920:2048] * sk_ref[...]
    qq = _dot(cqn, wq_ref[...])
    kn = _dot(ckvn, wk_ref[...])
    cq = cq_ref[...]
    sq = sq_ref[...]
    for h in range(MLA_HEADS):
        a = qq[:, h * LANES:(h + 1) * LANES]
        b = qq[:, 1024 + h * LANES:1024 + (h + 1) * LANES]
        q_ref[:, h * LANES:(h + 1) * LANES] = (a * cq + b * sq).astype(BF16)
        k_ref[:, h * LANES:(h + 1) * LANES] = (kn[:, h * LANES:(h + 1) * LANES] + kr).astype(BF16)
    vt_ref[...] = _dot_nt(wvt_ref[...], ckvn).astype(BF16)


def _proj(x2, w1, wq2, wk, wvt, gq, gkv, tabs, seq):
    T = x2.shape[0]
    tm = PROJ_TM
    nper = seq // tm
    full = lambda shape: pl.BlockSpec(shape, lambda i: (0, 0))
    tab = pl.BlockSpec((tm, LANES), lambda i: (i % nper, 0))
    row = lambda n: pl.BlockSpec((tm, n), lambda i: (i, 0))
    return pl.pallas_call(
        _proj_kernel,
        grid=(T // tm,),
        in_specs=[row(D_MODEL), full(w1.shape), full(wq2.shape), full(wk.shape), full(wvt.shape),
                  full(gq.shape), full(gkv.shape), tab, tab, tab, tab],
        out_specs=[row(1024), row(1024), pl.BlockSpec((512, tm), lambda i: (0, i)),
                   row(512), row(512), row(256)],
        out_shape=[jax.ShapeDtypeStruct((T, 1024), BF16), jax.ShapeDtypeStruct((T, 1024), BF16),
                   jax.ShapeDtypeStruct((512, T), BF16), jax.ShapeDtypeStruct((T, 512), BF16),
                   jax.ShapeDtypeStruct((T, 512), BF16), jax.ShapeDtypeStruct((T, 256), BF16)],
        compiler_params=pltpu.CompilerParams(dimension_semantics=("parallel",),
                                             vmem_limit_bytes=VMEM_LIMIT),
        name="proj",
    )(x2, w1, wq2, wk, wvt, gq, gkv, *tabs)


def _mla_kernel(q_ref, k_ref, vt_ref, o_ref, *acc_scr):
    tq = q_ref.shape[0]
    qi = pl.program_id(2)
    for acc in acc_scr:
        acc[...] = jnp.zeros(acc.shape, F32)

    def step(kc, stats, masked):
        ks = pl.multiple_of(kc * tq, tq)
        new_stats = []
        for h in range(MLA_HPS):
            m_prev, l_prev = stats[h]
            s = _dot_nt(k_ref[pl.ds(ks, tq), h * LANES:(h + 1) * LANES],
                        q_ref[:, h * LANES:(h + 1) * LANES])
            if masked:
                key = lax.broadcasted_iota(jnp.int32, s.shape, 0)
                qry = lax.broadcasted_iota(jnp.int32, s.shape, 1)
                s = jnp.where(key <= qry, s, NEG_INF)
            m_new = jnp.maximum(m_prev, jnp.max(s, axis=0, keepdims=True))
            alpha = jnp.exp2(m_prev - m_new)
            p = jnp.exp2(s - m_new)
            l_new = alpha * l_prev + jnp.sum(p, axis=0, keepdims=True)
            pv = _dot(vt_ref[h * MLA_V:(h + 1) * MLA_V, pl.ds(ks, tq)], p.astype(BF16))
            acc_scr[h][...] = acc_scr[h][...] * alpha + pv
            new_stats.append((m_new, l_new))
        return tuple(new_stats)

    init = tuple((jnp.full((1, tq), NEG_INF, F32), jnp.zeros((1, tq), F32)) for _ in range(MLA_HPS))
    stats = lax.fori_loop(0, qi, lambda kc, st: step(kc, st, False), init)
    stats = step(qi, stats, True)
    for h2 in range(MLA_HPS // 2):
        out_t = jnp.concatenate([acc_scr[2 * h2 + g][...] / stats[2 * h2 + g][1] for g in range(2)],
                                axis=0)
        o_ref[:, h2 * LANES:(h2 + 1) * LANES] = out_t.T.astype(BF16)


def _mla(q, k, vt, batch, seq):
    T = q.shape[0]
    tq = MLA_TQ
    nq = seq // tq
    hps = MLA_HPS
    return pl.pallas_call(
        _mla_kernel,
        grid=(batch, MLA_HEADS // hps, nq),
        in_specs=[pl.BlockSpec((tq, hps * LANES), lambda b, j, i: (b * nq + i, j)),
                  pl.BlockSpec((seq, hps * LANES), lambda b, j, i: (b, j)),
                  pl.BlockSpec((hps * MLA_V, seq), lambda b, j, i: (j, b))],
        out_specs=pl.BlockSpec((tq, hps * MLA_V), lambda b, j, i: (b * nq + i, j)),
        out_shape=jax.ShapeDtypeStruct((T, MLA_HEADS * MLA_V), BF16),
        scratch_shapes=[pltpu.VMEM((MLA_V, tq), F32) for _ in range(hps)],
        compiler_params=pltpu.CompilerParams(
            dimension_semantics=("parallel", "parallel", "arbitrary"), vmem_limit_bytes=VMEM_LIMIT),
        name="mla",
    )(q, k, vt)


def _swa_kernel(sink_ref, q_ref, kc_ref, kp_ref, vc_ref, vp_ref, o_ref):
    W = SWA_WINDOW
    n = pl.program_id(1)
    i = lax.broadcasted_iota(jnp.int32, (W, W), 0)
    j = lax.broadcasted_iota(jnp.int32, (W, W), 1)
    cur_ok = j <= i
    prev_ok = (j > i) & (n > 0)
    dist_c = (i - j).astype(F32)
    dist_p = (i + W - j).astype(F32)
    lane_lo = lax.broadcasted_iota(jnp.int32, (W, LANES), 1) < 64
    for pair in range(SWA_HEADS // 2):
        kvh = pair // 2
        qp = q_ref[:, pair * LANES:(pair + 1) * LANES]
        vc = vc_ref[:, kvh * LANES:(kvh + 1) * LANES]
        vp = vp_ref[:, kvh * LANES:(kvh + 1) * LANES]
        outs = []
        for g in range(2):
            head = 2 * pair + g
            slope = 2.0 ** (-8.0 * (head + 1) / SWA_HEADS)
            col = (2 * kvh + g) * LANES
            s_c = _dot_nt(qp, kc_ref[:, col:col + LANES])
            s_p = _dot_nt(qp, kp_ref[:, col:col + LANES])
            s_c = jnp.where(cur_ok, s_c - slope * dist_c, NEG_INF)
            s_p = jnp.where(prev_ok, s_p - slope * dist_p, NEG_INF)
            sink = sink_ref[head]
            m = jnp.maximum(jnp.maximum(jnp.max(s_c, axis=1, keepdims=True),
                                        jnp.max(s_p, axis=1, keepdims=True)), sink)
            p_c = jnp.exp(s_c - m)
            p_p = jnp.exp(s_p - m)
            denom = (jnp.sum(p_c, axis=1, keepdims=True) + jnp.sum(p_p, axis=1, keepdims=True)
                     + jnp.exp(sink - m))
            outs.append((_dot(p_c.astype(BF16), vc) + _dot(p_p.astype(BF16), vp)) / denom)
        o_ref[:, pair * LANES:(pair + 1) * LANES] = jnp.where(lane_lo, outs[0], outs[1]).astype(BF16)


def _swa(sinks, qs, ks, vs, batch, seq):
    T = qs.shape[0]
    W = SWA_WINDOW
    nb = seq // W
    cur = lambda n: pl.BlockSpec((W, n), lambda b, i: (b * nb + i, 0))
    prev = lambda n: pl.BlockSpec((W, n), lambda b, i: (b * nb + jnp.maximum(i - 1, 0), 0))
    return pl.pallas_call(
        _swa_kernel,
        grid=(batch, nb),
        in_specs=[pl.BlockSpec(memory_space=pltpu.SMEM), cur(512), cur(512), prev(512), cur(256), prev(256)],
        out_specs=cur(512),
        out_shape=jax.ShapeDtypeStruct((T, 512), BF16),
        compiler_params=pltpu.CompilerParams(dimension_semantics=("parallel", "parallel"),
                                             vmem_limit_bytes=VMEM_LIMIT),
        name="swa",
    )(sinks, qs, ks, ks, vs, vs)


def _post_kernel(x_ref, om_ref, os_ref, wom_ref, wos_ref, wg_ref, wout_ref, g1_ref, b1_ref,
                 wr_ref, rb_ref, tri_ref,
                 h1_ref, h1p_ref, e_ref, w_ref, r_ref, cnt_ref, carry_scr):
    tm = x_ref.shape[0]
    step = pl.program_id(0)

    @pl.when(step == 0)
    def _():
        carry_scr[...] = jnp.zeros(carry_scr.shape, F32)

    x = x_ref[...]
    ya = _dot(om_ref[...], wom_ref[...])
    yb = _dot(os_ref[...], wos_ref[...])
    gates = _dot(x.astype(BF16), wg_ref[...])
    merged = _sigmoid(gates[:, :D_MODEL]) * ya + _sigmoid(gates[:, D_MODEL:]) * yb
    mix = _dot(merged.astype(BF16), wout_ref[...])
    z = DEEPNORM_ALPHA * x + mix
    mu = jnp.mean(z, axis=-1, keepdims=True)
    zc = z - mu
    var = jnp.mean(zc * zc, axis=-1, keepdims=True)
    h1 = zc * lax.rsqrt(var + LN_EPS) * g1_ref[...] + b1_ref[...]
    h1_ref[...] = h1
    _pack_rows(h1, h1p_ref)

    scores = _sigmoid(_dot_nt(wr_ref[...], h1.astype(BF16)))
    choice = scores + rb_ref[...]
    row = lax.broadcasted_iota(jnp.int32, (N_EXPERTS, tm), 0)
    grow = lax.broadcasted_iota(jnp.int32, (GROUP_SIZE, tm), 0)
    gscore = []
    for g in range(N_GROUPS):
        blk = choice[g * GROUP_SIZE:(g + 1) * GROUP_SIZE, :]
        m1 = jnp.max(blk, axis=0, keepdims=True)
        i1 = jnp.min(jnp.where(blk == m1, grow, GROUP_SIZE), axis=0, keepdims=True)
        m2 = jnp.max(jnp.where(grow == i1, NEG_INF, blk), axis=0, keepdims=True)
        gscore.append(m1 + m2)
    gsc = jnp.concatenate(gscore, axis=0)
    gidx = lax.broadcasted_iota(jnp.int32, (N_GROUPS, tm), 0)
    grank = jnp.zeros((N_GROUPS, tm), jnp.int32)
    for g in range(N_GROUPS):
        sg = gsc[g:g + 1, :]
        beats = (sg > gsc) | ((sg == gsc) & (gidx > g))
        grank = grank + beats.astype(jnp.int32)
    gsel = (grank < TOPK_GROUPS).astype(F32)
    emask = jnp.concatenate(
        [jnp.broadcast_to(gsel[g:g + 1, :], (GROUP_SIZE, tm)) for g in range(N_GROUPS)], axis=0)
    work = jnp.where(emask > 0.0, choice, NEG_INF)
    sel = jnp.zeros((N_EXPERTS, tm), F32)
    idxs, svals = [], []
    for _k in range(TOP_K):
        m = jnp.max(work, axis=0, keepdims=True)
        idx = jnp.min(jnp.where(work == m, row, N_EXPERTS), axis=0, keepdims=True)
        hit = row == idx
        svals.append(jnp.sum(jnp.where(hit, scores, 0.0), axis=0, keepdims=True))
        work = jnp.where(hit, NEG_INF, work)
        sel = jnp.where(hit, 1.0, sel)
        idxs.append(idx)
    ssum = svals[0]
    for sv in svals[1:]:
        ssum = ssum + sv
    e_ref[...] = jnp.concatenate(idxs, axis=0)
    w_ref[...] = jnp.concatenate([sv / ssum * ROUTED_SCALE for sv in svals], axis=0)

    carry = carry_scr[...]
    rank = _dot(sel.astype(BF16), tri_ref[...]) + carry[:, 0:1]
    r_ref[...] = jnp.concatenate(
        [jnp.sum(jnp.where(row == idx, rank, 0.0), axis=0, keepdims=True) for idx in idxs],
        axis=0).astype(jnp.int32)
    carry = carry + jnp.sum(sel, axis=1, keepdims=True)
    carry_scr[...] = carry
    cnt_ref[...] = carry.astype(jnp.int32)


def _post(x2, o_mla, o_swa, wom, wos, wg, wout, g1, b1, wr_t, rbias, tri):
    T = x2.shape[0]
    tm = POST_TM
    full = lambda a: pl.BlockSpec(a.shape, lambda i: (0, 0))
    row = lambda n: pl.BlockSpec((tm, n), lambda i: (i, 0))
    col = pl.BlockSpec((TOP_K, tm), lambda i: (0, i))
    return pl.pallas_call(
        _post_kernel,
        grid=(T // tm,),
        in_specs=[row(D_MODEL), row(512), row(512), full(wom), full(wos), full(wg), full(wout),
                  full(g1), full(b1), full(wr_t), full(rbias), full(tri)],
        out_specs=[row(D_MODEL), pl.BlockSpec((tm * ROW_SUB, LANES), lambda i: (i, 0)), col, col, col,
                   pl.BlockSpec((N_EXPERTS, LANES), lambda i: (0, 0))],
        out_shape=[jax.ShapeDtypeStruct((T, D_MODEL), F32),
                   jax.ShapeDtypeStruct((T * ROW_SUB, LANES), jnp.uint32),
                   jax.ShapeDtypeStruct((TOP_K, T), jnp.int32),
                   jax.ShapeDtypeStruct((TOP_K, T), F32),
                   jax.ShapeDtypeStruct((TOP_K, T), jnp.int32),
                   jax.ShapeDtypeStruct((N_EXPERTS, LANES), jnp.int32)],
        scratch_shapes=[pltpu.VMEM((N_EXPERTS, LANES), F32)],
        compiler_params=pltpu.CompilerParams(dimension_semantics=("arbitrary",),
                                             vmem_limit_bytes=VMEM_LIMIT),
        name="post",
    )(x2, o_mla, o_swa, wom, wos, wg, wout, g1, b1, wr_t, rbias, tri)


def _dest_kernel(start_ref, e_ref, r_ref, o_ref):
    e = e_ref[...]
    base = jnp.zeros(e.shape, jnp.int32)
    for j in range(N_EXPERTS):
        base = jnp.where(e == j, start_ref[j], base)
    o_ref[...] = base + r_ref[...]


def _dest(seg_start, e_t, r_t):
    T = e_t.shape[1]
    tn = min(T, 4096)
    col = pl.BlockSpec((TOP_K, tn), lambda i: (0, i))
    return pl.pallas_call(
        _dest_kernel,
        grid=(T // tn,),
        in_specs=[pl.BlockSpec(memory_space=pltpu.SMEM), col, col],
        out_specs=col,
        out_shape=jax.ShapeDtypeStruct((TOP_K, T), jnp.int32),
        compiler_params=pltpu.CompilerParams(dimension_semantics=("parallel",)),
        name="dest",
    )(seg_start, e_t, r_t)


def _sc_worker_chunks(n_tokens):
    info = plsc.get_sparse_core_info()
    n_workers = info.num_cores * info.num_subcores
    per_worker = n_tokens // SC_CHUNK // n_workers
    assert per_worker * n_workers * SC_CHUNK == n_tokens
    first = (lax.axis_index("s") * info.num_cores + lax.axis_index("c")) * per_worker
    return first, per_worker


def _sc_dispatch(h1p, idx, n_rows):
    n_tokens = h1p.shape[0] // ROW_SUB
    R = SC_CHUNK * ROW_SUB

    def body(h_hbm, idx_hbm, xs_hbm, idx_v, rows_v, sem):
        first, per_worker = _sc_worker_chunks(n_tokens)

        @pl.loop(0, per_worker)
        def _(ci):
            c = first + ci
            pltpu.sync_copy(idx_hbm.at[c], idx_v)
            pltpu.sync_copy(h_hbm.at[pl.ds(c * R, R)], rows_v)
            copies = [pltpu.make_async_copy(rows_v, xs_hbm.at[idx_v.at[k]], sem) for k in range(TOP_K)]
            for cp in copies:
                cp.start()
            for cp in copies:
                cp.wait()

    run = pl.kernel(
        body, out_type=(),
        mesh=plsc.VectorSubcoreMesh(core_axis_name="c", subcore_axis_name="s"),
        scratch_types=[pltpu.VMEM((TOP_K, R), jnp.int32), pltpu.VMEM((R, LANES), jnp.uint32),
                       pltpu.SemaphoreType.DMA],
        name="sc_dispatch")
    xs_ref = jax.new_ref(jnp.zeros((n_rows * ROW_SUB, LANES), jnp.uint32))
    run(h1p, idx, xs_ref)
    return jax.freeze(xs_ref)


def _sc_gather(ys, idx, n_tokens):
    R = SC_CHUNK * ROW_SUB
    half = TOP_K // 2

    def body(ys_hbm, idx_hbm, yg_hbm, idx_v, buf, gsem, wsem):
        first, per_worker = _sc_worker_chunks(n_tokens)

        @pl.loop(0, per_worker)
        def _(ci):
            c = first + ci
            pltpu.sync_copy(idx_hbm.at[c], idx_v)
            for k0 in (0, half):
                gathers = [pltpu.make_async_copy(ys_hbm.at[idx_v.at[k0 + k]], buf.at[k], gsem)
                           for k in range(half)]
                for cp in gathers:
                    cp.start()
                for cp in gathers:
                    cp.wait()
                writes = [pltpu.make_async_copy(buf.at[k], yg_hbm.at[k0 + k, pl.ds(c * R, R)], wsem)
                          for k in range(half)]
                for cp in writes:
                    cp.start()
                for cp in writes:
                    cp.wait()

    run = pl.kernel(
        body, out_type=jax.ShapeDtypeStruct((TOP_K, n_tokens * ROW_SUB, LANES), jnp.uint32),
        mesh=plsc.VectorSubcoreMesh(core_axis_name="c", subcore_axis_name="s"),
        scratch_types=[pltpu.VMEM((TOP_K, R), jnp.int32), pltpu.VMEM((half, R, LANES), jnp.uint32),
                       pltpu.SemaphoreType.DMA, pltpu.SemaphoreType.DMA],
        name="sc_gather")
    return run(ys, idx)


def _experts_kernel(first_ref, nblk_ref, xs_hbm, wup_ref, wdn_ref, ys_hbm,
                    wup_bf, wdn_bf, xbuf, ybuf, in_sem, out_sem):
    e = pl.program_id(0)
    n = nblk_ref[e]
    b0 = first_ref[e]
    rb = ROW_BLK * ROW_SUB

    def rows_of(j):
        return pl.ds(pl.multiple_of((b0 + j) * rb, rb), rb)

    def in_copy(j, slot):
        return pltpu.make_async_copy(xs_hbm.at[rows_of(j)], xbuf.at[slot], in_sem.at[slot])

    def out_copy(j, slot):
        return pltpu.make_async_copy(ybuf.at[slot], ys_hbm.at[rows_of(j)], out_sem.at[slot])

    @pl.when(n > 0)
    def _():
        in_copy(0, 0).start()
        wup_bf[...] = wup_ref[0].astype(BF16)
        wdn_bf[...] = wdn_ref[0].astype(BF16)

        def body(j, carry):
            slot = lax.rem(j, 2)
            in_copy(j, slot).wait()

            @pl.when(j + 1 < n)
            def _():
                in_copy(j + 1, 1 - slot).start()

            @pl.when(j >= 2)
            def _():
                out_copy(j - 2, slot).wait()

            xb = jnp.concatenate([c.astype(BF16) for c in _unpack_rows(xbuf.at[slot], ROW_BLK)], axis=1)
            gu = _dot(xb, wup_bf[...])
            g = gu[:, :EXPERT_HIDDEN]
            hid = g * _sigmoid(g) * gu[:, EXPERT_HIDDEN:]
            _pack_rows(_dot(hid.astype(BF16), wdn_bf[...]), ybuf.at[slot])
            out_copy(j, slot).start()
            return carry

        lax.fori_loop(0, n, body, 0)

        @pl.when(n >= 2)
        def _():
            out_copy(n - 2, lax.rem(n, 2)).wait()

        out_copy(n - 1, lax.rem(n - 1, 2)).wait()


def _experts(first_blk, n_blk, xs, w_exp_up, w_exp_down):
    rb = ROW_BLK * ROW_SUB
    grid_spec = pltpu.PrefetchScalarGridSpec(
        num_scalar_prefetch=2,
        grid=(N_EXPERTS,),
        in_specs=[pl.BlockSpec(memory_space=pl.ANY),
                  pl.BlockSpec((1, D_MODEL, 2 * EXPERT_HIDDEN), lambda e, fb, nb: (e, 0, 0)),
                  pl.BlockSpec((1, EXPERT_HIDDEN, D_MODEL), lambda e, fb, nb: (e, 0, 0))],
        out_specs=pl.BlockSpec(memory_space=pl.ANY),
        scratch_shapes=[pltpu.VMEM((D_MODEL, 2 * EXPERT_HIDDEN), BF16),
                        pltpu.VMEM((EXPERT_HIDDEN, D_MODEL), BF16),
                        pltpu.VMEM((2, rb, LANES), jnp.uint32),
                        pltpu.VMEM((2, rb, LANES), jnp.uint32),
                        pltpu.SemaphoreType.DMA((2,)), pltpu.SemaphoreType.DMA((2,))],
    )
    return pl.pallas_call(
        _experts_kernel,
        grid_spec=grid_spec,
        out_shape=jax.ShapeDtypeStruct(xs.shape, jnp.uint32),
        compiler_params=pltpu.CompilerParams(dimension_semantics=("arbitrary",),
                                             vmem_limit_bytes=VMEM_LIMIT),
        name="experts",
    )(first_blk, n_blk, xs, w_exp_up, w_exp_down)


def _combine_kernel(w_ref, h1_ref, yg_ref, wsu_ref, wsd_ref, g2_ref, b2_ref, o_ref):
    tm = h1_ref.shape[0]
    h1 = h1_ref[...]
    gu = _dot(h1.astype(BF16), wsu_ref[...])
    g = gu[:, :SHARED_HIDDEN]
    hid = g * _sigmoid(g) * gu[:, SHARED_HIDDEN:]
    ffn = _dot(hid.astype(BF16), wsd_ref[...])

    w = w_ref[...]
    acc = [None] * (2 * ROW_SUB)
    for k in range(TOP_K):
        wk = w[:, k:k + 1]
        chunks = _unpack_rows(yg_ref.at[k], tm)
        for c in range(2 * ROW_SUB):
            acc[c] = wk * chunks[c] if acc[c] is None else acc[c] + wk * chunks[c]
    routed = jnp.concatenate(acc, axis=1)
    z = DEEPNORM_ALPHA * h1 + (routed + ffn)
    mu = jnp.mean(z, axis=-1, keepdims=True)
    zc = z - mu
    var = jnp.mean(zc * zc, axis=-1, keepdims=True)
    o_ref[...] = zc * lax.rsqrt(var + LN_EPS) * g2_ref[...] + b2_ref[...]


def _combine(w_tok, h1, yg, wsu, wsd, g2, b2):
    T = h1.shape[0]
    tm = COMBINE_TM
    full = lambda a: pl.BlockSpec(a.shape, lambda i: (0, 0))
    return pl.pallas_call(
        _combine_kernel,
        grid=(T // tm,),
        in_specs=[pl.BlockSpec((tm, TOP_K), lambda i: (i, 0)),
                  pl.BlockSpec((tm, D_MODEL), lambda i: (i, 0)),
                  pl.BlockSpec((TOP_K, tm * ROW_SUB, LANES), lambda i: (0, i, 0)),
                  full(wsu), full(wsd), full(g2), full(b2)],
        out_specs=pl.BlockSpec((tm, D_MODEL), lambda i: (i, 0)),
        out_shape=jax.ShapeDtypeStruct((T, D_MODEL), F32),
        compiler_params=pltpu.CompilerParams(dimension_semantics=("parallel",),
                                             vmem_limit_bytes=VMEM_LIMIT),
        name="combine",
    )(w_tok, h1, yg, wsu, wsd, g2, b2)


def _prep_weights(w_in, w_uq, w_ukv, seq):
    z = lambda n: jnp.zeros((D_MODEL, n), F32)
    kr = w_in[:, 512:544]
    qs = w_in[:, 544:1056] * (1.0 / math.sqrt(SWA_HEAD_DIM))
    ks0, ks1 = w_in[:, 1056:1120], w_in[:, 1120:1184]
    vs0, vs1 = w_in[:, 1184:1248], w_in[:, 1248:1312]
    half = MLA_ROPE // 2
    w1 = jnp.concatenate([
        w_in[:, 0:512], qs,
        ks0, z(64), z(64), ks0, ks1, z(64), z(64), ks1,
        vs0, vs0, vs1, vs1,
        z(64), kr, z(32),
        z(64), -kr[:, half:], kr[:, :half], z(32)], axis=1).astype(BF16)
    wg = w_in[:, 1312:3360].astype(BF16)

    zq = lambda n: jnp.zeros((MLA_Q_LORA, n), F32)
    qd = MLA_NOPE + MLA_ROPE
    q_cols, r_cols, k_cols, v_cols = [], [], [], []
    for h in range(MLA_HEADS):
        wq = w_uq[:, h * qd:(h + 1) * qd]
        q_cols += [wq, zq(32)]
        r_cols += [zq(64), -wq[:, MLA_NOPE + half:], wq[:, MLA_NOPE:MLA_NOPE + half], zq(32)]
        wk = w_ukv[:, h * 128:(h + 1) * 128]
        k_cols += [wk[:, :MLA_NOPE], zq(64)]
        v_cols += [wk[:, MLA_NOPE:]]
    wq2 = jnp.concatenate(q_cols + r_cols, axis=1).astype(BF16)
    wk = jnp.concatenate(k_cols, axis=1).astype(BF16)
    wvt = jnp.concatenate(v_cols, axis=1).T.astype(BF16)

    inv_freq = ROPE_THETA ** (-jnp.arange(0, MLA_ROPE, 2, dtype=F32) / MLA_ROPE)
    ang = jnp.arange(seq, dtype=F32)[:, None] * inv_freq[None, :]
    cos, sin = jnp.cos(ang), jnp.sin(ang)
    one, zero = jnp.ones((seq, 64), F32), jnp.zeros((seq, 64), F32)
    z32 = jnp.zeros((seq, 32), F32)
    scale = math.log2(math.e) / math.sqrt(MLA_NOPE + MLA_ROPE)
    tabs = (jnp.concatenate([one, cos, cos, z32], axis=1) * scale,
            jnp.concatenate([zero, sin, sin, z32], axis=1) * scale,
            jnp.concatenate([zero, cos, cos, z32], axis=1),
            jnp.concatenate([zero, sin, sin, z32], axis=1))
    return w1, wg, wq2, wk, wvt, tabs


def kernel(x, w_in, mla_q_norm, mla_kv_norm, w_uq, w_ukv, attn_sinks, w_o_mla, w_o_swa, w_out,
           ln1_g, ln1_b, w_router, router_bias, w_exp_up, w_exp_down, w_sh_up, w_sh_down,
           ln2_g, ln2_b):
    batch, seq, _ = x.shape
    T = batch * seq
    x2 = x.reshape(T, D_MODEL)
    w1, wg, wq2, wk, wvt, tabs = _prep_weights(w_in, w_uq, w_ukv, seq)
    q, k, vt, qs, ks, vs = _proj(x2, w1, wq2, wk, wvt, mla_q_norm.reshape(1, -1),
                                 mla_kv_norm.reshape(1, -1), tabs, seq)
    o_mla = _mla(q, k, vt, batch, seq)
    o_swa = _swa(attn_sinks.astype(F32), qs, ks, vs, batch, seq)

    tri = (lax.broadcasted_iota(jnp.int32, (POST_TM, POST_TM), 0)
           < lax.broadcasted_iota(jnp.int32, (POST_TM, POST_TM), 1)).astype(BF16)
    h1, h1p, e_t, w_t, r_t, cnt = _post(
        x2, o_mla, o_swa, w_o_mla.astype(BF16), w_o_swa.astype(BF16), wg, w_out.astype(BF16),
        ln1_g.reshape(1, -1), ln1_b.reshape(1, -1), w_router.T.astype(BF16),
        router_bias.reshape(-1, 1).astype(F32), tri)

    counts = cnt[:, 0]
    pad = (counts + ROW_BLK - 1) // ROW_BLK * ROW_BLK
    pad_end = jnp.cumsum(pad)
    pad_start = pad_end - pad
    n_rows = (T * TOP_K // ROW_BLK + N_EXPERTS) * ROW_BLK
    dest = _dest(pad_start.astype(jnp.int32), e_t, r_t)
    idx = (dest.reshape(TOP_K, T // SC_CHUNK, SC_CHUNK, 1) * ROW_SUB
           + jnp.arange(ROW_SUB, dtype=jnp.int32))
    idx = idx.transpose(1, 0, 2, 3).reshape(T // SC_CHUNK, TOP_K, SC_CHUNK * ROW_SUB)
    xs = _sc_dispatch(h1p, idx, n_rows)
    ys = _experts((pad_start // ROW_BLK).astype(jnp.int32), (pad // ROW_BLK).astype(jnp.int32),
                  xs, w_exp_up, w_exp_down)
    yg = _sc_gather(ys, idx, T)
    out = _combine(w_t.T, h1, yg, w_sh_up.astype(BF16), w_sh_down.astype(BF16),
                   ln2_g.reshape(1, -1), ln2_b.reshape(1, -1))
    return out.reshape(batch, seq, D_MODEL)
```

```python
import math

import jax
import jax.numpy as jnp
from jax import lax
from jax.experimental import pallas as pl
from jax.experimental.pallas import tpu as pltpu
from jax.experimental.pallas import tpu_sc as plsc

D_MODEL = 1024
MLA_HEADS = 8
MLA_Q_LORA = 256
MLA_KV_LORA = 256
MLA_NOPE = 64
MLA_ROPE = 32
MLA_V = 64
ROPE_THETA = 10000.0
SWA_HEADS = 8
SWA_KV_HEADS = 2
SWA_HEAD_DIM = 64
SWA_WINDOW = 128
N_EXPERTS = 256
TOP_K = 8
N_GROUPS = 8
GROUP_SIZE = N_EXPERTS // N_GROUPS
TOPK_GROUPS = 4
EXPERT_HIDDEN = 256
SHARED_HIDDEN = 256
ROUTED_SCALE = 2.5
DEEPNORM_ALPHA = 2.0 ** 0.25
LN_EPS = 1e-5
RMS_EPS = 1e-6

LANES = 128
ROW_WORDS = D_MODEL // 2
ROW_SUB = ROW_WORDS // LANES
VMEM_LIMIT = 48 * 1024 * 1024

PROJ_TM = 512
MLA_TQ = 256
MLA_HPS = 4
POST_TM = 256
ROW_BLK = 256
EXPERT_SLOTS = 4
COMBINE_TM = 256
SC_CHUNK = 32

BF16 = jnp.bfloat16
F32 = jnp.float32
NEG_INF = float("-inf")


def _sigmoid(v):
    return 1.0 / (1.0 + jnp.exp(-v))


def _dot(a, b):
    return jnp.dot(a, b, preferred_element_type=F32)


def _dot_nt(a, b):
    return lax.dot_general(a, b, (((1,), (1,)), ((), ())), preferred_element_type=F32)


def _pack_rows(y, out_ref):
    rows = y.shape[0]
    for j in range(ROW_SUB):
        a = y[:, j * LANES:(j + 1) * LANES].astype(BF16).astype(F32)
        b = y[:, ROW_WORDS + j * LANES:ROW_WORDS + (j + 1) * LANES].astype(BF16).astype(F32)
        ua = pltpu.bitcast(a, jnp.uint32) >> 16
        ub = pltpu.bitcast(b, jnp.uint32)
        out_ref[pl.ds(j, rows, stride=ROW_SUB), :] = ua | ub


def _unpack_rows(ref, rows):
    lo, hi = [], []
    for j in range(ROW_SUB):
        u = ref[pl.ds(j, rows, stride=ROW_SUB), :]
        lo.append(pltpu.bitcast(u << 16, F32))
        hi.append(pltpu.bitcast(u & jnp.uint32(0xFFFF0000), F32))
    return lo + hi


def _proj_kernel(x_ref, w1_ref, wq_ref, wk_ref, wvt_ref, gq_ref, gkv_ref, cq_ref, sq_ref, ck_ref, sk_ref,
                 q_ref, k_ref, vt_ref, qs_ref, ks_ref, vs_ref):
    xb = x_ref[...].astype(BF16)
    p = _dot(xb, w1_ref[...])

    def rms(c, g):
        return c * lax.rsqrt(jnp.mean(c * c, axis=-1, keepdims=True) + RMS_EPS) * g

    cqn = rms(p[:, 0:256], gq_ref[...]).astype(BF16)
    ckvn = rms(p[:, 256:512], gkv_ref[...]).astype(BF16)
    qs_ref[...] = p[:, 512:1024].astype(BF16)
    ks_ref[...] = p[:, 1024:1536].astype(BF16)
    vs_ref[...] = p[:, 1536:1792].astype(BF16)
    kr = p[:, 1792:1920] * ck_ref[...] + p[:, 1920:2048] * sk_ref[...]
    qq = _dot(cqn, wq_ref[...])
    kn = _dot(ckvn, wk_ref[...])
    cq = cq_ref[...]
    sq = sq_ref[...]
    for h in range(MLA_HEADS):
        a = qq[:, h * LANES:(h + 1) * LANES]
        b = qq[:, 1024 + h * LANES:1024 + (h + 1) * LANES]
        q_ref[:, h * LANES:(h + 1) * LANES] = (a * cq + b * sq).astype(BF16)
        k_ref[:, h * LANES:(h + 1) * LANES] = (kn[:, h * LANES:(h + 1) * LANES] + kr).astype(BF16)
    vt_ref[...] = _dot_nt(wvt_ref[...], ckvn).astype(BF16)


def _proj(x2, w1, wq2, wk, wvt, gq, gkv, tabs, seq):
    T = x2.shape[0]
    tm = PROJ_TM
    nper = seq // tm
    full = lambda shape: pl.BlockSpec(shape, lambda i: (0, 0))
    tab = pl.BlockSpec((tm, LANES), lambda i: (i % nper, 0))
    row = lambda n: pl.BlockSpec((tm, n), lambda i: (i, 0))
    return pl.pallas_call(
        _proj_kernel,
        grid=(T // tm,),
        in_specs=[row(D_MODEL), full(w1.shape), full(wq2.shape), full(wk.shape), full(wvt.shape),
                  full(gq.shape), full(gkv.shape), tab, tab, tab, tab],
        out_specs=[row(1024), row(1024), pl.BlockSpec((512, tm), lambda i: (0, i)),
                   row(512), row(512), row(256)],
        out_shape=[jax.ShapeDtypeStruct((T, 1024), BF16), jax.ShapeDtypeStruct((T, 1024), BF16),
                   jax.ShapeDtypeStruct((512, T), BF16), jax.ShapeDtypeStruct((T, 512), BF16),
                   jax.ShapeDtypeStruct((T, 512), BF16), jax.ShapeDtypeStruct((T, 256), BF16)],
        compiler_params=pltpu.CompilerParams(dimension_semantics=("parallel",),
                                             vmem_limit_bytes=VMEM_LIMIT),
        name="proj",
    )(x2, w1, wq2, wk, wvt, gq, gkv, *tabs)


def _mla_kernel(q_ref, k_ref, vt_ref, o_ref, *acc_scr):
    tq = q_ref.shape[0]
    qi = pl.program_id(2)
    for acc in acc_scr:
        acc[...] = jnp.zeros(acc.shape, F32)

    def step(kc, stats, masked):
        ks = pl.multiple_of(kc * tq, tq)
        new_stats = []
        for h in range(MLA_HPS):
            m_prev, l_prev = stats[h]
            s = _dot_nt(k_ref[pl.ds(ks, tq), h * LANES:(h + 1) * LANES],
                        q_ref[:, h * LANES:(h + 1) * LANES])
            if masked:
                key = lax.broadcasted_iota(jnp.int32, s.shape, 0)
                qry = lax.broadcasted_iota(jnp.int32, s.shape, 1)
                s = jnp.where(key <= qry, s, NEG_INF)
            m_new = jnp.maximum(m_prev, jnp.max(s, axis=0, keepdims=True))
            alpha = jnp.exp2(m_prev - m_new)
            p = jnp.exp2(s - m_new)
            l_new = alpha * l_prev + jnp.sum(p, axis=0, keepdims=True)
            pv = _dot(vt_ref[h * MLA_V:(h + 1) * MLA_V, pl.ds(ks, tq)], p.astype(BF16))
            acc_scr[h][...] = acc_scr[h][...] * alpha + pv
            new_stats.append((m_new, l_new))
        return tuple(new_stats)

    init = tuple((jnp.full((1, tq), NEG_INF, F32), jnp.zeros((1, tq), F32)) for _ in range(MLA_HPS))
    stats = lax.fori_loop(0, qi, lambda kc, st: step(kc, st, False), init)
    stats = step(qi, stats, True)
    for h2 in range(MLA_HPS // 2):
        out_t = jnp.concatenate([acc_scr[2 * h2 + g][...] / stats[2 * h2 + g][1] for g in range(2)],
                                axis=0)
        o_ref[:, h2 * LANES:(h2 + 1) * LANES] = out_t.T.astype(BF16)


def _mla(q, k, vt, batch, seq):
    T = q.shape[0]
    tq = MLA_TQ
    nq = seq // tq
    hps = MLA_HPS
    return pl.pallas_call(
        _mla_kernel,
        grid=(batch, MLA_HEADS // hps, nq),
        in_specs=[pl.BlockSpec((tq, hps * LANES), lambda b, j, i: (b * nq + i, j)),
                  pl.BlockSpec((seq, hps * LANES), lambda b, j, i: (b, j)),
                  pl.BlockSpec((hps * MLA_V, seq), lambda b, j, i: (j, b))],
        out_specs=pl.BlockSpec((tq, hps * MLA_V), lambda b, j, i: (b * nq + i, j)),
        out_shape=jax.ShapeDtypeStruct((T, MLA_HEADS * MLA_V), BF16),
        scratch_shapes=[pltpu.VMEM((MLA_V, tq), F32) for _ in range(hps)],
        compiler_params=pltpu.CompilerParams(
            dimension_semantics=("parallel", "parallel", "arbitrary"), vmem_limit_bytes=VMEM_LIMIT),
        name="mla",
    )(q, k, vt)


def _swa_kernel(sink_ref, q_ref, kc_ref, kp_ref, vc_ref, vp_ref, o_ref):
    W = SWA_WINDOW
    n = pl.program_id(1)
    i = lax.broadcasted_iota(jnp.int32, (W, W), 0)
    j = lax.broadcasted_iota(jnp.int32, (W, W), 1)
    cur_ok = j <= i
    prev_ok = (j > i) & (n > 0)
    dist_c = (i - j).astype(F32)
    dist_p = (i + W - j).astype(F32)
    lane_lo = lax.broadcasted_iota(jnp.int32, (W, LANES), 1) < 64
    for pair in range(SWA_HEADS // 2):
        kvh = pair // 2
        qp = q_ref[:, pair * LANES:(pair + 1) * LANES]
        vc = vc_ref[:, kvh * LANES:(kvh + 1) * LANES]
        vp = vp_ref[:, kvh * LANES:(kvh + 1) * LANES]
        outs = []
        for g in range(2):
            head = 2 * pair + g
            slope = 2.0 ** (-8.0 * (head + 1) / SWA_HEADS)
            col = (2 * kvh + g) * LANES
            s_c = _dot_nt(qp, kc_ref[:, col:col + LANES])
            s_p = _dot_nt(qp, kp_ref[:, col:col + LANES])
            s_c = jnp.where(cur_ok, s_c - slope * dist_c, NEG_INF)
            s_p = jnp.where(prev_ok, s_p - slope * dist_p, NEG_INF)
            sink = sink_ref[head]
            m = jnp.maximum(jnp.maximum(jnp.max(s_c, axis=1, keepdims=True),
                                        jnp.max(s_p, axis=1, keepdims=True)), sink)
            p_c = jnp.exp(s_c - m)
            p_p = jnp.exp(s_p - m)
            denom = (jnp.sum(p_c, axis=1, keepdims=True) + jnp.sum(p_p, axis=1, keepdims=True)
                     + jnp.exp(sink - m))
            outs.append((_dot(p_c.astype(BF16), vc) + _dot(p_p.astype(BF16), vp)) / denom)
        o_ref[:, pair * LANES:(pair + 1) * LANES] = jnp.where(lane_lo, outs[0], outs[1]).astype(BF16)


def _swa(sinks, qs, ks, vs, batch, seq):
    T = qs.shape[0]
    W = SWA_WINDOW
    nb = seq // W
    cur = lambda n: pl.BlockSpec((W, n), lambda b, i: (b * nb + i, 0))
    prev = lambda n: pl.BlockSpec((W, n), lambda b, i: (b * nb + jnp.maximum(i - 1, 0), 0))
    return pl.pallas_call(
        _swa_kernel,
        grid=(batch, nb),
        in_specs=[pl.BlockSpec(memory_space=pltpu.SMEM), cur(512), cur(512), prev(512), cur(256), prev(256)],
        out_specs=cur(512),
        out_shape=jax.ShapeDtypeStruct((T, 512), BF16),
        compiler_params=pltpu.CompilerParams(dimension_semantics=("parallel", "parallel"),
                                             vmem_limit_bytes=VMEM_LIMIT),
        name="swa",
    )(sinks, qs, ks, ks, vs, vs)


def _post_kernel(x_ref, om_ref, os_ref, wom_ref, wos_ref, wg_ref, wout_ref, g1_ref, b1_ref,
                 wr_ref, rb_ref, tri_ref,
                 h1_ref, h1p_ref, e_ref, w_ref, r_ref, cnt_ref, carry_scr):
    tm = x_ref.shape[0]
    step = pl.program_id(0)

    @pl.when(step == 0)
    def _():
        carry_scr[...] = jnp.zeros(carry_scr.shape, F32)

    x = x_ref[...]
    ya = _dot(om_ref[...], wom_ref[...])
    yb = _dot(os_ref[...], wos_ref[...])
    gates = _dot(x.astype(BF16), wg_ref[...])
    merged = _sigmoid(gates[:, :D_MODEL]) * ya + _sigmoid(gates[:, D_MODEL:]) * yb
    mix = _dot(merged.astype(BF16), wout_ref[...])
    z = DEEPNORM_ALPHA * x + mix
    mu = jnp.mean(z, axis=-1, keepdims=True)
    zc = z - mu
    var = jnp.mean(zc * zc, axis=-1, keepdims=True)
    h1 = zc * lax.rsqrt(var + LN_EPS) * g1_ref[...] + b1_ref[...]
    h1_ref[...] = h1
    _pack_rows(h1, h1p_ref)

    scores = _sigmoid(_dot_nt(wr_ref[...], h1.astype(BF16)))
    choice = scores + rb_ref[...]
    row = lax.broadcasted_iota(jnp.int32, (N_EXPERTS, tm), 0)
    grow = lax.broadcasted_iota(jnp.int32, (GROUP_SIZE, tm), 0)
    gscore = []
    for g in range(N_GROUPS):
        blk = choice[g * GROUP_SIZE:(g + 1) * GROUP_SIZE, :]
        m1 = jnp.max(blk, axis=0, keepdims=True)
        i1 = jnp.min(jnp.where(blk == m1, grow, GROUP_SIZE), axis=0, keepdims=True)
        m2 = jnp.max(jnp.where(grow == i1, NEG_INF, blk), axis=0, keepdims=True)
        gscore.append(m1 + m2)
    gsc = jnp.concatenate(gscore, axis=0)
    gidx = lax.broadcasted_iota(jnp.int32, (N_GROUPS, tm), 0)
    grank = jnp.zeros((N_GROUPS, tm), jnp.int32)
    for g in range(N_GROUPS):
        sg = gsc[g:g + 1, :]
        beats = (sg > gsc) | ((sg == gsc) & (gidx > g))
        grank = grank + beats.astype(jnp.int32)
    gsel = (grank < TOPK_GROUPS).astype(F32)
    emask = jnp.concatenate(
        [jnp.broadcast_to(gsel[g:g + 1, :], (GROUP_SIZE, tm)) for g in range(N_GROUPS)], axis=0)
    work = jnp.where(emask > 0.0, choice, NEG_INF)
    sel = jnp.zeros((N_EXPERTS, tm), F32)
    idxs, svals = [], []
    for _k in range(TOP_K):
        m = jnp.max(work, axis=0, keepdims=True)
        idx = jnp.min(jnp.where(work == m, row, N_EXPERTS), axis=0, keepdims=True)
        hit = row == idx
        svals.append(jnp.sum(jnp.where(hit, scores, 0.0), axis=0, keepdims=True))
        work = jnp.where(hit, NEG_INF, work)
        sel = jnp.where(hit, 1.0, sel)
        idxs.append(idx)
    ssum = svals[0]
    for sv in svals[1:]:
        ssum = ssum + sv
    e_ref[...] = jnp.concatenate(idxs, axis=0)
    w_ref[...] = jnp.concatenate([sv / ssum * ROUTED_SCALE for sv in svals], axis=0)

    carry = carry_scr[...]
    rank = _dot(sel.astype(BF16), tri_ref[...]) + carry[:, 0:1]
    r_ref[...] = jnp.concatenate(
        [jnp.sum(jnp.where(row == idx, rank, 0.0), axis=0, keepdims=True) for idx in idxs],
        axis=0).astype(jnp.int32)
    carry = carry + jnp.sum(sel, axis=1, keepdims=True)
    carry_scr[...] = carry
    cnt_ref[...] = carry.astype(jnp.int32)


def _post(x2, o_mla, o_swa, wom, wos, wg, wout, g1, b1, wr_t, rbias, tri):
    T = x2.shape[0]
    tm = POST_TM
    full = lambda a: pl.BlockSpec(a.shape, lambda i: (0, 0))
    row = lambda n: pl.BlockSpec((tm, n), lambda i: (i, 0))
    col = pl.BlockSpec((TOP_K, tm), lambda i: (0, i))
    return pl.pallas_call(
        _post_kernel,
        grid=(T // tm,),
        in_specs=[row(D_MODEL), row(512), row(512), full(wom), full(wos), full(wg), full(wout),
                  full(g1), full(b1), full(wr_t), full(rbias), full(tri)],
        out_specs=[row(D_MODEL), pl.BlockSpec((tm * ROW_SUB, LANES), lambda i: (i, 0)), col, col, col,
                   pl.BlockSpec((N_EXPERTS, LANES), lambda i: (0, 0))],
        out_shape=[jax.ShapeDtypeStruct((T, D_MODEL), F32),
                   jax.ShapeDtypeStruct((T * ROW_SUB, LANES), jnp.uint32),
                   jax.ShapeDtypeStruct((TOP_K, T), jnp.int32),
                   jax.ShapeDtypeStruct((TOP_K, T), F32),
                   jax.ShapeDtypeStruct((TOP_K, T), jnp.int32),
                   jax.ShapeDtypeStruct((N_EXPERTS, LANES), jnp.int32)],
        scratch_shapes=[pltpu.VMEM((N_EXPERTS, LANES), F32)],
        compiler_params=pltpu.CompilerParams(dimension_semantics=("arbitrary",),
                                             vmem_limit_bytes=VMEM_LIMIT),
        name="post",
    )(x2, o_mla, o_swa, wom, wos, wg, wout, g1, b1, wr_t, rbias, tri)


def _dest_kernel(start_ref, e_ref, r_ref, o_ref):
    e = e_ref[...]
    base = jnp.zeros(e.shape, jnp.int32)
    for j in range(N_EXPERTS):
        base = jnp.where(e == j, start_ref[j], base)
    o_ref[...] = base + r_ref[...]


def _dest(seg_start, e_t, r_t):
    T = e_t.shape[1]
    tn = min(T, 4096)
    col = pl.BlockSpec((TOP_K, tn), lambda i: (0, i))
    return pl.pallas_call(
        _dest_kernel,
        grid=(T // tn,),
        in_specs=[pl.BlockSpec(memory_space=pltpu.SMEM), col, col],
        out_specs=col,
        out_shape=jax.ShapeDtypeStruct((TOP_K, T), jnp.int32),
        compiler_params=pltpu.CompilerParams(dimension_semantics=("parallel",)),
        name="dest",
    )(seg_start, e_t, r_t)


def _sc_worker_chunks(n_tokens):
    info = plsc.get_sparse_core_info()
    n_workers = info.num_cores * info.num_subcores
    per_worker = n_tokens // SC_CHUNK // n_workers
    assert per_worker * n_workers * SC_CHUNK == n_tokens
    first = (lax.axis_index("s") * info.num_cores + lax.axis_index("c")) * per_worker
    return first, per_worker


def _sc_dispatch(h1p, idx, n_rows):
    n_tokens = h1p.shape[0] // ROW_SUB
    R = SC_CHUNK * ROW_SUB

    def body(h_hbm, idx_hbm, xs_hbm, idx_v, rows_v, sem):
        first, per_worker = _sc_worker_chunks(n_tokens)

        @pl.loop(0, per_worker)
        def _(ci):
            c = first + ci
            pltpu.sync_copy(idx_hbm.at[c], idx_v)
            pltpu.sync_copy(h_hbm.at[pl.ds(c * R, R)], rows_v)
            copies = [pltpu.make_async_copy(rows_v, xs_hbm.at[idx_v.at[k]], sem) for k in range(TOP_K)]
            for cp in copies:
                cp.start()
            for cp in copies:
                cp.wait()

    run = pl.kernel(
        body, out_type=(),
        mesh=plsc.VectorSubcoreMesh(core_axis_name="c", subcore_axis_name="s"),
        scratch_types=[pltpu.VMEM((TOP_K, R), jnp.int32), pltpu.VMEM((R, LANES), jnp.uint32),
                       pltpu.SemaphoreType.DMA],
        name="sc_dispatch")
    xs_ref = jax.new_ref(jnp.zeros((n_rows * ROW_SUB, LANES), jnp.uint32))
    run(h1p, idx, xs_ref)
    return jax.freeze(xs_ref)


def _sc_gather(ys, idx, n_tokens):
    R = SC_CHUNK * ROW_SUB
    half = TOP_K // 2

    def body(ys_hbm, idx_hbm, yg_hbm, idx_v, buf, gsem, wsem):
        first, per_worker = _sc_worker_chunks(n_tokens)

        @pl.loop(0, per_worker)
        def _(ci):
            c = first + ci
            pltpu.sync_copy(idx_hbm.at[c], idx_v)
            for k0 in (0, half):
                gathers = [pltpu.make_async_copy(ys_hbm.at[idx_v.at[k0 + k]], buf.at[k], gsem)
                           for k in range(half)]
                for cp in gathers:
                    cp.start()
                for cp in gathers:
                    cp.wait()
                writes = [pltpu.make_async_copy(buf.at[k], yg_hbm.at[k0 + k, pl.ds(c * R, R)], wsem)
                          for k in range(half)]
                for cp in writes:
                    cp.start()
                for cp in writes:
                    cp.wait()

    run = pl.kernel(
        body, out_type=jax.ShapeDtypeStruct((TOP_K, n_tokens * ROW_SUB, LANES), jnp.uint32),
        mesh=plsc.VectorSubcoreMesh(core_axis_name="c", subcore_axis_name="s"),
        scratch_types=[pltpu.VMEM((TOP_K, R), jnp.int32), pltpu.VMEM((half, R, LANES), jnp.uint32),
                       pltpu.SemaphoreType.DMA, pltpu.SemaphoreType.DMA],
        name="sc_gather")
    return run(ys, idx)


def _experts_kernel(first_ref, nblk_ref, xs_hbm, wup_ref, wdn_ref, ys_hbm,
                    wup_bf, wdn_bf, xbuf, ybuf, in_sem, out_sem):
    e = pl.program_id(0)
    n = nblk_ref[e]
    b0 = first_ref[e]
    total = first_ref[N_EXPERTS - 1] + nblk_ref[N_EXPERTS - 1]
    rb = ROW_BLK * ROW_SUB
    depth = EXPERT_SLOTS

    def slot_of(b):
        return lax.rem(b, depth)

    def rows_of(b):
        return pl.ds(pl.multiple_of(b * rb, rb), rb)

    def in_copy(b):
        return pltpu.make_async_copy(xs_hbm.at[rows_of(b)], xbuf.at[slot_of(b)], in_sem.at[slot_of(b)])

    def out_copy(b):
        return pltpu.make_async_copy(ybuf.at[slot_of(b)], ys_hbm.at[rows_of(b)], out_sem.at[slot_of(b)])

    @pl.when(e == 0)
    def _():
        for b in range(depth - 1):
            @pl.when(b < total)
            def _():
                in_copy(b).start()

    @pl.when(n > 0)
    def _():
        wup_bf[...] = wup_ref[0].astype(BF16)
        wdn_bf[...] = wdn_ref[0].astype(BF16)

        def body(b, carry):
            slot = slot_of(b)
            in_copy(b).wait()

            @pl.when(b + depth - 1 < total)
            def _():
                in_copy(b + depth - 1).start()

            @pl.when(b >= depth)
            def _():
                out_copy(b - depth).wait()

            xb = jnp.concatenate([c.astype(BF16) for c in _unpack_rows(xbuf.at[slot], ROW_BLK)], axis=1)
            gu = _dot(xb, wup_bf[...])
            g = gu[:, :EXPERT_HIDDEN]
            hid = g * _sigmoid(g) * gu[:, EXPERT_HIDDEN:]
            _pack_rows(_dot(hid.astype(BF16), wdn_bf[...]), ybuf.at[slot])
            out_copy(b).start()
            return carry

        lax.fori_loop(b0, b0 + n, body, 0)

    @pl.when(e == N_EXPERTS - 1)
    def _():
        for back in range(depth, 0, -1):
            @pl.when(total - back >= 0)
            def _():
                out_copy(total - back).wait()


def _experts(first_blk, n_blk, xs, w_exp_up, w_exp_down):
    rb = ROW_BLK * ROW_SUB
    grid_spec = pltpu.PrefetchScalarGridSpec(
        num_scalar_prefetch=2,
        grid=(N_EXPERTS,),
        in_specs=[pl.BlockSpec(memory_space=pl.ANY),
                  pl.BlockSpec((1, D_MODEL, 2 * EXPERT_HIDDEN), lambda e, fb, nb: (e, 0, 0)),
                  pl.BlockSpec((1, EXPERT_HIDDEN, D_MODEL), lambda e, fb, nb: (e, 0, 0))],
        out_specs=pl.BlockSpec(memory_space=pl.ANY),
        scratch_shapes=[pltpu.VMEM((D_MODEL, 2 * EXPERT_HIDDEN), BF16),
                        pltpu.VMEM((EXPERT_HIDDEN, D_MODEL), BF16),
                        pltpu.VMEM((EXPERT_SLOTS, rb, LANES), jnp.uint32),
                        pltpu.VMEM((EXPERT_SLOTS, rb, LANES), jnp.uint32),
                        pltpu.SemaphoreType.DMA((EXPERT_SLOTS,)),
                        pltpu.SemaphoreType.DMA((EXPERT_SLOTS,))],
    )
    return pl.pallas_call(
        _experts_kernel,
        grid_spec=grid_spec,
        out_shape=jax.ShapeDtypeStruct(xs.shape, jnp.uint32),
        compiler_params=pltpu.CompilerParams(dimension_semantics=("arbitrary",),
                                             vmem_limit_bytes=VMEM_LIMIT),
        name="experts",
    )(first_blk, n_blk, xs, w_exp_up, w_exp_down)


def _combine_kernel(w_ref, h1_ref, yg_ref, wsu_ref, wsd_ref, g2_ref, b2_ref, o_ref):
    tm = h1_ref.shape[0]
    h1 = h1_ref[...]
    gu = _dot(h1.astype(BF16), wsu_ref[...])
    g = gu[:, :SHARED_HIDDEN]
    hid = g * _sigmoid(g) * gu[:, SHARED_HIDDEN:]
    ffn = _dot(hid.astype(BF16), wsd_ref[...])

    w = w_ref[...]
    acc = [None] * (2 * ROW_SUB)
    for k in range(TOP_K):
        wk = w[:, k:k + 1]
        chunks = _unpack_rows(yg_ref.at[k], tm)
        for c in range(2 * ROW_SUB):
            acc[c] = wk * chunks[c] if acc[c] is None else acc[c] + wk * chunks[c]
    routed = jnp.concatenate(acc, axis=1)
    z = DEEPNORM_ALPHA * h1 + (routed + ffn)
    mu = jnp.mean(z, axis=-1, keepdims=True)
    zc = z - mu
    var = jnp.mean(zc * zc, axis=-1, keepdims=True)
    o_ref[...] = zc * lax.rsqrt(var + LN_EPS) * g2_ref[...] + b2_ref[...]


def _combine(w_tok, h1, yg, wsu, wsd, g2, b2):
    T = h1.shape[0]
    tm = COMBINE_TM
    full = lambda a: pl.BlockSpec(a.shape, lambda i: (0, 0))
    return pl.pallas_call(
        _combine_kernel,
        grid=(T // tm,),
        in_specs=[pl.BlockSpec((tm, TOP_K), lambda i: (i, 0)),
                  pl.BlockSpec((tm, D_MODEL), lambda i: (i, 0)),
                  pl.BlockSpec((TOP_K, tm * ROW_SUB, LANES), lambda i: (0, i, 0)),
                  full(wsu), full(wsd), full(g2), full(b2)],
        out_specs=pl.BlockSpec((tm, D_MODEL), lambda i: (i, 0)),
        out_shape=jax.ShapeDtypeStruct((T, D_MODEL), F32),
        compiler_params=pltpu.CompilerParams(dimension_semantics=("parallel",),
                                             vmem_limit_bytes=VMEM_LIMIT),
        name="combine",
    )(w_tok, h1, yg, wsu, wsd, g2, b2)


def _prep_weights(w_in, w_uq, w_ukv, seq):
    z = lambda n: jnp.zeros((D_MODEL, n), F32)
    kr = w_in[:, 512:544]
    qs = w_in[:, 544:1056] * (1.0 / math.sqrt(SWA_HEAD_DIM))
    ks0, ks1 = w_in[:, 1056:1120], w_in[:, 1120:1184]
    vs0, vs1 = w_in[:, 1184:1248], w_in[:, 1248:1312]
    half = MLA_ROPE // 2
    w1 = jnp.concatenate([
        w_in[:, 0:512], qs,
        ks0, z(64), z(64), ks0, ks1, z(64), z(64), ks1,
        vs0, vs0, vs1, vs1,
        z(64), kr, z(32),
        z(64), -kr[:, half:], kr[:, :half], z(32)], axis=1).astype(BF16)
    wg = w_in[:, 1312:3360].astype(BF16)

    zq = lambda n: jnp.zeros((MLA_Q_LORA, n), F32)
    qd = MLA_NOPE + MLA_ROPE
    q_cols, r_cols, k_cols, v_cols = [], [], [], []
    for h in range(MLA_HEADS):
        wq = w_uq[:, h * qd:(h + 1) * qd]
        q_cols += [wq, zq(32)]
        r_cols += [zq(64), -wq[:, MLA_NOPE + half:], wq[:, MLA_NOPE:MLA_NOPE + half], zq(32)]
        wk = w_ukv[:, h * 128:(h + 1) * 128]
        k_cols += [wk[:, :MLA_NOPE], zq(64)]
        v_cols += [wk[:, MLA_NOPE:]]
    wq2 = jnp.concatenate(q_cols + r_cols, axis=1).astype(BF16)
    wk = jnp.concatenate(k_cols, axis=1).astype(BF16)
    wvt = jnp.concatenate(v_cols, axis=1).T.astype(BF16)

    inv_freq = ROPE_THETA ** (-jnp.arange(0, MLA_ROPE, 2, dtype=F32) / MLA_ROPE)
    ang = jnp.arange(seq, dtype=F32)[:, None] * inv_freq[None, :]
    cos, sin = jnp.cos(ang), jnp.sin(ang)
    one, zero = jnp.ones((seq, 64), F32), jnp.zeros((seq, 64), F32)
    z32 = jnp.zeros((seq, 32), F32)
    scale = math.log2(math.e) / math.sqrt(MLA_NOPE + MLA_ROPE)
    tabs = (jnp.concatenate([one, cos, cos, z32], axis=1) * scale,
            jnp.concatenate([zero, sin, sin, z32], axis=1) * scale,
            jnp.concatenate([zero, cos, cos, z32], axis=1),
            jnp.concatenate([zero, sin, sin, z32], axis=1))
    return w1, wg, wq2, wk, wvt, tabs


def kernel(x, w_in, mla_q_norm, mla_kv_norm, w_uq, w_ukv, attn_sinks, w_o_mla, w_o_swa, w_out,
           ln1_g, ln1_b, w_router, router_bias, w_exp_up, w_exp_down, w_sh_up, w_sh_down,
           ln2_g, ln2_b):
    batch, seq, _ = x.shape
    T = batch * seq
    x2 = x.reshape(T, D_MODEL)
    w1, wg, wq2, wk, wvt, tabs = _prep_weights(w_in, w_uq, w_ukv, seq)
    q, k, vt, qs, ks, vs = _proj(x2, w1, wq2, wk, wvt, mla_q_norm.reshape(1, -1),
                                 mla_kv_norm.reshape(1, -1), tabs, seq)
    o_mla = _mla(q, k, vt, batch, seq)
    o_swa = _swa(attn_sinks.astype(F32), qs, ks, vs, batch, seq)

    tri = (lax.broadcasted_iota(jnp.int32, (POST_TM, POST_TM), 0)
           < lax.broadcasted_iota(jnp.int32, (POST_TM, POST_TM), 1)).astype(BF16)
    h1, h1p, e_t, w_t, r_t, cnt = _post(
        x2, o_mla, o_swa, w_o_mla.astype(BF16), w_o_swa.astype(BF16), wg, w_out.astype(BF16),
        ln1_g.reshape(1, -1), ln1_b.reshape(1, -1), w_router.T.astype(BF16),
        router_bias.reshape(-1, 1).astype(F32), tri)

    counts = cnt[:, 0]
    pad = (counts + ROW_BLK - 1) // ROW_BLK * ROW_BLK
    pad_end = jnp.cumsum(pad)
    pad_start = pad_end - pad
    n_rows = (T * TOP_K // ROW_BLK + N_EXPERTS) * ROW_BLK
    dest = _dest(pad_start.astype(jnp.int32), e_t, r_t)
    idx = (dest.reshape(TOP_K, T // SC_CHUNK, SC_CHUNK, 1) * ROW_SUB
           + jnp.arange(ROW_SUB, dtype=jnp.int32))
    idx = idx.transpose(1, 0, 2, 3).reshape(T // SC_CHUNK, TOP_K, SC_CHUNK * ROW_SUB)
    xs = _sc_dispatch(h1p, idx, n_rows)
    ys = _experts((pad_start // ROW_BLK).astype(jnp.int32), (pad // ROW_BLK).astype(jnp.int32),
                  xs, w_exp_up, w_exp_down)
    yg = _sc_gather(ys, idx, T)
    out = _combine(w_t.T, h1, yg, w_sh_up.astype(BF16), w_sh_down.astype(BF16),
                   ln2_g.reshape(1, -1), ln2_b.reshape(1, -1))
    return out.reshape(batch, seq, D_MODEL)
```

```python
import math

import jax
import jax.numpy as jnp
from jax import lax
from jax.experimental import pallas as pl
from jax.experimental.pallas import tpu as pltpu
from jax.experimental.pallas import tpu_sc as plsc

D_MODEL = 1024
MLA_HEADS = 8
MLA_Q_LORA = 256
MLA_KV_LORA = 256
MLA_NOPE = 64
MLA_ROPE = 32
MLA_V = 64
ROPE_THETA = 10000.0
SWA_HEADS = 8
SWA_KV_HEADS = 2
SWA_GROUP = SWA_HEADS // SWA_KV_HEADS
SWA_HEAD_DIM = 64
SWA_WINDOW = 128
N_EXPERTS = 256
TOP_K = 8
N_GROUPS = 8
GROUP_SIZE = N_EXPERTS // N_GROUPS
TOPK_GROUPS = 4
EXPERT_HIDDEN = 256
SHARED_HIDDEN = 256
ROUTED_SCALE = 2.5
DEEPNORM_ALPHA = 2.0 ** 0.25
LN_EPS = 1e-5
RMS_EPS = 1e-6

LANES = 128
ROW_WORDS = D_MODEL // 2
ROW_SUB = ROW_WORDS // LANES
VMEM_LIMIT = 48 * 1024 * 1024

PROJ_TM = 512
MLA_TQ = 256
MLA_TK = 256
MLA_HPS = 8
POST_TM = 256
ROW_BLK = 256
EXPERT_SLOTS = 4
COMBINE_TM = 256
SC_CHUNK = 32

BF16 = jnp.bfloat16
F32 = jnp.float32
NEG_INF = float("-inf")
LOG2E = math.log2(math.e)


def _sigmoid(v):
    return 1.0 / (1.0 + jnp.exp(-v))


def _dot(a, b):
    return jnp.dot(a, b, preferred_element_type=F32)


def _dot_nt(a, b):
    return lax.dot_general(a, b, (((1,), (1,)), ((), ())), preferred_element_type=F32)


def _pack_rows(y, out_ref):
    rows = y.shape[0]
    for j in range(ROW_SUB):
        a = y[:, j * LANES:(j + 1) * LANES].astype(BF16).astype(F32)
        b = y[:, ROW_WORDS + j * LANES:ROW_WORDS + (j + 1) * LANES].astype(BF16).astype(F32)
        ua = pltpu.bitcast(a, jnp.uint32) >> 16
        ub = pltpu.bitcast(b, jnp.uint32)
        out_ref[pl.ds(j, rows, stride=ROW_SUB), :] = ua | ub


def _unpack_rows(ref, rows, n_valid=None):
    lo, hi = [], []
    if n_valid is not None:
        live = lax.broadcasted_iota(jnp.int32, (rows, LANES), 0) < n_valid
    for j in range(ROW_SUB):
        u = ref[pl.ds(j, rows, stride=ROW_SUB), :]
        if n_valid is not None:
            u = jnp.where(live, u, jnp.uint32(0))
        lo.append(pltpu.bitcast(u << 16, F32))
        hi.append(pltpu.bitcast(u & jnp.uint32(0xFFFF0000), F32))
    return lo + hi


def _proj_kernel(x_ref, w1_ref, wvst_ref, wq_ref, wk_ref, wvt_ref, gq_ref, gkv_ref,
                 cq_ref, sq_ref, ck_ref, sk_ref,
                 q_ref, k_ref, vt_ref, qs_ref, ks_ref, vst_ref):
    xb = x_ref[...].astype(BF16)
    p = _dot(xb, w1_ref[...])

    def rms(c, g):
        return c * lax.rsqrt(jnp.mean(c * c, axis=-1, keepdims=True) + RMS_EPS) * g

    cqn = rms(p[:, 0:256], gq_ref[...]).astype(BF16)
    ckvn = rms(p[:, 256:512], gkv_ref[...]).astype(BF16)
    qs_ref[...] = p[:, 512:1024].astype(BF16)
    ks_ref[...] = p[:, 1024:1536].astype(BF16)
    vst_ref[...] = _dot_nt(wvst_ref[...], xb).astype(BF16)
    kr = p[:, 1536:1664] * ck_ref[...] + p[:, 1664:1792] * sk_ref[...]
    qq = _dot(cqn, wq_ref[...])
    kn = _dot(ckvn, wk_ref[...])
    cq = cq_ref[...]
    sq = sq_ref[...]
    for h in range(MLA_HEADS):
        a = qq[:, h * LANES:(h + 1) * LANES]
        b = qq[:, 1024 + h * LANES:1024 + (h + 1) * LANES]
        q_ref[:, h * LANES:(h + 1) * LANES] = (a * cq + b * sq).astype(BF16)
        k_ref[:, h * LANES:(h + 1) * LANES] = (kn[:, h * LANES:(h + 1) * LANES] + kr).astype(BF16)
    vt_ref[...] = _dot_nt(wvt_ref[...], ckvn).astype(BF16)


def _proj(x2, w1, wvst, wq2, wk, wvt, gq, gkv, tabs, seq):
    T = x2.shape[0]
    tm = PROJ_TM
    nper = seq // tm
    full = lambda shape: pl.BlockSpec(shape, lambda i: (0, 0))
    tab = pl.BlockSpec((tm, LANES), lambda i: (i % nper, 0))
    row = lambda n: pl.BlockSpec((tm, n), lambda i: (i, 0))
    col = lambda n: pl.BlockSpec((n, tm), lambda i: (0, i))
    return pl.pallas_call(
        _proj_kernel,
        grid=(T // tm,),
        in_specs=[row(D_MODEL), full(w1.shape), full(wvst.shape), full(wq2.shape), full(wk.shape),
                  full(wvt.shape), full(gq.shape), full(gkv.shape), tab, tab, tab, tab],
        out_specs=[row(1024), row(1024), col(512), row(512), row(512), col(128)],
        out_shape=[jax.ShapeDtypeStruct((T, 1024), BF16), jax.ShapeDtypeStruct((T, 1024), BF16),
                   jax.ShapeDtypeStruct((512, T), BF16), jax.ShapeDtypeStruct((T, 512), BF16),
                   jax.ShapeDtypeStruct((T, 512), BF16), jax.ShapeDtypeStruct((128, T), BF16)],
        compiler_params=pltpu.CompilerParams(dimension_semantics=("parallel",),
                                             vmem_limit_bytes=VMEM_LIMIT),
        name="proj",
    )(x2, w1, wvst, wq2, wk, wvt, gq, gkv, *tabs)


def _mla_kernel(q_ref, k_ref, vt_ref, o_ref, *acc_scr):
    tq = q_ref.shape[0]
    qi = pl.program_id(2)
    for acc in acc_scr:
        acc[...] = jnp.zeros(acc.shape, F32)

    tk = MLA_TK

    def step(kc, stats, masked):
        ks = pl.multiple_of(kc * tk, tk)
        scores = [_dot_nt(k_ref[pl.ds(ks, tk), h * LANES:(h + 1) * LANES],
                          q_ref[:, h * LANES:(h + 1) * LANES]) for h in range(MLA_HPS)]
        new_stats, probs, alphas = [], [], []
        for h in range(MLA_HPS):
            m_prev, l_prev = stats[h]
            s = scores[h]
            if masked:
                key = ks + lax.broadcasted_iota(jnp.int32, s.shape, 0)
                qry = qi * tq + lax.broadcasted_iota(jnp.int32, s.shape, 1)
                s = jnp.where(key <= qry, s, NEG_INF)
            m_new = jnp.maximum(m_prev, jnp.max(s, axis=0, keepdims=True))
            alpha = jnp.exp2(m_prev - m_new)
            p = jnp.exp2(s - m_new)
            new_stats.append((m_new, alpha * l_prev + jnp.sum(p, axis=0, keepdims=True)))
            probs.append(p.astype(BF16))
            alphas.append(alpha)
        for h in range(MLA_HPS):
            pv = _dot(vt_ref[h * MLA_V:(h + 1) * MLA_V, pl.ds(ks, tk)], probs[h])
            acc_scr[h][...] = acc_scr[h][...] * alphas[h] + pv
        return tuple(new_stats)

    init = tuple((jnp.full((1, tq), NEG_INF, F32), jnp.zeros((1, tq), F32)) for _ in range(MLA_HPS))
    n_full = qi * (tq // tk)
    stats = lax.fori_loop(0, n_full, lambda kc, st: step(kc, st, False), init)
    for d in range(tq // tk):
        stats = step(n_full + d, stats, True)
    for h2 in range(MLA_HPS // 2):
        out_t = jnp.concatenate([acc_scr[2 * h2 + g][...] / stats[2 * h2 + g][1] for g in range(2)],
                                axis=0)
        o_ref[:, h2 * LANES:(h2 + 1) * LANES] = out_t.T.astype(BF16)


def _mla(q, k, vt, batch, seq):
    T = q.shape[0]
    tq = MLA_TQ
    nq = seq // tq
    hps = MLA_HPS
    return pl.pallas_call(
        _mla_kernel,
        grid=(batch, MLA_HEADS // hps, nq),
        in_specs=[pl.BlockSpec((tq, hps * LANES), lambda b, j, i: (b * nq + i, j)),
                  pl.BlockSpec((seq, hps * LANES), lambda b, j, i: (b, j)),
                  pl.BlockSpec((hps * MLA_V, seq), lambda b, j, i: (j, b))],
        out_specs=pl.BlockSpec((tq, hps * MLA_V), lambda b, j, i: (b * nq + i, j)),
        out_shape=jax.ShapeDtypeStruct((T, MLA_HEADS * MLA_V), BF16),
        scratch_shapes=[pltpu.VMEM((MLA_V, tq), F32) for _ in range(hps)],
        compiler_params=pltpu.CompilerParams(
            dimension_semantics=("parallel", "parallel", "arbitrary"), vmem_limit_bytes=VMEM_LIMIT),
        name="mla",
    )(q, k, vt)


def _swa_kernel(sink_ref, q_ref, kc_ref, kp_ref, vtc_ref, vtp_ref, bias_ref, o_ref):
    scores = []
    for head in range(SWA_HEADS):
        pair, g, kvh = head // 2, head % 2, head // SWA_GROUP
        col = (2 * kvh + g) * LANES
        band = jnp.concatenate([kp_ref[:, col:col + LANES], kc_ref[:, col:col + LANES]], axis=0)
        scores.append(_dot_nt(band, q_ref[:, pair * LANES:(pair + 1) * LANES]))
    probs, denoms = [], []
    for head in range(SWA_HEADS):
        s = scores[head] + bias_ref[0, head]
        sink = sink_ref[head] * LOG2E
        m = jnp.maximum(jnp.max(s, axis=0, keepdims=True), sink)
        p = jnp.exp2(s - m)
        denoms.append(jnp.sum(p, axis=0, keepdims=True) + jnp.exp2(sink - m))
        probs.append(p.astype(BF16))
    outs = []
    for head in range(SWA_HEADS):
        rows = slice((head // SWA_GROUP) * SWA_HEAD_DIM, (head // SWA_GROUP + 1) * SWA_HEAD_DIM)
        v_band = jnp.concatenate([vtp_ref[rows, :], vtc_ref[rows, :]], axis=1)
        outs.append(_dot(v_band, probs[head]) / denoms[head])
    o_ref[...] = jnp.concatenate(outs, axis=0).T.astype(BF16)


def _swa_bias():
    W = SWA_WINDOW
    j = jnp.arange(2 * W, dtype=jnp.int32)[:, None]
    i = jnp.arange(W, dtype=jnp.int32)[None, :]
    dist = i + W - j
    valid = (dist >= 0) & (dist < W)
    slopes = 2.0 ** (-8.0 * jnp.arange(1, SWA_HEADS + 1, dtype=F32) / SWA_HEADS)
    pen = -(slopes[:, None, None] * dist.astype(F32)[None]) * LOG2E
    general = jnp.where(valid[None], pen, NEG_INF)
    first = jnp.where((valid & (j >= W))[None], pen, NEG_INF)
    return jnp.stack([first, general])


def _swa(sinks, qs, ks, vst, batch, seq):
    T = qs.shape[0]
    W = SWA_WINDOW
    nb = seq // W
    cur = lambda n: pl.BlockSpec((W, n), lambda b, i: (b * nb + i, 0))
    prev = lambda n: pl.BlockSpec((W, n), lambda b, i: (b * nb + jnp.maximum(i - 1, 0), 0))
    vt_cur = pl.BlockSpec((SWA_KV_HEADS * SWA_HEAD_DIM, W), lambda b, i: (0, b * nb + i))
    vt_prev = pl.BlockSpec((SWA_KV_HEADS * SWA_HEAD_DIM, W), lambda b, i: (0, b * nb + jnp.maximum(i - 1, 0)))
    bias = pl.BlockSpec((1, SWA_HEADS, 2 * W, W), lambda b, i: (jnp.minimum(i, 1), 0, 0, 0))
    return pl.pallas_call(
        _swa_kernel,
        grid=(batch, nb),
        in_specs=[pl.BlockSpec(memory_space=pltpu.SMEM), cur(512), cur(512), prev(512),
                  vt_cur, vt_prev, bias],
        out_specs=cur(512),
        out_shape=jax.ShapeDtypeStruct((T, 512), BF16),
        compiler_params=pltpu.CompilerParams(dimension_semantics=("parallel", "parallel"),
                                             vmem_limit_bytes=VMEM_LIMIT),
        name="swa",
    )(sinks, qs, ks, ks, vst, vst, _swa_bias())


def _post_kernel(x_ref, om_ref, os_ref, wom_ref, wos_ref, wg_ref, wout_ref, g1_ref, b1_ref,
                 wr_ref, rb_ref, tri_ref,
                 h1_ref, h1p_ref, e_ref, w_ref, r_ref, cnt_ref, carry_scr):
    tm = x_ref.shape[0]
    step = pl.program_id(0)

    @pl.when(step == 0)
    def _():
        carry_scr[...] = jnp.zeros(carry_scr.shape, F32)

    x = x_ref[...]
    ya = _dot(om_ref[...], wom_ref[...])
    yb = _dot(os_ref[...], wos_ref[...])
    gates = _dot(x.astype(BF16), wg_ref[...])
    merged = _sigmoid(gates[:, :D_MODEL]) * ya + _sigmoid(gates[:, D_MODEL:]) * yb
    mix = _dot(merged.astype(BF16), wout_ref[...])
    z = DEEPNORM_ALPHA * x + mix
    mu = jnp.mean(z, axis=-1, keepdims=True)
    zc = z - mu
    var = jnp.mean(zc * zc, axis=-1, keepdims=True)
    h1 = zc * lax.rsqrt(var + LN_EPS) * g1_ref[...] + b1_ref[...]
    h1_ref[...] = h1
    _pack_rows(h1, h1p_ref)

    scores = _sigmoid(_dot_nt(wr_ref[...], h1.astype(BF16)))
    choice = scores + rb_ref[...]
    row = lax.broadcasted_iota(jnp.int32, (N_EXPERTS, tm), 0)
    grow = lax.broadcasted_iota(jnp.int32, (GROUP_SIZE, tm), 0)
    gscore = []
    for g in range(N_GROUPS):
        blk = choice[g * GROUP_SIZE:(g + 1) * GROUP_SIZE, :]
        m1 = jnp.max(blk, axis=0, keepdims=True)
        i1 = jnp.min(jnp.where(blk == m1, grow, GROUP_SIZE), axis=0, keepdims=True)
        m2 = jnp.max(jnp.where(grow == i1, NEG_INF, blk), axis=0, keepdims=True)
        gscore.append(m1 + m2)
    gsc = jnp.concatenate(gscore, axis=0)
    gidx = lax.broadcasted_iota(jnp.int32, (N_GROUPS, tm), 0)
    grank = jnp.zeros((N_GROUPS, tm), jnp.int32)
    for g in range(N_GROUPS):
        sg = gsc[g:g + 1, :]
        beats = (sg > gsc) | ((sg == gsc) & (gidx > g))
        grank = grank + beats.astype(jnp.int32)
    gsel = (grank < TOPK_GROUPS).astype(F32)
    emask = jnp.concatenate(
        [jnp.broadcast_to(gsel[g:g + 1, :], (GROUP_SIZE, tm)) for g in range(N_GROUPS)], axis=0)
    work = jnp.where(emask > 0.0, choice, NEG_INF)
    sel = jnp.zeros((N_EXPERTS, tm), F32)
    idxs, svals = [], []
    for _k in range(TOP_K):
        m = jnp.max(work, axis=0, keepdims=True)
        idx = jnp.min(jnp.where(work == m, row, N_EXPERTS), axis=0, keepdims=True)
        hit = row == idx
        svals.append(jnp.sum(jnp.where(hit, scores, 0.0), axis=0, keepdims=True))
        work = jnp.where(hit, NEG_INF, work)
        sel = jnp.where(hit, 1.0, sel)
        idxs.append(idx)
    ssum = svals[0]
    for sv in svals[1:]:
        ssum = ssum + sv
    e_ref[...] = jnp.concatenate(idxs, axis=0)
    w_ref[...] = jnp.concatenate([sv / ssum * ROUTED_SCALE for sv in svals], axis=0)

    carry = carry_scr[...]
    rank = _dot(sel.astype(BF16), tri_ref[...]) + carry[:, 0:1]
    r_ref[...] = jnp.concatenate(
        [jnp.sum(jnp.where(row == idx, rank, 0.0), axis=0, keepdims=True) for idx in idxs],
        axis=0).astype(jnp.int32)
    carry = carry + jnp.sum(sel, axis=1, keepdims=True)
    carry_scr[...] = carry
    cnt_ref[...] = carry.astype(jnp.int32)


def _post(x2, o_mla, o_swa, wom, wos, wg, wout, g1, b1, wr_t, rbias, tri):
    T = x2.shape[0]
    tm = POST_TM
    full = lambda a: pl.BlockSpec(a.shape, lambda i: (0, 0))
    row = lambda n: pl.BlockSpec((tm, n), lambda i: (i, 0))
    col = pl.BlockSpec((TOP_K, tm), lambda i: (0, i))
    return pl.pallas_call(
        _post_kernel,
        grid=(T // tm,),
        in_specs=[row(D_MODEL), row(512), row(512), full(wom), full(wos), full(wg), full(wout),
                  full(g1), full(b1), full(wr_t), full(rbias), full(tri)],
        out_specs=[row(D_MODEL), pl.BlockSpec((tm * ROW_SUB, LANES), lambda i: (i, 0)), col, col, col,
                   pl.BlockSpec((N_EXPERTS, LANES), lambda i: (0, 0))],
        out_shape=[jax.ShapeDtypeStruct((T, D_MODEL), F32),
                   jax.ShapeDtypeStruct((T * ROW_SUB, LANES), jnp.uint32),
                   jax.ShapeDtypeStruct((TOP_K, T), jnp.int32),
                   jax.ShapeDtypeStruct((TOP_K, T), F32),
                   jax.ShapeDtypeStruct((TOP_K, T), jnp.int32),
                   jax.ShapeDtypeStruct((N_EXPERTS, LANES), jnp.int32)],
        scratch_shapes=[pltpu.VMEM((N_EXPERTS, LANES), F32)],
        compiler_params=pltpu.CompilerParams(dimension_semantics=("arbitrary",),
                                             vmem_limit_bytes=VMEM_LIMIT),
        name="post",
    )(x2, o_mla, o_swa, wom, wos, wg, wout, g1, b1, wr_t, rbias, tri)


def _dest_kernel(start_ref, e_ref, r_ref, o_ref):
    e = e_ref[...]
    base = jnp.zeros(e.shape, jnp.int32)
    for j in range(N_EXPERTS):
        base = jnp.where(e == j, start_ref[j], base)
    o_ref[...] = base + r_ref[...]


def _dest(seg_start, e_t, r_t):
    T = e_t.shape[1]
    tn = min(T, 4096)
    col = pl.BlockSpec((TOP_K, tn), lambda i: (0, i))
    return pl.pallas_call(
        _dest_kernel,
        grid=(T // tn,),
        in_specs=[pl.BlockSpec(memory_space=pltpu.SMEM), col, col],
        out_specs=col,
        out_shape=jax.ShapeDtypeStruct((TOP_K, T), jnp.int32),
        compiler_params=pltpu.CompilerParams(dimension_semantics=("parallel",)),
        name="dest",
    )(seg_start, e_t, r_t)


def _sc_worker_chunks(n_tokens):
    info = plsc.get_sparse_core_info()
    n_workers = info.num_cores * info.num_subcores
    per_worker = n_tokens // SC_CHUNK // n_workers
    assert per_worker * n_workers * SC_CHUNK == n_tokens
    first = (lax.axis_index("s") * info.num_cores + lax.axis_index("c")) * per_worker
    return first, per_worker


def _sc_dispatch(h1p, idx, n_rows):
    n_tokens = h1p.shape[0] // ROW_SUB
    R = SC_CHUNK * ROW_SUB

    def body(h_hbm, idx_hbm, xs_hbm, idx_v, rows_v, sem):
        first, per_worker = _sc_worker_chunks(n_tokens)

        @pl.loop(0, per_worker)
        def _(ci):
            c = first + ci
            pltpu.sync_copy(idx_hbm.at[c], idx_v)
            pltpu.sync_copy(h_hbm.at[pl.ds(c * R, R)], rows_v)
            copies = [pltpu.make_async_copy(rows_v, xs_hbm.at[idx_v.at[k]], sem) for k in range(TOP_K)]
            for cp in copies:
                cp.start()
            for cp in copies:
                cp.wait()

    run = pl.kernel(
        body, out_type=jax.ShapeDtypeStruct((n_rows * ROW_SUB, LANES), jnp.uint32),
        mesh=plsc.VectorSubcoreMesh(core_axis_name="c", subcore_axis_name="s"),
        scratch_types=[pltpu.VMEM((TOP_K, R), jnp.int32), pltpu.VMEM((R, LANES), jnp.uint32),
                       pltpu.SemaphoreType.DMA],
        name="sc_dispatch")
    return run(h1p, idx)


def _sc_gather(ys, idx, n_tokens):
    R = SC_CHUNK * ROW_SUB
    half = TOP_K // 2

    def body(ys_hbm, idx_hbm, yg_hbm, idx_v, buf, gsem, wsem):
        first, per_worker = _sc_worker_chunks(n_tokens)

        @pl.loop(0, per_worker)
        def _(ci):
            c = first + ci
            pltpu.sync_copy(idx_hbm.at[c], idx_v)
            for k0 in (0, half):
                gathers = [pltpu.make_async_copy(ys_hbm.at[idx_v.at[k0 + k]], buf.at[k], gsem)
                           for k in range(half)]
                for cp in gathers:
                    cp.start()
                for cp in gathers:
                    cp.wait()
                writes = [pltpu.make_async_copy(buf.at[k], yg_hbm.at[k0 + k, pl.ds(c * R, R)], wsem)
                          for k in range(half)]
                for cp in writes:
                    cp.start()
                for cp in writes:
                    cp.wait()

    run = pl.kernel(
        body, out_type=jax.ShapeDtypeStruct((TOP_K, n_tokens * ROW_SUB, LANES), jnp.uint32),
        mesh=plsc.VectorSubcoreMesh(core_axis_name="c", subcore_axis_name="s"),
        scratch_types=[pltpu.VMEM((TOP_K, R), jnp.int32), pltpu.VMEM((half, R, LANES), jnp.uint32),
                       pltpu.SemaphoreType.DMA, pltpu.SemaphoreType.DMA],
        name="sc_gather")
    return run(ys, idx)


def _experts_kernel(first_ref, nblk_ref, cnt_ref, xs_hbm, wup_ref, wdn_ref, ys_hbm,
                    wup_bf, wdn_bf, xbuf, ybuf, in_sem, out_sem):
    e = pl.program_id(0)
    n = nblk_ref[e]
    b0 = first_ref[e]
    count = cnt_ref[e]
    total = first_ref[N_EXPERTS - 1] + nblk_ref[N_EXPERTS - 1]
    rb = ROW_BLK * ROW_SUB
    depth = EXPERT_SLOTS

    def slot_of(b):
        return lax.rem(b, depth)

    def rows_of(b):
        return pl.ds(pl.multiple_of(b * rb, rb), rb)

    def in_copy(b):
        return pltpu.make_async_copy(xs_hbm.at[rows_of(b)], xbuf.at[slot_of(b)], in_sem.at[slot_of(b)])

    def out_copy(b):
        return pltpu.make_async_copy(ybuf.at[slot_of(b)], ys_hbm.at[rows_of(b)], out_sem.at[slot_of(b)])

    @pl.when(e == 0)
    def _():
        for b in range(depth - 1):
            @pl.when(b < total)
            def _():
                in_copy(b).start()

    @pl.when(n > 0)
    def _():
        wup_bf[...] = wup_ref[0].astype(BF16)
        wdn_bf[...] = wdn_ref[0].astype(BF16)

        def body(b, carry):
            slot = slot_of(b)
            in_copy(b).wait()

            @pl.when(b + depth - 1 < total)
            def _():
                in_copy(b + depth - 1).start()

            @pl.when(b >= depth)
            def _():
                out_copy(b - depth).wait()

            chunks = _unpack_rows(xbuf.at[slot], ROW_BLK, n_valid=count - (b - b0) * ROW_BLK)
            xb = jnp.concatenate([c.astype(BF16) for c in chunks], axis=1)
            gu = _dot(xb, wup_bf[...])
            g = gu[:, :EXPERT_HIDDEN]
            hid = g * _sigmoid(g) * gu[:, EXPERT_HIDDEN:]
            _pack_rows(_dot(hid.astype(BF16), wdn_bf[...]), ybuf.at[slot])
            out_copy(b).start()
            return carry

        lax.fori_loop(b0, b0 + n, body, 0)

    @pl.when(e == N_EXPERTS - 1)
    def _():
        for back in range(depth, 0, -1):
            @pl.when(total - back >= 0)
            def _():
                out_copy(total - back).wait()


def _experts(first_blk, n_blk, counts, xs, w_exp_up, w_exp_down):
    rb = ROW_BLK * ROW_SUB
    grid_spec = pltpu.PrefetchScalarGridSpec(
        num_scalar_prefetch=3,
        grid=(N_EXPERTS,),
        in_specs=[pl.BlockSpec(memory_space=pl.ANY),
                  pl.BlockSpec((1, D_MODEL, 2 * EXPERT_HIDDEN), lambda e, fb, nb, ct: (e, 0, 0)),
                  pl.BlockSpec((1, EXPERT_HIDDEN, D_MODEL), lambda e, fb, nb, ct: (e, 0, 0))],
        out_specs=pl.BlockSpec(memory_space=pl.ANY),
        scratch_shapes=[pltpu.VMEM((D_MODEL, 2 * EXPERT_HIDDEN), BF16),
                        pltpu.VMEM((EXPERT_HIDDEN, D_MODEL), BF16),
                        pltpu.VMEM((EXPERT_SLOTS, rb, LANES), jnp.uint32),
                        pltpu.VMEM((EXPERT_SLOTS, rb, LANES), jnp.uint32),
                        pltpu.SemaphoreType.DMA((EXPERT_SLOTS,)),
                        pltpu.SemaphoreType.DMA((EXPERT_SLOTS,))],
    )
    return pl.pallas_call(
        _experts_kernel,
        grid_spec=grid_spec,
        out_shape=jax.ShapeDtypeStruct(xs.shape, jnp.uint32),
        compiler_params=pltpu.CompilerParams(dimension_semantics=("arbitrary",),
                                             vmem_limit_bytes=VMEM_LIMIT),
        name="experts",
    )(first_blk, n_blk, counts, xs, w_exp_up, w_exp_down)


def _combine_kernel(w_ref, h1_ref, yg_ref, wsu_ref, wsd_ref, g2_ref, b2_ref, o_ref):
    tm = h1_ref.shape[0]
    h1 = h1_ref[...]
    gu = _dot(h1.astype(BF16), wsu_ref[...])
    g = gu[:, :SHARED_HIDDEN]
    hid = g * _sigmoid(g) * gu[:, SHARED_HIDDEN:]
    ffn = _dot(hid.astype(BF16), wsd_ref[...])

    w = w_ref[...]
    acc = [None] * (2 * ROW_SUB)
    for k in range(TOP_K):
        wk = w[:, k:k + 1]
        chunks = _unpack_rows(yg_ref.at[k], tm)
        for c in range(2 * ROW_SUB):
            acc[c] = wk * chunks[c] if acc[c] is None else acc[c] + wk * chunks[c]
    routed = jnp.concatenate(acc, axis=1)
    z = DEEPNORM_ALPHA * h1 + (routed + ffn)
    mu = jnp.mean(z, axis=-1, keepdims=True)
    zc = z - mu
    var = jnp.mean(zc * zc, axis=-1, keepdims=True)
    o_ref[...] = zc * lax.rsqrt(var + LN_EPS) * g2_ref[...] + b2_ref[...]


def _combine(w_tok, h1, yg, wsu, wsd, g2, b2):
    T = h1.shape[0]
    tm = COMBINE_TM
    full = lambda a: pl.BlockSpec(a.shape, lambda i: (0, 0))
    return pl.pallas_call(
        _combine_kernel,
        grid=(T // tm,),
        in_specs=[pl.BlockSpec((tm, TOP_K), lambda i: (i, 0)),
                  pl.BlockSpec((tm, D_MODEL), lambda i: (i, 0)),
                  pl.BlockSpec((TOP_K, tm * ROW_SUB, LANES), lambda i: (0, i, 0)),
                  full(wsu), full(wsd), full(g2), full(b2)],
        out_specs=pl.BlockSpec((tm, D_MODEL), lambda i: (i, 0)),
        out_shape=jax.ShapeDtypeStruct((T, D_MODEL), F32),
        compiler_params=pltpu.CompilerParams(dimension_semantics=("parallel",),
                                             vmem_limit_bytes=VMEM_LIMIT),
        name="combine",
    )(w_tok, h1, yg, wsu, wsd, g2, b2)


def _prep_weights(w_in, w_uq, w_ukv, seq):
    z = lambda n: jnp.zeros((D_MODEL, n), F32)
    kr = w_in[:, 512:544]
    qs = w_in[:, 544:1056] * (LOG2E / math.sqrt(SWA_HEAD_DIM))
    ks0, ks1 = w_in[:, 1056:1120], w_in[:, 1120:1184]
    half = MLA_ROPE // 2
    w1 = jnp.concatenate([
        w_in[:, 0:512], qs,
        ks0, z(64), z(64), ks0, ks1, z(64), z(64), ks1,
        z(64), kr, z(32),
        z(64), -kr[:, half:], kr[:, :half], z(32)], axis=1).astype(BF16)
    wvst = w_in[:, 1184:1312].T.astype(BF16)
    wg = w_in[:, 1312:3360].astype(BF16)

    zq = lambda n: jnp.zeros((MLA_Q_LORA, n), F32)
    qd = MLA_NOPE + MLA_ROPE
    q_cols, r_cols, k_cols, v_cols = [], [], [], []
    for h in range(MLA_HEADS):
        wq = w_uq[:, h * qd:(h + 1) * qd]
        q_cols += [wq, zq(32)]
        r_cols += [zq(64), -wq[:, MLA_NOPE + half:], wq[:, MLA_NOPE:MLA_NOPE + half], zq(32)]
        wk = w_ukv[:, h * 128:(h + 1) * 128]
        k_cols += [wk[:, :MLA_NOPE], zq(64)]
        v_cols += [wk[:, MLA_NOPE:]]
    wq2 = jnp.concatenate(q_cols + r_cols, axis=1).astype(BF16)
    wk = jnp.concatenate(k_cols, axis=1).astype(BF16)
    wvt = jnp.concatenate(v_cols, axis=1).T.astype(BF16)

    inv_freq = ROPE_THETA ** (-jnp.arange(0, MLA_ROPE, 2, dtype=F32) / MLA_ROPE)
    ang = jnp.arange(seq, dtype=F32)[:, None] * inv_freq[None, :]
    cos, sin = jnp.cos(ang), jnp.sin(ang)
    one, zero = jnp.ones((seq, 64), F32), jnp.zeros((seq, 64), F32)
    z32 = jnp.zeros((seq, 32), F32)
    scale = LOG2E / math.sqrt(MLA_NOPE + MLA_ROPE)
    tabs = (jnp.concatenate([one, cos, cos, z32], axis=1) * scale,
            jnp.concatenate([zero, sin, sin, z32], axis=1) * scale,
            jnp.concatenate([zero, cos, cos, z32], axis=1),
            jnp.concatenate([zero, sin, sin, z32], axis=1))
    return w1, wvst, wg, wq2, wk, wvt, tabs


def kernel(x, w_in, mla_q_norm, mla_kv_norm, w_uq, w_ukv, attn_sinks, w_o_mla, w_o_swa, w_out,
           ln1_g, ln1_b, w_router, router_bias, w_exp_up, w_exp_down, w_sh_up, w_sh_down,
           ln2_g, ln2_b):
    batch, seq, _ = x.shape
    T = batch * seq
    x2 = x.reshape(T, D_MODEL)
    w1, wvst, wg, wq2, wk, wvt, tabs = _prep_weights(w_in, w_uq, w_ukv, seq)
    q, k, vt, qs, ks, vst = _proj(x2, w1, wvst, wq2, wk, wvt, mla_q_norm.reshape(1, -1),
                                  mla_kv_norm.reshape(1, -1), tabs, seq)
    o_mla = _mla(q, k, vt, batch, seq)
    o_swa = _swa(attn_sinks.astype(F32), qs, ks, vst, batch, seq)

    tri = (lax.broadcasted_iota(jnp.int32, (POST_TM, POST_TM), 0)
           < lax.broadcasted_iota(jnp.int32, (POST_TM, POST_TM), 1)).astype(BF16)
    h1, h1p, e_t, w_t, r_t, cnt = _post(
        x2, o_mla, o_swa, w_o_mla.astype(BF16), w_o_swa.astype(BF16), wg, w_out.astype(BF16),
        ln1_g.reshape(1, -1), ln1_b.reshape(1, -1), w_router.T.astype(BF16),
        router_bias.reshape(-1, 1).astype(F32), tri)

    counts = cnt[:, 0]
    pad = (counts + ROW_BLK - 1) // ROW_BLK * ROW_BLK
    pad_end = jnp.cumsum(pad)
    pad_start = pad_end - pad
    n_rows = (T * TOP_K // ROW_BLK + N_EXPERTS) * ROW_BLK
    dest = _dest(pad_start.astype(jnp.int32), e_t, r_t)
    idx = (dest.reshape(TOP_K, T // SC_CHUNK, SC_CHUNK, 1) * ROW_SUB
           + jnp.arange(ROW_SUB, dtype=jnp.int32))
    idx = idx.transpose(1, 0, 2, 3).reshape(T // SC_CHUNK, TOP_K, SC_CHUNK * ROW_SUB)
    xs = _sc_dispatch(h1p, idx, n_rows)
    ys = _experts((pad_start // ROW_BLK).astype(jnp.int32), (pad // ROW_BLK).astype(jnp.int32),
                  counts.astype(jnp.int32), xs, w_exp_up, w_exp_down)
    yg = _sc_gather(ys, idx, T)
    out = _combine(w_t.T, h1, yg, w_sh_up.astype(BF16), w_sh_down.astype(BF16),
                   ln2_g.reshape(1, -1), ln2_b.reshape(1, -1))
    return out.reshape(batch, seq, D_MODEL)
```

```python
import math

import jax
import jax.numpy as jnp
from jax import lax
from jax.experimental import pallas as pl
from jax.experimental.pallas import tpu as pltpu
from jax.experimental.pallas import tpu_sc as plsc

D_MODEL = 1024
MLA_HEADS = 8
MLA_Q_LORA = 256
MLA_KV_LORA = 256
MLA_NOPE = 64
MLA_ROPE = 32
MLA_V = 64
ROPE_THETA = 10000.0
SWA_HEADS = 8
SWA_KV_HEADS = 2
SWA_GROUP = SWA_HEADS // SWA_KV_HEADS
SWA_HEAD_DIM = 64
SWA_WINDOW = 128
N_EXPERTS = 256
TOP_K = 8
N_GROUPS = 8
GROUP_SIZE = N_EXPERTS // N_GROUPS
TOPK_GROUPS = 4
EXPERT_HIDDEN = 256
SHARED_HIDDEN = 256
ROUTED_SCALE = 2.5
DEEPNORM_ALPHA = 2.0 ** 0.25
LN_EPS = 1e-5
RMS_EPS = 1e-6

LANES = 128
ROW_WORDS = D_MODEL // 2
ROW_SUB = ROW_WORDS // LANES
VMEM_LIMIT = 48 * 1024 * 1024

PROJ_TM = 512
MLA_TQ = 256
MLA_TK = 256
MLA_HPS = 8
POST_TM = 512
ROW_BLK = 256
EXPERT_SLOTS = 4
EXPERT_SPLIT = 2
SHARED_TM = 512
COMBINE_TM = 256
SC_CHUNK = 32
MOE_PARTS = 2

BF16 = jnp.bfloat16
F32 = jnp.float32
NEG_INF = float("-inf")
LOG2E = math.log2(math.e)


def _sigmoid(v):
    return 1.0 / (1.0 + jnp.exp(-v))


def _dot(a, b):
    return jnp.dot(a, b, preferred_element_type=F32)


def _dot_nt(a, b):
    return lax.dot_general(a, b, (((1,), (1,)), ((), ())), preferred_element_type=F32)


def _pack_rows(y, out_ref, first=0):
    rows = y.shape[0]
    for j in range(ROW_SUB):
        a = y[:, j * LANES:(j + 1) * LANES].astype(BF16).astype(F32)
        b = y[:, ROW_WORDS + j * LANES:ROW_WORDS + (j + 1) * LANES].astype(BF16).astype(F32)
        ua = pltpu.bitcast(a, jnp.uint32) >> 16
        ub = pltpu.bitcast(b, jnp.uint32)
        out_ref[pl.ds(first * ROW_SUB + j, rows, stride=ROW_SUB), :] = ua | ub


def _unpack_rows(ref, rows, first=0, n_valid=None):
    lo, hi = [], []
    if n_valid is not None:
        live = first + lax.broadcasted_iota(jnp.int32, (rows, LANES), 0) < n_valid
    for j in range(ROW_SUB):
        u = ref[pl.ds(first * ROW_SUB + j, rows, stride=ROW_SUB), :]
        if n_valid is not None:
            u = jnp.where(live, u, jnp.uint32(0))
        lo.append(pltpu.bitcast(u << 16, F32))
        hi.append(pltpu.bitcast(u & jnp.uint32(0xFFFF0000), F32))
    return lo + hi


def _proj_kernel(x_ref, w1_ref, wvst_ref, wq_ref, wk_ref, wvt_ref, gq_ref, gkv_ref,
                 cq_ref, sq_ref, ck_ref, sk_ref,
                 q_ref, k_ref, vt_ref, qs_ref, ks_ref, vst_ref):
    xb = x_ref[...].astype(BF16)
    p = _dot(xb, w1_ref[...])

    def rms(c, g):
        return c * lax.rsqrt(jnp.mean(c * c, axis=-1, keepdims=True) + RMS_EPS) * g

    cqn = rms(p[:, 0:256], gq_ref[...]).astype(BF16)
    ckvn = rms(p[:, 256:512], gkv_ref[...]).astype(BF16)
    qs_ref[...] = p[:, 512:1024].astype(BF16)
    ks_ref[...] = p[:, 1024:1536].astype(BF16)
    vst_ref[...] = _dot_nt(wvst_ref[...], xb).astype(BF16)
    kr = p[:, 1536:1664] * ck_ref[...] + p[:, 1664:1792] * sk_ref[...]
    qq = _dot(cqn, wq_ref[...])
    kn = _dot(ckvn, wk_ref[...])
    cq = cq_ref[...]
    sq = sq_ref[...]
    for h in range(MLA_HEADS):
        a = qq[:, h * LANES:(h + 1) * LANES]
        b = qq[:, 1024 + h * LANES:1024 + (h + 1) * LANES]
        q_ref[:, h * LANES:(h + 1) * LANES] = (a * cq + b * sq).astype(BF16)
        k_ref[:, h * LANES:(h + 1) * LANES] = (kn[:, h * LANES:(h + 1) * LANES] + kr).astype(BF16)
    vt_ref[...] = _dot_nt(wvt_ref[...], ckvn).astype(BF16)


def _proj(x2, w1, wvst, wq2, wk, wvt, gq, gkv, tabs, seq):
    T = x2.shape[0]
    tm = PROJ_TM
    nper = seq // tm
    full = lambda shape: pl.BlockSpec(shape, lambda i: (0, 0))
    tab = pl.BlockSpec((tm, LANES), lambda i: (i % nper, 0))
    row = lambda n: pl.BlockSpec((tm, n), lambda i: (i, 0))
    col = lambda n: pl.BlockSpec((n, tm), lambda i: (0, i))
    return pl.pallas_call(
        _proj_kernel,
        grid=(T // tm,),
        in_specs=[row(D_MODEL), full(w1.shape), full(wvst.shape), full(wq2.shape), full(wk.shape),
                  full(wvt.shape), full(gq.shape), full(gkv.shape), tab, tab, tab, tab],
        out_specs=[row(1024), row(1024), col(512), row(512), row(512), col(128)],
        out_shape=[jax.ShapeDtypeStruct((T, 1024), BF16), jax.ShapeDtypeStruct((T, 1024), BF16),
                   jax.ShapeDtypeStruct((512, T), BF16), jax.ShapeDtypeStruct((T, 512), BF16),
                   jax.ShapeDtypeStruct((T, 512), BF16), jax.ShapeDtypeStruct((128, T), BF16)],
        compiler_params=pltpu.CompilerParams(dimension_semantics=("parallel",),
                                             vmem_limit_bytes=VMEM_LIMIT),
        name="proj",
    )(x2, w1, wvst, wq2, wk, wvt, gq, gkv, *tabs)


def _mla_kernel(q_ref, k_ref, vt_ref, o_ref, *acc_scr):
    tq = q_ref.shape[0]
    qi = pl.program_id(2)
    for acc in acc_scr:
        acc[...] = jnp.zeros(acc.shape, F32)

    tk = MLA_TK

    def step(kc, stats, masked):
        ks = pl.multiple_of(kc * tk, tk)
        scores = [_dot_nt(k_ref[pl.ds(ks, tk), h * LANES:(h + 1) * LANES],
                          q_ref[:, h * LANES:(h + 1) * LANES]) for h in range(MLA_HPS)]
        new_stats, probs, alphas = [], [], []
        for h in range(MLA_HPS):
            m_prev, l_prev = stats[h]
            s = scores[h]
            if masked:
                key = ks + lax.broadcasted_iota(jnp.int32, s.shape, 0)
                qry = qi * tq + lax.broadcasted_iota(jnp.int32, s.shape, 1)
                s = jnp.where(key <= qry, s, NEG_INF)
            m_new = jnp.maximum(m_prev, jnp.max(s, axis=0, keepdims=True))
            alpha = jnp.exp2(m_prev - m_new)
            p = jnp.exp2(s - m_new)
            new_stats.append((m_new, alpha * l_prev + jnp.sum(p, axis=0, keepdims=True)))
            probs.append(p.astype(BF16))
            alphas.append(alpha)
        for h in range(MLA_HPS):
            pv = _dot(vt_ref[h * MLA_V:(h + 1) * MLA_V, pl.ds(ks, tk)], probs[h])
            acc_scr[h][...] = acc_scr[h][...] * alphas[h] + pv
        return tuple(new_stats)

    init = tuple((jnp.full((1, tq), NEG_INF, F32), jnp.zeros((1, tq), F32)) for _ in range(MLA_HPS))
    n_full = qi * (tq // tk)
    stats = lax.fori_loop(0, n_full, lambda kc, st: step(kc, st, False), init)
    for d in range(tq // tk):
        stats = step(n_full + d, stats, True)
    for h2 in range(MLA_HPS // 2):
        out_t = jnp.concatenate([acc_scr[2 * h2 + g][...] / stats[2 * h2 + g][1] for g in range(2)],
                                axis=0)
        o_ref[:, h2 * LANES:(h2 + 1) * LANES] = out_t.T.astype(BF16)


def _mla(q, k, vt, batch, seq):
    T = q.shape[0]
    tq = MLA_TQ
    nq = seq // tq
    hps = MLA_HPS
    return pl.pallas_call(
        _mla_kernel,
        grid=(batch, MLA_HEADS // hps, nq),
        in_specs=[pl.BlockSpec((tq, hps * LANES), lambda b, j, i: (b * nq + i, j)),
                  pl.BlockSpec((seq, hps * LANES), lambda b, j, i: (b, j)),
                  pl.BlockSpec((hps * MLA_V, seq), lambda b, j, i: (j, b))],
        out_specs=pl.BlockSpec((tq, hps * MLA_V), lambda b, j, i: (b * nq + i, j)),
        out_shape=jax.ShapeDtypeStruct((T, MLA_HEADS * MLA_V), BF16),
        scratch_shapes=[pltpu.VMEM((MLA_V, tq), F32) for _ in range(hps)],
        compiler_params=pltpu.CompilerParams(
            dimension_semantics=("parallel", "parallel", "arbitrary"), vmem_limit_bytes=VMEM_LIMIT),
        name="mla",
    )(q, k, vt)


def _swa_kernel(sink_ref, q_ref, kc_ref, kp_ref, vtc_ref, vtp_ref, bias_ref, o_ref):
    scores = []
    for head in range(SWA_HEADS):
        pair, g, kvh = head // 2, head % 2, head // SWA_GROUP
        col = (2 * kvh + g) * LANES
        band = jnp.concatenate([kp_ref[:, col:col + LANES], kc_ref[:, col:col + LANES]], axis=0)
        scores.append(_dot_nt(band, q_ref[:, pair * LANES:(pair + 1) * LANES]))
    probs, denoms = [], []
    for head in range(SWA_HEADS):
        s = scores[head] + bias_ref[0, head]
        sink = sink_ref[head] * LOG2E
        m = jnp.maximum(jnp.max(s, axis=0, keepdims=True), sink)
        p = jnp.exp2(s - m)
        denoms.append(jnp.sum(p, axis=0, keepdims=True) + jnp.exp2(sink - m))
        probs.append(p.astype(BF16))
    outs = []
    for head in range(SWA_HEADS):
        rows = slice((head // SWA_GROUP) * SWA_HEAD_DIM, (head // SWA_GROUP + 1) * SWA_HEAD_DIM)
        v_band = jnp.concatenate([vtp_ref[rows, :], vtc_ref[rows, :]], axis=1)
        outs.append(_dot(v_band, probs[head]) / denoms[head])
    o_ref[...] = jnp.concatenate(outs, axis=0).T.astype(BF16)


def _swa_bias():
    W = SWA_WINDOW
    j = jnp.arange(2 * W, dtype=jnp.int32)[:, None]
    i = jnp.arange(W, dtype=jnp.int32)[None, :]
    dist = i + W - j
    valid = (dist >= 0) & (dist < W)
    slopes = 2.0 ** (-8.0 * jnp.arange(1, SWA_HEADS + 1, dtype=F32) / SWA_HEADS)
    pen = -(slopes[:, None, None] * dist.astype(F32)[None]) * LOG2E
    general = jnp.where(valid[None], pen, NEG_INF)
    first = jnp.where((valid & (j >= W))[None], pen, NEG_INF)
    return jnp.stack([first, general])


def _swa(sinks, qs, ks, vst, batch, seq):
    T = qs.shape[0]
    W = SWA_WINDOW
    nb = seq // W
    cur = lambda n: pl.BlockSpec((W, n), lambda b, i: (b * nb + i, 0))
    prev = lambda n: pl.BlockSpec((W, n), lambda b, i: (b * nb + jnp.maximum(i - 1, 0), 0))
    vt_cur = pl.BlockSpec((SWA_KV_HEADS * SWA_HEAD_DIM, W), lambda b, i: (0, b * nb + i))
    vt_prev = pl.BlockSpec((SWA_KV_HEADS * SWA_HEAD_DIM, W), lambda b, i: (0, b * nb + jnp.maximum(i - 1, 0)))
    bias = pl.BlockSpec((1, SWA_HEADS, 2 * W, W), lambda b, i: (jnp.minimum(i, 1), 0, 0, 0))
    return pl.pallas_call(
        _swa_kernel,
        grid=(batch, nb),
        in_specs=[pl.BlockSpec(memory_space=pltpu.SMEM), cur(512), cur(512), prev(512),
                  vt_cur, vt_prev, bias],
        out_specs=cur(512),
        out_shape=jax.ShapeDtypeStruct((T, 512), BF16),
        compiler_params=pltpu.CompilerParams(dimension_semantics=("parallel", "parallel"),
                                             vmem_limit_bytes=VMEM_LIMIT),
        name="swa",
    )(sinks, qs, ks, ks, vst, vst, _swa_bias())


def _post_kernel(x_ref, om_ref, os_ref, wom_ref, wos_ref, wg_ref, wout_ref, g1_ref, b1_ref,
                 wr_ref, rb_ref, tri_ref,
                 h1_ref, h1p_ref, e_ref, w_ref, r_ref, cnt_ref, carry_scr):
    tm = x_ref.shape[0]
    step = pl.program_id(0)

    @pl.when(step == 0)
    def _():
        carry_scr[...] = jnp.zeros(carry_scr.shape, F32)

    x = x_ref[...]
    ya = _dot(om_ref[...], wom_ref[...])
    yb = _dot(os_ref[...], wos_ref[...])
    gates = _dot(x.astype(BF16), wg_ref[...])
    merged = _sigmoid(gates[:, :D_MODEL]) * ya + _sigmoid(gates[:, D_MODEL:]) * yb
    mix = _dot(merged.astype(BF16), wout_ref[...])
    z = DEEPNORM_ALPHA * x + mix
    mu = jnp.mean(z, axis=-1, keepdims=True)
    zc = z - mu
    var = jnp.mean(zc * zc, axis=-1, keepdims=True)
    h1 = zc * lax.rsqrt(var + LN_EPS) * g1_ref[...] + b1_ref[...]
    h1_ref[...] = h1
    _pack_rows(h1, h1p_ref)

    scores = _sigmoid(_dot_nt(wr_ref[...], h1.astype(BF16)))
    choice = scores + rb_ref[...]
    row = lax.broadcasted_iota(jnp.int32, (N_EXPERTS, tm), 0)
    grow = lax.broadcasted_iota(jnp.int32, (GROUP_SIZE, tm), 0)
    gscore = []
    for g in range(N_GROUPS):
        blk = choice[g * GROUP_SIZE:(g + 1) * GROUP_SIZE, :]
        m1 = jnp.max(blk, axis=0, keepdims=True)
        i1 = jnp.min(jnp.where(blk == m1, grow, GROUP_SIZE), axis=0, keepdims=True)
        m2 = jnp.max(jnp.where(grow == i1, NEG_INF, blk), axis=0, keepdims=True)
        gscore.append(m1 + m2)
    gsc = jnp.concatenate(gscore, axis=0)
    gidx = lax.broadcasted_iota(jnp.int32, (N_GROUPS, tm), 0)
    grank = jnp.zeros((N_GROUPS, tm), jnp.int32)
    for g in range(N_GROUPS):
        sg = gsc[g:g + 1, :]
        beats = (sg > gsc) | ((sg == gsc) & (gidx > g))
        grank = grank + beats.astype(jnp.int32)
    gsel = (grank < TOPK_GROUPS).astype(F32)
    emask = jnp.concatenate(
        [jnp.broadcast_to(gsel[g:g + 1, :], (GROUP_SIZE, tm)) for g in range(N_GROUPS)], axis=0)
    work = jnp.where(emask > 0.0, choice, NEG_INF)
    eligible = work
    idxs, svals = [], []
    for _k in range(TOP_K):
        m = jnp.max(work, axis=0, keepdims=True)
        idx = jnp.min(jnp.where(work == m, row, N_EXPERTS), axis=0, keepdims=True)
        hit = row == idx
        svals.append(jnp.sum(jnp.where(hit, scores, 0.0), axis=0, keepdims=True))
        work = jnp.where(hit, NEG_INF, work)
        idxs.append(idx)
    sel = jnp.where(work != eligible, 1.0, 0.0)
    ssum = svals[0]
    for sv in svals[1:]:
        ssum = ssum + sv
    e_ref[...] = jnp.concatenate(idxs, axis=0)
    w_ref[...] = jnp.concatenate([sv / ssum * ROUTED_SCALE for sv in svals], axis=0)

    carry = carry_scr[...]
    rank = _dot(sel.astype(BF16), tri_ref[...]) + carry[:, 0:1]
    r_ref[...] = jnp.concatenate(
        [jnp.sum(jnp.where(row == idx, rank, 0.0), axis=0, keepdims=True) for idx in idxs],
        axis=0).astype(jnp.int32)
    carry = carry + jnp.sum(sel, axis=1, keepdims=True)
    carry_scr[...] = carry
    cnt_ref[...] = carry.astype(jnp.int32)


def _post(x2, o_mla, o_swa, wom, wos, wg, wout, g1, b1, wr_t, rbias, tri, part):
    T = x2.shape[0] // MOE_PARTS
    tm = POST_TM
    first = part * (T // tm)
    full = lambda a: pl.BlockSpec(a.shape, lambda i: (0, 0))
    row_in = lambda n: pl.BlockSpec((tm, n), lambda i: (first + i, 0))
    row = lambda n: pl.BlockSpec((tm, n), lambda i: (i, 0))
    col = pl.BlockSpec((TOP_K, tm), lambda i: (0, i))
    return pl.pallas_call(
        _post_kernel,
        grid=(T // tm,),
        in_specs=[row_in(D_MODEL), row_in(512), row_in(512), full(wom), full(wos), full(wg), full(wout),
                  full(g1), full(b1), full(wr_t), full(rbias), full(tri)],
        out_specs=[row(D_MODEL), pl.BlockSpec((tm * ROW_SUB, LANES), lambda i: (i, 0)), col, col, col,
                   pl.BlockSpec((N_EXPERTS, LANES), lambda i: (0, 0))],
        out_shape=[jax.ShapeDtypeStruct((T, D_MODEL), F32),
                   jax.ShapeDtypeStruct((T * ROW_SUB, LANES), jnp.uint32),
                   jax.ShapeDtypeStruct((TOP_K, T), jnp.int32),
                   jax.ShapeDtypeStruct((TOP_K, T), F32),
                   jax.ShapeDtypeStruct((TOP_K, T), jnp.int32),
                   jax.ShapeDtypeStruct((N_EXPERTS, LANES), jnp.int32)],
        scratch_shapes=[pltpu.VMEM((N_EXPERTS, LANES), F32)],
        compiler_params=pltpu.CompilerParams(dimension_semantics=("arbitrary",),
                                             vmem_limit_bytes=VMEM_LIMIT),
        name="post",
    )(x2, o_mla, o_swa, wom, wos, wg, wout, g1, b1, wr_t, rbias, tri)


def _dest_kernel(start_ref, e_ref, r_ref, o_ref):
    e = e_ref[...]
    base = jnp.zeros(e.shape, jnp.int32)
    for j in range(N_EXPERTS):
        base = jnp.where(e == j, start_ref[j], base)
    o_ref[...] = base + r_ref[...]


def _dest(seg_start, e_t, r_t):
    T = e_t.shape[1]
    tn = min(T, 4096)
    col = pl.BlockSpec((TOP_K, tn), lambda i: (0, i))
    return pl.pallas_call(
        _dest_kernel,
        grid=(T // tn,),
        in_specs=[pl.BlockSpec(memory_space=pltpu.SMEM), col, col],
        out_specs=col,
        out_shape=jax.ShapeDtypeStruct((TOP_K, T), jnp.int32),
        compiler_params=pltpu.CompilerParams(dimension_semantics=("parallel",)),
        name="dest",
    )(seg_start, e_t, r_t)


def _sc_worker_chunks(n_tokens):
    info = plsc.get_sparse_core_info()
    n_workers = info.num_cores * info.num_subcores
    per_worker = n_tokens // SC_CHUNK // n_workers
    assert per_worker * n_workers * SC_CHUNK == n_tokens
    first = (lax.axis_index("s") * info.num_cores + lax.axis_index("c")) * per_worker
    return first, per_worker


def _sc_dispatch(h1p, idx, n_rows):
    n_tokens = h1p.shape[0] // ROW_SUB
    R = SC_CHUNK * ROW_SUB

    def body(h_hbm, idx_hbm, xs_hbm, idx_v, rows_v, sem):
        first, per_worker = _sc_worker_chunks(n_tokens)

        @pl.loop(0, per_worker)
        def _(ci):
            c = first + ci
            pltpu.sync_copy(idx_hbm.at[c], idx_v)
            pltpu.sync_copy(h_hbm.at[pl.ds(c * R, R)], rows_v)
            copies = [pltpu.make_async_copy(rows_v, xs_hbm.at[idx_v.at[k]], sem) for k in range(TOP_K)]
            for cp in copies:
                cp.start()
            for cp in copies:
                cp.wait()

    run = pl.kernel(
        body, out_type=jax.ShapeDtypeStruct((n_rows * ROW_SUB, LANES), jnp.uint32),
        mesh=plsc.VectorSubcoreMesh(core_axis_name="c", subcore_axis_name="s"),
        scratch_types=[pltpu.VMEM((TOP_K, R), jnp.int32), pltpu.VMEM((R, LANES), jnp.uint32),
                       pltpu.SemaphoreType.DMA],
        name="sc_dispatch")
    return run(h1p, idx)


def _sc_gather(ys, idx, n_tokens):
    R = SC_CHUNK * ROW_SUB
    half = TOP_K // 2

    def body(ys_hbm, idx_hbm, yg_hbm, idx_v, buf, gsem, wsem):
        first, per_worker = _sc_worker_chunks(n_tokens)

        @pl.loop(0, per_worker)
        def _(ci):
            c = first + ci
            pltpu.sync_copy(idx_hbm.at[c], idx_v)
            for k0 in (0, half):
                gathers = [pltpu.make_async_copy(ys_hbm.at[idx_v.at[k0 + k]], buf.at[k], gsem)
                           for k in range(half)]
                for cp in gathers:
                    cp.start()
                for cp in gathers:
                    cp.wait()
                writes = [pltpu.make_async_copy(buf.at[k], yg_hbm.at[k0 + k, pl.ds(c * R, R)], wsem)
                          for k in range(half)]
                for cp in writes:
                    cp.start()
                for cp in writes:
                    cp.wait()

    run = pl.kernel(
        body, out_type=jax.ShapeDtypeStruct((TOP_K, n_tokens * ROW_SUB, LANES), jnp.uint32),
        mesh=plsc.VectorSubcoreMesh(core_axis_name="c", subcore_axis_name="s"),
        scratch_types=[pltpu.VMEM((TOP_K, R), jnp.int32), pltpu.VMEM((half, R, LANES), jnp.uint32),
                       pltpu.SemaphoreType.DMA, pltpu.SemaphoreType.DMA],
        name="sc_gather")
    return run(ys, idx)


def _experts_kernel(first_ref, nblk_ref, cnt_ref, xs_hbm, wup_ref, wdn_ref, ys_hbm,
                    wup_bf, wdn_bf, xbuf, ybuf, in_sem, out_sem):
    e = pl.program_id(0)
    n = nblk_ref[e]
    b0 = first_ref[e]
    count = cnt_ref[e]
    total = first_ref[N_EXPERTS - 1] + nblk_ref[N_EXPERTS - 1]
    rb = ROW_BLK * ROW_SUB
    depth = EXPERT_SLOTS

    def slot_of(b):
        return lax.rem(b, depth)

    def rows_of(b):
        return pl.ds(pl.multiple_of(b * rb, rb), rb)

    def in_copy(b):
        return pltpu.make_async_copy(xs_hbm.at[rows_of(b)], xbuf.at[slot_of(b)], in_sem.at[slot_of(b)])

    def out_copy(b):
        return pltpu.make_async_copy(ybuf.at[slot_of(b)], ys_hbm.at[rows_of(b)], out_sem.at[slot_of(b)])

    @pl.when(e == 0)
    def _():
        for b in range(depth - 1):
            @pl.when(b < total)
            def _():
                in_copy(b).start()

    @pl.when(n > 0)
    def _():
        wup_bf[...] = wup_ref[0].astype(BF16)
        wdn_bf[...] = wdn_ref[0].astype(BF16)

        def body(b, carry):
            slot = slot_of(b)
            in_copy(b).wait()

            @pl.when(b + depth - 1 < total)
            def _():
                in_copy(b + depth - 1).start()

            @pl.when(b >= depth)
            def _():
                out_copy(b - depth).wait()

            sub = ROW_BLK // EXPERT_SPLIT
            n_valid = count - (b - b0) * ROW_BLK
            xbs = [jnp.concatenate(
                [c.astype(BF16) for c in _unpack_rows(xbuf.at[slot], sub, first=i * sub, n_valid=n_valid)],
                axis=1) for i in range(EXPERT_SPLIT)]
            gus = [_dot(xb, wup_bf[...]) for xb in xbs]
            hids = [(gu[:, :EXPERT_HIDDEN] * _sigmoid(gu[:, :EXPERT_HIDDEN])
                     * gu[:, EXPERT_HIDDEN:]).astype(BF16) for gu in gus]
            ys = [_dot(hid, wdn_bf[...]) for hid in hids]
            for i in range(EXPERT_SPLIT):
                _pack_rows(ys[i], ybuf.at[slot], first=i * sub)
            out_copy(b).start()
            return carry

        lax.fori_loop(b0, b0 + n, body, 0)

    @pl.when(e == N_EXPERTS - 1)
    def _():
        for back in range(depth, 0, -1):
            @pl.when(total - back >= 0)
            def _():
                out_copy(total - back).wait()


def _experts(first_blk, n_blk, counts, xs, w_exp_up, w_exp_down):
    rb = ROW_BLK * ROW_SUB
    grid_spec = pltpu.PrefetchScalarGridSpec(
        num_scalar_prefetch=3,
        grid=(N_EXPERTS,),
        in_specs=[pl.BlockSpec(memory_space=pl.ANY),
                  pl.BlockSpec((1, D_MODEL, 2 * EXPERT_HIDDEN), lambda e, fb, nb, ct: (e, 0, 0)),
                  pl.BlockSpec((1, EXPERT_HIDDEN, D_MODEL), lambda e, fb, nb, ct: (e, 0, 0))],
        out_specs=pl.BlockSpec(memory_space=pl.ANY),
        scratch_shapes=[pltpu.VMEM((D_MODEL, 2 * EXPERT_HIDDEN), BF16),
                        pltpu.VMEM((EXPERT_HIDDEN, D_MODEL), BF16),
                        pltpu.VMEM((EXPERT_SLOTS, rb, LANES), jnp.uint32),
                        pltpu.VMEM((EXPERT_SLOTS, rb, LANES), jnp.uint32),
                        pltpu.SemaphoreType.DMA((EXPERT_SLOTS,)),
                        pltpu.SemaphoreType.DMA((EXPERT_SLOTS,))],
    )
    return pl.pallas_call(
        _experts_kernel,
        grid_spec=grid_spec,
        out_shape=jax.ShapeDtypeStruct(xs.shape, jnp.uint32),
        compiler_params=pltpu.CompilerParams(dimension_semantics=("arbitrary",),
                                             vmem_limit_bytes=VMEM_LIMIT),
        name="experts",
    )(first_blk, n_blk, counts, xs, w_exp_up, w_exp_down)


def _shared_kernel(h1_ref, wsu_ref, wsd_ref, o_ref):
    gu = _dot(h1_ref[...].astype(BF16), wsu_ref[...])
    g = gu[:, :SHARED_HIDDEN]
    hid = g * _sigmoid(g) * gu[:, SHARED_HIDDEN:]
    o_ref[...] = _dot(hid.astype(BF16), wsd_ref[...]).astype(BF16)


def _shared(h1, wsu, wsd):
    T = h1.shape[0]
    tm = SHARED_TM
    full = lambda a: pl.BlockSpec(a.shape, lambda i: (0, 0))
    row = pl.BlockSpec((tm, D_MODEL), lambda i: (i, 0))
    return pl.pallas_call(
        _shared_kernel,
        grid=(T // tm,),
        in_specs=[row, full(wsu), full(wsd)],
        out_specs=row,
        out_shape=jax.ShapeDtypeStruct((T, D_MODEL), BF16),
        compiler_params=pltpu.CompilerParams(dimension_semantics=("parallel",),
                                             vmem_limit_bytes=VMEM_LIMIT),
        name="shared",
    )(h1, wsu, wsd)


def _combine_kernel(w_ref, h1_ref, sh_ref, yg_ref, g2_ref, b2_ref, *rest):
    o_ref = rest[-1]
    tm = h1_ref.shape[0]
    h1 = h1_ref[...]
    ffn = sh_ref[...].astype(F32)

    w = w_ref[...]
    acc = [None] * (2 * ROW_SUB)
    for k in range(TOP_K):
        wk = w[:, k:k + 1]
        chunks = _unpack_rows(yg_ref.at[k], tm)
        for c in range(2 * ROW_SUB):
            acc[c] = wk * chunks[c] if acc[c] is None else acc[c] + wk * chunks[c]
    routed = jnp.concatenate(acc, axis=1)
    z = DEEPNORM_ALPHA * h1 + (routed + ffn)
    mu = jnp.mean(z, axis=-1, keepdims=True)
    zc = z - mu
    var = jnp.mean(zc * zc, axis=-1, keepdims=True)
    o_ref[...] = zc * lax.rsqrt(var + LN_EPS) * g2_ref[...] + b2_ref[...]


def _combine(w_tok, h1, sh, yg, g2, b2, part, out_prev):
    T = h1.shape[0]
    tm = COMBINE_TM
    first = part * (T // tm)
    full = lambda a: pl.BlockSpec(a.shape, lambda i: (0, 0))
    row = pl.BlockSpec((tm, D_MODEL), lambda i: (i, 0))
    in_specs = [pl.BlockSpec((tm, TOP_K), lambda i: (i, 0)), row, row,
                pl.BlockSpec((TOP_K, tm * ROW_SUB, LANES), lambda i: (0, i, 0)),
                full(g2), full(b2)]
    args = [w_tok, h1, sh, yg, g2, b2]
    aliases = {}
    if out_prev is not None:
        in_specs.append(pl.BlockSpec(memory_space=pl.ANY))
        args.append(out_prev)
        aliases = {len(args) - 1: 0}
    return pl.pallas_call(
        _combine_kernel,
        grid=(T // tm,),
        in_specs=in_specs,
        out_specs=pl.BlockSpec((tm, D_MODEL), lambda i: (first + i, 0)),
        out_shape=jax.ShapeDtypeStruct((T * MOE_PARTS, D_MODEL), F32),
        input_output_aliases=aliases,
        compiler_params=pltpu.CompilerParams(dimension_semantics=("parallel",),
                                             vmem_limit_bytes=VMEM_LIMIT),
        name="combine",
    )(*args)


def _prep_weights(w_in, w_uq, w_ukv, seq):
    z = lambda n: jnp.zeros((D_MODEL, n), F32)
    kr = w_in[:, 512:544]
    qs = w_in[:, 544:1056] * (LOG2E / math.sqrt(SWA_HEAD_DIM))
    ks0, ks1 = w_in[:, 1056:1120], w_in[:, 1120:1184]
    half = MLA_ROPE // 2
    w1 = jnp.concatenate([
        w_in[:, 0:512], qs,
        ks0, z(64), z(64), ks0, ks1, z(64), z(64), ks1,
        z(64), kr, z(32),
        z(64), -kr[:, half:], kr[:, :half], z(32)], axis=1).astype(BF16)
    wvst = w_in[:, 1184:1312].T.astype(BF16)
    wg = w_in[:, 1312:3360].astype(BF16)

    zq = lambda n: jnp.zeros((MLA_Q_LORA, n), F32)
    qd = MLA_NOPE + MLA_ROPE
    q_cols, r_cols, k_cols, v_cols = [], [], [], []
    for h in range(MLA_HEADS):
        wq = w_uq[:, h * qd:(h + 1) * qd]
        q_cols += [wq, zq(32)]
        r_cols += [zq(64), -wq[:, MLA_NOPE + half:], wq[:, MLA_NOPE:MLA_NOPE + half], zq(32)]
        wk = w_ukv[:, h * 128:(h + 1) * 128]
        k_cols += [wk[:, :MLA_NOPE], zq(64)]
        v_cols += [wk[:, MLA_NOPE:]]
    wq2 = jnp.concatenate(q_cols + r_cols, axis=1).astype(BF16)
    wk = jnp.concatenate(k_cols, axis=1).astype(BF16)
    wvt = jnp.concatenate(v_cols, axis=1).T.astype(BF16)

    inv_freq = ROPE_THETA ** (-jnp.arange(0, MLA_ROPE, 2, dtype=F32) / MLA_ROPE)
    ang = jnp.arange(seq, dtype=F32)[:, None] * inv_freq[None, :]
    cos, sin = jnp.cos(ang), jnp.sin(ang)
    one, zero = jnp.ones((seq, 64), F32), jnp.zeros((seq, 64), F32)
    z32 = jnp.zeros((seq, 32), F32)
    scale = LOG2E / math.sqrt(MLA_NOPE + MLA_ROPE)
    tabs = (jnp.concatenate([one, cos, cos, z32], axis=1) * scale,
            jnp.concatenate([zero, sin, sin, z32], axis=1) * scale,
            jnp.concatenate([zero, cos, cos, z32], axis=1),
            jnp.concatenate([zero, sin, sin, z32], axis=1))
    return w1, wvst, wg, wq2, wk, wvt, tabs


def kernel(x, w_in, mla_q_norm, mla_kv_norm, w_uq, w_ukv, attn_sinks, w_o_mla, w_o_swa, w_out,
           ln1_g, ln1_b, w_router, router_bias, w_exp_up, w_exp_down, w_sh_up, w_sh_down,
           ln2_g, ln2_b):
    batch, seq, _ = x.shape
    T = batch * seq
    x2 = x.reshape(T, D_MODEL)
    w1, wvst, wg, wq2, wk, wvt, tabs = _prep_weights(w_in, w_uq, w_ukv, seq)
    q, k, vt, qs, ks, vst = _proj(x2, w1, wvst, wq2, wk, wvt, mla_q_norm.reshape(1, -1),
                                  mla_kv_norm.reshape(1, -1), tabs, seq)
    o_mla = _mla(q, k, vt, batch, seq)
    o_swa = _swa(attn_sinks.astype(F32), qs, ks, vst, batch, seq)

    tri = (lax.broadcasted_iota(jnp.int32, (POST_TM, POST_TM), 0)
           < lax.broadcasted_iota(jnp.int32, (POST_TM, POST_TM), 1)).astype(BF16)
    post_weights = (w_o_mla.astype(BF16), w_o_swa.astype(BF16), wg, w_out.astype(BF16),
                    ln1_g.reshape(1, -1), ln1_b.reshape(1, -1), w_router.T.astype(BF16),
                    router_bias.reshape(-1, 1).astype(F32), tri)
    wsu, wsd = w_sh_up.astype(BF16), w_sh_down.astype(BF16)
    g2, b2 = ln2_g.reshape(1, -1), ln2_b.reshape(1, -1)

    Tp = T // MOE_PARTS
    n_rows = (Tp * TOP_K // ROW_BLK + N_EXPERTS) * ROW_BLK
    out = None
    for part in range(MOE_PARTS):
        h1, h1p, e_t, w_t, r_t, cnt = _post(x2, o_mla, o_swa, *post_weights, part)
        counts = cnt[:, 0]
        pad = (counts + ROW_BLK - 1) // ROW_BLK * ROW_BLK
        pad_start = jnp.cumsum(pad) - pad
        dest = _dest(pad_start.astype(jnp.int32), e_t, r_t)
        idx = (dest.reshape(TOP_K, Tp // SC_CHUNK, SC_CHUNK, 1) * ROW_SUB
               + jnp.arange(ROW_SUB, dtype=jnp.int32))
        idx = idx.transpose(1, 0, 2, 3).reshape(Tp // SC_CHUNK, TOP_K, SC_CHUNK * ROW_SUB)
        xs = _sc_dispatch(h1p, idx, n_rows)
        sh = _shared(h1, wsu, wsd)
        ys = _experts((pad_start // ROW_BLK).astype(jnp.int32), (pad // ROW_BLK).astype(jnp.int32),
                      counts.astype(jnp.int32), xs, w_exp_up, w_exp_down)
        yg = _sc_gather(ys, idx, Tp)
        out = _combine(w_t.T, h1, sh, yg, g2, b2, part, out)
    return out.reshape(batch, seq, D_MODEL)
```

```python
import math

import jax
import jax.numpy as jnp
from jax import lax
from jax.experimental import pallas as pl
from jax.experimental.pallas import tpu as pltpu
from jax.experimental.pallas import tpu_sc as plsc

D_MODEL = 1024
MLA_HEADS = 8
MLA_Q_LORA = 256
MLA_KV_LORA = 256
MLA_NOPE = 64
MLA_ROPE = 32
MLA_V = 64
ROPE_THETA = 10000.0
SWA_HEADS = 8
SWA_KV_HEADS = 2
SWA_GROUP = SWA_HEADS // SWA_KV_HEADS
SWA_HEAD_DIM = 64
SWA_WINDOW = 128
N_EXPERTS = 256
TOP_K = 8
N_GROUPS = 8
GROUP_SIZE = N_EXPERTS // N_GROUPS
TOPK_GROUPS = 4
EXPERT_HIDDEN = 256
SHARED_HIDDEN = 256
ROUTED_SCALE = 2.5
DEEPNORM_ALPHA = 2.0 ** 0.25
LN_EPS = 1e-5
RMS_EPS = 1e-6

LANES = 128
ROW_WORDS = D_MODEL // 2
ROW_SUB = ROW_WORDS // LANES
VMEM_LIMIT = 48 * 1024 * 1024

PROJ_TM = 512
MLA_TQ = 256
MLA_TK = 256
MLA_HPS = 8
POST_TM = 512
ROW_BLK = 256
EXPERT_SLOTS = 4
EXPERT_SPLIT = 2
SHARED_TM = 512
COMBINE_TM = 256
SC_CHUNK = 32
MOE_PARTS = 1
COMBINE_PARTS = 4

BF16 = jnp.bfloat16
F32 = jnp.float32
NEG_INF = float("-inf")
LOG2E = math.log2(math.e)


def _sigmoid(v):
    return 1.0 / (1.0 + jnp.exp(-v))


def _dot(a, b):
    return jnp.dot(a, b, preferred_element_type=F32)


def _dot_nt(a, b):
    return lax.dot_general(a, b, (((1,), (1,)), ((), ())), preferred_element_type=F32)


def _pack_rows(y, out_ref, first=0):
    rows = y.shape[0]
    for j in range(ROW_SUB):
        a = y[:, j * LANES:(j + 1) * LANES].astype(BF16).astype(F32)
        b = y[:, ROW_WORDS + j * LANES:ROW_WORDS + (j + 1) * LANES].astype(BF16).astype(F32)
        ua = pltpu.bitcast(a, jnp.uint32) >> 16
        ub = pltpu.bitcast(b, jnp.uint32)
        out_ref[pl.ds(first * ROW_SUB + j, rows, stride=ROW_SUB), :] = ua | ub


def _unpack_rows(ref, rows, first=0, n_valid=None):
    lo, hi = [], []
    if n_valid is not None:
        live = first + lax.broadcasted_iota(jnp.int32, (rows, LANES), 0) < n_valid
    for j in range(ROW_SUB):
        u = ref[pl.ds(first * ROW_SUB + j, rows, stride=ROW_SUB), :]
        if n_valid is not None:
            u = jnp.where(live, u, jnp.uint32(0))
        lo.append(pltpu.bitcast(u << 16, F32))
        hi.append(pltpu.bitcast(u & jnp.uint32(0xFFFF0000), F32))
    return lo + hi


def _proj_kernel(x_ref, w1_ref, wvst_ref, wq_ref, wk_ref, wvt_ref, gq_ref, gkv_ref,
                 cq_ref, sq_ref, ck_ref, sk_ref,
                 q_ref, k_ref, vt_ref, qs_ref, ks_ref, vst_ref):
    xb = x_ref[...].astype(BF16)
    p = _dot(xb, w1_ref[...])

    def rms(c, g):
        return c * lax.rsqrt(jnp.mean(c * c, axis=-1, keepdims=True) + RMS_EPS) * g

    cqn = rms(p[:, 0:256], gq_ref[...]).astype(BF16)
    ckvn = rms(p[:, 256:512], gkv_ref[...]).astype(BF16)
    qs_ref[...] = p[:, 512:1024].astype(BF16)
    ks_ref[...] = p[:, 1024:1536].astype(BF16)
    vst_ref[...] = _dot_nt(wvst_ref[...], xb).astype(BF16)
    kr = p[:, 1536:1664] * ck_ref[...] + p[:, 1664:1792] * sk_ref[...]
    qq = _dot(cqn, wq_ref[...])
    kn = _dot(ckvn, wk_ref[...])
    cq = cq_ref[...]
    sq = sq_ref[...]
    for h in range(MLA_HEADS):
        a = qq[:, h * LANES:(h + 1) * LANES]
        b = qq[:, 1024 + h * LANES:1024 + (h + 1) * LANES]
        q_ref[:, h * LANES:(h + 1) * LANES] = (a * cq + b * sq).astype(BF16)
        k_ref[:, h * LANES:(h + 1) * LANES] = (kn[:, h * LANES:(h + 1) * LANES] + kr).astype(BF16)
    vt_ref[...] = _dot_nt(wvt_ref[...], ckvn).astype(BF16)


def _proj(x2, w1, wvst, wq2, wk, wvt, gq, gkv, tabs, seq):
    T = x2.shape[0]
    tm = PROJ_TM
    nper = seq // tm
    full = lambda shape: pl.BlockSpec(shape, lambda i: (0, 0))
    tab = pl.BlockSpec((tm, LANES), lambda i: (i % nper, 0))
    row = lambda n: pl.BlockSpec((tm, n), lambda i: (i, 0))
    col = lambda n: pl.BlockSpec((n, tm), lambda i: (0, i))
    return pl.pallas_call(
        _proj_kernel,
        grid=(T // tm,),
        in_specs=[row(D_MODEL), full(w1.shape), full(wvst.shape), full(wq2.shape), full(wk.shape),
                  full(wvt.shape), full(gq.shape), full(gkv.shape), tab, tab, tab, tab],
        out_specs=[row(1024), row(1024), col(512), row(512), row(512), col(128)],
        out_shape=[jax.ShapeDtypeStruct((T, 1024), BF16), jax.ShapeDtypeStruct((T, 1024), BF16),
                   jax.ShapeDtypeStruct((512, T), BF16), jax.ShapeDtypeStruct((T, 512), BF16),
                   jax.ShapeDtypeStruct((T, 512), BF16), jax.ShapeDtypeStruct((128, T), BF16)],
        compiler_params=pltpu.CompilerParams(dimension_semantics=("parallel",),
                                             vmem_limit_bytes=VMEM_LIMIT),
        name="proj",
    )(x2, w1, wvst, wq2, wk, wvt, gq, gkv, *tabs)


def _mla_kernel(q_ref, k_ref, vt_ref, o_ref, *acc_scr):
    tq = q_ref.shape[0]
    qi = pl.program_id(2)
    for acc in acc_scr:
        acc[...] = jnp.zeros(acc.shape, F32)

    tk = MLA_TK

    def step(kc, stats, masked):
        ks = pl.multiple_of(kc * tk, tk)
        scores = [_dot_nt(k_ref[pl.ds(ks, tk), h * LANES:(h + 1) * LANES],
                          q_ref[:, h * LANES:(h + 1) * LANES]) for h in range(MLA_HPS)]
        new_stats, probs, alphas = [], [], []
        for h in range(MLA_HPS):
            m_prev, l_prev = stats[h]
            s = scores[h]
            if masked:
                key = ks + lax.broadcasted_iota(jnp.int32, s.shape, 0)
                qry = qi * tq + lax.broadcasted_iota(jnp.int32, s.shape, 1)
                s = jnp.where(key <= qry, s, NEG_INF)
            m_new = jnp.maximum(m_prev, jnp.max(s, axis=0, keepdims=True))
            alpha = jnp.exp2(m_prev - m_new)
            p = jnp.exp2(s - m_new)
            new_stats.append((m_new, alpha * l_prev + jnp.sum(p, axis=0, keepdims=True)))
            probs.append(p.astype(BF16))
            alphas.append(alpha)
        for h in range(MLA_HPS):
            pv = _dot(vt_ref[h * MLA_V:(h + 1) * MLA_V, pl.ds(ks, tk)], probs[h])
            acc_scr[h][...] = acc_scr[h][...] * alphas[h] + pv
        return tuple(new_stats)

    init = tuple((jnp.full((1, tq), NEG_INF, F32), jnp.zeros((1, tq), F32)) for _ in range(MLA_HPS))
    n_full = qi * (tq // tk)
    stats = lax.fori_loop(0, n_full, lambda kc, st: step(kc, st, False), init)
    for d in range(tq // tk):
        stats = step(n_full + d, stats, True)
    for h2 in range(MLA_HPS // 2):
        out_t = jnp.concatenate([acc_scr[2 * h2 + g][...] / stats[2 * h2 + g][1] for g in range(2)],
                                axis=0)
        o_ref[:, h2 * LANES:(h2 + 1) * LANES] = out_t.T.astype(BF16)


def _mla(q, k, vt, batch, seq):
    T = q.shape[0]
    tq = MLA_TQ
    nq = seq // tq
    hps = MLA_HPS
    return pl.pallas_call(
        _mla_kernel,
        grid=(batch, MLA_HEADS // hps, nq),
        in_specs=[pl.BlockSpec((tq, hps * LANES), lambda b, j, i: (b * nq + i, j)),
                  pl.BlockSpec((seq, hps * LANES), lambda b, j, i: (b, j)),
                  pl.BlockSpec((hps * MLA_V, seq), lambda b, j, i: (j, b))],
        out_specs=pl.BlockSpec((tq, hps * MLA_V), lambda b, j, i: (b * nq + i, j)),
        out_shape=jax.ShapeDtypeStruct((T, MLA_HEADS * MLA_V), BF16),
        scratch_shapes=[pltpu.VMEM((MLA_V, tq), F32) for _ in range(hps)],
        compiler_params=pltpu.CompilerParams(
            dimension_semantics=("parallel", "parallel", "arbitrary"), vmem_limit_bytes=VMEM_LIMIT),
        name="mla",
    )(q, k, vt)


def _swa_kernel(sink_ref, q_ref, kc_ref, kp_ref, vtc_ref, vtp_ref, bias_ref, o_ref):
    scores = []
    for head in range(SWA_HEADS):
        pair, g, kvh = head // 2, head % 2, head // SWA_GROUP
        col = (2 * kvh + g) * LANES
        band = jnp.concatenate([kp_ref[:, col:col + LANES], kc_ref[:, col:col + LANES]], axis=0)
        scores.append(_dot_nt(band, q_ref[:, pair * LANES:(pair + 1) * LANES]))
    probs, denoms = [], []
    for head in range(SWA_HEADS):
        s = scores[head] + bias_ref[0, head]
        sink = sink_ref[head] * LOG2E
        m = jnp.maximum(jnp.max(s, axis=0, keepdims=True), sink)
        p = jnp.exp2(s - m)
        denoms.append(jnp.sum(p, axis=0, keepdims=True) + jnp.exp2(sink - m))
        probs.append(p.astype(BF16))
    outs = []
    for head in range(SWA_HEADS):
        rows = slice((head // SWA_GROUP) * SWA_HEAD_DIM, (head // SWA_GROUP + 1) * SWA_HEAD_DIM)
        v_band = jnp.concatenate([vtp_ref[rows, :], vtc_ref[rows, :]], axis=1)
        outs.append(_dot(v_band, probs[head]) / denoms[head])
    o_ref[...] = jnp.concatenate(outs, axis=0).T.astype(BF16)


def _swa_bias():
    W = SWA_WINDOW
    j = jnp.arange(2 * W, dtype=jnp.int32)[:, None]
    i = jnp.arange(W, dtype=jnp.int32)[None, :]
    dist = i + W - j
    valid = (dist >= 0) & (dist < W)
    slopes = 2.0 ** (-8.0 * jnp.arange(1, SWA_HEADS + 1, dtype=F32) / SWA_HEADS)
    pen = -(slopes[:, None, None] * dist.astype(F32)[None]) * LOG2E
    general = jnp.where(valid[None], pen, NEG_INF)
    first = jnp.where((valid & (j >= W))[None], pen, NEG_INF)
    return jnp.stack([first, general])


def _swa(sinks, qs, ks, vst, batch, seq):
    T = qs.shape[0]
    W = SWA_WINDOW
    nb = seq // W
    cur = lambda n: pl.BlockSpec((W, n), lambda b, i: (b * nb + i, 0))
    prev = lambda n: pl.BlockSpec((W, n), lambda b, i: (b * nb + jnp.maximum(i - 1, 0), 0))
    vt_cur = pl.BlockSpec((SWA_KV_HEADS * SWA_HEAD_DIM, W), lambda b, i: (0, b * nb + i))
    vt_prev = pl.BlockSpec((SWA_KV_HEADS * SWA_HEAD_DIM, W), lambda b, i: (0, b * nb + jnp.maximum(i - 1, 0)))
    bias = pl.BlockSpec((1, SWA_HEADS, 2 * W, W), lambda b, i: (jnp.minimum(i, 1), 0, 0, 0))
    return pl.pallas_call(
        _swa_kernel,
        grid=(batch, nb),
        in_specs=[pl.BlockSpec(memory_space=pltpu.SMEM), cur(512), cur(512), prev(512),
                  vt_cur, vt_prev, bias],
        out_specs=cur(512),
        out_shape=jax.ShapeDtypeStruct((T, 512), BF16),
        compiler_params=pltpu.CompilerParams(dimension_semantics=("parallel", "parallel"),
                                             vmem_limit_bytes=VMEM_LIMIT),
        name="swa",
    )(sinks, qs, ks, ks, vst, vst, _swa_bias())


def _post_kernel(x_ref, om_ref, os_ref, wom_ref, wos_ref, wg_ref, wout_ref, g1_ref, b1_ref,
                 wr_ref, rb_ref, tri_ref,
                 h1_ref, h1p_ref, e_ref, w_ref, r_ref, cnt_ref, carry_scr):
    tm = x_ref.shape[0]
    step = pl.program_id(0)

    @pl.when(step == 0)
    def _():
        carry_scr[...] = jnp.zeros(carry_scr.shape, F32)

    x = x_ref[...]
    ya = _dot(om_ref[...], wom_ref[...])
    yb = _dot(os_ref[...], wos_ref[...])
    gates = _dot(x.astype(BF16), wg_ref[...])
    merged = _sigmoid(gates[:, :D_MODEL]) * ya + _sigmoid(gates[:, D_MODEL:]) * yb
    mix = _dot(merged.astype(BF16), wout_ref[...])
    z = DEEPNORM_ALPHA * x + mix
    mu = jnp.mean(z, axis=-1, keepdims=True)
    zc = z - mu
    var = jnp.mean(zc * zc, axis=-1, keepdims=True)
    h1 = zc * lax.rsqrt(var + LN_EPS) * g1_ref[...] + b1_ref[...]
    h1_ref[...] = h1
    _pack_rows(h1, h1p_ref)

    scores = _sigmoid(_dot_nt(wr_ref[...], h1.astype(BF16)))
    choice = scores + rb_ref[...]
    row = lax.broadcasted_iota(jnp.int32, (N_EXPERTS, tm), 0)
    grow = lax.broadcasted_iota(jnp.int32, (GROUP_SIZE, tm), 0)
    gscore = []
    for g in range(N_GROUPS):
        blk = choice[g * GROUP_SIZE:(g + 1) * GROUP_SIZE, :]
        m1 = jnp.max(blk, axis=0, keepdims=True)
        i1 = jnp.min(jnp.where(blk == m1, grow, GROUP_SIZE), axis=0, keepdims=True)
        m2 = jnp.max(jnp.where(grow == i1, NEG_INF, blk), axis=0, keepdims=True)
        gscore.append(m1 + m2)
    gsc = jnp.concatenate(gscore, axis=0)
    gidx = lax.broadcasted_iota(jnp.int32, (N_GROUPS, tm), 0)
    grank = jnp.zeros((N_GROUPS, tm), jnp.int32)
    for g in range(N_GROUPS):
        sg = gsc[g:g + 1, :]
        beats = (sg > gsc) | ((sg == gsc) & (gidx > g))
        grank = grank + beats.astype(jnp.int32)
    gsel = (grank < TOPK_GROUPS).astype(F32)
    emask = jnp.concatenate(
        [jnp.broadcast_to(gsel[g:g + 1, :], (GROUP_SIZE, tm)) for g in range(N_GROUPS)], axis=0)
    work = jnp.where(emask > 0.0, choice, NEG_INF)
    eligible = work
    idxs, svals = [], []
    for _k in range(TOP_K):
        m = jnp.max(work, axis=0, keepdims=True)
        idx = jnp.min(jnp.where(work == m, row, N_EXPERTS), axis=0, keepdims=True)
        hit = row == idx
        svals.append(jnp.sum(jnp.where(hit, scores, 0.0), axis=0, keepdims=True))
        work = jnp.where(hit, NEG_INF, work)
        idxs.append(idx)
    sel = jnp.where(work != eligible, 1.0, 0.0)
    ssum = svals[0]
    for sv in svals[1:]:
        ssum = ssum + sv
    e_ref[...] = jnp.concatenate(idxs, axis=0)
    w_ref[...] = jnp.concatenate([sv / ssum * ROUTED_SCALE for sv in svals], axis=0)

    carry = carry_scr[...]
    rank = _dot(sel.astype(BF16), tri_ref[...]) + carry[:, 0:1]
    r_ref[...] = jnp.concatenate(
        [jnp.sum(jnp.where(row == idx, rank, 0.0), axis=0, keepdims=True) for idx in idxs],
        axis=0).astype(jnp.int32)
    carry = carry + jnp.sum(sel, axis=1, keepdims=True)
    carry_scr[...] = carry
    cnt_ref[...] = carry.astype(jnp.int32)


def _post(x2, o_mla, o_swa, wom, wos, wg, wout, g1, b1, wr_t, rbias, tri, part):
    T = x2.shape[0] // MOE_PARTS
    tm = POST_TM
    first = part * (T // tm)
    full = lambda a: pl.BlockSpec(a.shape, lambda i: (0, 0))
    row_in = lambda n: pl.BlockSpec((tm, n), lambda i: (first + i, 0))
    row = lambda n: pl.BlockSpec((tm, n), lambda i: (i, 0))
    col = pl.BlockSpec((TOP_K, tm), lambda i: (0, i))
    return pl.pallas_call(
        _post_kernel,
        grid=(T // tm,),
        in_specs=[row_in(D_MODEL), row_in(512), row_in(512), full(wom), full(wos), full(wg), full(wout),
                  full(g1), full(b1), full(wr_t), full(rbias), full(tri)],
        out_specs=[row(D_MODEL), pl.BlockSpec((tm * ROW_SUB, LANES), lambda i: (i, 0)), col, col, col,
                   pl.BlockSpec((N_EXPERTS, LANES), lambda i: (0, 0))],
        out_shape=[jax.ShapeDtypeStruct((T, D_MODEL), F32),
                   jax.ShapeDtypeStruct((T * ROW_SUB, LANES), jnp.uint32),
                   jax.ShapeDtypeStruct((TOP_K, T), jnp.int32),
                   jax.ShapeDtypeStruct((TOP_K, T), F32),
                   jax.ShapeDtypeStruct((TOP_K, T), jnp.int32),
                   jax.ShapeDtypeStruct((N_EXPERTS, LANES), jnp.int32)],
        scratch_shapes=[pltpu.VMEM((N_EXPERTS, LANES), F32)],
        compiler_params=pltpu.CompilerParams(dimension_semantics=("arbitrary",),
                                             vmem_limit_bytes=VMEM_LIMIT),
        name="post",
    )(x2, o_mla, o_swa, wom, wos, wg, wout, g1, b1, wr_t, rbias, tri)


def _dest_kernel(start_ref, e_ref, r_ref, o_ref):
    e = e_ref[...]
    base = jnp.zeros(e.shape, jnp.int32)
    for j in range(N_EXPERTS):
        base = jnp.where(e == j, start_ref[j], base)
    o_ref[...] = base + r_ref[...]


def _dest(seg_start, e_t, r_t):
    T = e_t.shape[1]
    tn = min(T, 4096)
    col = pl.BlockSpec((TOP_K, tn), lambda i: (0, i))
    return pl.pallas_call(
        _dest_kernel,
        grid=(T // tn,),
        in_specs=[pl.BlockSpec(memory_space=pltpu.SMEM), col, col],
        out_specs=col,
        out_shape=jax.ShapeDtypeStruct((TOP_K, T), jnp.int32),
        compiler_params=pltpu.CompilerParams(dimension_semantics=("parallel",)),
        name="dest",
    )(seg_start, e_t, r_t)


def _sc_worker_chunks(n_tokens):
    info = plsc.get_sparse_core_info()
    n_workers = info.num_cores * info.num_subcores
    per_worker = n_tokens // SC_CHUNK // n_workers
    assert per_worker * n_workers * SC_CHUNK == n_tokens
    first = (lax.axis_index("s") * info.num_cores + lax.axis_index("c")) * per_worker
    return first, per_worker


def _sc_dispatch(h1p, idx, n_rows):
    n_tokens = h1p.shape[0] // ROW_SUB
    R = SC_CHUNK * ROW_SUB

    def body(h_hbm, idx_hbm, xs_hbm, idx_v, rows_v, sem):
        first, per_worker = _sc_worker_chunks(n_tokens)

        @pl.loop(0, per_worker)
        def _(ci):
            c = first + ci
            pltpu.sync_copy(idx_hbm.at[c], idx_v)
            pltpu.sync_copy(h_hbm.at[pl.ds(c * R, R)], rows_v)
            copies = [pltpu.make_async_copy(rows_v, xs_hbm.at[idx_v.at[k]], sem) for k in range(TOP_K)]
            for cp in copies:
                cp.start()
            for cp in copies:
                cp.wait()

    run = pl.kernel(
        body, out_type=jax.ShapeDtypeStruct((n_rows * ROW_SUB, LANES), jnp.uint32),
        mesh=plsc.VectorSubcoreMesh(core_axis_name="c", subcore_axis_name="s"),
        scratch_types=[pltpu.VMEM((TOP_K, R), jnp.int32), pltpu.VMEM((R, LANES), jnp.uint32),
                       pltpu.SemaphoreType.DMA],
        name="sc_dispatch")
    return run(h1p, idx)


def _sc_gather(ys, idx, n_tokens):
    R = SC_CHUNK * ROW_SUB
    half = TOP_K // 2

    def body(ys_hbm, idx_hbm, yg_hbm, idx_v, buf, gsem, wsem):
        first, per_worker = _sc_worker_chunks(n_tokens)

        @pl.loop(0, per_worker)
        def _(ci):
            c = first + ci
            pltpu.sync_copy(idx_hbm.at[c], idx_v)
            for k0 in (0, half):
                gathers = [pltpu.make_async_copy(ys_hbm.at[idx_v.at[k0 + k]], buf.at[k], gsem)
                           for k in range(half)]
                for cp in gathers:
                    cp.start()
                for cp in gathers:
                    cp.wait()
                writes = [pltpu.make_async_copy(buf.at[k], yg_hbm.at[k0 + k, pl.ds(c * R, R)], wsem)
                          for k in range(half)]
                for cp in writes:
                    cp.start()
                for cp in writes:
                    cp.wait()

    run = pl.kernel(
        body, out_type=jax.ShapeDtypeStruct((TOP_K, n_tokens * ROW_SUB, LANES), jnp.uint32),
        mesh=plsc.VectorSubcoreMesh(core_axis_name="c", subcore_axis_name="s"),
        scratch_types=[pltpu.VMEM((TOP_K, R), jnp.int32), pltpu.VMEM((half, R, LANES), jnp.uint32),
                       pltpu.SemaphoreType.DMA, pltpu.SemaphoreType.DMA],
        name="sc_gather")
    return run(ys, idx)


def _experts_kernel(first_ref, nblk_ref, cnt_ref, xs_hbm, wup_ref, wdn_ref, after_ref, ys_hbm,
                    wup_bf, wdn_bf, xbuf, ybuf, in_sem, out_sem):
    del after_ref
    e = pl.program_id(0)
    n = nblk_ref[e]
    b0 = first_ref[e]
    count = cnt_ref[e]
    total = first_ref[N_EXPERTS - 1] + nblk_ref[N_EXPERTS - 1]
    rb = ROW_BLK * ROW_SUB
    depth = EXPERT_SLOTS

    def slot_of(b):
        return lax.rem(b, depth)

    def rows_of(b):
        return pl.ds(pl.multiple_of(b * rb, rb), rb)

    def in_copy(b):
        return pltpu.make_async_copy(xs_hbm.at[rows_of(b)], xbuf.at[slot_of(b)], in_sem.at[slot_of(b)])

    def out_copy(b):
        return pltpu.make_async_copy(ybuf.at[slot_of(b)], ys_hbm.at[rows_of(b)], out_sem.at[slot_of(b)])

    @pl.when(e == 0)
    def _():
        for b in range(depth - 1):
            @pl.when(b < total)
            def _():
                in_copy(b).start()

    @pl.when(n > 0)
    def _():
        wup_bf[...] = wup_ref[0].astype(BF16)
        wdn_bf[...] = wdn_ref[0].astype(BF16)

        def body(b, carry):
            slot = slot_of(b)
            in_copy(b).wait()

            @pl.when(b + depth - 1 < total)
            def _():
                in_copy(b + depth - 1).start()

            @pl.when(b >= depth)
            def _():
                out_copy(b - depth).wait()

            sub = ROW_BLK // EXPERT_SPLIT
            n_valid = count - (b - b0) * ROW_BLK
            xbs = [jnp.concatenate(
                [c.astype(BF16) for c in _unpack_rows(xbuf.at[slot], sub, first=i * sub, n_valid=n_valid)],
                axis=1) for i in range(EXPERT_SPLIT)]
            gus = [_dot(xb, wup_bf[...]) for xb in xbs]
            hids = [(gu[:, :EXPERT_HIDDEN] * _sigmoid(gu[:, :EXPERT_HIDDEN])
                     * gu[:, EXPERT_HIDDEN:]).astype(BF16) for gu in gus]
            ys = [_dot(hid, wdn_bf[...]) for hid in hids]
            for i in range(EXPERT_SPLIT):
                _pack_rows(ys[i], ybuf.at[slot], first=i * sub)
            out_copy(b).start()
            return carry

        lax.fori_loop(b0, b0 + n, body, 0)

    @pl.when(e == N_EXPERTS - 1)
    def _():
        for back in range(depth, 0, -1):
            @pl.when(total - back >= 0)
            def _():
                out_copy(total - back).wait()


def _experts(first_blk, n_blk, counts, xs, w_exp_up, w_exp_down, run_after):
    rb = ROW_BLK * ROW_SUB
    grid_spec = pltpu.PrefetchScalarGridSpec(
        num_scalar_prefetch=3,
        grid=(N_EXPERTS,),
        in_specs=[pl.BlockSpec(memory_space=pl.ANY),
                  pl.BlockSpec((1, D_MODEL, 2 * EXPERT_HIDDEN), lambda e, fb, nb, ct: (e, 0, 0)),
                  pl.BlockSpec((1, EXPERT_HIDDEN, D_MODEL), lambda e, fb, nb, ct: (e, 0, 0)),
                  pl.BlockSpec(memory_space=pl.ANY)],
        out_specs=pl.BlockSpec(memory_space=pl.ANY),
        scratch_shapes=[pltpu.VMEM((D_MODEL, 2 * EXPERT_HIDDEN), BF16),
                        pltpu.VMEM((EXPERT_HIDDEN, D_MODEL), BF16),
                        pltpu.VMEM((EXPERT_SLOTS, rb, LANES), jnp.uint32),
                        pltpu.VMEM((EXPERT_SLOTS, rb, LANES), jnp.uint32),
                        pltpu.SemaphoreType.DMA((EXPERT_SLOTS,)),
                        pltpu.SemaphoreType.DMA((EXPERT_SLOTS,))],
    )
    return pl.pallas_call(
        _experts_kernel,
        grid_spec=grid_spec,
        out_shape=jax.ShapeDtypeStruct(xs.shape, jnp.uint32),
        compiler_params=pltpu.CompilerParams(dimension_semantics=("arbitrary",),
                                             vmem_limit_bytes=VMEM_LIMIT),
        name="experts",
    )(first_blk, n_blk, counts, xs, w_exp_up, w_exp_down, run_after)


def _shared_kernel(h1_ref, wsu_ref, wsd_ref, o_ref):
    gu = _dot(h1_ref[...].astype(BF16), wsu_ref[...])
    g = gu[:, :SHARED_HIDDEN]
    hid = g * _sigmoid(g) * gu[:, SHARED_HIDDEN:]
    o_ref[...] = _dot(hid.astype(BF16), wsd_ref[...]).astype(BF16)


def _shared(h1, wsu, wsd):
    T = h1.shape[0]
    tm = SHARED_TM
    full = lambda a: pl.BlockSpec(a.shape, lambda i: (0, 0))
    row = pl.BlockSpec((tm, D_MODEL), lambda i: (i, 0))
    return pl.pallas_call(
        _shared_kernel,
        grid=(T // tm,),
        in_specs=[row, full(wsu), full(wsd)],
        out_specs=row,
        out_shape=jax.ShapeDtypeStruct((T, D_MODEL), BF16),
        compiler_params=pltpu.CompilerParams(dimension_semantics=("parallel",),
                                             vmem_limit_bytes=VMEM_LIMIT),
        name="shared",
    )(h1, wsu, wsd)


def _combine_kernel(w_ref, h1_ref, sh_ref, yg_ref, g2_ref, b2_ref, *rest):
    o_ref = rest[-1]
    tm = h1_ref.shape[0]
    h1 = h1_ref[...]
    ffn = sh_ref[...].astype(F32)

    w = w_ref[...]
    acc = [None] * (2 * ROW_SUB)
    for k in range(TOP_K):
        wk = w[:, k:k + 1]
        chunks = _unpack_rows(yg_ref.at[k], tm)
        for c in range(2 * ROW_SUB):
            acc[c] = wk * chunks[c] if acc[c] is None else acc[c] + wk * chunks[c]
    routed = jnp.concatenate(acc, axis=1)
    z = DEEPNORM_ALPHA * h1 + (routed + ffn)
    mu = jnp.mean(z, axis=-1, keepdims=True)
    zc = z - mu
    var = jnp.mean(zc * zc, axis=-1, keepdims=True)
    o_ref[...] = zc * lax.rsqrt(var + LN_EPS) * g2_ref[...] + b2_ref[...]


def _combine(w_tok, h1, sh, yg, g2, b2, in_first, out_first, out_rows, out_prev):
    n_tok = yg.shape[1] // ROW_SUB
    tm = COMBINE_TM
    fi, fo = in_first // tm, out_first // tm
    full = lambda a: pl.BlockSpec(a.shape, lambda i: (0, 0))
    row = pl.BlockSpec((tm, D_MODEL), lambda i: (fi + i, 0))
    in_specs = [pl.BlockSpec((tm, TOP_K), lambda i: (fi + i, 0)), row, row,
                pl.BlockSpec((TOP_K, tm * ROW_SUB, LANES), lambda i: (0, i, 0)),
                full(g2), full(b2)]
    args = [w_tok, h1, sh, yg, g2, b2]
    aliases = {}
    if out_prev is not None:
        in_specs.append(pl.BlockSpec(memory_space=pl.ANY))
        args.append(out_prev)
        aliases = {len(args) - 1: 0}
    return pl.pallas_call(
        _combine_kernel,
        grid=(n_tok // tm,),
        in_specs=in_specs,
        out_specs=pl.BlockSpec((tm, D_MODEL), lambda i: (fo + i, 0)),
        out_shape=jax.ShapeDtypeStruct((out_rows, D_MODEL), F32),
        input_output_aliases=aliases,
        compiler_params=pltpu.CompilerParams(dimension_semantics=("parallel",),
                                             vmem_limit_bytes=VMEM_LIMIT),
        name="combine",
    )(*args)


def _prep_weights(w_in, w_uq, w_ukv, seq):
    z = lambda n: jnp.zeros((D_MODEL, n), F32)
    kr = w_in[:, 512:544]
    qs = w_in[:, 544:1056] * (LOG2E / math.sqrt(SWA_HEAD_DIM))
    ks0, ks1 = w_in[:, 1056:1120], w_in[:, 1120:1184]
    half = MLA_ROPE // 2
    w1 = jnp.concatenate([
        w_in[:, 0:512], qs,
        ks0, z(64), z(64), ks0, ks1, z(64), z(64), ks1,
        z(64), kr, z(32),
        z(64), -kr[:, half:], kr[:, :half], z(32)], axis=1).astype(BF16)
    wvst = w_in[:, 1184:1312].T.astype(BF16)
    wg = w_in[:, 1312:3360].astype(BF16)

    zq = lambda n: jnp.zeros((MLA_Q_LORA, n), F32)
    qd = MLA_NOPE + MLA_ROPE
    q_cols, r_cols, k_cols, v_cols = [], [], [], []
    for h in range(MLA_HEADS):
        wq = w_uq[:, h * qd:(h + 1) * qd]
        q_cols += [wq, zq(32)]
        r_cols += [zq(64), -wq[:, MLA_NOPE + half:], wq[:, MLA_NOPE:MLA_NOPE + half], zq(32)]
        wk = w_ukv[:, h * 128:(h + 1) * 128]
        k_cols += [wk[:, :MLA_NOPE], zq(64)]
        v_cols += [wk[:, MLA_NOPE:]]
    wq2 = jnp.concatenate(q_cols + r_cols, axis=1).astype(BF16)
    wk = jnp.concatenate(k_cols, axis=1).astype(BF16)
    wvt = jnp.concatenate(v_cols, axis=1).T.astype(BF16)

    inv_freq = ROPE_THETA ** (-jnp.arange(0, MLA_ROPE, 2, dtype=F32) / MLA_ROPE)
    ang = jnp.arange(seq, dtype=F32)[:, None] * inv_freq[None, :]
    cos, sin = jnp.cos(ang), jnp.sin(ang)
    one, zero = jnp.ones((seq, 64), F32), jnp.zeros((seq, 64), F32)
    z32 = jnp.zeros((seq, 32), F32)
    scale = LOG2E / math.sqrt(MLA_NOPE + MLA_ROPE)
    tabs = (jnp.concatenate([one, cos, cos, z32], axis=1) * scale,
            jnp.concatenate([zero, sin, sin, z32], axis=1) * scale,
            jnp.concatenate([zero, cos, cos, z32], axis=1),
            jnp.concatenate([zero, sin, sin, z32], axis=1))
    return w1, wvst, wg, wq2, wk, wvt, tabs


def kernel(x, w_in, mla_q_norm, mla_kv_norm, w_uq, w_ukv, attn_sinks, w_o_mla, w_o_swa, w_out,
           ln1_g, ln1_b, w_router, router_bias, w_exp_up, w_exp_down, w_sh_up, w_sh_down,
           ln2_g, ln2_b):
    batch, seq, _ = x.shape
    T = batch * seq
    x2 = x.reshape(T, D_MODEL)
    w1, wvst, wg, wq2, wk, wvt, tabs = _prep_weights(w_in, w_uq, w_ukv, seq)
    q, k, vt, qs, ks, vst = _proj(x2, w1, wvst, wq2, wk, wvt, mla_q_norm.reshape(1, -1),
                                  mla_kv_norm.reshape(1, -1), tabs, seq)
    o_mla = _mla(q, k, vt, batch, seq)
    o_swa = _swa(attn_sinks.astype(F32), qs, ks, vst, batch, seq)

    tri = (lax.broadcasted_iota(jnp.int32, (POST_TM, POST_TM), 0)
           < lax.broadcasted_iota(jnp.int32, (POST_TM, POST_TM), 1)).astype(BF16)
    post_weights = (w_o_mla.astype(BF16), w_o_swa.astype(BF16), wg, w_out.astype(BF16),
                    ln1_g.reshape(1, -1), ln1_b.reshape(1, -1), w_router.T.astype(BF16),
                    router_bias.reshape(-1, 1).astype(F32), tri)
    wsu, wsd = w_sh_up.astype(BF16), w_sh_down.astype(BF16)
    g2, b2 = ln2_g.reshape(1, -1), ln2_b.reshape(1, -1)

    Tp = T // MOE_PARTS
    n_rows = (Tp * TOP_K // ROW_BLK + N_EXPERTS) * ROW_BLK
    out = None
    for part in range(MOE_PARTS):
        h1, h1p, e_t, w_t, r_t, cnt = _post(x2, o_mla, o_swa, *post_weights, part)
        counts = cnt[:, 0]
        pad = (counts + ROW_BLK - 1) // ROW_BLK * ROW_BLK
        pad_start = jnp.cumsum(pad) - pad
        dest = _dest(pad_start.astype(jnp.int32), e_t, r_t)
        idx = (dest.reshape(TOP_K, Tp // SC_CHUNK, SC_CHUNK, 1) * ROW_SUB
               + jnp.arange(ROW_SUB, dtype=jnp.int32))
        idx = idx.transpose(1, 0, 2, 3).reshape(Tp // SC_CHUNK, TOP_K, SC_CHUNK * ROW_SUB)
        xs = _sc_dispatch(h1p, idx, n_rows)
        sh = _shared(h1, wsu, wsd)
        ys = _experts((pad_start // ROW_BLK).astype(jnp.int32), (pad // ROW_BLK).astype(jnp.int32),
                      counts.astype(jnp.int32), xs, w_exp_up, w_exp_down, sh)
        Tc = Tp // COMBINE_PARTS
        n_chunks = Tc // SC_CHUNK
        w_tok = w_t.T
        for cpart in range(COMBINE_PARTS):
            yg = _sc_gather(ys, idx[cpart * n_chunks:(cpart + 1) * n_chunks], Tc)
            out = _combine(w_tok, h1, sh, yg, g2, b2, cpart * Tc, part * Tp + cpart * Tc, T, out)
    return out.reshape(batch, seq, D_MODEL)
```

```python
import math

import jax
import jax.numpy as jnp
from jax import lax
from jax.experimental import pallas as pl
from jax.experimental.pallas import tpu as pltpu
from jax.experimental.pallas import tpu_sc as plsc

D_MODEL = 1024
MLA_HEADS = 8
MLA_Q_LORA = 256
MLA_KV_LORA = 256
MLA_NOPE = 64
MLA_ROPE = 32
MLA_V = 64
VT_ROWS = 80
ROPE_THETA = 10000.0
SWA_HEADS = 8
SWA_KV_HEADS = 2
SWA_GROUP = SWA_HEADS // SWA_KV_HEADS
SWA_HEAD_DIM = 64
SWA_WINDOW = 128
N_EXPERTS = 256
TOP_K = 8
N_GROUPS = 8
GROUP_SIZE = N_EXPERTS // N_GROUPS
TOPK_GROUPS = 4
EXPERT_HIDDEN = 256
SHARED_HIDDEN = 256
ROUTED_SCALE = 2.5
DEEPNORM_ALPHA = 2.0 ** 0.25
LN_EPS = 1e-5
RMS_EPS = 1e-6

LANES = 128
ROW_WORDS = D_MODEL // 2
ROW_SUB = ROW_WORDS // LANES
VMEM_LIMIT = 48 * 1024 * 1024

PROJ_TM = 512
MLA_TQ = 256
MLA_TK = 256
MLA_HPS = 8
POST_TM = 512
ROW_BLK = 256
EXPERT_SLOTS = 4
EXPERT_SPLIT = 2
SHARED_TM = 512
COMBINE_TM = 256
SC_CHUNK = 32
MOE_PARTS = 1
COMBINE_PARTS = 4

BF16 = jnp.bfloat16
F32 = jnp.float32
NEG_INF = float("-inf")
LOG2E = math.log2(math.e)


def _sigmoid(v):
    return 1.0 / (1.0 + jnp.exp(-v))


def _dot(a, b):
    return jnp.dot(a, b, preferred_element_type=F32)


def _dot_nt(a, b):
    return lax.dot_general(a, b, (((1,), (1,)), ((), ())), preferred_element_type=F32)


def _pack_rows(y, out_ref, first=0):
    rows = y.shape[0]
    for j in range(ROW_SUB):
        a = y[:, j * LANES:(j + 1) * LANES].astype(BF16).astype(F32)
        b = y[:, ROW_WORDS + j * LANES:ROW_WORDS + (j + 1) * LANES].astype(BF16).astype(F32)
        ua = pltpu.bitcast(a, jnp.uint32) >> 16
        ub = pltpu.bitcast(b, jnp.uint32)
        out_ref[pl.ds(first * ROW_SUB + j, rows, stride=ROW_SUB), :] = ua | ub


def _unpack_rows(ref, rows, first=0, n_valid=None):
    lo, hi = [], []
    if n_valid is not None:
        live = first + lax.broadcasted_iota(jnp.int32, (rows, LANES), 0) < n_valid
    for j in range(ROW_SUB):
        u = ref[pl.ds(first * ROW_SUB + j, rows, stride=ROW_SUB), :]
        if n_valid is not None:
            u = jnp.where(live, u, jnp.uint32(0))
        lo.append(pltpu.bitcast(u << 16, F32))
        hi.append(pltpu.bitcast(u & jnp.uint32(0xFFFF0000), F32))
    return lo + hi


def _proj_kernel(x_ref, w1_ref, wvst_ref, wq_ref, wk_ref, wvt_ref, gq_ref, gkv_ref,
                 cq_ref, sq_ref, ck_ref, sk_ref,
                 q_ref, k_ref, vt_ref, qs_ref, ks_ref, vst_ref):
    xb = x_ref[...].astype(BF16)
    p = _dot(xb, w1_ref[...])

    def rms(c, g):
        return c * lax.rsqrt(jnp.mean(c * c, axis=-1, keepdims=True) + RMS_EPS) * g

    cqn = rms(p[:, 0:256], gq_ref[...]).astype(BF16)
    ckvn = rms(p[:, 256:512], gkv_ref[...]).astype(BF16)
    qs_ref[...] = p[:, 512:1024].astype(BF16)
    ks_ref[...] = p[:, 1024:1536].astype(BF16)
    ones = jnp.ones((VT_ROWS - MLA_V, xb.shape[0]), BF16)
    vst = _dot_nt(wvst_ref[...], xb).astype(BF16)
    for h in range(SWA_KV_HEADS):
        vst_ref[h * VT_ROWS:h * VT_ROWS + SWA_HEAD_DIM, :] = vst[h * SWA_HEAD_DIM:(h + 1) * SWA_HEAD_DIM, :]
        vst_ref[h * VT_ROWS + SWA_HEAD_DIM:(h + 1) * VT_ROWS, :] = ones
    kr = p[:, 1536:1664] * ck_ref[...] + p[:, 1664:1792] * sk_ref[...]
    qq = _dot(cqn, wq_ref[...])
    kn = _dot(ckvn, wk_ref[...])
    cq = cq_ref[...]
    sq = sq_ref[...]
    for h in range(MLA_HEADS):
        a = qq[:, h * LANES:(h + 1) * LANES]
        b = qq[:, 1024 + h * LANES:1024 + (h + 1) * LANES]
        q_ref[:, h * LANES:(h + 1) * LANES] = (a * cq + b * sq).astype(BF16)
        k_ref[:, h * LANES:(h + 1) * LANES] = (kn[:, h * LANES:(h + 1) * LANES] + kr).astype(BF16)
    vt = _dot_nt(wvt_ref[...], ckvn).astype(BF16)
    for h in range(MLA_HEADS):
        vt_ref[h * VT_ROWS:h * VT_ROWS + MLA_V, :] = vt[h * MLA_V:(h + 1) * MLA_V, :]
        vt_ref[h * VT_ROWS + MLA_V:(h + 1) * VT_ROWS, :] = ones


def _proj(x2, w1, wvst, wq2, wk, wvt, gq, gkv, tabs, seq):
    T = x2.shape[0]
    tm = PROJ_TM
    nper = seq // tm
    full = lambda shape: pl.BlockSpec(shape, lambda i: (0, 0))
    tab = pl.BlockSpec((tm, LANES), lambda i: (i % nper, 0))
    row = lambda n: pl.BlockSpec((tm, n), lambda i: (i, 0))
    col = lambda n: pl.BlockSpec((n, tm), lambda i: (0, i))
    return pl.pallas_call(
        _proj_kernel,
        grid=(T // tm,),
        in_specs=[row(D_MODEL), full(w1.shape), full(wvst.shape), full(wq2.shape), full(wk.shape),
                  full(wvt.shape), full(gq.shape), full(gkv.shape), tab, tab, tab, tab],
        out_specs=[row(1024), row(1024), col(MLA_HEADS * VT_ROWS), row(512), row(512),
                   col(SWA_KV_HEADS * VT_ROWS)],
        out_shape=[jax.ShapeDtypeStruct((T, 1024), BF16), jax.ShapeDtypeStruct((T, 1024), BF16),
                   jax.ShapeDtypeStruct((MLA_HEADS * VT_ROWS, T), BF16), jax.ShapeDtypeStruct((T, 512), BF16),
                   jax.ShapeDtypeStruct((T, 512), BF16),
                   jax.ShapeDtypeStruct((SWA_KV_HEADS * VT_ROWS, T), BF16)],
        compiler_params=pltpu.CompilerParams(dimension_semantics=("parallel",),
                                             vmem_limit_bytes=VMEM_LIMIT),
        name="proj",
    )(x2, w1, wvst, wq2, wk, wvt, gq, gkv, *tabs)


def _mla_kernel(q_ref, k_ref, vt_ref, o_ref, *acc_scr):
    tq = q_ref.shape[0]
    qi = pl.program_id(2)
    for acc in acc_scr:
        acc[...] = jnp.zeros(acc.shape, F32)

    tk = MLA_TK

    def step(kc, maxes, masked):
        ks = pl.multiple_of(kc * tk, tk)
        scores = [_dot_nt(k_ref[pl.ds(ks, tk), h * LANES:(h + 1) * LANES],
                          q_ref[:, h * LANES:(h + 1) * LANES]) for h in range(MLA_HPS)]
        new_maxes, probs, alphas = [], [], []
        for h in range(MLA_HPS):
            s = scores[h]
            if masked:
                key = ks + lax.broadcasted_iota(jnp.int32, s.shape, 0)
                qry = qi * tq + lax.broadcasted_iota(jnp.int32, s.shape, 1)
                s = jnp.where(key <= qry, s, NEG_INF)
            m_new = jnp.maximum(maxes[h], jnp.max(s, axis=0, keepdims=True))
            alphas.append(jnp.exp2(maxes[h] - m_new))
            probs.append(jnp.exp2(s - m_new).astype(BF16))
            new_maxes.append(m_new)
        for h in range(MLA_HPS):
            pv = _dot(vt_ref[h * VT_ROWS:(h + 1) * VT_ROWS, pl.ds(ks, tk)], probs[h])
            acc_scr[h][...] = acc_scr[h][...] * alphas[h] + pv
        return tuple(new_maxes)

    init = tuple(jnp.full((1, tq), NEG_INF, F32) for _ in range(MLA_HPS))
    n_full = qi * (tq // tk)
    maxes = lax.fori_loop(0, n_full, lambda kc, mx: step(kc, mx, False), init)
    for d in range(tq // tk):
        maxes = step(n_full + d, maxes, True)
    for h2 in range(MLA_HPS // 2):
        out_t = jnp.concatenate(
            [acc_scr[2 * h2 + g][0:MLA_V, :] / acc_scr[2 * h2 + g][MLA_V:MLA_V + 1, :] for g in range(2)],
            axis=0)
        o_ref[:, h2 * LANES:(h2 + 1) * LANES] = out_t.T.astype(BF16)


def _mla(q, k, vt, batch, seq):
    T = q.shape[0]
    tq = MLA_TQ
    nq = seq // tq
    hps = MLA_HPS
    return pl.pallas_call(
        _mla_kernel,
        grid=(batch, MLA_HEADS // hps, nq),
        in_specs=[pl.BlockSpec((tq, hps * LANES), lambda b, j, i: (b * nq + i, j)),
                  pl.BlockSpec((seq, hps * LANES), lambda b, j, i: (b, j)),
                  pl.BlockSpec((hps * VT_ROWS, seq), lambda b, j, i: (j, b))],
        out_specs=pl.BlockSpec((tq, hps * MLA_V), lambda b, j, i: (b * nq + i, j)),
        out_shape=jax.ShapeDtypeStruct((T, MLA_HEADS * MLA_V), BF16),
        scratch_shapes=[pltpu.VMEM((VT_ROWS, tq), F32) for _ in range(hps)],
        compiler_params=pltpu.CompilerParams(
            dimension_semantics=("parallel", "parallel", "arbitrary"), vmem_limit_bytes=VMEM_LIMIT),
        name="mla",
    )(q, k, vt)


def _swa_kernel(sink_ref, q_ref, kc_ref, kp_ref, vtc_ref, vtp_ref, bias_ref, o_ref):
    scores = []
    for head in range(SWA_HEADS):
        pair, g, kvh = head // 2, head % 2, head // SWA_GROUP
        col = (2 * kvh + g) * LANES
        band = jnp.concatenate([kp_ref[:, col:col + LANES], kc_ref[:, col:col + LANES]], axis=0)
        scores.append(_dot_nt(band, q_ref[:, pair * LANES:(pair + 1) * LANES]))
    probs, sink_terms = [], []
    for head in range(SWA_HEADS):
        s = scores[head] + bias_ref[0, head]
        sink = sink_ref[head] * LOG2E
        m = jnp.maximum(jnp.max(s, axis=0, keepdims=True), sink)
        probs.append(jnp.exp2(s - m).astype(BF16))
        sink_terms.append(jnp.exp2(sink - m))
    outs = []
    for head in range(SWA_HEADS):
        rows = slice((head // SWA_GROUP) * VT_ROWS, (head // SWA_GROUP + 1) * VT_ROWS)
        v_band = jnp.concatenate([vtp_ref[rows, :], vtc_ref[rows, :]], axis=1)
        pv = _dot(v_band, probs[head])
        outs.append(pv[0:SWA_HEAD_DIM, :] / (pv[SWA_HEAD_DIM:SWA_HEAD_DIM + 1, :] + sink_terms[head]))
    o_ref[...] = jnp.concatenate(outs, axis=0).T.astype(BF16)


def _swa_bias():
    W = SWA_WINDOW
    j = jnp.arange(2 * W, dtype=jnp.int32)[:, None]
    i = jnp.arange(W, dtype=jnp.int32)[None, :]
    dist = i + W - j
    valid = (dist >= 0) & (dist < W)
    slopes = 2.0 ** (-8.0 * jnp.arange(1, SWA_HEADS + 1, dtype=F32) / SWA_HEADS)
    pen = -(slopes[:, None, None] * dist.astype(F32)[None]) * LOG2E
    general = jnp.where(valid[None], pen, NEG_INF)
    first = jnp.where((valid & (j >= W))[None], pen, NEG_INF)
    return jnp.stack([first, general])


def _swa(sinks, qs, ks, vst, batch, seq):
    T = qs.shape[0]
    W = SWA_WINDOW
    nb = seq // W
    cur = lambda n: pl.BlockSpec((W, n), lambda b, i: (b * nb + i, 0))
    prev = lambda n: pl.BlockSpec((W, n), lambda b, i: (b * nb + jnp.maximum(i - 1, 0), 0))
    vt_cur = pl.BlockSpec((SWA_KV_HEADS * VT_ROWS, W), lambda b, i: (0, b * nb + i))
    vt_prev = pl.BlockSpec((SWA_KV_HEADS * VT_ROWS, W), lambda b, i: (0, b * nb + jnp.maximum(i - 1, 0)))
    bias = pl.BlockSpec((1, SWA_HEADS, 2 * W, W), lambda b, i: (jnp.minimum(i, 1), 0, 0, 0))
    return pl.pallas_call(
        _swa_kernel,
        grid=(batch, nb),
        in_specs=[pl.BlockSpec(memory_space=pltpu.SMEM), cur(512), cur(512), prev(512),
                  vt_cur, vt_prev, bias],
        out_specs=cur(512),
        out_shape=jax.ShapeDtypeStruct((T, 512), BF16),
        compiler_params=pltpu.CompilerParams(dimension_semantics=("parallel", "parallel"),
                                             vmem_limit_bytes=VMEM_LIMIT),
        name="swa",
    )(sinks, qs, ks, ks, vst, vst, _swa_bias())


def _post_kernel(x_ref, om_ref, os_ref, wom_ref, wos_ref, wg_ref, wout_ref, g1_ref, b1_ref,
                 wr_ref, rb_ref, tri_ref,
                 h1_ref, h1p_ref, e_ref, w_ref, r_ref, cnt_ref, carry_scr):
    tm = x_ref.shape[0]
    step = pl.program_id(0)

    @pl.when(step == 0)
    def _():
        carry_scr[...] = jnp.zeros(carry_scr.shape, F32)

    x = x_ref[...]
    ya = _dot(om_ref[...], wom_ref[...])
    yb = _dot(os_ref[...], wos_ref[...])
    gates = _dot(x.astype(BF16), wg_ref[...])
    merged = _sigmoid(gates[:, :D_MODEL]) * ya + _sigmoid(gates[:, D_MODEL:]) * yb
    mix = _dot(merged.astype(BF16), wout_ref[...])
    z = DEEPNORM_ALPHA * x + mix
    mu = jnp.mean(z, axis=-1, keepdims=True)
    zc = z - mu
    var = jnp.mean(zc * zc, axis=-1, keepdims=True)
    h1 = zc * lax.rsqrt(var + LN_EPS) * g1_ref[...] + b1_ref[...]
    h1_ref[...] = h1
    _pack_rows(h1, h1p_ref)

    scores = _sigmoid(_dot_nt(wr_ref[...], h1.astype(BF16)))
    choice = scores + rb_ref[...]
    row = lax.broadcasted_iota(jnp.int32, (N_EXPERTS, tm), 0)
    grow = lax.broadcasted_iota(jnp.int32, (GROUP_SIZE, tm), 0)
    gscore = []
    for g in range(N_GROUPS):
        blk = choice[g * GROUP_SIZE:(g + 1) * GROUP_SIZE, :]
        m1 = jnp.max(blk, axis=0, keepdims=True)
        i1 = jnp.min(jnp.where(blk == m1, grow, GROUP_SIZE), axis=0, keepdims=True)
        m2 = jnp.max(jnp.where(grow == i1, NEG_INF, blk), axis=0, keepdims=True)
        gscore.append(m1 + m2)
    gsc = jnp.concatenate(gscore, axis=0)
    gidx = lax.broadcasted_iota(jnp.int32, (N_GROUPS, tm), 0)
    grank = jnp.zeros((N_GROUPS, tm), jnp.int32)
    for g in range(N_GROUPS):
        sg = gsc[g:g + 1, :]
        beats = (sg > gsc) | ((sg == gsc) & (gidx > g))
        grank = grank + beats.astype(jnp.int32)
    gsel = (grank < TOPK_GROUPS).astype(F32)
    emask = jnp.concatenate(
        [jnp.broadcast_to(gsel[g:g + 1, :], (GROUP_SIZE, tm)) for g in range(N_GROUPS)], axis=0)
    work = jnp.where(emask > 0.0, choice, NEG_INF)
    eligible = work
    idxs, svals = [], []
    for _k in range(TOP_K):
        m = jnp.max(work, axis=0, keepdims=True)
        idx = jnp.min(jnp.where(work == m, row, N_EXPERTS), axis=0, keepdims=True)
        hit = row == idx
        svals.append(jnp.sum(jnp.where(hit, scores, 0.0), axis=0, keepdims=True))
        work = jnp.where(hit, NEG_INF, work)
        idxs.append(idx)
    sel = jnp.where(work != eligible, 1.0, 0.0)
    ssum = svals[0]
    for sv in svals[1:]:
        ssum = ssum + sv
    e_ref[...] = jnp.concatenate(idxs, axis=0)
    w_ref[...] = jnp.concatenate([sv / ssum * ROUTED_SCALE for sv in svals], axis=0)

    carry = carry_scr[...]
    rank = _dot(sel.astype(BF16), tri_ref[...]) + carry[:, 0:1]
    r_ref[...] = jnp.concatenate(
        [jnp.sum(jnp.where(row == idx, rank, 0.0), axis=0, keepdims=True) for idx in idxs],
        axis=0).astype(jnp.int32)
    carry = carry + jnp.sum(sel, axis=1, keepdims=True)
    carry_scr[...] = carry
    cnt_ref[...] = carry.astype(jnp.int32)


def _post(x2, o_mla, o_swa, wom, wos, wg, wout, g1, b1, wr_t, rbias, tri, part):
    T = x2.shape[0] // MOE_PARTS
    tm = POST_TM
    first = part * (T // tm)
    full = lambda a: pl.BlockSpec(a.shape, lambda i: (0, 0))
    row_in = lambda n: pl.BlockSpec((tm, n), lambda i: (first + i, 0))
    row = lambda n: pl.BlockSpec((tm, n), lambda i: (i, 0))
    col = pl.BlockSpec((TOP_K, tm), lambda i: (0, i))
    return pl.pallas_call(
        _post_kernel,
        grid=(T // tm,),
        in_specs=[row_in(D_MODEL), row_in(512), row_in(512), full(wom), full(wos), full(wg), full(wout),
                  full(g1), full(b1), full(wr_t), full(rbias), full(tri)],
        out_specs=[row(D_MODEL), pl.BlockSpec((tm * ROW_SUB, LANES), lambda i: (i, 0)), col, col, col,
                   pl.BlockSpec((N_EXPERTS, LANES), lambda i: (0, 0))],
        out_shape=[jax.ShapeDtypeStruct((T, D_MODEL), F32),
                   jax.ShapeDtypeStruct((T * ROW_SUB, LANES), jnp.uint32),
                   jax.ShapeDtypeStruct((TOP_K, T), jnp.int32),
                   jax.ShapeDtypeStruct((TOP_K, T), F32),
                   jax.ShapeDtypeStruct((TOP_K, T), jnp.int32),
                   jax.ShapeDtypeStruct((N_EXPERTS, LANES), jnp.int32)],
        scratch_shapes=[pltpu.VMEM((N_EXPERTS, LANES), F32)],
        compiler_params=pltpu.CompilerParams(dimension_semantics=("arbitrary",),
                                             vmem_limit_bytes=VMEM_LIMIT),
        name="post",
    )(x2, o_mla, o_swa, wom, wos, wg, wout, g1, b1, wr_t, rbias, tri)


def _dest_kernel(start_ref, e_ref, r_ref, o_ref):
    e = e_ref[...]
    base = jnp.zeros(e.shape, jnp.int32)
    for j in range(N_EXPERTS):
        base = jnp.where(e == j, start_ref[j], base)
    o_ref[...] = base + r_ref[...]


def _dest(seg_start, e_t, r_t):
    T = e_t.shape[1]
    tn = min(T, 4096)
    col = pl.BlockSpec((TOP_K, tn), lambda i: (0, i))
    return pl.pallas_call(
        _dest_kernel,
        grid=(T // tn,),
        in_specs=[pl.BlockSpec(memory_space=pltpu.SMEM), col, col],
        out_specs=col,
        out_shape=jax.ShapeDtypeStruct((TOP_K, T), jnp.int32),
        compiler_params=pltpu.CompilerParams(dimension_semantics=("parallel",)),
        name="dest",
    )(seg_start, e_t, r_t)


def _sc_worker_chunks(n_tokens):
    info = plsc.get_sparse_core_info()
    n_workers = info.num_cores * info.num_subcores
    per_worker = n_tokens // SC_CHUNK // n_workers
    assert per_worker * n_workers * SC_CHUNK == n_tokens
    first = (lax.axis_index("s") * info.num_cores + lax.axis_index("c")) * per_worker
    return first, per_worker


def _sc_dispatch(h1p, idx, n_rows):
    n_tokens = h1p.shape[0] // ROW_SUB
    R = SC_CHUNK * ROW_SUB

    def body(h_hbm, idx_hbm, xs_hbm, idx_v, rows_v, sem):
        first, per_worker = _sc_worker_chunks(n_tokens)

        @pl.loop(0, per_worker)
        def _(ci):
            c = first + ci
            pltpu.sync_copy(idx_hbm.at[c], idx_v)
            pltpu.sync_copy(h_hbm.at[pl.ds(c * R, R)], rows_v)
            copies = [pltpu.make_async_copy(rows_v, xs_hbm.at[idx_v.at[k]], sem) for k in range(TOP_K)]
            for cp in copies:
                cp.start()
            for cp in copies:
                cp.wait()

    run = pl.kernel(
        body, out_type=jax.ShapeDtypeStruct((n_rows * ROW_SUB, LANES), jnp.uint32),
        mesh=plsc.VectorSubcoreMesh(core_axis_name="c", subcore_axis_name="s"),
        scratch_types=[pltpu.VMEM((TOP_K, R), jnp.int32), pltpu.VMEM((R, LANES), jnp.uint32),
                       pltpu.SemaphoreType.DMA],
        name="sc_dispatch")
    return run(h1p, idx)


def _sc_gather(ys, idx, n_tokens):
    R = SC_CHUNK * ROW_SUB
    half = TOP_K // 2

    def body(ys_hbm, idx_hbm, yg_hbm, idx_v, buf, gsem, wsem):
        first, per_worker = _sc_worker_chunks(n_tokens)

        @pl.loop(0, per_worker)
        def _(ci):
            c = first + ci
            pltpu.sync_copy(idx_hbm.at[c], idx_v)
            for k0 in (0, half):
                gathers = [pltpu.make_async_copy(ys_hbm.at[idx_v.at[k0 + k]], buf.at[k], gsem)
                           for k in range(half)]
                for cp in gathers:
                    cp.start()
                for cp in gathers:
                    cp.wait()
                writes = [pltpu.make_async_copy(buf.at[k], yg_hbm.at[k0 + k, pl.ds(c * R, R)], wsem)
                          for k in range(half)]
                for cp in writes:
                    cp.start()
                for cp in writes:
                    cp.wait()

    run = pl.kernel(
        body, out_type=jax.ShapeDtypeStruct((TOP_K, n_tokens * ROW_SUB, LANES), jnp.uint32),
        mesh=plsc.VectorSubcoreMesh(core_axis_name="c", subcore_axis_name="s"),
        scratch_types=[pltpu.VMEM((TOP_K, R), jnp.int32), pltpu.VMEM((half, R, LANES), jnp.uint32),
                       pltpu.SemaphoreType.DMA, pltpu.SemaphoreType.DMA],
        name="sc_gather")
    return run(ys, idx)


def _experts_kernel(first_ref, nblk_ref, cnt_ref, xs_hbm, wup_ref, wdn_ref, after_ref, ys_hbm,
                    wup_bf, wdn_bf, xbuf, ybuf, in_sem, out_sem):
    del after_ref
    e = pl.program_id(0)
    n = nblk_ref[e]
    b0 = first_ref[e]
    count = cnt_ref[e]
    total = first_ref[N_EXPERTS - 1] + nblk_ref[N_EXPERTS - 1]
    rb = ROW_BLK * ROW_SUB
    depth = EXPERT_SLOTS

    def slot_of(b):
        return lax.rem(b, depth)

    def rows_of(b):
        return pl.ds(pl.multiple_of(b * rb, rb), rb)

    def in_copy(b):
        return pltpu.make_async_copy(xs_hbm.at[rows_of(b)], xbuf.at[slot_of(b)], in_sem.at[slot_of(b)])

    def out_copy(b):
        return pltpu.make_async_copy(ybuf.at[slot_of(b)], ys_hbm.at[rows_of(b)], out_sem.at[slot_of(b)])

    @pl.when(e == 0)
    def _():
        for b in range(depth - 1):
            @pl.when(b < total)
            def _():
                in_copy(b).start()

    @pl.when(n > 0)
    def _():
        wup_bf[...] = wup_ref[0].astype(BF16)
        wdn_bf[...] = wdn_ref[0].astype(BF16)

        def body(b, carry):
            slot = slot_of(b)
            in_copy(b).wait()

            @pl.when(b + depth - 1 < total)
            def _():
                in_copy(b + depth - 1).start()

            @pl.when(b >= depth)
            def _():
                out_copy(b - depth).wait()

            sub = ROW_BLK // EXPERT_SPLIT
            n_valid = count - (b - b0) * ROW_BLK
            xbs = [jnp.concatenate(
                [c.astype(BF16) for c in _unpack_rows(xbuf.at[slot], sub, first=i * sub, n_valid=n_valid)],
                axis=1) for i in range(EXPERT_SPLIT)]
            gus = [_dot(xb, wup_bf[...]) for xb in xbs]
            hids = [(gu[:, :EXPERT_HIDDEN] * _sigmoid(gu[:, :EXPERT_HIDDEN])
                     * gu[:, EXPERT_HIDDEN:]).astype(BF16) for gu in gus]
            ys = [_dot(hid, wdn_bf[...]) for hid in hids]
            for i in range(EXPERT_SPLIT):
                _pack_rows(ys[i], ybuf.at[slot], first=i * sub)
            out_copy(b).start()
            return carry

        lax.fori_loop(b0, b0 + n, body, 0)

    @pl.when(e == N_EXPERTS - 1)
    def _():
        for back in range(depth, 0, -1):
            @pl.when(total - back >= 0)
            def _():
                out_copy(total - back).wait()


def _experts(first_blk, n_blk, counts, xs, w_exp_up, w_exp_down, run_after):
    rb = ROW_BLK * ROW_SUB
    grid_spec = pltpu.PrefetchScalarGridSpec(
        num_scalar_prefetch=3,
        grid=(N_EXPERTS,),
        in_specs=[pl.BlockSpec(memory_space=pl.ANY),
                  pl.BlockSpec((1, D_MODEL, 2 * EXPERT_HIDDEN), lambda e, fb, nb, ct: (e, 0, 0)),
                  pl.BlockSpec((1, EXPERT_HIDDEN, D_MODEL), lambda e, fb, nb, ct: (e, 0, 0)),
                  pl.BlockSpec(memory_space=pl.ANY)],
        out_specs=pl.BlockSpec(memory_space=pl.ANY),
        scratch_shapes=[pltpu.VMEM((D_MODEL, 2 * EXPERT_HIDDEN), BF16),
                        pltpu.VMEM((EXPERT_HIDDEN, D_MODEL), BF16),
                        pltpu.VMEM((EXPERT_SLOTS, rb, LANES), jnp.uint32),
                        pltpu.VMEM((EXPERT_SLOTS, rb, LANES), jnp.uint32),
                        pltpu.SemaphoreType.DMA((EXPERT_SLOTS,)),
                        pltpu.SemaphoreType.DMA((EXPERT_SLOTS,))],
    )
    return pl.pallas_call(
        _experts_kernel,
        grid_spec=grid_spec,
        out_shape=jax.ShapeDtypeStruct(xs.shape, jnp.uint32),
        compiler_params=pltpu.CompilerParams(dimension_semantics=("arbitrary",),
                                             vmem_limit_bytes=VMEM_LIMIT),
        name="experts",
    )(first_blk, n_blk, counts, xs, w_exp_up, w_exp_down, run_after)


def _shared_kernel(h1_ref, wsu_ref, wsd_ref, o_ref):
    gu = _dot(h1_ref[...].astype(BF16), wsu_ref[...])
    g = gu[:, :SHARED_HIDDEN]
    hid = g * _sigmoid(g) * gu[:, SHARED_HIDDEN:]
    o_ref[...] = _dot(hid.astype(BF16), wsd_ref[...]).astype(BF16)


def _shared(h1, wsu, wsd):
    T = h1.shape[0]
    tm = SHARED_TM
    full = lambda a: pl.BlockSpec(a.shape, lambda i: (0, 0))
    row = pl.BlockSpec((tm, D_MODEL), lambda i: (i, 0))
    return pl.pallas_call(
        _shared_kernel,
        grid=(T // tm,),
        in_specs=[row, full(wsu), full(wsd)],
        out_specs=row,
        out_shape=jax.ShapeDtypeStruct((T, D_MODEL), BF16),
        compiler_params=pltpu.CompilerParams(dimension_semantics=("parallel",),
                                             vmem_limit_bytes=VMEM_LIMIT),
        name="shared",
    )(h1, wsu, wsd)


def _combine_kernel(w_ref, h1_ref, sh_ref, yg_ref, g2_ref, b2_ref, *rest):
    o_ref = rest[-1]
    tm = h1_ref.shape[0]
    h1 = h1_ref[...]
    ffn = sh_ref[...].astype(F32)

    w = w_ref[...]
    acc = [None] * (2 * ROW_SUB)
    for k in range(TOP_K):
        wk = w[:, k:k + 1]
        chunks = _unpack_rows(yg_ref.at[k], tm)
        for c in range(2 * ROW_SUB):
            acc[c] = wk * chunks[c] if acc[c] is None else acc[c] + wk * chunks[c]
    routed = jnp.concatenate(acc, axis=1)
    z = DEEPNORM_ALPHA * h1 + (routed + ffn)
    mu = jnp.mean(z, axis=-1, keepdims=True)
    zc = z - mu
    var = jnp.mean(zc * zc, axis=-1, keepdims=True)
    o_ref[...] = zc * lax.rsqrt(var + LN_EPS) * g2_ref[...] + b2_ref[...]


def _combine(w_tok, h1, sh, yg, g2, b2, in_first, out_first, out_rows, out_prev):
    n_tok = yg.shape[1] // ROW_SUB
    tm = COMBINE_TM
    fi, fo = in_first // tm, out_first // tm
    full = lambda a: pl.BlockSpec(a.shape, lambda i: (0, 0))
    row = pl.BlockSpec((tm, D_MODEL), lambda i: (fi + i, 0))
    in_specs = [pl.BlockSpec((tm, TOP_K), lambda i: (fi + i, 0)), row, row,
                pl.BlockSpec((TOP_K, tm * ROW_SUB, LANES), lambda i: (0, i, 0)),
                full(g2), full(b2)]
    args = [w_tok, h1, sh, yg, g2, b2]
    aliases = {}
    if out_prev is not None:
        in_specs.append(pl.BlockSpec(memory_space=pl.ANY))
        args.append(out_prev)
        aliases = {len(args) - 1: 0}
    return pl.pallas_call(
        _combine_kernel,
        grid=(n_tok // tm,),
        in_specs=in_specs,
        out_specs=pl.BlockSpec((tm, D_MODEL), lambda i: (fo + i, 0)),
        out_shape=jax.ShapeDtypeStruct((out_rows, D_MODEL), F32),
        input_output_aliases=aliases,
        compiler_params=pltpu.CompilerParams(dimension_semantics=("parallel",),
                                             vmem_limit_bytes=VMEM_LIMIT),
        name="combine",
    )(*args)


def _prep_weights(w_in, w_uq, w_ukv, seq):
    z = lambda n: jnp.zeros((D_MODEL, n), F32)
    kr = w_in[:, 512:544]
    qs = w_in[:, 544:1056] * (LOG2E / math.sqrt(SWA_HEAD_DIM))
    ks0, ks1 = w_in[:, 1056:1120], w_in[:, 1120:1184]
    half = MLA_ROPE // 2
    w1 = jnp.concatenate([
        w_in[:, 0:512], qs,
        ks0, z(64), z(64), ks0, ks1, z(64), z(64), ks1,
        z(64), kr, z(32),
        z(64), -kr[:, half:], kr[:, :half], z(32)], axis=1).astype(BF16)
    wvst = w_in[:, 1184:1312].T.astype(BF16)
    wg = w_in[:, 1312:3360].astype(BF16)

    zq = lambda n: jnp.zeros((MLA_Q_LORA, n), F32)
    qd = MLA_NOPE + MLA_ROPE
    q_cols, r_cols, k_cols, v_cols = [], [], [], []
    for h in range(MLA_HEADS):
        wq = w_uq[:, h * qd:(h + 1) * qd]
        q_cols += [wq, zq(32)]
        r_cols += [zq(64), -wq[:, MLA_NOPE + half:], wq[:, MLA_NOPE:MLA_NOPE + half], zq(32)]
        wk = w_ukv[:, h * 128:(h + 1) * 128]
        k_cols += [wk[:, :MLA_NOPE], zq(64)]
        v_cols += [wk[:, MLA_NOPE:]]
    wq2 = jnp.concatenate(q_cols + r_cols, axis=1).astype(BF16)
    wk = jnp.concatenate(k_cols, axis=1).astype(BF16)
    wvt = jnp.concatenate(v_cols, axis=1).T.astype(BF16)

    inv_freq = ROPE_THETA ** (-jnp.arange(0, MLA_ROPE, 2, dtype=F32) / MLA_ROPE)
    ang = jnp.arange(seq, dtype=F32)[:, None] * inv_freq[None, :]
    cos, sin = jnp.cos(ang), jnp.sin(ang)
    one, zero = jnp.ones((seq, 64), F32), jnp.zeros((seq, 64), F32)
    z32 = jnp.zeros((seq, 32), F32)
    scale = LOG2E / math.sqrt(MLA_NOPE + MLA_ROPE)
    tabs = (jnp.concatenate([one, cos, cos, z32], axis=1) * scale,
            jnp.concatenate([zero, sin, sin, z32], axis=1) * scale,
            jnp.concatenate([zero, cos, cos, z32], axis=1),
            jnp.concatenate([zero, sin, sin, z32], axis=1))
    return w1, wvst, wg, wq2, wk, wvt, tabs


def kernel(x, w_in, mla_q_norm, mla_kv_norm, w_uq, w_ukv, attn_sinks, w_o_mla, w_o_swa, w_out,
           ln1_g, ln1_b, w_router, router_bias, w_exp_up, w_exp_down, w_sh_up, w_sh_down,
           ln2_g, ln2_b):
    batch, seq, _ = x.shape
    T = batch * seq
    x2 = x.reshape(T, D_MODEL)
    w1, wvst, wg, wq2, wk, wvt, tabs = _prep_weights(w_in, w_uq, w_ukv, seq)
    q, k, vt, qs, ks, vst = _proj(x2, w1, wvst, wq2, wk, wvt, mla_q_norm.reshape(1, -1),
                                  mla_kv_norm.reshape(1, -1), tabs, seq)
    o_mla = _mla(q, k, vt, batch, seq)
    o_swa = _swa(attn_sinks.astype(F32), qs, ks, vst, batch, seq)

    tri = (lax.broadcasted_iota(jnp.int32, (POST_TM, POST_TM), 0)
           < lax.broadcasted_iota(jnp.int32, (POST_TM, POST_TM), 1)).astype(BF16)
    post_weights = (w_o_mla.astype(BF16), w_o_swa.astype(BF16), wg, w_out.astype(BF16),
                    ln1_g.reshape(1, -1), ln1_b.reshape(1, -1), w_router.T.astype(BF16),
                    router_bias.reshape(-1, 1).astype(F32), tri)
    wsu, wsd = w_sh_up.astype(BF16), w_sh_down.astype(BF16)
    g2, b2 = ln2_g.reshape(1, -1), ln2_b.reshape(1, -1)

    Tp = T // MOE_PARTS
    n_rows = (Tp * TOP_K // ROW_BLK + N_EXPERTS) * ROW_BLK
    out = None
    for part in range(MOE_PARTS):
        h1, h1p, e_t, w_t, r_t, cnt = _post(x2, o_mla, o_swa, *post_weights, part)
        counts = cnt[:, 0]
        pad = (counts + ROW_BLK - 1) // ROW_BLK * ROW_BLK
        pad_start = jnp.cumsum(pad) - pad
        dest = _dest(pad_start.astype(jnp.int32), e_t, r_t)
        idx = (dest.reshape(TOP_K, Tp // SC_CHUNK, SC_CHUNK, 1) * ROW_SUB
               + jnp.arange(ROW_SUB, dtype=jnp.int32))
        idx = idx.transpose(1, 0, 2, 3).reshape(Tp // SC_CHUNK, TOP_K, SC_CHUNK * ROW_SUB)
        xs = _sc_dispatch(h1p, idx, n_rows)
        sh = _shared(h1, wsu, wsd)
        ys = _experts((pad_start // ROW_BLK).astype(jnp.int32), (pad // ROW_BLK).astype(jnp.int32),
                      counts.astype(jnp.int32), xs, w_exp_up, w_exp_down, sh)
        Tc = Tp // COMBINE_PARTS
        n_chunks = Tc // SC_CHUNK
        w_tok = w_t.T
        for cpart in range(COMBINE_PARTS):
            yg = _sc_gather(ys, idx[cpart * n_chunks:(cpart + 1) * n_chunks], Tc)
            out = _combine(w_tok, h1, sh, yg, g2, b2, cpart * Tc, part * Tp + cpart * Tc, T, out)
    return out.reshape(batch, seq, D_MODEL)
```

```python
import math

import jax
import jax.numpy as jnp
from jax import lax
from jax.experimental import pallas as pl
from jax.experimental.pallas import tpu as pltpu
from jax.experimental.pallas import tpu_sc as plsc

D_MODEL = 1024
MLA_HEADS = 8
MLA_Q_LORA = 256
MLA_KV_LORA = 256
MLA_NOPE = 64
MLA_ROPE = 32
MLA_V = 64
VT_ROWS = 80
ROPE_THETA = 10000.0
SWA_HEADS = 8
SWA_KV_HEADS = 2
SWA_GROUP = SWA_HEADS // SWA_KV_HEADS
SWA_HEAD_DIM = 64
SWA_WINDOW = 128
N_EXPERTS = 256
TOP_K = 8
N_GROUPS = 8
GROUP_SIZE = N_EXPERTS // N_GROUPS
TOPK_GROUPS = 4
EXPERT_HIDDEN = 256
SHARED_HIDDEN = 256
ROUTED_SCALE = 2.5
DEEPNORM_ALPHA = 2.0 ** 0.25
LN_EPS = 1e-5
RMS_EPS = 1e-6

LANES = 128
ROW_WORDS = D_MODEL // 2
ROW_SUB = ROW_WORDS // LANES
VMEM_LIMIT = 48 * 1024 * 1024

PROJ_TM = 512
MLA_TQ = 256
MLA_TK = 512
MLA_HPS = 8
POST_TM = 512
ROUTE_TN = 256
ROW_BLK = 256
EXPERT_SLOTS = 4
EXPERT_SPLIT = 2
SHARED_TM = 512
COMBINE_TM = 256
SC_CHUNK = 32
MOE_PARTS = 1
COMBINE_PARTS = 4

BF16 = jnp.bfloat16
F32 = jnp.float32
NEG_INF = float("-inf")
LOG2E = math.log2(math.e)


def _sigmoid(v):
    return 1.0 / (1.0 + jnp.exp(-v))


def _dot(a, b):
    return jnp.dot(a, b, preferred_element_type=F32)


def _dot_nt(a, b):
    return lax.dot_general(a, b, (((1,), (1,)), ((), ())), preferred_element_type=F32)


def _pack_rows(y, out_ref, first=0):
    rows = y.shape[0]
    for j in range(ROW_SUB):
        a = y[:, j * LANES:(j + 1) * LANES].astype(BF16).astype(F32)
        b = y[:, ROW_WORDS + j * LANES:ROW_WORDS + (j + 1) * LANES].astype(BF16).astype(F32)
        ua = pltpu.bitcast(a, jnp.uint32) >> 16
        ub = pltpu.bitcast(b, jnp.uint32)
        out_ref[pl.ds(first * ROW_SUB + j, rows, stride=ROW_SUB), :] = ua | ub


def _unpack_rows(ref, rows, first=0, n_valid=None):
    lo, hi = [], []
    if n_valid is not None:
        live = first + lax.broadcasted_iota(jnp.int32, (rows, LANES), 0) < n_valid
    for j in range(ROW_SUB):
        u = ref[pl.ds(first * ROW_SUB + j, rows, stride=ROW_SUB), :]
        if n_valid is not None:
            u = jnp.where(live, u, jnp.uint32(0))
        lo.append(pltpu.bitcast(u << 16, F32))
        hi.append(pltpu.bitcast(u & jnp.uint32(0xFFFF0000), F32))
    return lo + hi


def _proj_kernel(x_ref, w1_ref, wvst_ref, wq_ref, wk_ref, wvt_ref, gq_ref, gkv_ref,
                 cq_ref, sq_ref, ck_ref, sk_ref,
                 q_ref, k_ref, vt_ref, qs_ref, ks_ref, vst_ref):
    xb = x_ref[...].astype(BF16)
    p = _dot(xb, w1_ref[...])

    def rms(c, g):
        return c * lax.rsqrt(jnp.mean(c * c, axis=-1, keepdims=True) + RMS_EPS) * g

    cqn = rms(p[:, 0:256], gq_ref[...]).astype(BF16)
    ckvn = rms(p[:, 256:512], gkv_ref[...]).astype(BF16)
    qs_ref[...] = p[:, 512:1024].astype(BF16)
    ks_ref[...] = p[:, 1024:1536].astype(BF16)
    ones = jnp.ones((VT_ROWS - MLA_V, xb.shape[0]), BF16)
    vst = _dot_nt(wvst_ref[...], xb).astype(BF16)
    for h in range(SWA_KV_HEADS):
        vst_ref[h * VT_ROWS:h * VT_ROWS + SWA_HEAD_DIM, :] = vst[h * SWA_HEAD_DIM:(h + 1) * SWA_HEAD_DIM, :]
        vst_ref[h * VT_ROWS + SWA_HEAD_DIM:(h + 1) * VT_ROWS, :] = ones
    kr = p[:, 1536:1664] * ck_ref[...] + p[:, 1664:1792] * sk_ref[...]
    qq = _dot(cqn, wq_ref[...])
    kn = _dot(ckvn, wk_ref[...])
    cq = cq_ref[...]
    sq = sq_ref[...]
    for h in range(MLA_HEADS):
        a = qq[:, h * LANES:(h + 1) * LANES]
        b = qq[:, 1024 + h * LANES:1024 + (h + 1) * LANES]
        q_ref[:, h * LANES:(h + 1) * LANES] = (a * cq + b * sq).astype(BF16)
        k_ref[:, h * LANES:(h + 1) * LANES] = (kn[:, h * LANES:(h + 1) * LANES] + kr).astype(BF16)
    vt = _dot_nt(wvt_ref[...], ckvn).astype(BF16)
    for h in range(MLA_HEADS):
        vt_ref[h * VT_ROWS:h * VT_ROWS + MLA_V, :] = vt[h * MLA_V:(h + 1) * MLA_V, :]
        vt_ref[h * VT_ROWS + MLA_V:(h + 1) * VT_ROWS, :] = ones


def _proj(x2, w1, wvst, wq2, wk, wvt, gq, gkv, tabs, seq):
    T = x2.shape[0]
    tm = PROJ_TM
    nper = seq // tm
    full = lambda shape: pl.BlockSpec(shape, lambda i: (0, 0))
    tab = pl.BlockSpec((tm, LANES), lambda i: (i % nper, 0))
    row = lambda n: pl.BlockSpec((tm, n), lambda i: (i, 0))
    col = lambda n: pl.BlockSpec((n, tm), lambda i: (0, i))
    return pl.pallas_call(
        _proj_kernel,
        grid=(T // tm,),
        in_specs=[row(D_MODEL), full(w1.shape), full(wvst.shape), full(wq2.shape), full(wk.shape),
                  full(wvt.shape), full(gq.shape), full(gkv.shape), tab, tab, tab, tab],
        out_specs=[row(1024), row(1024), col(MLA_HEADS * VT_ROWS), row(512), row(512),
                   col(SWA_KV_HEADS * VT_ROWS)],
        out_shape=[jax.ShapeDtypeStruct((T, 1024), BF16), jax.ShapeDtypeStruct((T, 1024), BF16),
                   jax.ShapeDtypeStruct((MLA_HEADS * VT_ROWS, T), BF16), jax.ShapeDtypeStruct((T, 512), BF16),
                   jax.ShapeDtypeStruct((T, 512), BF16),
                   jax.ShapeDtypeStruct((SWA_KV_HEADS * VT_ROWS, T), BF16)],
        compiler_params=pltpu.CompilerParams(dimension_semantics=("parallel",),
                                             vmem_limit_bytes=VMEM_LIMIT),
        name="proj",
    )(x2, w1, wvst, wq2, wk, wvt, gq, gkv, *tabs)


def _mla_kernel(q_ref, k_ref, vt_ref, o_ref, *acc_scr):
    tq = q_ref.shape[0]
    qi = pl.program_id(2)
    for acc in acc_scr:
        acc[...] = jnp.zeros(acc.shape, F32)

    def step(ks, tk, maxes, masked):
        scores = [_dot_nt(k_ref[pl.ds(ks, tk), h * LANES:(h + 1) * LANES],
                          q_ref[:, h * LANES:(h + 1) * LANES]) for h in range(MLA_HPS)]
        new_maxes, probs, alphas = [], [], []
        for h in range(MLA_HPS):
            s = scores[h]
            if masked:
                key = ks + lax.broadcasted_iota(jnp.int32, s.shape, 0)
                qry = qi * tq + lax.broadcasted_iota(jnp.int32, s.shape, 1)
                s = jnp.where(key <= qry, s, NEG_INF)
            m_new = jnp.maximum(maxes[h], jnp.max(s, axis=0, keepdims=True))
            alphas.append(jnp.exp2(maxes[h] - m_new))
            probs.append(jnp.exp2(s - m_new).astype(BF16))
            new_maxes.append(m_new)
        for h in range(MLA_HPS):
            pv = _dot(vt_ref[h * VT_ROWS:(h + 1) * VT_ROWS, pl.ds(ks, tk)], probs[h])
            acc_scr[h][...] = acc_scr[h][...] * alphas[h] + pv
        return tuple(new_maxes)

    init = tuple(jnp.full((1, tq), NEG_INF, F32) for _ in range(MLA_HPS))
    big = MLA_TK
    n_big = (qi * tq) // big
    maxes = lax.fori_loop(
        0, n_big, lambda kc, mx: step(pl.multiple_of(kc * big, big), big, mx, False), init)
    rest = pl.multiple_of(n_big * big, tq)
    maxes = lax.cond(
        rest < qi * tq,
        lambda mx: step(rest, 2 * tq, mx, True),
        lambda mx: step(rest, tq, mx, True),
        maxes)
    for h2 in range(MLA_HPS // 2):
        out_t = jnp.concatenate(
            [acc_scr[2 * h2 + g][0:MLA_V, :] / acc_scr[2 * h2 + g][MLA_V:MLA_V + 1, :] for g in range(2)],
            axis=0)
        o_ref[:, h2 * LANES:(h2 + 1) * LANES] = out_t.T.astype(BF16)


def _mla(q, k, vt, batch, seq):
    T = q.shape[0]
    tq = MLA_TQ
    nq = seq // tq
    hps = MLA_HPS
    return pl.pallas_call(
        _mla_kernel,
        grid=(batch, MLA_HEADS // hps, nq),
        in_specs=[pl.BlockSpec((tq, hps * LANES), lambda b, j, i: (b * nq + i, j)),
                  pl.BlockSpec((seq, hps * LANES), lambda b, j, i: (b, j)),
                  pl.BlockSpec((hps * VT_ROWS, seq), lambda b, j, i: (j, b))],
        out_specs=pl.BlockSpec((tq, hps * MLA_V), lambda b, j, i: (b * nq + i, j)),
        out_shape=jax.ShapeDtypeStruct((T, MLA_HEADS * MLA_V), BF16),
        scratch_shapes=[pltpu.VMEM((VT_ROWS, tq), F32) for _ in range(hps)],
        compiler_params=pltpu.CompilerParams(
            dimension_semantics=("parallel", "parallel", "arbitrary"), vmem_limit_bytes=VMEM_LIMIT),
        name="mla",
    )(q, k, vt)


def _swa_kernel(sink_ref, q_ref, kc_ref, kp_ref, vtc_ref, vtp_ref, bias_ref, o_ref):
    scores = []
    for head in range(SWA_HEADS):
        pair, g, kvh = head // 2, head % 2, head // SWA_GROUP
        col = (2 * kvh + g) * LANES
        band = jnp.concatenate([kp_ref[:, col:col + LANES], kc_ref[:, col:col + LANES]], axis=0)
        scores.append(_dot_nt(band, q_ref[:, pair * LANES:(pair + 1) * LANES]))
    probs, sink_terms = [], []
    for head in range(SWA_HEADS):
        s = scores[head] + bias_ref[0, head]
        sink = sink_ref[head] * LOG2E
        m = jnp.maximum(jnp.max(s, axis=0, keepdims=True), sink)
        probs.append(jnp.exp2(s - m).astype(BF16))
        sink_terms.append(jnp.exp2(sink - m))
    outs = []
    for head in range(SWA_HEADS):
        rows = slice((head // SWA_GROUP) * VT_ROWS, (head // SWA_GROUP + 1) * VT_ROWS)
        v_band = jnp.concatenate([vtp_ref[rows, :], vtc_ref[rows, :]], axis=1)
        pv = _dot(v_band, probs[head])
        outs.append(pv[0:SWA_HEAD_DIM, :] / (pv[SWA_HEAD_DIM:SWA_HEAD_DIM + 1, :] + sink_terms[head]))
    o_ref[...] = jnp.concatenate(outs, axis=0).T.astype(BF16)


def _swa_bias():
    W = SWA_WINDOW
    j = jnp.arange(2 * W, dtype=jnp.int32)[:, None]
    i = jnp.arange(W, dtype=jnp.int32)[None, :]
    dist = i + W - j
    valid = (dist >= 0) & (dist < W)
    slopes = 2.0 ** (-8.0 * jnp.arange(1, SWA_HEADS + 1, dtype=F32) / SWA_HEADS)
    pen = -(slopes[:, None, None] * dist.astype(F32)[None]) * LOG2E
    general = jnp.where(valid[None], pen, NEG_INF)
    first = jnp.where((valid & (j >= W))[None], pen, NEG_INF)
    return jnp.stack([first, general])


def _swa(sinks, qs, ks, vst, batch, seq):
    T = qs.shape[0]
    W = SWA_WINDOW
    nb = seq // W
    cur = lambda n: pl.BlockSpec((W, n), lambda b, i: (b * nb + i, 0))
    prev = lambda n: pl.BlockSpec((W, n), lambda b, i: (b * nb + jnp.maximum(i - 1, 0), 0))
    vt_cur = pl.BlockSpec((SWA_KV_HEADS * VT_ROWS, W), lambda b, i: (0, b * nb + i))
    vt_prev = pl.BlockSpec((SWA_KV_HEADS * VT_ROWS, W), lambda b, i: (0, b * nb + jnp.maximum(i - 1, 0)))
    bias = pl.BlockSpec((1, SWA_HEADS, 2 * W, W), lambda b, i: (jnp.minimum(i, 1), 0, 0, 0))
    return pl.pallas_call(
        _swa_kernel,
        grid=(batch, nb),
        in_specs=[pl.BlockSpec(memory_space=pltpu.SMEM), cur(512), cur(512), prev(512),
                  vt_cur, vt_prev, bias],
        out_specs=cur(512),
        out_shape=jax.ShapeDtypeStruct((T, 512), BF16),
        compiler_params=pltpu.CompilerParams(dimension_semantics=("parallel", "parallel"),
                                             vmem_limit_bytes=VMEM_LIMIT),
        name="swa",
    )(sinks, qs, ks, ks, vst, vst, _swa_bias())


def _post_kernel(x_ref, om_ref, os_ref, wom_ref, wos_ref, wg_ref, wout_ref, g1_ref, b1_ref,
                 wr_ref, rb_ref, tri_ref,
                 h1_ref, h1p_ref, e_ref, w_ref, r_ref, cnt_ref, carry_scr):
    tm = x_ref.shape[0]
    step = pl.program_id(0)

    @pl.when(step == 0)
    def _():
        carry_scr[...] = jnp.zeros(carry_scr.shape, F32)

    x = x_ref[...]
    ya = _dot(om_ref[...], wom_ref[...])
    yb = _dot(os_ref[...], wos_ref[...])
    gates = _dot(x.astype(BF16), wg_ref[...])
    merged = _sigmoid(gates[:, :D_MODEL]) * ya + _sigmoid(gates[:, D_MODEL:]) * yb
    mix = _dot(merged.astype(BF16), wout_ref[...])
    z = DEEPNORM_ALPHA * x + mix
    mu = jnp.mean(z, axis=-1, keepdims=True)
    zc = z - mu
    var = jnp.mean(zc * zc, axis=-1, keepdims=True)
    h1 = zc * lax.rsqrt(var + LN_EPS) * g1_ref[...] + b1_ref[...]
    h1_ref[...] = h1
    _pack_rows(h1, h1p_ref)

    all_scores = _sigmoid(_dot_nt(wr_ref[...], h1.astype(BF16)))
    carry = carry_scr[...]
    for c in range(tm // ROUTE_TN):
        cols = slice(c * ROUTE_TN, (c + 1) * ROUTE_TN)
        idxs, weights, ranks, carry = _route(all_scores[:, cols], rb_ref[...], tri_ref[...], carry)
        e_ref[:, cols] = idxs
        w_ref[:, cols] = weights
        r_ref[:, cols] = ranks
    carry_scr[...] = carry
    cnt_ref[...] = carry.astype(jnp.int32)


def _route(scores, bias, tri, carry):
    tn = scores.shape[1]
    choice = scores + bias
    row = lax.broadcasted_iota(jnp.int32, (N_EXPERTS, tn), 0)
    grow = lax.broadcasted_iota(jnp.int32, (GROUP_SIZE, tn), 0)
    gscore = []
    for g in range(N_GROUPS):
        blk = choice[g * GROUP_SIZE:(g + 1) * GROUP_SIZE, :]
        m1 = jnp.max(blk, axis=0, keepdims=True)
        i1 = jnp.min(jnp.where(blk == m1, grow, GROUP_SIZE), axis=0, keepdims=True)
        m2 = jnp.max(jnp.where(grow == i1, NEG_INF, blk), axis=0, keepdims=True)
        gscore.append(m1 + m2)
    gsc = jnp.concatenate(gscore, axis=0)
    gidx = lax.broadcasted_iota(jnp.int32, (N_GROUPS, tn), 0)
    grank = jnp.zeros((N_GROUPS, tn), jnp.int32)
    for g in range(N_GROUPS):
        sg = gsc[g:g + 1, :]
        beats = (sg > gsc) | ((sg == gsc) & (gidx > g))
        grank = grank + beats.astype(jnp.int32)
    gsel = (grank < TOPK_GROUPS).astype(F32)
    emask = jnp.concatenate(
        [jnp.broadcast_to(gsel[g:g + 1, :], (GROUP_SIZE, tn)) for g in range(N_GROUPS)], axis=0)
    work = jnp.where(emask > 0.0, choice, NEG_INF)
    eligible = work
    idxs, svals = [], []
    for _k in range(TOP_K):
        m = jnp.max(work, axis=0, keepdims=True)
        idx = jnp.min(jnp.where(work == m, row, N_EXPERTS), axis=0, keepdims=True)
        hit = row == idx
        svals.append(jnp.sum(jnp.where(hit, scores, 0.0), axis=0, keepdims=True))
        work = jnp.where(hit, NEG_INF, work)
        idxs.append(idx)
    sel = jnp.where(work != eligible, 1.0, 0.0)
    ssum = svals[0]
    for sv in svals[1:]:
        ssum = ssum + sv
    weights = jnp.concatenate([sv / ssum * ROUTED_SCALE for sv in svals], axis=0)

    rank = _dot(sel.astype(BF16), tri) + carry[:, 0:1]
    ranks = jnp.concatenate(
        [jnp.sum(jnp.where(row == idx, rank, 0.0), axis=0, keepdims=True) for idx in idxs],
        axis=0).astype(jnp.int32)
    carry = carry + jnp.sum(sel, axis=1, keepdims=True)
    return jnp.concatenate(idxs, axis=0), weights, ranks, carry


def _post(x2, o_mla, o_swa, wom, wos, wg, wout, g1, b1, wr_t, rbias, tri, part):
    T = x2.shape[0] // MOE_PARTS
    tm = POST_TM
    first = part * (T // tm)
    full = lambda a: pl.BlockSpec(a.shape, lambda i: (0, 0))
    row_in = lambda n: pl.BlockSpec((tm, n), lambda i: (first + i, 0))
    row = lambda n: pl.BlockSpec((tm, n), lambda i: (i, 0))
    col = pl.BlockSpec((TOP_K, tm), lambda i: (0, i))
    return pl.pallas_call(
        _post_kernel,
        grid=(T // tm,),
        in_specs=[row_in(D_MODEL), row_in(512), row_in(512), full(wom), full(wos), full(wg), full(wout),
                  full(g1), full(b1), full(wr_t), full(rbias), full(tri)],
        out_specs=[row(D_MODEL), pl.BlockSpec((tm * ROW_SUB, LANES), lambda i: (i, 0)), col, col, col,
                   pl.BlockSpec((N_EXPERTS, LANES), lambda i: (0, 0))],
        out_shape=[jax.ShapeDtypeStruct((T, D_MODEL), F32),
                   jax.ShapeDtypeStruct((T * ROW_SUB, LANES), jnp.uint32),
                   jax.ShapeDtypeStruct((TOP_K, T), jnp.int32),
                   jax.ShapeDtypeStruct((TOP_K, T), F32),
                   jax.ShapeDtypeStruct((TOP_K, T), jnp.int32),
                   jax.ShapeDtypeStruct((N_EXPERTS, LANES), jnp.int32)],
        scratch_shapes=[pltpu.VMEM((N_EXPERTS, LANES), F32)],
        compiler_params=pltpu.CompilerParams(dimension_semantics=("arbitrary",),
                                             vmem_limit_bytes=VMEM_LIMIT),
        name="post",
    )(x2, o_mla, o_swa, wom, wos, wg, wout, g1, b1, wr_t, rbias, tri)


def _dest_kernel(start_ref, e_ref, r_ref, o_ref):
    e = e_ref[...]
    base = jnp.zeros(e.shape, jnp.int32)
    for j in range(N_EXPERTS):
        base = jnp.where(e == j, start_ref[j], base)
    o_ref[...] = base + r_ref[...]


def _dest(seg_start, e_t, r_t):
    T = e_t.shape[1]
    tn = min(T, 4096)
    col = pl.BlockSpec((TOP_K, tn), lambda i: (0, i))
    return pl.pallas_call(
        _dest_kernel,
        grid=(T // tn,),
        in_specs=[pl.BlockSpec(memory_space=pltpu.SMEM), col, col],
        out_specs=col,
        out_shape=jax.ShapeDtypeStruct((TOP_K, T), jnp.int32),
        compiler_params=pltpu.CompilerParams(dimension_semantics=("parallel",)),
        name="dest",
    )(seg_start, e_t, r_t)


def _sc_worker_chunks(n_tokens):
    info = plsc.get_sparse_core_info()
    n_workers = info.num_cores * info.num_subcores
    per_worker = n_tokens // SC_CHUNK // n_workers
    assert per_worker * n_workers * SC_CHUNK == n_tokens
    first = (lax.axis_index("s") * info.num_cores + lax.axis_index("c")) * per_worker
    return first, per_worker


def _sc_dispatch(h1p, idx, n_rows):
    n_tokens = h1p.shape[0] // ROW_SUB
    R = SC_CHUNK * ROW_SUB

    def body(h_hbm, idx_hbm, xs_hbm, idx_v, rows_v, sem):
        first, per_worker = _sc_worker_chunks(n_tokens)

        @pl.loop(0, per_worker)
        def _(ci):
            c = first + ci
            pltpu.sync_copy(idx_hbm.at[c], idx_v)
            pltpu.sync_copy(h_hbm.at[pl.ds(c * R, R)], rows_v)
            copies = [pltpu.make_async_copy(rows_v, xs_hbm.at[idx_v.at[k]], sem) for k in range(TOP_K)]
            for cp in copies:
                cp.start()
            for cp in copies:
                cp.wait()

    run = pl.kernel(
        body, out_type=jax.ShapeDtypeStruct((n_rows * ROW_SUB, LANES), jnp.uint32),
        mesh=plsc.VectorSubcoreMesh(core_axis_name="c", subcore_axis_name="s"),
        scratch_types=[pltpu.VMEM((TOP_K, R), jnp.int32), pltpu.VMEM((R, LANES), jnp.uint32),
                       pltpu.SemaphoreType.DMA],
        name="sc_dispatch")
    return run(h1p, idx)


def _sc_gather(ys, idx, n_tokens):
    R = SC_CHUNK * ROW_SUB
    half = TOP_K // 2

    def body(ys_hbm, idx_hbm, yg_hbm, idx_v, buf, gsem, wsem):
        first, per_worker = _sc_worker_chunks(n_tokens)

        @pl.loop(0, per_worker)
        def _(ci):
            c = first + ci
            pltpu.sync_copy(idx_hbm.at[c], idx_v)
            for k0 in (0, half):
                gathers = [pltpu.make_async_copy(ys_hbm.at[idx_v.at[k0 + k]], buf.at[k], gsem)
                           for k in range(half)]
                for cp in gathers:
                    cp.start()
                for cp in gathers:
                    cp.wait()
                writes = [pltpu.make_async_copy(buf.at[k], yg_hbm.at[k0 + k, pl.ds(c * R, R)], wsem)
                          for k in range(half)]
                for cp in writes:
                    cp.start()
                for cp in writes:
                    cp.wait()

    run = pl.kernel(
        body, out_type=jax.ShapeDtypeStruct((TOP_K, n_tokens * ROW_SUB, LANES), jnp.uint32),
        mesh=plsc.VectorSubcoreMesh(core_axis_name="c", subcore_axis_name="s"),
        scratch_types=[pltpu.VMEM((TOP_K, R), jnp.int32), pltpu.VMEM((half, R, LANES), jnp.uint32),
                       pltpu.SemaphoreType.DMA, pltpu.SemaphoreType.DMA],
        name="sc_gather")
    return run(ys, idx)


def _experts_kernel(first_ref, nblk_ref, cnt_ref, xs_hbm, wup_ref, wdn_ref, after_ref, ys_hbm,
                    wup_bf, wdn_bf, xbuf, ybuf, in_sem, out_sem):
    del after_ref
    e = pl.program_id(0)
    n = nblk_ref[e]
    b0 = first_ref[e]
    count = cnt_ref[e]
    total = first_ref[N_EXPERTS - 1] + nblk_ref[N_EXPERTS - 1]
    rb = ROW_BLK * ROW_SUB
    depth = EXPERT_SLOTS

    def slot_of(b):
        return lax.rem(b, depth)

    def rows_of(b):
        return pl.ds(pl.multiple_of(b * rb, rb), rb)

    def in_copy(b):
        return pltpu.make_async_copy(xs_hbm.at[rows_of(b)], xbuf.at[slot_of(b)], in_sem.at[slot_of(b)])

    def out_copy(b):
        return pltpu.make_async_copy(ybuf.at[slot_of(b)], ys_hbm.at[rows_of(b)], out_sem.at[slot_of(b)])

    @pl.when(e == 0)
    def _():
        for b in range(depth - 1):
            @pl.when(b < total)
            def _():
                in_copy(b).start()

    @pl.when(n > 0)
    def _():
        wup_bf[...] = wup_ref[0].astype(BF16)
        wdn_bf[...] = wdn_ref[0].astype(BF16)

        def body(b, carry):
            slot = slot_of(b)
            in_copy(b).wait()

            @pl.when(b + depth - 1 < total)
            def _():
                in_copy(b + depth - 1).start()

            @pl.when(b >= depth)
            def _():
                out_copy(b - depth).wait()

            sub = ROW_BLK // EXPERT_SPLIT
            n_valid = count - (b - b0) * ROW_BLK
            xbs = [jnp.concatenate(
                [c.astype(BF16) for c in _unpack_rows(xbuf.at[slot], sub, first=i * sub, n_valid=n_valid)],
                axis=1) for i in range(EXPERT_SPLIT)]
            gus = [_dot(xb, wup_bf[...]) for xb in xbs]
            hids = [(gu[:, :EXPERT_HIDDEN] * _sigmoid(gu[:, :EXPERT_HIDDEN])
                     * gu[:, EXPERT_HIDDEN:]).astype(BF16) for gu in gus]
            ys = [_dot(hid, wdn_bf[...]) for hid in hids]
            for i in range(EXPERT_SPLIT):
                _pack_rows(ys[i], ybuf.at[slot], first=i * sub)
            out_copy(b).start()
            return carry

        lax.fori_loop(b0, b0 + n, body, 0)

    @pl.when(e == N_EXPERTS - 1)
    def _():
        for back in range(depth, 0, -1):
            @pl.when(total - back >= 0)
            def _():
                out_copy(total - back).wait()


def _experts(first_blk, n_blk, counts, xs, w_exp_up, w_exp_down, run_after):
    rb = ROW_BLK * ROW_SUB
    grid_spec = pltpu.PrefetchScalarGridSpec(
        num_scalar_prefetch=3,
        grid=(N_EXPERTS,),
        in_specs=[pl.BlockSpec(memory_space=pl.ANY),
                  pl.BlockSpec((1, D_MODEL, 2 * EXPERT_HIDDEN), lambda e, fb, nb, ct: (e, 0, 0)),
                  pl.BlockSpec((1, EXPERT_HIDDEN, D_MODEL), lambda e, fb, nb, ct: (e, 0, 0)),
                  pl.BlockSpec(memory_space=pl.ANY)],
        out_specs=pl.BlockSpec(memory_space=pl.ANY),
        scratch_shapes=[pltpu.VMEM((D_MODEL, 2 * EXPERT_HIDDEN), BF16),
                        pltpu.VMEM((EXPERT_HIDDEN, D_MODEL), BF16),
                        pltpu.VMEM((EXPERT_SLOTS, rb, LANES), jnp.uint32),
                        pltpu.VMEM((EXPERT_SLOTS, rb, LANES), jnp.uint32),
                        pltpu.SemaphoreType.DMA((EXPERT_SLOTS,)),
                        pltpu.SemaphoreType.DMA((EXPERT_SLOTS,))],
    )
    return pl.pallas_call(
        _experts_kernel,
        grid_spec=grid_spec,
        out_shape=jax.ShapeDtypeStruct(xs.shape, jnp.uint32),
        compiler_params=pltpu.CompilerParams(dimension_semantics=("arbitrary",),
                                             vmem_limit_bytes=VMEM_LIMIT),
        name="experts",
    )(first_blk, n_blk, counts, xs, w_exp_up, w_exp_down, run_after)


def _shared_kernel(h1p_ref, wsu_ref, wsd_ref, o_ref):
    tm = o_ref.shape[0]
    hb = jnp.concatenate([c.astype(BF16) for c in _unpack_rows(h1p_ref, tm)], axis=1)
    gu = _dot(hb, wsu_ref[...])
    g = gu[:, :SHARED_HIDDEN]
    hid = g * _sigmoid(g) * gu[:, SHARED_HIDDEN:]
    o_ref[...] = _dot(hid.astype(BF16), wsd_ref[...]).astype(BF16)


def _shared(h1p, wsu, wsd):
    T = h1p.shape[0] // ROW_SUB
    tm = SHARED_TM
    full = lambda a: pl.BlockSpec(a.shape, lambda i: (0, 0))
    row = pl.BlockSpec((tm, D_MODEL), lambda i: (i, 0))
    return pl.pallas_call(
        _shared_kernel,
        grid=(T // tm,),
        in_specs=[pl.BlockSpec((tm * ROW_SUB, LANES), lambda i: (i, 0)), full(wsu), full(wsd)],
        out_specs=row,
        out_shape=jax.ShapeDtypeStruct((T, D_MODEL), BF16),
        compiler_params=pltpu.CompilerParams(dimension_semantics=("parallel",),
                                             vmem_limit_bytes=VMEM_LIMIT),
        name="shared",
    )(h1p, wsu, wsd)


def _combine_kernel(w_ref, h1_ref, sh_ref, yg_ref, g2_ref, b2_ref, *rest):
    o_ref = rest[-1]
    tm = h1_ref.shape[0]
    h1 = h1_ref[...]
    ffn = sh_ref[...].astype(F32)

    w = w_ref[...]
    acc = [None] * (2 * ROW_SUB)
    for k in range(TOP_K):
        wk = w[:, k:k + 1]
        chunks = _unpack_rows(yg_ref.at[k], tm)
        for c in range(2 * ROW_SUB):
            acc[c] = wk * chunks[c] if acc[c] is None else acc[c] + wk * chunks[c]
    routed = jnp.concatenate(acc, axis=1)
    z = DEEPNORM_ALPHA * h1 + (routed + ffn)
    mu = jnp.mean(z, axis=-1, keepdims=True)
    zc = z - mu
    var = jnp.mean(zc * zc, axis=-1, keepdims=True)
    o_ref[...] = zc * lax.rsqrt(var + LN_EPS) * g2_ref[...] + b2_ref[...]


def _combine(w_tok, h1, sh, yg, g2, b2, in_first, out_first, out_rows, out_prev):
    n_tok = yg.shape[1] // ROW_SUB
    tm = COMBINE_TM
    fi, fo = in_first // tm, out_first // tm
    full = lambda a: pl.BlockSpec(a.shape, lambda i: (0, 0))
    row = pl.BlockSpec((tm, D_MODEL), lambda i: (fi + i, 0))
    in_specs = [pl.BlockSpec((tm, TOP_K), lambda i: (fi + i, 0)), row, row,
                pl.BlockSpec((TOP_K, tm * ROW_SUB, LANES), lambda i: (0, i, 0)),
                full(g2), full(b2)]
    args = [w_tok, h1, sh, yg, g2, b2]
    aliases = {}
    if out_prev is not None:
        in_specs.append(pl.BlockSpec(memory_space=pl.ANY))
        args.append(out_prev)
        aliases = {len(args) - 1: 0}
    return pl.pallas_call(
        _combine_kernel,
        grid=(n_tok // tm,),
        in_specs=in_specs,
        out_specs=pl.BlockSpec((tm, D_MODEL), lambda i: (fo + i, 0)),
        out_shape=jax.ShapeDtypeStruct((out_rows, D_MODEL), F32),
        input_output_aliases=aliases,
        compiler_params=pltpu.CompilerParams(dimension_semantics=("parallel",),
                                             vmem_limit_bytes=VMEM_LIMIT),
        name="combine",
    )(*args)


def _prep_weights(w_in, w_uq, w_ukv, seq):
    z = lambda n: jnp.zeros((D_MODEL, n), F32)
    kr = w_in[:, 512:544]
    qs = w_in[:, 544:1056] * (LOG2E / math.sqrt(SWA_HEAD_DIM))
    ks0, ks1 = w_in[:, 1056:1120], w_in[:, 1120:1184]
    half = MLA_ROPE // 2
    w1 = jnp.concatenate([
        w_in[:, 0:512], qs,
        ks0, z(64), z(64), ks0, ks1, z(64), z(64), ks1,
        z(64), kr, z(32),
        z(64), -kr[:, half:], kr[:, :half], z(32)], axis=1).astype(BF16)
    wvst = w_in[:, 1184:1312].T.astype(BF16)
    wg = w_in[:, 1312:3360].astype(BF16)

    zq = lambda n: jnp.zeros((MLA_Q_LORA, n), F32)
    qd = MLA_NOPE + MLA_ROPE
    q_cols, r_cols, k_cols, v_cols = [], [], [], []
    for h in range(MLA_HEADS):
        wq = w_uq[:, h * qd:(h + 1) * qd]
        q_cols += [wq, zq(32)]
        r_cols += [zq(64), -wq[:, MLA_NOPE + half:], wq[:, MLA_NOPE:MLA_NOPE + half], zq(32)]
        wk = w_ukv[:, h * 128:(h + 1) * 128]
        k_cols += [wk[:, :MLA_NOPE], zq(64)]
        v_cols += [wk[:, MLA_NOPE:]]
    wq2 = jnp.concatenate(q_cols + r_cols, axis=1).astype(BF16)
    wk = jnp.concatenate(k_cols, axis=1).astype(BF16)
    wvt = jnp.concatenate(v_cols, axis=1).T.astype(BF16)

    inv_freq = ROPE_THETA ** (-jnp.arange(0, MLA_ROPE, 2, dtype=F32) / MLA_ROPE)
    ang = jnp.arange(seq, dtype=F32)[:, None] * inv_freq[None, :]
    cos, sin = jnp.cos(ang), jnp.sin(ang)
    one, zero = jnp.ones((seq, 64), F32), jnp.zeros((seq, 64), F32)
    z32 = jnp.zeros((seq, 32), F32)
    scale = LOG2E / math.sqrt(MLA_NOPE + MLA_ROPE)
    tabs = (jnp.concatenate([one, cos, cos, z32], axis=1) * scale,
            jnp.concatenate([zero, sin, sin, z32], axis=1) * scale,
            jnp.concatenate([zero, cos, cos, z32], axis=1),
            jnp.concatenate([zero, sin, sin, z32], axis=1))
    return w1, wvst, wg, wq2, wk, wvt, tabs


def kernel(x, w_in, mla_q_norm, mla_kv_norm, w_uq, w_ukv, attn_sinks, w_o_mla, w_o_swa, w_out,
           ln1_g, ln1_b, w_router, router_bias, w_exp_up, w_exp_down, w_sh_up, w_sh_down,
           ln2_g, ln2_b):
    batch, seq, _ = x.shape
    T = batch * seq
    x2 = x.reshape(T, D_MODEL)
    w1, wvst, wg, wq2, wk, wvt, tabs = _prep_weights(w_in, w_uq, w_ukv, seq)
    q, k, vt, qs, ks, vst = _proj(x2, w1, wvst, wq2, wk, wvt, mla_q_norm.reshape(1, -1),
                                  mla_kv_norm.reshape(1, -1), tabs, seq)
    o_mla = _mla(q, k, vt, batch, seq)
    o_swa = _swa(attn_sinks.astype(F32), qs, ks, vst, batch, seq)

    tri = (lax.broadcasted_iota(jnp.int32, (ROUTE_TN, ROUTE_TN), 0)
           < lax.broadcasted_iota(jnp.int32, (ROUTE_TN, ROUTE_TN), 1)).astype(BF16)
    post_weights = (w_o_mla.astype(BF16), w_o_swa.astype(BF16), wg, w_out.astype(BF16),
                    ln1_g.reshape(1, -1), ln1_b.reshape(1, -1), w_router.T.astype(BF16),
                    router_bias.reshape(-1, 1).astype(F32), tri)
    wsu, wsd = w_sh_up.astype(BF16), w_sh_down.astype(BF16)
    g2, b2 = ln2_g.reshape(1, -1), ln2_b.reshape(1, -1)

    Tp = T // MOE_PARTS
    n_rows = (Tp * TOP_K // ROW_BLK + N_EXPERTS) * ROW_BLK
    out = None
    for part in range(MOE_PARTS):
        h1, h1p, e_t, w_t, r_t, cnt = _post(x2, o_mla, o_swa, *post_weights, part)
        counts = cnt[:, 0]
        pad = (counts + ROW_BLK - 1) // ROW_BLK * ROW_BLK
        pad_start = jnp.cumsum(pad) - pad
        dest = _dest(pad_start.astype(jnp.int32), e_t, r_t)
        idx = (dest.reshape(TOP_K, Tp // SC_CHUNK, SC_CHUNK, 1) * ROW_SUB
               + jnp.arange(ROW_SUB, dtype=jnp.int32))
        idx = idx.transpose(1, 0, 2, 3).reshape(Tp // SC_CHUNK, TOP_K, SC_CHUNK * ROW_SUB)
        xs = _sc_dispatch(h1p, idx, n_rows)
        sh = _shared(h1p, wsu, wsd)
        ys = _experts((pad_start // ROW_BLK).astype(jnp.int32), (pad // ROW_BLK).astype(jnp.int32),
                      counts.astype(jnp.int32), xs, w_exp_up, w_exp_down, sh)
        Tc = Tp // COMBINE_PARTS
        n_chunks = Tc // SC_CHUNK
        w_tok = w_t.T
        for cpart in range(COMBINE_PARTS):
            yg = _sc_gather(ys, idx[cpart * n_chunks:(cpart + 1) * n_chunks], Tc)
            out = _combine(w_tok, h1, sh, yg, g2, b2, cpart * Tc, part * Tp + cpart * Tc, T, out)
    return out.reshape(batch, seq, D_MODEL)
```

```python
import math

import jax
import jax.numpy as jnp
from jax import lax
from jax.experimental import pallas as pl
from jax.experimental.pallas import tpu as pltpu
from jax.experimental.pallas import tpu_sc as plsc

D_MODEL = 1024
MLA_HEADS = 8
MLA_Q_LORA = 256
MLA_KV_LORA = 256
MLA_NOPE = 64
MLA_ROPE = 32
MLA_V = 64
VT_ROWS = 80
ROPE_THETA = 10000.0
SWA_HEADS = 8
SWA_KV_HEADS = 2
SWA_GROUP = SWA_HEADS // SWA_KV_HEADS
SWA_HEAD_DIM = 64
SWA_WINDOW = 128
SWA_TQ = 256
N_EXPERTS = 256
TOP_K = 8
N_GROUPS = 8
GROUP_SIZE = N_EXPERTS // N_GROUPS
TOPK_GROUPS = 4
EXPERT_HIDDEN = 256
SHARED_HIDDEN = 256
ROUTED_SCALE = 2.5
DEEPNORM_ALPHA = 2.0 ** 0.25
LN_EPS = 1e-5
RMS_EPS = 1e-6

LANES = 128
ROW_WORDS = D_MODEL // 2
ROW_SUB = ROW_WORDS // LANES
VMEM_LIMIT = 48 * 1024 * 1024

PROJ_TM = 512
MLA_TQ = 256
MLA_TK = 512
MLA_HPS = 8
POST_TM = 512
ROUTE_TN = 256
ROW_BLK = 256
EXPERT_SLOTS = 4
EXPERT_SPLIT = 2
SHARED_TM = 512
COMBINE_TM = 256
SC_CHUNK = 32
MOE_PARTS = 1
COMBINE_PARTS = 4

BF16 = jnp.bfloat16
F32 = jnp.float32
NEG_INF = float("-inf")
LOG2E = math.log2(math.e)


def _sigmoid(v):
    return 1.0 / (1.0 + jnp.exp(-v))


def _dot(a, b):
    return jnp.dot(a, b, preferred_element_type=F32)


def _dot_nt(a, b):
    return lax.dot_general(a, b, (((1,), (1,)), ((), ())), preferred_element_type=F32)


def _pack_rows(y, out_ref, first=0):
    rows = y.shape[0]
    for j in range(ROW_SUB):
        a = y[:, j * LANES:(j + 1) * LANES].astype(BF16).astype(F32)
        b = y[:, ROW_WORDS + j * LANES:ROW_WORDS + (j + 1) * LANES].astype(BF16).astype(F32)
        ua = pltpu.bitcast(a, jnp.uint32) >> 16
        ub = pltpu.bitcast(b, jnp.uint32)
        out_ref[pl.ds(first * ROW_SUB + j, rows, stride=ROW_SUB), :] = ua | ub


def _unpack_rows(ref, rows, first=0, n_valid=None):
    lo, hi = [], []
    if n_valid is not None:
        live = first + lax.broadcasted_iota(jnp.int32, (rows, LANES), 0) < n_valid
    for j in range(ROW_SUB):
        u = ref[pl.ds(first * ROW_SUB + j, rows, stride=ROW_SUB), :]
        if n_valid is not None:
            u = jnp.where(live, u, jnp.uint32(0))
        lo.append(pltpu.bitcast(u << 16, F32))
        hi.append(pltpu.bitcast(u & jnp.uint32(0xFFFF0000), F32))
    return lo + hi


def _proj_kernel(x_ref, w1_ref, wvst_ref, wq_ref, wk_ref, wvt_ref, gq_ref, gkv_ref,
                 cq_ref, sq_ref, ck_ref, sk_ref,
                 q_ref, k_ref, vt_ref, qs_ref, ks_ref, vst_ref):
    xb = x_ref[...].astype(BF16)
    p = _dot(xb, w1_ref[...])

    def rms(c, g):
        return c * lax.rsqrt(jnp.mean(c * c, axis=-1, keepdims=True) + RMS_EPS) * g

    cqn = rms(p[:, 0:256], gq_ref[...]).astype(BF16)
    ckvn = rms(p[:, 256:512], gkv_ref[...]).astype(BF16)
    qs_ref[...] = p[:, 512:1024].astype(BF16)
    ks_ref[...] = p[:, 1024:1536].astype(BF16)
    ones = jnp.ones((VT_ROWS - MLA_V, xb.shape[0]), BF16)
    vst = _dot_nt(wvst_ref[...], xb).astype(BF16)
    for h in range(SWA_KV_HEADS):
        vst_ref[h * VT_ROWS:h * VT_ROWS + SWA_HEAD_DIM, :] = vst[h * SWA_HEAD_DIM:(h + 1) * SWA_HEAD_DIM, :]
        vst_ref[h * VT_ROWS + SWA_HEAD_DIM:(h + 1) * VT_ROWS, :] = ones
    kr = p[:, 1536:1664] * ck_ref[...] + p[:, 1664:1792] * sk_ref[...]
    qq = _dot(cqn, wq_ref[...])
    kn = _dot(ckvn, wk_ref[...])
    cq = cq_ref[...]
    sq = sq_ref[...]
    for h in range(MLA_HEADS):
        a = qq[:, h * LANES:(h + 1) * LANES]
        b = qq[:, 1024 + h * LANES:1024 + (h + 1) * LANES]
        q_ref[:, h * LANES:(h + 1) * LANES] = (a * cq + b * sq).astype(BF16)
        k_ref[:, h * LANES:(h + 1) * LANES] = (kn[:, h * LANES:(h + 1) * LANES] + kr).astype(BF16)
    vt = _dot_nt(wvt_ref[...], ckvn).astype(BF16)
    for h in range(MLA_HEADS):
        vt_ref[h * VT_ROWS:h * VT_ROWS + MLA_V, :] = vt[h * MLA_V:(h + 1) * MLA_V, :]
        vt_ref[h * VT_ROWS + MLA_V:(h + 1) * VT_ROWS, :] = ones


def _proj(x2, w1, wvst, wq2, wk, wvt, gq, gkv, tabs, seq):
    T = x2.shape[0]
    tm = PROJ_TM
    nper = seq // tm
    full = lambda shape: pl.BlockSpec(shape, lambda i: (0, 0))
    tab = pl.BlockSpec((tm, LANES), lambda i: (i % nper, 0))
    row = lambda n: pl.BlockSpec((tm, n), lambda i: (i, 0))
    col = lambda n: pl.BlockSpec((n, tm), lambda i: (0, i))
    return pl.pallas_call(
        _proj_kernel,
        grid=(T // tm,),
        in_specs=[row(D_MODEL), full(w1.shape), full(wvst.shape), full(wq2.shape), full(wk.shape),
                  full(wvt.shape), full(gq.shape), full(gkv.shape), tab, tab, tab, tab],
        out_specs=[row(1024), row(1024), col(MLA_HEADS * VT_ROWS), row(512), row(512),
                   col(SWA_KV_HEADS * VT_ROWS)],
        out_shape=[jax.ShapeDtypeStruct((T, 1024), BF16), jax.ShapeDtypeStruct((T, 1024), BF16),
                   jax.ShapeDtypeStruct((MLA_HEADS * VT_ROWS, T), BF16), jax.ShapeDtypeStruct((T, 512), BF16),
                   jax.ShapeDtypeStruct((T, 512), BF16),
                   jax.ShapeDtypeStruct((SWA_KV_HEADS * VT_ROWS, T), BF16)],
        compiler_params=pltpu.CompilerParams(dimension_semantics=("parallel",),
                                             vmem_limit_bytes=VMEM_LIMIT),
        name="proj",
    )(x2, w1, wvst, wq2, wk, wvt, gq, gkv, *tabs)


def _mla_kernel(q_ref, k_ref, vt_ref, o_ref, *acc_scr):
    tq = q_ref.shape[0]
    qi = pl.program_id(2)
    for acc in acc_scr:
        acc[...] = jnp.zeros(acc.shape, F32)

    def step(ks, tk, maxes, masked):
        scores = [_dot_nt(k_ref[pl.ds(ks, tk), h * LANES:(h + 1) * LANES],
                          q_ref[:, h * LANES:(h + 1) * LANES]) for h in range(MLA_HPS)]
        new_maxes, probs, alphas = [], [], []
        for h in range(MLA_HPS):
            s = scores[h]
            if masked:
                key = ks + lax.broadcasted_iota(jnp.int32, s.shape, 0)
                qry = qi * tq + lax.broadcasted_iota(jnp.int32, s.shape, 1)
                s = jnp.where(key <= qry, s, NEG_INF)
            m_new = jnp.maximum(maxes[h], jnp.max(s, axis=0, keepdims=True))
            alphas.append(jnp.exp2(maxes[h] - m_new))
            probs.append(jnp.exp2(s - m_new).astype(BF16))
            new_maxes.append(m_new)
        for h in range(MLA_HPS):
            pv = _dot(vt_ref[h * VT_ROWS:(h + 1) * VT_ROWS, pl.ds(ks, tk)], probs[h])
            acc_scr[h][...] = acc_scr[h][...] * alphas[h] + pv
        return tuple(new_maxes)

    init = tuple(jnp.full((1, tq), NEG_INF, F32) for _ in range(MLA_HPS))
    big = MLA_TK
    n_big = (qi * tq) // big
    maxes = lax.fori_loop(
        0, n_big, lambda kc, mx: step(pl.multiple_of(kc * big, big), big, mx, False), init)
    rest = pl.multiple_of(n_big * big, tq)
    maxes = lax.cond(
        rest < qi * tq,
        lambda mx: step(rest, 2 * tq, mx, True),
        lambda mx: step(rest, tq, mx, True),
        maxes)
    for h2 in range(MLA_HPS // 2):
        out_t = jnp.concatenate(
            [acc_scr[2 * h2 + g][0:MLA_V, :] / acc_scr[2 * h2 + g][MLA_V:MLA_V + 1, :] for g in range(2)],
            axis=0)
        o_ref[:, h2 * LANES:(h2 + 1) * LANES] = out_t.T.astype(BF16)


def _mla(q, k, vt, batch, seq):
    T = q.shape[0]
    tq = MLA_TQ
    nq = seq // tq
    hps = MLA_HPS
    return pl.pallas_call(
        _mla_kernel,
        grid=(batch, MLA_HEADS // hps, nq),
        in_specs=[pl.BlockSpec((tq, hps * LANES), lambda b, j, i: (b * nq + i, j)),
                  pl.BlockSpec((seq, hps * LANES), lambda b, j, i: (b, j)),
                  pl.BlockSpec((hps * VT_ROWS, seq), lambda b, j, i: (j, b))],
        out_specs=pl.BlockSpec((tq, hps * MLA_V), lambda b, j, i: (b * nq + i, j)),
        out_shape=jax.ShapeDtypeStruct((T, MLA_HEADS * MLA_V), BF16),
        scratch_shapes=[pltpu.VMEM((VT_ROWS, tq), F32) for _ in range(hps)],
        compiler_params=pltpu.CompilerParams(
            dimension_semantics=("parallel", "parallel", "arbitrary"), vmem_limit_bytes=VMEM_LIMIT),
        name="mla",
    )(q, k, vt)


def _swa_kernel(sink_ref, q_ref, kc_ref, kp_ref, vtc_ref, vtp_ref, bias_ref, o_ref):
    scores = []
    for head in range(SWA_HEADS):
        pair, g, kvh = head // 2, head % 2, head // SWA_GROUP
        col = (2 * kvh + g) * LANES
        band = jnp.concatenate([kp_ref[:, col:col + LANES], kc_ref[:, col:col + LANES]], axis=0)
        scores.append(_dot_nt(band, q_ref[:, pair * LANES:(pair + 1) * LANES]))
    probs, sink_terms = [], []
    for head in range(SWA_HEADS):
        s = scores[head] + bias_ref[0, head]
        sink = sink_ref[head] * LOG2E
        m = jnp.maximum(jnp.max(s, axis=0, keepdims=True), sink)
        probs.append(jnp.exp2(s - m).astype(BF16))
        sink_terms.append(jnp.exp2(sink - m))
    outs = []
    for head in range(SWA_HEADS):
        rows = slice((head // SWA_GROUP) * VT_ROWS, (head // SWA_GROUP + 1) * VT_ROWS)
        v_band = jnp.concatenate([vtp_ref[rows, :], vtc_ref[rows, :]], axis=1)
        pv = _dot(v_band, probs[head])
        outs.append(pv[0:SWA_HEAD_DIM, :] / (pv[SWA_HEAD_DIM:SWA_HEAD_DIM + 1, :] + sink_terms[head]))
    o_ref[...] = jnp.concatenate(outs, axis=0).T.astype(BF16)


def _swa_bias():
    W = SWA_WINDOW
    j = jnp.arange(W + SWA_TQ, dtype=jnp.int32)[:, None]
    i = jnp.arange(SWA_TQ, dtype=jnp.int32)[None, :]
    dist = i + W - j
    valid = (dist >= 0) & (dist < W)
    slopes = 2.0 ** (-8.0 * jnp.arange(1, SWA_HEADS + 1, dtype=F32) / SWA_HEADS)
    pen = -(slopes[:, None, None] * dist.astype(F32)[None]) * LOG2E
    general = jnp.where(valid[None], pen, NEG_INF)
    first = jnp.where((valid & (j >= W))[None], pen, NEG_INF)
    return jnp.stack([first, general])


def _swa(sinks, qs, ks, vst, batch, seq):
    T = qs.shape[0]
    W, tq = SWA_WINDOW, SWA_TQ
    nb = seq // tq
    wpb = tq // W
    before = lambda b, i: b * nb * wpb + jnp.maximum(wpb * i - 1, 0)
    cur = lambda n: pl.BlockSpec((tq, n), lambda b, i: (b * nb + i, 0))
    prev = lambda n: pl.BlockSpec((W, n), lambda b, i: (before(b, i), 0))
    vt_cur = pl.BlockSpec((SWA_KV_HEADS * VT_ROWS, tq), lambda b, i: (0, b * nb + i))
    vt_prev = pl.BlockSpec((SWA_KV_HEADS * VT_ROWS, W), lambda b, i: (0, before(b, i)))
    bias = pl.BlockSpec((1, SWA_HEADS, W + tq, tq), lambda b, i: (jnp.minimum(i, 1), 0, 0, 0))
    return pl.pallas_call(
        _swa_kernel,
        grid=(batch, nb),
        in_specs=[pl.BlockSpec(memory_space=pltpu.SMEM), cur(512), cur(512), prev(512),
                  vt_cur, vt_prev, bias],
        out_specs=cur(512),
        out_shape=jax.ShapeDtypeStruct((T, 512), BF16),
        compiler_params=pltpu.CompilerParams(dimension_semantics=("parallel", "parallel"),
                                             vmem_limit_bytes=VMEM_LIMIT),
        name="swa",
    )(sinks, qs, ks, ks, vst, vst, _swa_bias())


def _post_kernel(x_ref, om_ref, os_ref, wom_ref, wos_ref, wg_ref, wout_ref, g1_ref, b1_ref,
                 wr_ref, rb_ref, tri_ref,
                 h1_ref, h1p_ref, e_ref, w_ref, r_ref, cnt_ref, carry_scr):
    tm = x_ref.shape[0]
    step = pl.program_id(0)

    @pl.when(step == 0)
    def _():
        carry_scr[...] = jnp.zeros(carry_scr.shape, F32)

    x = x_ref[...]
    ya = _dot(om_ref[...], wom_ref[...])
    yb = _dot(os_ref[...], wos_ref[...])
    gates = _dot(x.astype(BF16), wg_ref[...])
    merged = _sigmoid(gates[:, :D_MODEL]) * ya + _sigmoid(gates[:, D_MODEL:]) * yb
    mix = _dot(merged.astype(BF16), wout_ref[...])
    z = DEEPNORM_ALPHA * x + mix
    mu = jnp.mean(z, axis=-1, keepdims=True)
    zc = z - mu
    var = jnp.mean(zc * zc, axis=-1, keepdims=True)
    h1 = zc * lax.rsqrt(var + LN_EPS) * g1_ref[...] + b1_ref[...]
    h1_ref[...] = h1
    _pack_rows(h1, h1p_ref)

    all_scores = _sigmoid(_dot_nt(wr_ref[...], h1.astype(BF16)))
    carry = carry_scr[...]
    for c in range(tm // ROUTE_TN):
        cols = slice(c * ROUTE_TN, (c + 1) * ROUTE_TN)
        idxs, weights, ranks, carry = _route(all_scores[:, cols], rb_ref[...], tri_ref[...], carry)
        e_ref[:, cols] = idxs
        w_ref[:, cols] = weights
        r_ref[:, cols] = ranks
    carry_scr[...] = carry
    cnt_ref[...] = carry.astype(jnp.int32)


def _route(scores, bias, tri, carry):
    tn = scores.shape[1]
    choice = scores + bias
    row = lax.broadcasted_iota(jnp.int32, (N_EXPERTS, tn), 0)
    grow = lax.broadcasted_iota(jnp.int32, (GROUP_SIZE, tn), 0)
    gscore = []
    for g in range(N_GROUPS):
        blk = choice[g * GROUP_SIZE:(g + 1) * GROUP_SIZE, :]
        m1 = jnp.max(blk, axis=0, keepdims=True)
        i1 = jnp.min(jnp.where(blk == m1, grow, GROUP_SIZE), axis=0, keepdims=True)
        m2 = jnp.max(jnp.where(grow == i1, NEG_INF, blk), axis=0, keepdims=True)
        gscore.append(m1 + m2)
    gsc = jnp.concatenate(gscore, axis=0)
    gidx = lax.broadcasted_iota(jnp.int32, (N_GROUPS, tn), 0)
    grank = jnp.zeros((N_GROUPS, tn), jnp.int32)
    for g in range(N_GROUPS):
        sg = gsc[g:g + 1, :]
        beats = (sg > gsc) | ((sg == gsc) & (gidx > g))
        grank = grank + beats.astype(jnp.int32)
    gsel = (grank < TOPK_GROUPS).astype(F32)
    emask = jnp.concatenate(
        [jnp.broadcast_to(gsel[g:g + 1, :], (GROUP_SIZE, tn)) for g in range(N_GROUPS)], axis=0)
    work = jnp.where(emask > 0.0, choice, NEG_INF)
    eligible = work
    idxs, svals = [], []
    for _k in range(TOP_K):
        m = jnp.max(work, axis=0, keepdims=True)
        idx = jnp.min(jnp.where(work == m, row, N_EXPERTS), axis=0, keepdims=True)
        hit = row == idx
        svals.append(jnp.sum(jnp.where(hit, scores, 0.0), axis=0, keepdims=True))
        work = jnp.where(hit, NEG_INF, work)
        idxs.append(idx)
    sel = jnp.where(work != eligible, 1.0, 0.0)
    ssum = svals[0]
    for sv in svals[1:]:
        ssum = ssum + sv
    weights = jnp.concatenate([sv / ssum * ROUTED_SCALE for sv in svals], axis=0)

    rank = _dot(sel.astype(BF16), tri) + carry[:, 0:1]
    ranks = jnp.concatenate(
        [jnp.sum(jnp.where(row == idx, rank, 0.0), axis=0, keepdims=True) for idx in idxs],
        axis=0).astype(jnp.int32)
    carry = carry + jnp.sum(sel, axis=1, keepdims=True)
    return jnp.concatenate(idxs, axis=0), weights, ranks, carry


def _post(x2, o_mla, o_swa, wom, wos, wg, wout, g1, b1, wr_t, rbias, tri, part):
    T = x2.shape[0] // MOE_PARTS
    tm = POST_TM
    first = part * (T // tm)
    full = lambda a: pl.BlockSpec(a.shape, lambda i: (0, 0))
    row_in = lambda n: pl.BlockSpec((tm, n), lambda i: (first + i, 0))
    row = lambda n: pl.BlockSpec((tm, n), lambda i: (i, 0))
    col = pl.BlockSpec((TOP_K, tm), lambda i: (0, i))
    return pl.pallas_call(
        _post_kernel,
        grid=(T // tm,),
        in_specs=[row_in(D_MODEL), row_in(512), row_in(512), full(wom), full(wos), full(wg), full(wout),
                  full(g1), full(b1), full(wr_t), full(rbias), full(tri)],
        out_specs=[row(D_MODEL), pl.BlockSpec((tm * ROW_SUB, LANES), lambda i: (i, 0)), col, col, col,
                   pl.BlockSpec((N_EXPERTS, LANES), lambda i: (0, 0))],
        out_shape=[jax.ShapeDtypeStruct((T, D_MODEL), F32),
                   jax.ShapeDtypeStruct((T * ROW_SUB, LANES), jnp.uint32),
                   jax.ShapeDtypeStruct((TOP_K, T), jnp.int32),
                   jax.ShapeDtypeStruct((TOP_K, T), F32),
                   jax.ShapeDtypeStruct((TOP_K, T), jnp.int32),
                   jax.ShapeDtypeStruct((N_EXPERTS, LANES), jnp.int32)],
        scratch_shapes=[pltpu.VMEM((N_EXPERTS, LANES), F32)],
        compiler_params=pltpu.CompilerParams(dimension_semantics=("arbitrary",),
                                             vmem_limit_bytes=VMEM_LIMIT),
        name="post",
    )(x2, o_mla, o_swa, wom, wos, wg, wout, g1, b1, wr_t, rbias, tri)


def _dest_kernel(start_ref, e_ref, r_ref, o_ref):
    e = e_ref[...]
    base = jnp.zeros(e.shape, jnp.int32)
    for j in range(N_EXPERTS):
        base = jnp.where(e == j, start_ref[j], base)
    o_ref[...] = base + r_ref[...]


def _dest(seg_start, e_t, r_t):
    T = e_t.shape[1]
    tn = min(T, 4096)
    col = pl.BlockSpec((TOP_K, tn), lambda i: (0, i))
    return pl.pallas_call(
        _dest_kernel,
        grid=(T // tn,),
        in_specs=[pl.BlockSpec(memory_space=pltpu.SMEM), col, col],
        out_specs=col,
        out_shape=jax.ShapeDtypeStruct((TOP_K, T), jnp.int32),
        compiler_params=pltpu.CompilerParams(dimension_semantics=("parallel",)),
        name="dest",
    )(seg_start, e_t, r_t)


def _sc_worker_chunks(n_tokens):
    info = plsc.get_sparse_core_info()
    n_workers = info.num_cores * info.num_subcores
    per_worker = n_tokens // SC_CHUNK // n_workers
    assert per_worker * n_workers * SC_CHUNK == n_tokens
    first = (lax.axis_index("s") * info.num_cores + lax.axis_index("c")) * per_worker
    return first, per_worker


def _sc_dispatch(h1p, idx, n_rows):
    n_tokens = h1p.shape[0] // ROW_SUB
    R = SC_CHUNK * ROW_SUB

    def body(h_hbm, idx_hbm, xs_hbm, idx_v, rows_v, sem):
        first, per_worker = _sc_worker_chunks(n_tokens)

        @pl.loop(0, per_worker)
        def _(ci):
            c = first + ci
            pltpu.sync_copy(idx_hbm.at[c], idx_v)
            pltpu.sync_copy(h_hbm.at[pl.ds(c * R, R)], rows_v)
            copies = [pltpu.make_async_copy(rows_v, xs_hbm.at[idx_v.at[k]], sem) for k in range(TOP_K)]
            for cp in copies:
                cp.start()
            for cp in copies:
                cp.wait()

    run = pl.kernel(
        body, out_type=jax.ShapeDtypeStruct((n_rows * ROW_SUB, LANES), jnp.uint32),
        mesh=plsc.VectorSubcoreMesh(core_axis_name="c", subcore_axis_name="s"),
        scratch_types=[pltpu.VMEM((TOP_K, R), jnp.int32), pltpu.VMEM((R, LANES), jnp.uint32),
                       pltpu.SemaphoreType.DMA],
        name="sc_dispatch")
    return run(h1p, idx)


def _sc_gather(ys, idx, n_tokens):
    R = SC_CHUNK * ROW_SUB
    half = TOP_K // 2

    def body(ys_hbm, idx_hbm, yg_hbm, idx_v, buf, gsem, wsem):
        first, per_worker = _sc_worker_chunks(n_tokens)

        @pl.loop(0, per_worker)
        def _(ci):
            c = first + ci
            pltpu.sync_copy(idx_hbm.at[c], idx_v)
            for k0 in (0, half):
                gathers = [pltpu.make_async_copy(ys_hbm.at[idx_v.at[k0 + k]], buf.at[k], gsem)
                           for k in range(half)]
                for cp in gathers:
                    cp.start()
                for cp in gathers:
                    cp.wait()
                writes = [pltpu.make_async_copy(buf.at[k], yg_hbm.at[k0 + k, pl.ds(c * R, R)], wsem)
                          for k in range(half)]
                for cp in writes:
                    cp.start()
                for cp in writes:
                    cp.wait()

    run = pl.kernel(
        body, out_type=jax.ShapeDtypeStruct((TOP_K, n_tokens * ROW_SUB, LANES), jnp.uint32),
        mesh=plsc.VectorSubcoreMesh(core_axis_name="c", subcore_axis_name="s"),
        scratch_types=[pltpu.VMEM((TOP_K, R), jnp.int32), pltpu.VMEM((half, R, LANES), jnp.uint32),
                       pltpu.SemaphoreType.DMA, pltpu.SemaphoreType.DMA],
        name="sc_gather")
    return run(ys, idx)


def _experts_kernel(first_ref, nblk_ref, cnt_ref, xs_hbm, wup_ref, wdn_ref, after_ref, ys_hbm,
                    wup_bf, wdn_bf, xbuf, ybuf, in_sem, out_sem):
    del after_ref
    e = pl.program_id(0)
    n = nblk_ref[e]
    b0 = first_ref[e]
    count = cnt_ref[e]
    total = first_ref[N_EXPERTS - 1] + nblk_ref[N_EXPERTS - 1]
    rb = ROW_BLK * ROW_SUB
    depth = EXPERT_SLOTS

    def slot_of(b):
        return lax.rem(b, depth)

    def rows_of(b):
        return pl.ds(pl.multiple_of(b * rb, rb), rb)

    def in_copy(b):
        return pltpu.make_async_copy(xs_hbm.at[rows_of(b)], xbuf.at[slot_of(b)], in_sem.at[slot_of(b)])

    def out_copy(b):
        return pltpu.make_async_copy(ybuf.at[slot_of(b)], ys_hbm.at[rows_of(b)], out_sem.at[slot_of(b)])

    @pl.when(e == 0)
    def _():
        for b in range(depth - 1):
            @pl.when(b < total)
            def _():
                in_copy(b).start()

    @pl.when(n > 0)
    def _():
        wup_bf[...] = wup_ref[0].astype(BF16)
        wdn_bf[...] = wdn_ref[0].astype(BF16)

        def body(b, carry):
            slot = slot_of(b)
            in_copy(b).wait()

            @pl.when(b + depth - 1 < total)
            def _():
                in_copy(b + depth - 1).start()

            @pl.when(b >= depth)
            def _():
                out_copy(b - depth).wait()

            sub = ROW_BLK // EXPERT_SPLIT
            n_valid = count - (b - b0) * ROW_BLK
            xbs = [jnp.concatenate(
                [c.astype(BF16) for c in _unpack_rows(xbuf.at[slot], sub, first=i * sub, n_valid=n_valid)],
                axis=1) for i in range(EXPERT_SPLIT)]
            gus = [_dot(xb, wup_bf[...]) for xb in xbs]
            hids = [(gu[:, :EXPERT_HIDDEN] * _sigmoid(gu[:, :EXPERT_HIDDEN])
                     * gu[:, EXPERT_HIDDEN:]).astype(BF16) for gu in gus]
            ys = [_dot(hid, wdn_bf[...]) for hid in hids]
            for i in range(EXPERT_SPLIT):
                _pack_rows(ys[i], ybuf.at[slot], first=i * sub)
            out_copy(b).start()
            return carry

        lax.fori_loop(b0, b0 + n, body, 0)

    @pl.when(e == N_EXPERTS - 1)
    def _():
        for back in range(depth, 0, -1):
            @pl.when(total - back >= 0)
            def _():
                out_copy(total - back).wait()


def _experts(first_blk, n_blk, counts, xs, w_exp_up, w_exp_down, run_after):
    rb = ROW_BLK * ROW_SUB
    grid_spec = pltpu.PrefetchScalarGridSpec(
        num_scalar_prefetch=3,
        grid=(N_EXPERTS,),
        in_specs=[pl.BlockSpec(memory_space=pl.ANY),
                  pl.BlockSpec((1, D_MODEL, 2 * EXPERT_HIDDEN), lambda e, fb, nb, ct: (e, 0, 0)),
                  pl.BlockSpec((1, EXPERT_HIDDEN, D_MODEL), lambda e, fb, nb, ct: (e, 0, 0)),
                  pl.BlockSpec(memory_space=pl.ANY)],
        out_specs=pl.BlockSpec(memory_space=pl.ANY),
        scratch_shapes=[pltpu.VMEM((D_MODEL, 2 * EXPERT_HIDDEN), BF16),
                        pltpu.VMEM((EXPERT_HIDDEN, D_MODEL), BF16),
                        pltpu.VMEM((EXPERT_SLOTS, rb, LANES), jnp.uint32),
                        pltpu.VMEM((EXPERT_SLOTS, rb, LANES), jnp.uint32),
                        pltpu.SemaphoreType.DMA((EXPERT_SLOTS,)),
                        pltpu.SemaphoreType.DMA((EXPERT_SLOTS,))],
    )
    return pl.pallas_call(
        _experts_kernel,
        grid_spec=grid_spec,
        out_shape=jax.ShapeDtypeStruct(xs.shape, jnp.uint32),
        compiler_params=pltpu.CompilerParams(dimension_semantics=("arbitrary",),
                                             vmem_limit_bytes=VMEM_LIMIT),
        name="experts",
    )(first_blk, n_blk, counts, xs, w_exp_up, w_exp_down, run_after)


def _shared_kernel(h1p_ref, wsu_ref, wsd_ref, o_ref):
    tm = o_ref.shape[0]
    hb = jnp.concatenate([c.astype(BF16) for c in _unpack_rows(h1p_ref, tm)], axis=1)
    gu = _dot(hb, wsu_ref[...])
    g = gu[:, :SHARED_HIDDEN]
    hid = g * _sigmoid(g) * gu[:, SHARED_HIDDEN:]
    o_ref[...] = _dot(hid.astype(BF16), wsd_ref[...]).astype(BF16)


def _shared(h1p, wsu, wsd):
    T = h1p.shape[0] // ROW_SUB
    tm = SHARED_TM
    full = lambda a: pl.BlockSpec(a.shape, lambda i: (0, 0))
    row = pl.BlockSpec((tm, D_MODEL), lambda i: (i, 0))
    return pl.pallas_call(
        _shared_kernel,
        grid=(T // tm,),
        in_specs=[pl.BlockSpec((tm * ROW_SUB, LANES), lambda i: (i, 0)), full(wsu), full(wsd)],
        out_specs=row,
        out_shape=jax.ShapeDtypeStruct((T, D_MODEL), BF16),
        compiler_params=pltpu.CompilerParams(dimension_semantics=("parallel",),
                                             vmem_limit_bytes=VMEM_LIMIT),
        name="shared",
    )(h1p, wsu, wsd)


def _combine_kernel(w_ref, h1_ref, sh_ref, yg_ref, g2_ref, b2_ref, *rest):
    o_ref = rest[-1]
    tm = h1_ref.shape[0]
    h1 = h1_ref[...]
    ffn = sh_ref[...].astype(F32)

    w = w_ref[...]
    acc = [None] * (2 * ROW_SUB)
    for k in range(TOP_K):
        wk = w[:, k:k + 1]
        chunks = _unpack_rows(yg_ref.at[k], tm)
        for c in range(2 * ROW_SUB):
            acc[c] = wk * chunks[c] if acc[c] is None else acc[c] + wk * chunks[c]
    routed = jnp.concatenate(acc, axis=1)
    z = DEEPNORM_ALPHA * h1 + (routed + ffn)
    mu = jnp.mean(z, axis=-1, keepdims=True)
    zc = z - mu
    var = jnp.mean(zc * zc, axis=-1, keepdims=True)
    o_ref[...] = zc * lax.rsqrt(var + LN_EPS) * g2_ref[...] + b2_ref[...]


def _combine(w_tok, h1, sh, yg, g2, b2, in_first, out_first, out_rows, out_prev):
    n_tok = yg.shape[1] // ROW_SUB
    tm = COMBINE_TM
    fi, fo = in_first // tm, out_first // tm
    full = lambda a: pl.BlockSpec(a.shape, lambda i: (0, 0))
    row = pl.BlockSpec((tm, D_MODEL), lambda i: (fi + i, 0))
    in_specs = [pl.BlockSpec((tm, TOP_K), lambda i: (fi + i, 0)), row, row,
                pl.BlockSpec((TOP_K, tm * ROW_SUB, LANES), lambda i: (0, i, 0)),
                full(g2), full(b2)]
    args = [w_tok, h1, sh, yg, g2, b2]
    aliases = {}
    if out_prev is not None:
        in_specs.append(pl.BlockSpec(memory_space=pl.ANY))
        args.append(out_prev)
        aliases = {len(args) - 1: 0}
    return pl.pallas_call(
        _combine_kernel,
        grid=(n_tok // tm,),
        in_specs=in_specs,
        out_specs=pl.BlockSpec((tm, D_MODEL), lambda i: (fo + i, 0)),
        out_shape=jax.ShapeDtypeStruct((out_rows, D_MODEL), F32),
        input_output_aliases=aliases,
        compiler_params=pltpu.CompilerParams(dimension_semantics=("parallel",),
                                             vmem_limit_bytes=VMEM_LIMIT),
        name="combine",
    )(*args)


def _prep_weights(w_in, w_uq, w_ukv, seq):
    z = lambda n: jnp.zeros((D_MODEL, n), F32)
    kr = w_in[:, 512:544]
    qs = w_in[:, 544:1056] * (LOG2E / math.sqrt(SWA_HEAD_DIM))
    ks0, ks1 = w_in[:, 1056:1120], w_in[:, 1120:1184]
    half = MLA_ROPE // 2
    w1 = jnp.concatenate([
        w_in[:, 0:512], qs,
        ks0, z(64), z(64), ks0, ks1, z(64), z(64), ks1,
        z(64), kr, z(32),
        z(64), -kr[:, half:], kr[:, :half], z(32)], axis=1).astype(BF16)
    wvst = w_in[:, 1184:1312].T.astype(BF16)
    wg = w_in[:, 1312:3360].astype(BF16)

    zq = lambda n: jnp.zeros((MLA_Q_LORA, n), F32)
    qd = MLA_NOPE + MLA_ROPE
    q_cols, r_cols, k_cols, v_cols = [], [], [], []
    for h in range(MLA_HEADS):
        wq = w_uq[:, h * qd:(h + 1) * qd]
        q_cols += [wq, zq(32)]
        r_cols += [zq(64), -wq[:, MLA_NOPE + half:], wq[:, MLA_NOPE:MLA_NOPE + half], zq(32)]
        wk = w_ukv[:, h * 128:(h + 1) * 128]
        k_cols += [wk[:, :MLA_NOPE], zq(64)]
        v_cols += [wk[:, MLA_NOPE:]]
    wq2 = jnp.concatenate(q_cols + r_cols, axis=1).astype(BF16)
    wk = jnp.concatenate(k_cols, axis=1).astype(BF16)
    wvt = jnp.concatenate(v_cols, axis=1).T.astype(BF16)

    inv_freq = ROPE_THETA ** (-jnp.arange(0, MLA_ROPE, 2, dtype=F32) / MLA_ROPE)
    ang = jnp.arange(seq, dtype=F32)[:, None] * inv_freq[None, :]
    cos, sin = jnp.cos(ang), jnp.sin(ang)
    one, zero = jnp.ones((seq, 64), F32), jnp.zeros((seq, 64), F32)
    z32 = jnp.zeros((seq, 32), F32)
    scale = LOG2E / math.sqrt(MLA_NOPE + MLA_ROPE)
    tabs = (jnp.concatenate([one, cos, cos, z32], axis=1) * scale,
            jnp.concatenate([zero, sin, sin, z32], axis=1) * scale,
            jnp.concatenate([zero, cos, cos, z32], axis=1),
            jnp.concatenate([zero, sin, sin, z32], axis=1))
    return w1, wvst, wg, wq2, wk, wvt, tabs


def kernel(x, w_in, mla_q_norm, mla_kv_norm, w_uq, w_ukv, attn_sinks, w_o_mla, w_o_swa, w_out,
           ln1_g, ln1_b, w_router, router_bias, w_exp_up, w_exp_down, w_sh_up, w_sh_down,
           ln2_g, ln2_b):
    batch, seq, _ = x.shape
    T = batch * seq
    x2 = x.reshape(T, D_MODEL)
    w1, wvst, wg, wq2, wk, wvt, tabs = _prep_weights(w_in, w_uq, w_ukv, seq)
    q, k, vt, qs, ks, vst = _proj(x2, w1, wvst, wq2, wk, wvt, mla_q_norm.reshape(1, -1),
                                  mla_kv_norm.reshape(1, -1), tabs, seq)
    o_mla = _mla(q, k, vt, batch, seq)
    o_swa = _swa(attn_sinks.astype(F32), qs, ks, vst, batch, seq)

    tri = (lax.broadcasted_iota(jnp.int32, (ROUTE_TN, ROUTE_TN), 0)
           < lax.broadcasted_iota(jnp.int32, (ROUTE_TN, ROUTE_TN), 1)).astype(BF16)
    post_weights = (w_o_mla.astype(BF16), w_o_swa.astype(BF16), wg, w_out.astype(BF16),
                    ln1_g.reshape(1, -1), ln1_b.reshape(1, -1), w_router.T.astype(BF16),
                    router_bias.reshape(-1, 1).astype(F32), tri)
    wsu, wsd = w_sh_up.astype(BF16), w_sh_down.astype(BF16)
    g2, b2 = ln2_g.reshape(1, -1), ln2_b.reshape(1, -1)

    Tp = T // MOE_PARTS
    n_rows = (Tp * TOP_K // ROW_BLK + N_EXPERTS) * ROW_BLK
    out = None
    for part in range(MOE_PARTS):
        h1, h1p, e_t, w_t, r_t, cnt = _post(x2, o_mla, o_swa, *post_weights, part)
        counts = cnt[:, 0]
        pad = (counts + ROW_BLK - 1) // ROW_BLK * ROW_BLK
        pad_start = jnp.cumsum(pad) - pad
        dest = _dest(pad_start.astype(jnp.int32), e_t, r_t)
        idx = (dest.reshape(TOP_K, Tp // SC_CHUNK, SC_CHUNK, 1) * ROW_SUB
               + jnp.arange(ROW_SUB, dtype=jnp.int32))
        idx = idx.transpose(1, 0, 2, 3).reshape(Tp // SC_CHUNK, TOP_K, SC_CHUNK * ROW_SUB)
        xs = _sc_dispatch(h1p, idx, n_rows)
        sh = _shared(h1p, wsu, wsd)
        ys = _experts((pad_start // ROW_BLK).astype(jnp.int32), (pad // ROW_BLK).astype(jnp.int32),
                      counts.astype(jnp.int32), xs, w_exp_up, w_exp_down, sh)
        Tc = Tp // COMBINE_PARTS
        n_chunks = Tc // SC_CHUNK
        w_tok = w_t.T
        for cpart in range(COMBINE_PARTS):
            yg = _sc_gather(ys, idx[cpart * n_chunks:(cpart + 1) * n_chunks], Tc)
            out = _combine(w_tok, h1, sh, yg, g2, b2, cpart * Tc, part * Tp + cpart * Tc, T, out)
    return out.reshape(batch, seq, D_MODEL)
```

```python
import math

import jax
import jax.numpy as jnp
from jax import lax
from jax.experimental import pallas as pl
from jax.experimental.pallas import tpu as pltpu
from jax.experimental.pallas import tpu_sc as plsc

D_MODEL = 1024
MLA_HEADS = 8
MLA_Q_LORA = 256
MLA_KV_LORA = 256
MLA_NOPE = 64
MLA_ROPE = 32
MLA_V = 64
VT_ROWS = 80
ROPE_THETA = 10000.0
SWA_HEADS = 8
SWA_KV_HEADS = 2
SWA_GROUP = SWA_HEADS // SWA_KV_HEADS
SWA_HEAD_DIM = 64
SWA_WINDOW = 128
SWA_TQ = 256
N_EXPERTS = 256
TOP_K = 8
N_GROUPS = 8
GROUP_SIZE = N_EXPERTS // N_GROUPS
TOPK_GROUPS = 4
EXPERT_HIDDEN = 256
SHARED_HIDDEN = 256
ROUTED_SCALE = 2.5
DEEPNORM_ALPHA = 2.0 ** 0.25
LN_EPS = 1e-5
RMS_EPS = 1e-6

LANES = 128
ROW_WORDS = D_MODEL // 2
ROW_SUB = ROW_WORDS // LANES
VMEM_LIMIT = 48 * 1024 * 1024

PROJ_TM = 512
MLA_TQ = 256
MLA_TK = 512
MLA_HPS = 8
POST_TM = 512
ROUTE_TN = 256
ROW_BLK = 256
EXPERT_SLOTS = 6
EXPERT_SPLIT = 2
EXPERTS_PER_STEP = 4
SHARED_TM = 512
COMBINE_TM = 256
SC_CHUNK = 32
MOE_PARTS = 1
COMBINE_PARTS = 4

BF16 = jnp.bfloat16
F32 = jnp.float32
NEG_INF = float("-inf")
LOG2E = math.log2(math.e)


def _sigmoid(v):
    return 1.0 / (1.0 + jnp.exp(-v))


def _dot(a, b):
    return jnp.dot(a, b, preferred_element_type=F32)


def _dot_nt(a, b):
    return lax.dot_general(a, b, (((1,), (1,)), ((), ())), preferred_element_type=F32)


def _pack_rows(y, out_ref, first=0):
    rows = y.shape[0]
    for j in range(ROW_SUB):
        a = y[:, j * LANES:(j + 1) * LANES].astype(BF16).astype(F32)
        b = y[:, ROW_WORDS + j * LANES:ROW_WORDS + (j + 1) * LANES].astype(BF16).astype(F32)
        ua = pltpu.bitcast(a, jnp.uint32) >> 16
        ub = pltpu.bitcast(b, jnp.uint32)
        out_ref[pl.ds(first * ROW_SUB + j, rows, stride=ROW_SUB), :] = ua | ub


def _unpack_rows(ref, rows, first=0, n_valid=None):
    lo, hi = [], []
    if n_valid is not None:
        live = first + lax.broadcasted_iota(jnp.int32, (rows, LANES), 0) < n_valid
    for j in range(ROW_SUB):
        u = ref[pl.ds(first * ROW_SUB + j, rows, stride=ROW_SUB), :]
        if n_valid is not None:
            u = jnp.where(live, u, jnp.uint32(0))
        lo.append(pltpu.bitcast(u << 16, F32))
        hi.append(pltpu.bitcast(u & jnp.uint32(0xFFFF0000), F32))
    return lo + hi


def _proj_kernel(x_ref, w1_ref, wvst_ref, wq_ref, wk_ref, wvt_ref, gq_ref, gkv_ref,
                 cq_ref, sq_ref, ck_ref, sk_ref,
                 q_ref, k_ref, vt_ref, qs_ref, ks_ref, vst_ref):
    xb = x_ref[...].astype(BF16)
    p = _dot(xb, w1_ref[...])

    def rms(c, g):
        return c * lax.rsqrt(jnp.mean(c * c, axis=-1, keepdims=True) + RMS_EPS) * g

    cqn = rms(p[:, 0:256], gq_ref[...]).astype(BF16)
    ckvn = rms(p[:, 256:512], gkv_ref[...]).astype(BF16)
    qs_ref[...] = p[:, 512:1024].astype(BF16)
    ks_ref[...] = p[:, 1024:1536].astype(BF16)
    ones = jnp.ones((VT_ROWS - MLA_V, xb.shape[0]), BF16)
    vst = _dot_nt(wvst_ref[...], xb).astype(BF16)
    for h in range(SWA_KV_HEADS):
        vst_ref[h * VT_ROWS:h * VT_ROWS + SWA_HEAD_DIM, :] = vst[h * SWA_HEAD_DIM:(h + 1) * SWA_HEAD_DIM, :]
        vst_ref[h * VT_ROWS + SWA_HEAD_DIM:(h + 1) * VT_ROWS, :] = ones
    kr = p[:, 1536:1664] * ck_ref[...] + p[:, 1664:1792] * sk_ref[...]
    qq = _dot(cqn, wq_ref[...])
    kn = _dot(ckvn, wk_ref[...])
    cq = cq_ref[...]
    sq = sq_ref[...]
    for h in range(MLA_HEADS):
        a = qq[:, h * LANES:(h + 1) * LANES]
        b = qq[:, 1024 + h * LANES:1024 + (h + 1) * LANES]
        q_ref[:, h * LANES:(h + 1) * LANES] = (a * cq + b * sq).astype(BF16)
        k_ref[:, h * LANES:(h + 1) * LANES] = (kn[:, h * LANES:(h + 1) * LANES] + kr).astype(BF16)
    vt = _dot_nt(wvt_ref[...], ckvn).astype(BF16)
    for h in range(MLA_HEADS):
        vt_ref[h * VT_ROWS:h * VT_ROWS + MLA_V, :] = vt[h * MLA_V:(h + 1) * MLA_V, :]
        vt_ref[h * VT_ROWS + MLA_V:(h + 1) * VT_ROWS, :] = ones


def _proj(x2, w1, wvst, wq2, wk, wvt, gq, gkv, tabs, seq):
    T = x2.shape[0]
    tm = PROJ_TM
    nper = seq // tm
    full = lambda shape: pl.BlockSpec(shape, lambda i: (0, 0))
    tab = pl.BlockSpec((tm, LANES), lambda i: (i % nper, 0))
    row = lambda n: pl.BlockSpec((tm, n), lambda i: (i, 0))
    col = lambda n: pl.BlockSpec((n, tm), lambda i: (0, i))
    return pl.pallas_call(
        _proj_kernel,
        grid=(T // tm,),
        in_specs=[row(D_MODEL), full(w1.shape), full(wvst.shape), full(wq2.shape), full(wk.shape),
                  full(wvt.shape), full(gq.shape), full(gkv.shape), tab, tab, tab, tab],
        out_specs=[row(1024), row(1024), col(MLA_HEADS * VT_ROWS), row(512), row(512),
                   col(SWA_KV_HEADS * VT_ROWS)],
        out_shape=[jax.ShapeDtypeStruct((T, 1024), BF16), jax.ShapeDtypeStruct((T, 1024), BF16),
                   jax.ShapeDtypeStruct((MLA_HEADS * VT_ROWS, T), BF16), jax.ShapeDtypeStruct((T, 512), BF16),
                   jax.ShapeDtypeStruct((T, 512), BF16),
                   jax.ShapeDtypeStruct((SWA_KV_HEADS * VT_ROWS, T), BF16)],
        compiler_params=pltpu.CompilerParams(dimension_semantics=("parallel",),
                                             vmem_limit_bytes=VMEM_LIMIT),
        name="proj",
    )(x2, w1, wvst, wq2, wk, wvt, gq, gkv, *tabs)


def _mla_kernel(q_ref, k_ref, vt_ref, o_ref, *acc_scr):
    tq = q_ref.shape[0]
    qi = pl.program_id(2)
    for acc in acc_scr:
        acc[...] = jnp.zeros(acc.shape, F32)

    def step(ks, tk, maxes, masked):
        scores = [_dot_nt(k_ref[pl.ds(ks, tk), h * LANES:(h + 1) * LANES],
                          q_ref[:, h * LANES:(h + 1) * LANES]) for h in range(MLA_HPS)]
        new_maxes, probs, alphas = [], [], []
        for h in range(MLA_HPS):
            s = scores[h]
            if masked:
                key = ks + lax.broadcasted_iota(jnp.int32, s.shape, 0)
                qry = qi * tq + lax.broadcasted_iota(jnp.int32, s.shape, 1)
                s = jnp.where(key <= qry, s, NEG_INF)
            m_new = jnp.maximum(maxes[h], jnp.max(s, axis=0, keepdims=True))
            alphas.append(jnp.exp2(maxes[h] - m_new))
            probs.append(jnp.exp2(s - m_new).astype(BF16))
            new_maxes.append(m_new)
        for h in range(MLA_HPS):
            pv = _dot(vt_ref[h * VT_ROWS:(h + 1) * VT_ROWS, pl.ds(ks, tk)], probs[h])
            acc_scr[h][...] = acc_scr[h][...] * alphas[h] + pv
        return tuple(new_maxes)

    init = tuple(jnp.full((1, tq), NEG_INF, F32) for _ in range(MLA_HPS))
    big = MLA_TK
    n_big = (qi * tq) // big
    maxes = lax.fori_loop(
        0, n_big, lambda kc, mx: step(pl.multiple_of(kc * big, big), big, mx, False), init)
    rest = pl.multiple_of(n_big * big, tq)
    maxes = lax.cond(
        rest < qi * tq,
        lambda mx: step(rest, 2 * tq, mx, True),
        lambda mx: step(rest, tq, mx, True),
        maxes)
    for h2 in range(MLA_HPS // 2):
        out_t = jnp.concatenate(
            [acc_scr[2 * h2 + g][0:MLA_V, :] / acc_scr[2 * h2 + g][MLA_V:MLA_V + 1, :] for g in range(2)],
            axis=0)
        o_ref[:, h2 * LANES:(h2 + 1) * LANES] = out_t.T.astype(BF16)


def _mla(q, k, vt, batch, seq):
    T = q.shape[0]
    tq = MLA_TQ
    nq = seq // tq
    hps = MLA_HPS
    return pl.pallas_call(
        _mla_kernel,
        grid=(batch, MLA_HEADS // hps, nq),
        in_specs=[pl.BlockSpec((tq, hps * LANES), lambda b, j, i: (b * nq + i, j)),
                  pl.BlockSpec((seq, hps * LANES), lambda b, j, i: (b, j)),
                  pl.BlockSpec((hps * VT_ROWS, seq), lambda b, j, i: (j, b))],
        out_specs=pl.BlockSpec((tq, hps * MLA_V), lambda b, j, i: (b * nq + i, j)),
        out_shape=jax.ShapeDtypeStruct((T, MLA_HEADS * MLA_V), BF16),
        scratch_shapes=[pltpu.VMEM((VT_ROWS, tq), F32) for _ in range(hps)],
        compiler_params=pltpu.CompilerParams(
            dimension_semantics=("parallel", "parallel", "arbitrary"), vmem_limit_bytes=VMEM_LIMIT),
        name="mla",
    )(q, k, vt)


def _swa_kernel(sink_ref, q_ref, kc_ref, kp_ref, vtc_ref, vtp_ref, bias_ref, o_ref):
    scores = []
    for head in range(SWA_HEADS):
        pair, g, kvh = head // 2, head % 2, head // SWA_GROUP
        col = (2 * kvh + g) * LANES
        band = jnp.concatenate([kp_ref[:, col:col + LANES], kc_ref[:, col:col + LANES]], axis=0)
        scores.append(_dot_nt(band, q_ref[:, pair * LANES:(pair + 1) * LANES]))
    probs, sink_terms = [], []
    for head in range(SWA_HEADS):
        s = scores[head] + bias_ref[0, head]
        sink = sink_ref[head] * LOG2E
        m = jnp.maximum(jnp.max(s, axis=0, keepdims=True), sink)
        probs.append(jnp.exp2(s - m).astype(BF16))
        sink_terms.append(jnp.exp2(sink - m))
    outs = []
    for head in range(SWA_HEADS):
        rows = slice((head // SWA_GROUP) * VT_ROWS, (head // SWA_GROUP + 1) * VT_ROWS)
        v_band = jnp.concatenate([vtp_ref[rows, :], vtc_ref[rows, :]], axis=1)
        pv = _dot(v_band, probs[head])
        outs.append(pv[0:SWA_HEAD_DIM, :] / (pv[SWA_HEAD_DIM:SWA_HEAD_DIM + 1, :] + sink_terms[head]))
    o_ref[...] = jnp.concatenate(outs, axis=0).T.astype(BF16)


def _swa_bias():
    W = SWA_WINDOW
    j = jnp.arange(W + SWA_TQ, dtype=jnp.int32)[:, None]
    i = jnp.arange(SWA_TQ, dtype=jnp.int32)[None, :]
    dist = i + W - j
    valid = (dist >= 0) & (dist < W)
    slopes = 2.0 ** (-8.0 * jnp.arange(1, SWA_HEADS + 1, dtype=F32) / SWA_HEADS)
    pen = -(slopes[:, None, None] * dist.astype(F32)[None]) * LOG2E
    general = jnp.where(valid[None], pen, NEG_INF)
    first = jnp.where((valid & (j >= W))[None], pen, NEG_INF)
    return jnp.stack([first, general])


def _swa(sinks, qs, ks, vst, batch, seq):
    T = qs.shape[0]
    W, tq = SWA_WINDOW, SWA_TQ
    nb = seq // tq
    wpb = tq // W
    before = lambda b, i: b * nb * wpb + jnp.maximum(wpb * i - 1, 0)
    cur = lambda n: pl.BlockSpec((tq, n), lambda b, i: (b * nb + i, 0))
    prev = lambda n: pl.BlockSpec((W, n), lambda b, i: (before(b, i), 0))
    vt_cur = pl.BlockSpec((SWA_KV_HEADS * VT_ROWS, tq), lambda b, i: (0, b * nb + i))
    vt_prev = pl.BlockSpec((SWA_KV_HEADS * VT_ROWS, W), lambda b, i: (0, before(b, i)))
    bias = pl.BlockSpec((1, SWA_HEADS, W + tq, tq), lambda b, i: (jnp.minimum(i, 1), 0, 0, 0))
    return pl.pallas_call(
        _swa_kernel,
        grid=(batch, nb),
        in_specs=[pl.BlockSpec(memory_space=pltpu.SMEM), cur(512), cur(512), prev(512),
                  vt_cur, vt_prev, bias],
        out_specs=cur(512),
        out_shape=jax.ShapeDtypeStruct((T, 512), BF16),
        compiler_params=pltpu.CompilerParams(dimension_semantics=("parallel", "parallel"),
                                             vmem_limit_bytes=VMEM_LIMIT),
        name="swa",
    )(sinks, qs, ks, ks, vst, vst, _swa_bias())


def _post_kernel(x_ref, om_ref, os_ref, wom_ref, wos_ref, wg_ref, wout_ref, g1_ref, b1_ref,
                 wr_ref, rb_ref, tri_ref,
                 h1_ref, h1p_ref, e_ref, w_ref, r_ref, cnt_ref, carry_scr):
    tm = x_ref.shape[0]
    step = pl.program_id(0)

    @pl.when(step == 0)
    def _():
        carry_scr[...] = jnp.zeros(carry_scr.shape, F32)

    x = x_ref[...]
    ya = _dot(om_ref[...], wom_ref[...])
    yb = _dot(os_ref[...], wos_ref[...])
    gates = _dot(x.astype(BF16), wg_ref[...])
    merged = _sigmoid(gates[:, :D_MODEL]) * ya + _sigmoid(gates[:, D_MODEL:]) * yb
    mix = _dot(merged.astype(BF16), wout_ref[...])
    z = DEEPNORM_ALPHA * x + mix
    mu = jnp.mean(z, axis=-1, keepdims=True)
    zc = z - mu
    var = jnp.mean(zc * zc, axis=-1, keepdims=True)
    h1 = zc * lax.rsqrt(var + LN_EPS) * g1_ref[...] + b1_ref[...]
    h1_ref[...] = h1
    _pack_rows(h1, h1p_ref)

    all_scores = _sigmoid(_dot_nt(wr_ref[...], h1.astype(BF16)))
    carry = carry_scr[...]
    for c in range(tm // ROUTE_TN):
        cols = slice(c * ROUTE_TN, (c + 1) * ROUTE_TN)
        idxs, weights, ranks, carry = _route(all_scores[:, cols], rb_ref[...], tri_ref[...], carry)
        e_ref[:, cols] = idxs
        w_ref[:, cols] = weights
        r_ref[:, cols] = ranks
    carry_scr[...] = carry
    cnt_ref[...] = carry.astype(jnp.int32)


def _route(scores, bias, tri, carry):
    tn = scores.shape[1]
    choice = scores + bias
    row = lax.broadcasted_iota(jnp.int32, (N_EXPERTS, tn), 0)
    grow = lax.broadcasted_iota(jnp.int32, (GROUP_SIZE, tn), 0)
    gscore = []
    for g in range(N_GROUPS):
        blk = choice[g * GROUP_SIZE:(g + 1) * GROUP_SIZE, :]
        m1 = jnp.max(blk, axis=0, keepdims=True)
        i1 = jnp.min(jnp.where(blk == m1, grow, GROUP_SIZE), axis=0, keepdims=True)
        m2 = jnp.max(jnp.where(grow == i1, NEG_INF, blk), axis=0, keepdims=True)
        gscore.append(m1 + m2)
    gsc = jnp.concatenate(gscore, axis=0)
    gidx = lax.broadcasted_iota(jnp.int32, (N_GROUPS, tn), 0)
    grank = jnp.zeros((N_GROUPS, tn), jnp.int32)
    for g in range(N_GROUPS):
        sg = gsc[g:g + 1, :]
        beats = (sg > gsc) | ((sg == gsc) & (gidx > g))
        grank = grank + beats.astype(jnp.int32)
    gsel = (grank < TOPK_GROUPS).astype(F32)
    emask = jnp.concatenate(
        [jnp.broadcast_to(gsel[g:g + 1, :], (GROUP_SIZE, tn)) for g in range(N_GROUPS)], axis=0)
    work = jnp.where(emask > 0.0, choice, NEG_INF)
    eligible = work
    idxs, svals = [], []
    for _k in range(TOP_K):
        m = jnp.max(work, axis=0, keepdims=True)
        idx = jnp.min(jnp.where(work == m, row, N_EXPERTS), axis=0, keepdims=True)
        hit = row == idx
        svals.append(jnp.sum(jnp.where(hit, scores, 0.0), axis=0, keepdims=True))
        work = jnp.where(hit, NEG_INF, work)
        idxs.append(idx)
    sel = jnp.where(work != eligible, 1.0, 0.0)
    ssum = svals[0]
    for sv in svals[1:]:
        ssum = ssum + sv
    weights = jnp.concatenate([sv / ssum * ROUTED_SCALE for sv in svals], axis=0)

    rank = _dot(sel.astype(BF16), tri) + carry[:, 0:1]
    ranks = jnp.concatenate(
        [jnp.sum(jnp.where(row == idx, rank, 0.0), axis=0, keepdims=True) for idx in idxs],
        axis=0).astype(jnp.int32)
    carry = carry + jnp.sum(sel, axis=1, keepdims=True)
    return jnp.concatenate(idxs, axis=0), weights, ranks, carry


def _post(x2, o_mla, o_swa, wom, wos, wg, wout, g1, b1, wr_t, rbias, tri, part):
    T = x2.shape[0] // MOE_PARTS
    tm = POST_TM
    first = part * (T // tm)
    full = lambda a: pl.BlockSpec(a.shape, lambda i: (0, 0))
    row_in = lambda n: pl.BlockSpec((tm, n), lambda i: (first + i, 0))
    row = lambda n: pl.BlockSpec((tm, n), lambda i: (i, 0))
    col = pl.BlockSpec((TOP_K, tm), lambda i: (0, i))
    return pl.pallas_call(
        _post_kernel,
        grid=(T // tm,),
        in_specs=[row_in(D_MODEL), row_in(512), row_in(512), full(wom), full(wos), full(wg), full(wout),
                  full(g1), full(b1), full(wr_t), full(rbias), full(tri)],
        out_specs=[row(D_MODEL), pl.BlockSpec((tm * ROW_SUB, LANES), lambda i: (i, 0)), col, col, col,
                   pl.BlockSpec((N_EXPERTS, LANES), lambda i: (0, 0))],
        out_shape=[jax.ShapeDtypeStruct((T, D_MODEL), F32),
                   jax.ShapeDtypeStruct((T * ROW_SUB, LANES), jnp.uint32),
                   jax.ShapeDtypeStruct((TOP_K, T), jnp.int32),
                   jax.ShapeDtypeStruct((TOP_K, T), F32),
                   jax.ShapeDtypeStruct((TOP_K, T), jnp.int32),
                   jax.ShapeDtypeStruct((N_EXPERTS, LANES), jnp.int32)],
        scratch_shapes=[pltpu.VMEM((N_EXPERTS, LANES), F32)],
        compiler_params=pltpu.CompilerParams(dimension_semantics=("arbitrary",),
                                             vmem_limit_bytes=VMEM_LIMIT),
        name="post",
    )(x2, o_mla, o_swa, wom, wos, wg, wout, g1, b1, wr_t, rbias, tri)


def _dest_kernel(start_ref, e_ref, r_ref, o_ref):
    e = e_ref[...]
    base = jnp.zeros(e.shape, jnp.int32)
    for j in range(N_EXPERTS):
        base = jnp.where(e == j, start_ref[j], base)
    o_ref[...] = base + r_ref[...]


def _dest(seg_start, e_t, r_t):
    T = e_t.shape[1]
    tn = min(T, 4096)
    col = pl.BlockSpec((TOP_K, tn), lambda i: (0, i))
    return pl.pallas_call(
        _dest_kernel,
        grid=(T // tn,),
        in_specs=[pl.BlockSpec(memory_space=pltpu.SMEM), col, col],
        out_specs=col,
        out_shape=jax.ShapeDtypeStruct((TOP_K, T), jnp.int32),
        compiler_params=pltpu.CompilerParams(dimension_semantics=("parallel",)),
        name="dest",
    )(seg_start, e_t, r_t)


def _sc_worker_chunks(n_tokens):
    info = plsc.get_sparse_core_info()
    n_workers = info.num_cores * info.num_subcores
    per_worker = n_tokens // SC_CHUNK // n_workers
    assert per_worker * n_workers * SC_CHUNK == n_tokens
    first = (lax.axis_index("s") * info.num_cores + lax.axis_index("c")) * per_worker
    return first, per_worker


def _sc_dispatch(h1p, idx, n_rows):
    n_tokens = h1p.shape[0] // ROW_SUB
    R = SC_CHUNK * ROW_SUB

    def body(h_hbm, idx_hbm, xs_hbm, idx_v, rows_v, sem):
        first, per_worker = _sc_worker_chunks(n_tokens)

        @pl.loop(0, per_worker)
        def _(ci):
            c = first + ci
            pltpu.sync_copy(idx_hbm.at[c], idx_v)
            pltpu.sync_copy(h_hbm.at[pl.ds(c * R, R)], rows_v)
            copies = [pltpu.make_async_copy(rows_v, xs_hbm.at[idx_v.at[k]], sem) for k in range(TOP_K)]
            for cp in copies:
                cp.start()
            for cp in copies:
                cp.wait()

    run = pl.kernel(
        body, out_type=jax.ShapeDtypeStruct((n_rows * ROW_SUB, LANES), jnp.uint32),
        mesh=plsc.VectorSubcoreMesh(core_axis_name="c", subcore_axis_name="s"),
        scratch_types=[pltpu.VMEM((TOP_K, R), jnp.int32), pltpu.VMEM((R, LANES), jnp.uint32),
                       pltpu.SemaphoreType.DMA],
        name="sc_dispatch")
    return run(h1p, idx)


def _sc_gather(ys, idx, n_tokens):
    R = SC_CHUNK * ROW_SUB
    half = TOP_K // 2

    def body(ys_hbm, idx_hbm, yg_hbm, idx_v, buf, gsem, wsem):
        first, per_worker = _sc_worker_chunks(n_tokens)

        @pl.loop(0, per_worker)
        def _(ci):
            c = first + ci
            pltpu.sync_copy(idx_hbm.at[c], idx_v)
            for k0 in (0, half):
                gathers = [pltpu.make_async_copy(ys_hbm.at[idx_v.at[k0 + k]], buf.at[k], gsem)
                           for k in range(half)]
                for cp in gathers:
                    cp.start()
                for cp in gathers:
                    cp.wait()
                writes = [pltpu.make_async_copy(buf.at[k], yg_hbm.at[k0 + k, pl.ds(c * R, R)], wsem)
                          for k in range(half)]
                for cp in writes:
                    cp.start()
                for cp in writes:
                    cp.wait()

    run = pl.kernel(
        body, out_type=jax.ShapeDtypeStruct((TOP_K, n_tokens * ROW_SUB, LANES), jnp.uint32),
        mesh=plsc.VectorSubcoreMesh(core_axis_name="c", subcore_axis_name="s"),
        scratch_types=[pltpu.VMEM((TOP_K, R), jnp.int32), pltpu.VMEM((half, R, LANES), jnp.uint32),
                       pltpu.SemaphoreType.DMA, pltpu.SemaphoreType.DMA],
        name="sc_gather")
    return run(ys, idx)


def _experts_kernel(first_ref, nblk_ref, cnt_ref, xs_hbm, wup_ref, wdn_ref, after_ref, ys_hbm,
                    wup_bf, wdn_bf, xbuf, ybuf, in_sem, out_sem):
    del after_ref
    step = pl.program_id(0)
    total = first_ref[N_EXPERTS - 1] + nblk_ref[N_EXPERTS - 1]
    rb = ROW_BLK * ROW_SUB
    depth = EXPERT_SLOTS

    def slot_of(b):
        return lax.rem(b, depth)

    def rows_of(b):
        return pl.ds(pl.multiple_of(b * rb, rb), rb)

    def in_copy(b):
        return pltpu.make_async_copy(xs_hbm.at[rows_of(b)], xbuf.at[slot_of(b)], in_sem.at[slot_of(b)])

    def out_copy(b):
        return pltpu.make_async_copy(ybuf.at[slot_of(b)], ys_hbm.at[rows_of(b)], out_sem.at[slot_of(b)])

    @pl.when(step == 0)
    def _():
        for b in range(depth - 1):
            @pl.when(b < total)
            def _():
                in_copy(b).start()

    for j in range(EXPERTS_PER_STEP):
        _run_expert(step * EXPERTS_PER_STEP + j, j, first_ref, nblk_ref, cnt_ref, wup_ref, wdn_ref,
                    wup_bf, wdn_bf, xbuf, ybuf, in_copy, out_copy, slot_of, total)

    @pl.when(step == N_EXPERTS // EXPERTS_PER_STEP - 1)
    def _():
        for back in range(depth, 0, -1):
            @pl.when(total - back >= 0)
            def _():
                out_copy(total - back).wait()


def _run_expert(e, j, first_ref, nblk_ref, cnt_ref, wup_ref, wdn_ref, wup_bf, wdn_bf, xbuf, ybuf,
                in_copy, out_copy, slot_of, total):
    n = nblk_ref[e]
    b0 = first_ref[e]
    count = cnt_ref[e]
    depth = EXPERT_SLOTS

    @pl.when(n > 0)
    def _():
        wup_bf[...] = wup_ref[j].astype(BF16)
        wdn_bf[...] = wdn_ref[j].astype(BF16)

        def body(b, carry):
            slot = slot_of(b)
            in_copy(b).wait()

            @pl.when(b + depth - 1 < total)
            def _():
                in_copy(b + depth - 1).start()

            @pl.when(b >= depth)
            def _():
                out_copy(b - depth).wait()

            sub = ROW_BLK // EXPERT_SPLIT
            n_valid = count - (b - b0) * ROW_BLK
            xbs = [jnp.concatenate(
                [c.astype(BF16) for c in _unpack_rows(xbuf.at[slot], sub, first=i * sub, n_valid=n_valid)],
                axis=1) for i in range(EXPERT_SPLIT)]
            gus = [_dot(xb, wup_bf[...]) for xb in xbs]
            hids = [(gu[:, :EXPERT_HIDDEN] * _sigmoid(gu[:, :EXPERT_HIDDEN])
                     * gu[:, EXPERT_HIDDEN:]).astype(BF16) for gu in gus]
            ys = [_dot(hid, wdn_bf[...]) for hid in hids]
            for i in range(EXPERT_SPLIT):
                _pack_rows(ys[i], ybuf.at[slot], first=i * sub)
            out_copy(b).start()
            return carry

        lax.fori_loop(b0, b0 + n, body, 0)


def _experts(first_blk, n_blk, counts, xs, w_exp_up, w_exp_down, run_after):
    rb = ROW_BLK * ROW_SUB
    eps = EXPERTS_PER_STEP
    grid_spec = pltpu.PrefetchScalarGridSpec(
        num_scalar_prefetch=3,
        grid=(N_EXPERTS // eps,),
        in_specs=[pl.BlockSpec(memory_space=pl.ANY),
                  pl.BlockSpec((eps, D_MODEL, 2 * EXPERT_HIDDEN), lambda s, fb, nb, ct: (s, 0, 0)),
                  pl.BlockSpec((eps, EXPERT_HIDDEN, D_MODEL), lambda s, fb, nb, ct: (s, 0, 0)),
                  pl.BlockSpec(memory_space=pl.ANY)],
        out_specs=pl.BlockSpec(memory_space=pl.ANY),
        scratch_shapes=[pltpu.VMEM((D_MODEL, 2 * EXPERT_HIDDEN), BF16),
                        pltpu.VMEM((EXPERT_HIDDEN, D_MODEL), BF16),
                        pltpu.VMEM((EXPERT_SLOTS, rb, LANES), jnp.uint32),
                        pltpu.VMEM((EXPERT_SLOTS, rb, LANES), jnp.uint32),
                        pltpu.SemaphoreType.DMA((EXPERT_SLOTS,)),
                        pltpu.SemaphoreType.DMA((EXPERT_SLOTS,))],
    )
    return pl.pallas_call(
        _experts_kernel,
        grid_spec=grid_spec,
        out_shape=jax.ShapeDtypeStruct(xs.shape, jnp.uint32),
        compiler_params=pltpu.CompilerParams(dimension_semantics=("arbitrary",),
                                             vmem_limit_bytes=VMEM_LIMIT),
        name="experts",
    )(first_blk, n_blk, counts, xs, w_exp_up, w_exp_down, run_after)


def _shared_kernel(h1p_ref, wsu_ref, wsd_ref, o_ref):
    tm = o_ref.shape[0]
    hb = jnp.concatenate([c.astype(BF16) for c in _unpack_rows(h1p_ref, tm)], axis=1)
    gu = _dot(hb, wsu_ref[...])
    g = gu[:, :SHARED_HIDDEN]
    hid = g * _sigmoid(g) * gu[:, SHARED_HIDDEN:]
    o_ref[...] = _dot(hid.astype(BF16), wsd_ref[...]).astype(BF16)


def _shared(h1p, wsu, wsd):
    T = h1p.shape[0] // ROW_SUB
    tm = SHARED_TM
    full = lambda a: pl.BlockSpec(a.shape, lambda i: (0, 0))
    row = pl.BlockSpec((tm, D_MODEL), lambda i: (i, 0))
    return pl.pallas_call(
        _shared_kernel,
        grid=(T // tm,),
        in_specs=[pl.BlockSpec((tm * ROW_SUB, LANES), lambda i: (i, 0)), full(wsu), full(wsd)],
        out_specs=row,
        out_shape=jax.ShapeDtypeStruct((T, D_MODEL), BF16),
        compiler_params=pltpu.CompilerParams(dimension_semantics=("parallel",),
                                             vmem_limit_bytes=VMEM_LIMIT),
        name="shared",
    )(h1p, wsu, wsd)


def _combine_kernel(w_ref, h1_ref, sh_ref, yg_ref, g2_ref, b2_ref, *rest):
    o_ref = rest[-1]
    tm = h1_ref.shape[0]
    h1 = h1_ref[...]
    ffn = sh_ref[...].astype(F32)

    w = w_ref[...]
    acc = [None] * (2 * ROW_SUB)
    for k in range(TOP_K):
        wk = w[:, k:k + 1]
        chunks = _unpack_rows(yg_ref.at[k], tm)
        for c in range(2 * ROW_SUB):
            acc[c] = wk * chunks[c] if acc[c] is None else acc[c] + wk * chunks[c]
    routed = jnp.concatenate(acc, axis=1)
    z = DEEPNORM_ALPHA * h1 + (routed + ffn)
    mu = jnp.mean(z, axis=-1, keepdims=True)
    zc = z - mu
    var = jnp.mean(zc * zc, axis=-1, keepdims=True)
    o_ref[...] = zc * lax.rsqrt(var + LN_EPS) * g2_ref[...] + b2_ref[...]


def _combine(w_tok, h1, sh, yg, g2, b2, in_first, out_first, out_rows, out_prev):
    n_tok = yg.shape[1] // ROW_SUB
    tm = COMBINE_TM
    fi, fo = in_first // tm, out_first // tm
    full = lambda a: pl.BlockSpec(a.shape, lambda i: (0, 0))
    row = pl.BlockSpec((tm, D_MODEL), lambda i: (fi + i, 0))
    in_specs = [pl.BlockSpec((tm, TOP_K), lambda i: (fi + i, 0)), row, row,
                pl.BlockSpec((TOP_K, tm * ROW_SUB, LANES), lambda i: (0, i, 0)),
                full(g2), full(b2)]
    args = [w_tok, h1, sh, yg, g2, b2]
    aliases = {}
    if out_prev is not None:
        in_specs.append(pl.BlockSpec(memory_space=pl.ANY))
        args.append(out_prev)
        aliases = {len(args) - 1: 0}
    return pl.pallas_call(
        _combine_kernel,
        grid=(n_tok // tm,),
        in_specs=in_specs,
        out_specs=pl.BlockSpec((tm, D_MODEL), lambda i: (fo + i, 0)),
        out_shape=jax.ShapeDtypeStruct((out_rows, D_MODEL), F32),
        input_output_aliases=aliases,
        compiler_params=pltpu.CompilerParams(dimension_semantics=("parallel",),
                                             vmem_limit_bytes=VMEM_LIMIT),
        name="combine",
    )(*args)


def _prep_weights(w_in, w_uq, w_ukv, seq):
    z = lambda n: jnp.zeros((D_MODEL, n), F32)
    kr = w_in[:, 512:544]
    qs = w_in[:, 544:1056] * (LOG2E / math.sqrt(SWA_HEAD_DIM))
    ks0, ks1 = w_in[:, 1056:1120], w_in[:, 1120:1184]
    half = MLA_ROPE // 2
    w1 = jnp.concatenate([
        w_in[:, 0:512], qs,
        ks0, z(64), z(64), ks0, ks1, z(64), z(64), ks1,
        z(64), kr, z(32),
        z(64), -kr[:, half:], kr[:, :half], z(32)], axis=1).astype(BF16)
    wvst = w_in[:, 1184:1312].T.astype(BF16)
    wg = w_in[:, 1312:3360].astype(BF16)

    zq = lambda n: jnp.zeros((MLA_Q_LORA, n), F32)
    qd = MLA_NOPE + MLA_ROPE
    q_cols, r_cols, k_cols, v_cols = [], [], [], []
    for h in range(MLA_HEADS):
        wq = w_uq[:, h * qd:(h + 1) * qd]
        q_cols += [wq, zq(32)]
        r_cols += [zq(64), -wq[:, MLA_NOPE + half:], wq[:, MLA_NOPE:MLA_NOPE + half], zq(32)]
        wk = w_ukv[:, h * 128:(h + 1) * 128]
        k_cols += [wk[:, :MLA_NOPE], zq(64)]
        v_cols += [wk[:, MLA_NOPE:]]
    wq2 = jnp.concatenate(q_cols + r_cols, axis=1).astype(BF16)
    wk = jnp.concatenate(k_cols, axis=1).astype(BF16)
    wvt = jnp.concatenate(v_cols, axis=1).T.astype(BF16)

    inv_freq = ROPE_THETA ** (-jnp.arange(0, MLA_ROPE, 2, dtype=F32) / MLA_ROPE)
    ang = jnp.arange(seq, dtype=F32)[:, None] * inv_freq[None, :]
    cos, sin = jnp.cos(ang), jnp.sin(ang)
    one, zero = jnp.ones((seq, 64), F32), jnp.zeros((seq, 64), F32)
    z32 = jnp.zeros((seq, 32), F32)
    scale = LOG2E / math.sqrt(MLA_NOPE + MLA_ROPE)
    tabs = (jnp.concatenate([one, cos, cos, z32], axis=1) * scale,
            jnp.concatenate([zero, sin, sin, z32], axis=1) * scale,
            jnp.concatenate([zero, cos, cos, z32], axis=1),
            jnp.concatenate([zero, sin, sin, z32], axis=1))
    return w1, wvst, wg, wq2, wk, wvt, tabs


def kernel(x, w_in, mla_q_norm, mla_kv_norm, w_uq, w_ukv, attn_sinks, w_o_mla, w_o_swa, w_out,
           ln1_g, ln1_b, w_router, router_bias, w_exp_up, w_exp_down, w_sh_up, w_sh_down,
           ln2_g, ln2_b):
    batch, seq, _ = x.shape
    T = batch * seq
    x2 = x.reshape(T, D_MODEL)
    w1, wvst, wg, wq2, wk, wvt, tabs = _prep_weights(w_in, w_uq, w_ukv, seq)
    q, k, vt, qs, ks, vst = _proj(x2, w1, wvst, wq2, wk, wvt, mla_q_norm.reshape(1, -1),
                                  mla_kv_norm.reshape(1, -1), tabs, seq)
    o_mla = _mla(q, k, vt, batch, seq)
    o_swa = _swa(attn_sinks.astype(F32), qs, ks, vst, batch, seq)

    tri = (lax.broadcasted_iota(jnp.int32, (ROUTE_TN, ROUTE_TN), 0)
           < lax.broadcasted_iota(jnp.int32, (ROUTE_TN, ROUTE_TN), 1)).astype(BF16)
    post_weights = (w_o_mla.astype(BF16), w_o_swa.astype(BF16), wg, w_out.astype(BF16),
                    ln1_g.reshape(1, -1), ln1_b.reshape(1, -1), w_router.T.astype(BF16),
                    router_bias.reshape(-1, 1).astype(F32), tri)
    wsu, wsd = w_sh_up.astype(BF16), w_sh_down.astype(BF16)
    g2, b2 = ln2_g.reshape(1, -1), ln2_b.reshape(1, -1)

    Tp = T // MOE_PARTS
    n_rows = (Tp * TOP_K // ROW_BLK + N_EXPERTS) * ROW_BLK
    out = None
    for part in range(MOE_PARTS):
        h1, h1p, e_t, w_t, r_t, cnt = _post(x2, o_mla, o_swa, *post_weights, part)
        counts = cnt[:, 0]
        pad = (counts + ROW_BLK - 1) // ROW_BLK * ROW_BLK
        pad_start = jnp.cumsum(pad) - pad
        dest = _dest(pad_start.astype(jnp.int32), e_t, r_t)
        idx = (dest.reshape(TOP_K, Tp // SC_CHUNK, SC_CHUNK, 1) * ROW_SUB
               + jnp.arange(ROW_SUB, dtype=jnp.int32))
        idx = idx.transpose(1, 0, 2, 3).reshape(Tp // SC_CHUNK, TOP_K, SC_CHUNK * ROW_SUB)
        xs = _sc_dispatch(h1p, idx, n_rows)
        sh = _shared(h1p, wsu, wsd)
        ys = _experts((pad_start // ROW_BLK).astype(jnp.int32), (pad // ROW_BLK).astype(jnp.int32),
                      counts.astype(jnp.int32), xs, w_exp_up, w_exp_down, sh)
        Tc = Tp // COMBINE_PARTS
        n_chunks = Tc // SC_CHUNK
        w_tok = w_t.T
        for cpart in range(COMBINE_PARTS):
            yg = _sc_gather(ys, idx[cpart * n_chunks:(cpart + 1) * n_chunks], Tc)
            out = _combine(w_tok, h1, sh, yg, g2, b2, cpart * Tc, part * Tp + cpart * Tc, T, out)
    return out.reshape(batch, seq, D_MODEL)
```

```python
import math

import jax
import jax.numpy as jnp
from jax import lax
from jax.experimental import pallas as pl
from jax.experimental.pallas import tpu as pltpu
from jax.experimental.pallas import tpu_sc as plsc

D_MODEL = 1024
MLA_HEADS = 8
MLA_Q_LORA = 256
MLA_KV_LORA = 256
MLA_NOPE = 64
MLA_ROPE = 32
MLA_V = 64
VT_ROWS = 80
ROPE_THETA = 10000.0
SWA_HEADS = 8
SWA_KV_HEADS = 2
SWA_GROUP = SWA_HEADS // SWA_KV_HEADS
SWA_HEAD_DIM = 64
SWA_WINDOW = 128
SWA_TQ = 256
N_EXPERTS = 256
TOP_K = 8
N_GROUPS = 8
GROUP_SIZE = N_EXPERTS // N_GROUPS
TOPK_GROUPS = 4
EXPERT_HIDDEN = 256
SHARED_HIDDEN = 256
ROUTED_SCALE = 2.5
DEEPNORM_ALPHA = 2.0 ** 0.25
LN_EPS = 1e-5
RMS_EPS = 1e-6

LANES = 128
ROW_WORDS = D_MODEL // 2
ROW_SUB = ROW_WORDS // LANES
VMEM_LIMIT = 48 * 1024 * 1024

PROJ_TM = 512
MLA_TQ = 256
MLA_TK = 512
MLA_HPS = 8
POST_TM = 512
ROUTE_TN = 256
ROW_BLK = 256
EXPERT_SLOTS = 4
EXPERT_SPLIT = 2
EXPERTS_PER_STEP = 1
SHARED_TM = 512
COMBINE_TM = 256
SC_CHUNK = 32
MOE_PARTS = 1
COMBINE_PARTS = 4

BF16 = jnp.bfloat16
F32 = jnp.float32
NEG_INF = float("-inf")
LOG2E = math.log2(math.e)


def _sigmoid(v):
    return 1.0 / (1.0 + jnp.exp(-v))


def _dot(a, b):
    return jnp.dot(a, b, preferred_element_type=F32)


def _dot_nt(a, b):
    return lax.dot_general(a, b, (((1,), (1,)), ((), ())), preferred_element_type=F32)


def _pack_rows(y, out_ref, first=0):
    rows = y.shape[0]
    for j in range(ROW_SUB):
        a = y[:, j * LANES:(j + 1) * LANES].astype(BF16).astype(F32)
        b = y[:, ROW_WORDS + j * LANES:ROW_WORDS + (j + 1) * LANES].astype(BF16).astype(F32)
        ua = pltpu.bitcast(a, jnp.uint32) >> 16
        ub = pltpu.bitcast(b, jnp.uint32)
        out_ref[pl.ds(first * ROW_SUB + j, rows, stride=ROW_SUB), :] = ua | ub


def _unpack_rows(ref, rows, first=0, n_valid=None):
    lo, hi = [], []
    if n_valid is not None:
        live = first + lax.broadcasted_iota(jnp.int32, (rows, LANES), 0) < n_valid
    for j in range(ROW_SUB):
        u = ref[pl.ds(first * ROW_SUB + j, rows, stride=ROW_SUB), :]
        if n_valid is not None:
            u = jnp.where(live, u, jnp.uint32(0))
        lo.append(pltpu.bitcast(u << 16, F32))
        hi.append(pltpu.bitcast(u & jnp.uint32(0xFFFF0000), F32))
    return lo + hi


def _proj_kernel(x_ref, w1_ref, wvst_ref, wq_ref, wk_ref, wvt_ref, gq_ref, gkv_ref,
                 cq_ref, sq_ref, ck_ref, sk_ref,
                 q_ref, k_ref, vt_ref, qs_ref, ks_ref, vst_ref):
    xb = x_ref[...].astype(BF16)
    p = _dot(xb, w1_ref[...])

    def rms(c, g):
        return c * lax.rsqrt(jnp.mean(c * c, axis=-1, keepdims=True) + RMS_EPS) * g

    cqn = rms(p[:, 0:256], gq_ref[...]).astype(BF16)
    ckvn = rms(p[:, 256:512], gkv_ref[...]).astype(BF16)
    qs_ref[...] = p[:, 512:1024].astype(BF16)
    k01 = p[:, 1024:1152].astype(BF16)
    k10 = p[:, 1152:1280].astype(BF16)
    lane_lo = lax.broadcasted_iota(jnp.int32, k01.shape, 1) < SWA_HEAD_DIM
    zero = jnp.zeros_like(k01)
    ks_ref[:, 0 * LANES:1 * LANES] = jnp.where(lane_lo, k01, zero)
    ks_ref[:, 1 * LANES:2 * LANES] = jnp.where(lane_lo, zero, k10)
    ks_ref[:, 2 * LANES:3 * LANES] = jnp.where(lane_lo, k10, zero)
    ks_ref[:, 3 * LANES:4 * LANES] = jnp.where(lane_lo, zero, k01)
    ones = jnp.ones((VT_ROWS - MLA_V, xb.shape[0]), BF16)
    vst = _dot_nt(wvst_ref[...], xb).astype(BF16)
    for h in range(SWA_KV_HEADS):
        vst_ref[h * VT_ROWS:h * VT_ROWS + SWA_HEAD_DIM, :] = vst[h * SWA_HEAD_DIM:(h + 1) * SWA_HEAD_DIM, :]
        vst_ref[h * VT_ROWS + SWA_HEAD_DIM:(h + 1) * VT_ROWS, :] = ones
    kr = p[:, 1280:1408] * ck_ref[...] + p[:, 1408:1536] * sk_ref[...]
    qq = _dot(cqn, wq_ref[...])
    kn = _dot(ckvn, wk_ref[...])
    cq = cq_ref[...]
    sq = sq_ref[...]
    for h in range(MLA_HEADS):
        a = qq[:, h * LANES:(h + 1) * LANES]
        b = qq[:, 1024 + h * LANES:1024 + (h + 1) * LANES]
        q_ref[:, h * LANES:(h + 1) * LANES] = (a * cq + b * sq).astype(BF16)
        k_ref[:, h * LANES:(h + 1) * LANES] = (kn[:, h * LANES:(h + 1) * LANES] + kr).astype(BF16)
    vt = _dot_nt(wvt_ref[...], ckvn).astype(BF16)
    for h in range(MLA_HEADS):
        vt_ref[h * VT_ROWS:h * VT_ROWS + MLA_V, :] = vt[h * MLA_V:(h + 1) * MLA_V, :]
        vt_ref[h * VT_ROWS + MLA_V:(h + 1) * VT_ROWS, :] = ones


def _proj(x2, w1, wvst, wq2, wk, wvt, gq, gkv, tabs, seq):
    T = x2.shape[0]
    tm = PROJ_TM
    nper = seq // tm
    full = lambda shape: pl.BlockSpec(shape, lambda i: (0, 0))
    tab = pl.BlockSpec((tm, LANES), lambda i: (i % nper, 0))
    row = lambda n: pl.BlockSpec((tm, n), lambda i: (i, 0))
    col = lambda n: pl.BlockSpec((n, tm), lambda i: (0, i))
    return pl.pallas_call(
        _proj_kernel,
        grid=(T // tm,),
        in_specs=[row(D_MODEL), full(w1.shape), full(wvst.shape), full(wq2.shape), full(wk.shape),
                  full(wvt.shape), full(gq.shape), full(gkv.shape), tab, tab, tab, tab],
        out_specs=[row(1024), row(1024), col(MLA_HEADS * VT_ROWS), row(512), row(512),
                   col(SWA_KV_HEADS * VT_ROWS)],
        out_shape=[jax.ShapeDtypeStruct((T, 1024), BF16), jax.ShapeDtypeStruct((T, 1024), BF16),
                   jax.ShapeDtypeStruct((MLA_HEADS * VT_ROWS, T), BF16), jax.ShapeDtypeStruct((T, 512), BF16),
                   jax.ShapeDtypeStruct((T, 512), BF16),
                   jax.ShapeDtypeStruct((SWA_KV_HEADS * VT_ROWS, T), BF16)],
        compiler_params=pltpu.CompilerParams(dimension_semantics=("parallel",),
                                             vmem_limit_bytes=VMEM_LIMIT),
        name="proj",
    )(x2, w1, wvst, wq2, wk, wvt, gq, gkv, *tabs)


def _mla_kernel(q_ref, k_ref, vt_ref, o_ref, *acc_scr):
    tq = q_ref.shape[0]
    qi = pl.program_id(2)
    for acc in acc_scr:
        acc[...] = jnp.zeros(acc.shape, F32)

    def step(ks, tk, maxes, masked):
        scores = [_dot_nt(k_ref[pl.ds(ks, tk), h * LANES:(h + 1) * LANES],
                          q_ref[:, h * LANES:(h + 1) * LANES]) for h in range(MLA_HPS)]
        new_maxes, probs, alphas = [], [], []
        for h in range(MLA_HPS):
            s = scores[h]
            if masked:
                key = ks + lax.broadcasted_iota(jnp.int32, s.shape, 0)
                qry = qi * tq + lax.broadcasted_iota(jnp.int32, s.shape, 1)
                s = jnp.where(key <= qry, s, NEG_INF)
            m_new = jnp.maximum(maxes[h], jnp.max(s, axis=0, keepdims=True))
            alphas.append(jnp.exp2(maxes[h] - m_new))
            probs.append(jnp.exp2(s - m_new).astype(BF16))
            new_maxes.append(m_new)
        for h in range(MLA_HPS):
            pv = _dot(vt_ref[h * VT_ROWS:(h + 1) * VT_ROWS, pl.ds(ks, tk)], probs[h])
            acc_scr[h][...] = acc_scr[h][...] * alphas[h] + pv
        return tuple(new_maxes)

    init = tuple(jnp.full((1, tq), NEG_INF, F32) for _ in range(MLA_HPS))
    big = MLA_TK
    n_big = (qi * tq) // big
    maxes = lax.fori_loop(
        0, n_big, lambda kc, mx: step(pl.multiple_of(kc * big, big), big, mx, False), init)
    rest = pl.multiple_of(n_big * big, tq)
    maxes = lax.cond(
        rest < qi * tq,
        lambda mx: step(rest, 2 * tq, mx, True),
        lambda mx: step(rest, tq, mx, True),
        maxes)
    for h2 in range(MLA_HPS // 2):
        out_t = jnp.concatenate(
            [acc_scr[2 * h2 + g][0:MLA_V, :] / acc_scr[2 * h2 + g][MLA_V:MLA_V + 1, :] for g in range(2)],
            axis=0)
        o_ref[:, h2 * LANES:(h2 + 1) * LANES] = out_t.T.astype(BF16)


def _mla(q, k, vt, batch, seq):
    T = q.shape[0]
    tq = MLA_TQ
    nq = seq // tq
    hps = MLA_HPS
    return pl.pallas_call(
        _mla_kernel,
        grid=(batch, MLA_HEADS // hps, nq),
        in_specs=[pl.BlockSpec((tq, hps * LANES), lambda b, j, i: (b * nq + i, j)),
                  pl.BlockSpec((seq, hps * LANES), lambda b, j, i: (b, j)),
                  pl.BlockSpec((hps * VT_ROWS, seq), lambda b, j, i: (j, b))],
        out_specs=pl.BlockSpec((tq, hps * MLA_V), lambda b, j, i: (b * nq + i, j)),
        out_shape=jax.ShapeDtypeStruct((T, MLA_HEADS * MLA_V), BF16),
        scratch_shapes=[pltpu.VMEM((VT_ROWS, tq), F32) for _ in range(hps)],
        compiler_params=pltpu.CompilerParams(
            dimension_semantics=("parallel", "parallel", "arbitrary"), vmem_limit_bytes=VMEM_LIMIT),
        name="mla",
    )(q, k, vt)


def _swa_kernel(sink_ref, q_ref, kc_ref, kp_ref, vtc_ref, vtp_ref, bias_ref, o_ref):
    scores = []
    for head in range(SWA_HEADS):
        pair, g, kvh = head // 2, head % 2, head // SWA_GROUP
        col = (2 * kvh + g) * LANES
        band = jnp.concatenate([kp_ref[:, col:col + LANES], kc_ref[:, col:col + LANES]], axis=0)
        scores.append(_dot_nt(band, q_ref[:, pair * LANES:(pair + 1) * LANES]))
    probs, sink_terms = [], []
    for head in range(SWA_HEADS):
        s = scores[head] + bias_ref[0, head]
        sink = sink_ref[head] * LOG2E
        m = jnp.maximum(jnp.max(s, axis=0, keepdims=True), sink)
        probs.append(jnp.exp2(s - m).astype(BF16))
        sink_terms.append(jnp.exp2(sink - m))
    outs = []
    for head in range(SWA_HEADS):
        rows = slice((head // SWA_GROUP) * VT_ROWS, (head // SWA_GROUP + 1) * VT_ROWS)
        v_band = jnp.concatenate([vtp_ref[rows, :], vtc_ref[rows, :]], axis=1)
        pv = _dot(v_band, probs[head])
        outs.append(pv[0:SWA_HEAD_DIM, :] / (pv[SWA_HEAD_DIM:SWA_HEAD_DIM + 1, :] + sink_terms[head]))
    o_ref[...] = jnp.concatenate(outs, axis=0).T.astype(BF16)


def _swa_bias():
    W = SWA_WINDOW
    j = jnp.arange(W + SWA_TQ, dtype=jnp.int32)[:, None]
    i = jnp.arange(SWA_TQ, dtype=jnp.int32)[None, :]
    dist = i + W - j
    valid = (dist >= 0) & (dist < W)
    slopes = 2.0 ** (-8.0 * jnp.arange(1, SWA_HEADS + 1, dtype=F32) / SWA_HEADS)
    pen = -(slopes[:, None, None] * dist.astype(F32)[None]) * LOG2E
    general = jnp.where(valid[None], pen, NEG_INF)
    first = jnp.where((valid & (j >= W))[None], pen, NEG_INF)
    return jnp.stack([first, general])


def _swa(sinks, qs, ks, vst, batch, seq):
    T = qs.shape[0]
    W, tq = SWA_WINDOW, SWA_TQ
    nb = seq // tq
    wpb = tq // W
    before = lambda b, i: b * nb * wpb + jnp.maximum(wpb * i - 1, 0)
    cur = lambda n: pl.BlockSpec((tq, n), lambda b, i: (b * nb + i, 0))
    prev = lambda n: pl.BlockSpec((W, n), lambda b, i: (before(b, i), 0))
    vt_cur = pl.BlockSpec((SWA_KV_HEADS * VT_ROWS, tq), lambda b, i: (0, b * nb + i))
    vt_prev = pl.BlockSpec((SWA_KV_HEADS * VT_ROWS, W), lambda b, i: (0, before(b, i)))
    bias = pl.BlockSpec((1, SWA_HEADS, W + tq, tq), lambda b, i: (jnp.minimum(i, 1), 0, 0, 0))
    return pl.pallas_call(
        _swa_kernel,
        grid=(batch, nb),
        in_specs=[pl.BlockSpec(memory_space=pltpu.SMEM), cur(512), cur(512), prev(512),
                  vt_cur, vt_prev, bias],
        out_specs=cur(512),
        out_shape=jax.ShapeDtypeStruct((T, 512), BF16),
        compiler_params=pltpu.CompilerParams(dimension_semantics=("parallel", "parallel"),
                                             vmem_limit_bytes=VMEM_LIMIT),
        name="swa",
    )(sinks, qs, ks, ks, vst, vst, _swa_bias())


def _post_kernel(x_ref, om_ref, os_ref, wom_ref, wos_ref, wg_ref, wout_ref, g1_ref, b1_ref,
                 wr_ref, rb_ref, tri_ref,
                 h1_ref, h1p_ref, e_ref, w_ref, r_ref, cnt_ref, carry_scr):
    tm = x_ref.shape[0]
    step = pl.program_id(0)

    @pl.when(step == 0)
    def _():
        carry_scr[...] = jnp.zeros(carry_scr.shape, F32)

    x = x_ref[...]
    ya = _dot(om_ref[...], wom_ref[...])
    yb = _dot(os_ref[...], wos_ref[...])
    gates = _dot(x.astype(BF16), wg_ref[...])
    merged = _sigmoid(gates[:, :D_MODEL]) * ya + _sigmoid(gates[:, D_MODEL:]) * yb
    mix = _dot(merged.astype(BF16), wout_ref[...])
    z = DEEPNORM_ALPHA * x + mix
    mu = jnp.mean(z, axis=-1, keepdims=True)
    zc = z - mu
    var = jnp.mean(zc * zc, axis=-1, keepdims=True)
    h1 = zc * lax.rsqrt(var + LN_EPS) * g1_ref[...] + b1_ref[...]
    h1_ref[...] = h1
    _pack_rows(h1, h1p_ref)

    all_scores = _sigmoid(_dot_nt(wr_ref[...], h1.astype(BF16)))
    carry = carry_scr[...]
    for c in range(tm // ROUTE_TN):
        cols = slice(c * ROUTE_TN, (c + 1) * ROUTE_TN)
        idxs, weights, ranks, carry = _route(all_scores[:, cols], rb_ref[...], tri_ref[...], carry)
        e_ref[:, cols] = idxs
        w_ref[:, cols] = weights
        r_ref[:, cols] = ranks
    carry_scr[...] = carry
    cnt_ref[...] = carry.astype(jnp.int32)


def _route(scores, bias, tri, carry):
    tn = scores.shape[1]
    choice = scores + bias
    row = lax.broadcasted_iota(jnp.int32, (N_EXPERTS, tn), 0)
    grow = lax.broadcasted_iota(jnp.int32, (GROUP_SIZE, tn), 0)
    gscore = []
    for g in range(N_GROUPS):
        blk = choice[g * GROUP_SIZE:(g + 1) * GROUP_SIZE, :]
        m1 = jnp.max(blk, axis=0, keepdims=True)
        i1 = jnp.min(jnp.where(blk == m1, grow, GROUP_SIZE), axis=0, keepdims=True)
        m2 = jnp.max(jnp.where(grow == i1, NEG_INF, blk), axis=0, keepdims=True)
        gscore.append(m1 + m2)
    gsc = jnp.concatenate(gscore, axis=0)
    gidx = lax.broadcasted_iota(jnp.int32, (N_GROUPS, tn), 0)
    grank = jnp.zeros((N_GROUPS, tn), jnp.int32)
    for g in range(N_GROUPS):
        sg = gsc[g:g + 1, :]
        beats = (sg > gsc) | ((sg == gsc) & (gidx > g))
        grank = grank + beats.astype(jnp.int32)
    gsel = (grank < TOPK_GROUPS).astype(F32)
    emask = jnp.concatenate(
        [jnp.broadcast_to(gsel[g:g + 1, :], (GROUP_SIZE, tn)) for g in range(N_GROUPS)], axis=0)
    work = jnp.where(emask > 0.0, choice, NEG_INF)
    eligible = work
    idxs, svals = [], []
    for _k in range(TOP_K):
        m = jnp.max(work, axis=0, keepdims=True)
        idx = jnp.min(jnp.where(work == m, row, N_EXPERTS), axis=0, keepdims=True)
        hit = row == idx
        svals.append(jnp.sum(jnp.where(hit, scores, 0.0), axis=0, keepdims=True))
        work = jnp.where(hit, NEG_INF, work)
        idxs.append(idx)
    sel = jnp.where(work != eligible, 1.0, 0.0)
    ssum = svals[0]
    for sv in svals[1:]:
        ssum = ssum + sv
    weights = jnp.concatenate([sv / ssum * ROUTED_SCALE for sv in svals], axis=0)

    rank = _dot(sel.astype(BF16), tri) + carry[:, 0:1]
    ranks = jnp.concatenate(
        [jnp.sum(jnp.where(row == idx, rank, 0.0), axis=0, keepdims=True) for idx in idxs],
        axis=0).astype(jnp.int32)
    carry = carry + jnp.sum(sel, axis=1, keepdims=True)
    return jnp.concatenate(idxs, axis=0), weights, ranks, carry


def _post(x2, o_mla, o_swa, wom, wos, wg, wout, g1, b1, wr_t, rbias, tri, part):
    T = x2.shape[0] // MOE_PARTS
    tm = POST_TM
    first = part * (T // tm)
    full = lambda a: pl.BlockSpec(a.shape, lambda i: (0, 0))
    row_in = lambda n: pl.BlockSpec((tm, n), lambda i: (first + i, 0))
    row = lambda n: pl.BlockSpec((tm, n), lambda i: (i, 0))
    col = pl.BlockSpec((TOP_K, tm), lambda i: (0, i))
    return pl.pallas_call(
        _post_kernel,
        grid=(T // tm,),
        in_specs=[row_in(D_MODEL), row_in(512), row_in(512), full(wom), full(wos), full(wg), full(wout),
                  full(g1), full(b1), full(wr_t), full(rbias), full(tri)],
        out_specs=[row(D_MODEL), pl.BlockSpec((tm * ROW_SUB, LANES), lambda i: (i, 0)), col, col, col,
                   pl.BlockSpec((N_EXPERTS, LANES), lambda i: (0, 0))],
        out_shape=[jax.ShapeDtypeStruct((T, D_MODEL), F32),
                   jax.ShapeDtypeStruct((T * ROW_SUB, LANES), jnp.uint32),
                   jax.ShapeDtypeStruct((TOP_K, T), jnp.int32),
                   jax.ShapeDtypeStruct((TOP_K, T), F32),
                   jax.ShapeDtypeStruct((TOP_K, T), jnp.int32),
                   jax.ShapeDtypeStruct((N_EXPERTS, LANES), jnp.int32)],
        scratch_shapes=[pltpu.VMEM((N_EXPERTS, LANES), F32)],
        compiler_params=pltpu.CompilerParams(dimension_semantics=("arbitrary",),
                                             vmem_limit_bytes=VMEM_LIMIT),
        name="post",
    )(x2, o_mla, o_swa, wom, wos, wg, wout, g1, b1, wr_t, rbias, tri)


def _dest_kernel(start_ref, e_ref, r_ref, o_ref):
    e = e_ref[...]
    base = jnp.zeros(e.shape, jnp.int32)
    for j in range(N_EXPERTS):
        base = jnp.where(e == j, start_ref[j], base)
    o_ref[...] = base + r_ref[...]


def _dest(seg_start, e_t, r_t):
    T = e_t.shape[1]
    tn = min(T, 4096)
    col = pl.BlockSpec((TOP_K, tn), lambda i: (0, i))
    return pl.pallas_call(
        _dest_kernel,
        grid=(T // tn,),
        in_specs=[pl.BlockSpec(memory_space=pltpu.SMEM), col, col],
        out_specs=col,
        out_shape=jax.ShapeDtypeStruct((TOP_K, T), jnp.int32),
        compiler_params=pltpu.CompilerParams(dimension_semantics=("parallel",)),
        name="dest",
    )(seg_start, e_t, r_t)


def _sc_worker_chunks(n_tokens):
    info = plsc.get_sparse_core_info()
    n_workers = info.num_cores * info.num_subcores
    per_worker = n_tokens // SC_CHUNK // n_workers
    assert per_worker * n_workers * SC_CHUNK == n_tokens
    first = (lax.axis_index("s") * info.num_cores + lax.axis_index("c")) * per_worker
    return first, per_worker


def _sc_dispatch(h1p, idx, n_rows):
    n_tokens = h1p.shape[0] // ROW_SUB
    R = SC_CHUNK * ROW_SUB

    def body(h_hbm, idx_hbm, xs_hbm, idx_v, rows_v, sem):
        first, per_worker = _sc_worker_chunks(n_tokens)

        @pl.loop(0, per_worker)
        def _(ci):
            c = first + ci
            pltpu.sync_copy(idx_hbm.at[c], idx_v)
            pltpu.sync_copy(h_hbm.at[pl.ds(c * R, R)], rows_v)
            copies = [pltpu.make_async_copy(rows_v, xs_hbm.at[idx_v.at[k]], sem) for k in range(TOP_K)]
            for cp in copies:
                cp.start()
            for cp in copies:
                cp.wait()

    run = pl.kernel(
        body, out_type=jax.ShapeDtypeStruct((n_rows * ROW_SUB, LANES), jnp.uint32),
        mesh=plsc.VectorSubcoreMesh(core_axis_name="c", subcore_axis_name="s"),
        scratch_types=[pltpu.VMEM((TOP_K, R), jnp.int32), pltpu.VMEM((R, LANES), jnp.uint32),
                       pltpu.SemaphoreType.DMA],
        name="sc_dispatch")
    return run(h1p, idx)


def _sc_gather(ys, idx, n_tokens):
    R = SC_CHUNK * ROW_SUB
    half = TOP_K // 2

    def body(ys_hbm, idx_hbm, yg_hbm, idx_v, buf, gsem, wsem):
        first, per_worker = _sc_worker_chunks(n_tokens)

        @pl.loop(0, per_worker)
        def _(ci):
            c = first + ci
            pltpu.sync_copy(idx_hbm.at[c], idx_v)
            for k0 in (0, half):
                gathers = [pltpu.make_async_copy(ys_hbm.at[idx_v.at[k0 + k]], buf.at[k], gsem)
                           for k in range(half)]
                for cp in gathers:
                    cp.start()
                for cp in gathers:
                    cp.wait()
                writes = [pltpu.make_async_copy(buf.at[k], yg_hbm.at[k0 + k, pl.ds(c * R, R)], wsem)
                          for k in range(half)]
                for cp in writes:
                    cp.start()
                for cp in writes:
                    cp.wait()

    run = pl.kernel(
        body, out_type=jax.ShapeDtypeStruct((TOP_K, n_tokens * ROW_SUB, LANES), jnp.uint32),
        mesh=plsc.VectorSubcoreMesh(core_axis_name="c", subcore_axis_name="s"),
        scratch_types=[pltpu.VMEM((TOP_K, R), jnp.int32), pltpu.VMEM((half, R, LANES), jnp.uint32),
                       pltpu.SemaphoreType.DMA, pltpu.SemaphoreType.DMA],
        name="sc_gather")
    return run(ys, idx)


def _experts_kernel(first_ref, nblk_ref, cnt_ref, xs_hbm, wup_ref, wdn_ref, after_ref, ys_hbm,
                    wup_bf, wdn_bf, xbuf, ybuf, in_sem, out_sem):
    del after_ref
    step = pl.program_id(0)
    total = first_ref[N_EXPERTS - 1] + nblk_ref[N_EXPERTS - 1]
    rb = ROW_BLK * ROW_SUB
    depth = EXPERT_SLOTS

    def slot_of(b):
        return lax.rem(b, depth)

    def rows_of(b):
        return pl.ds(pl.multiple_of(b * rb, rb), rb)

    def in_copy(b):
        return pltpu.make_async_copy(xs_hbm.at[rows_of(b)], xbuf.at[slot_of(b)], in_sem.at[slot_of(b)])

    def out_copy(b):
        return pltpu.make_async_copy(ybuf.at[slot_of(b)], ys_hbm.at[rows_of(b)], out_sem.at[slot_of(b)])

    @pl.when(step == 0)
    def _():
        for b in range(depth - 1):
            @pl.when(b < total)
            def _():
                in_copy(b).start()

    for j in range(EXPERTS_PER_STEP):
        _run_expert(step * EXPERTS_PER_STEP + j, j, first_ref, nblk_ref, cnt_ref, wup_ref, wdn_ref,
                    wup_bf, wdn_bf, xbuf, ybuf, in_copy, out_copy, slot_of, total)

    @pl.when(step == N_EXPERTS // EXPERTS_PER_STEP - 1)
    def _():
        for back in range(depth, 0, -1):
            @pl.when(total - back >= 0)
            def _():
                out_copy(total - back).wait()


def _run_expert(e, j, first_ref, nblk_ref, cnt_ref, wup_ref, wdn_ref, wup_bf, wdn_bf, xbuf, ybuf,
                in_copy, out_copy, slot_of, total):
    n = nblk_ref[e]
    b0 = first_ref[e]
    count = cnt_ref[e]
    depth = EXPERT_SLOTS

    @pl.when(n > 0)
    def _():
        wup_bf[...] = wup_ref[j].astype(BF16)
        wdn_bf[...] = wdn_ref[j].astype(BF16)

        def body(b, carry):
            slot = slot_of(b)
            in_copy(b).wait()

            @pl.when(b + depth - 1 < total)
            def _():
                in_copy(b + depth - 1).start()

            @pl.when(b >= depth)
            def _():
                out_copy(b - depth).wait()

            sub = ROW_BLK // EXPERT_SPLIT
            n_valid = count - (b - b0) * ROW_BLK
            xbs = [jnp.concatenate(
                [c.astype(BF16) for c in _unpack_rows(xbuf.at[slot], sub, first=i * sub, n_valid=n_valid)],
                axis=1) for i in range(EXPERT_SPLIT)]
            gus = [_dot(xb, wup_bf[...]) for xb in xbs]
            hids = [(gu[:, :EXPERT_HIDDEN] * _sigmoid(gu[:, :EXPERT_HIDDEN])
                     * gu[:, EXPERT_HIDDEN:]).astype(BF16) for gu in gus]
            ys = [_dot(hid, wdn_bf[...]) for hid in hids]
            for i in range(EXPERT_SPLIT):
                _pack_rows(ys[i], ybuf.at[slot], first=i * sub)
            out_copy(b).start()
            return carry

        lax.fori_loop(b0, b0 + n, body, 0)


def _experts(first_blk, n_blk, counts, xs, w_exp_up, w_exp_down, run_after):
    rb = ROW_BLK * ROW_SUB
    eps = EXPERTS_PER_STEP
    grid_spec = pltpu.PrefetchScalarGridSpec(
        num_scalar_prefetch=3,
        grid=(N_EXPERTS // eps,),
        in_specs=[pl.BlockSpec(memory_space=pl.ANY),
                  pl.BlockSpec((eps, D_MODEL, 2 * EXPERT_HIDDEN), lambda s, fb, nb, ct: (s, 0, 0)),
                  pl.BlockSpec((eps, EXPERT_HIDDEN, D_MODEL), lambda s, fb, nb, ct: (s, 0, 0)),
                  pl.BlockSpec(memory_space=pl.ANY)],
        out_specs=pl.BlockSpec(memory_space=pl.ANY),
        scratch_shapes=[pltpu.VMEM((D_MODEL, 2 * EXPERT_HIDDEN), BF16),
                        pltpu.VMEM((EXPERT_HIDDEN, D_MODEL), BF16),
                        pltpu.VMEM((EXPERT_SLOTS, rb, LANES), jnp.uint32),
                        pltpu.VMEM((EXPERT_SLOTS, rb, LANES), jnp.uint32),
                        pltpu.SemaphoreType.DMA((EXPERT_SLOTS,)),
                        pltpu.SemaphoreType.DMA((EXPERT_SLOTS,))],
    )
    return pl.pallas_call(
        _experts_kernel,
        grid_spec=grid_spec,
        out_shape=jax.ShapeDtypeStruct(xs.shape, jnp.uint32),
        compiler_params=pltpu.CompilerParams(dimension_semantics=("arbitrary",),
                                             vmem_limit_bytes=VMEM_LIMIT),
        name="experts",
    )(first_blk, n_blk, counts, xs, w_exp_up, w_exp_down, run_after)


def _shared_kernel(h1p_ref, wsu_ref, wsd_ref, o_ref):
    tm = o_ref.shape[0]
    hb = jnp.concatenate([c.astype(BF16) for c in _unpack_rows(h1p_ref, tm)], axis=1)
    gu = _dot(hb, wsu_ref[...])
    g = gu[:, :SHARED_HIDDEN]
    hid = g * _sigmoid(g) * gu[:, SHARED_HIDDEN:]
    o_ref[...] = _dot(hid.astype(BF16), wsd_ref[...]).astype(BF16)


def _shared(h1p, wsu, wsd):
    T = h1p.shape[0] // ROW_SUB
    tm = SHARED_TM
    full = lambda a: pl.BlockSpec(a.shape, lambda i: (0, 0))
    row = pl.BlockSpec((tm, D_MODEL), lambda i: (i, 0))
    return pl.pallas_call(
        _shared_kernel,
        grid=(T // tm,),
        in_specs=[pl.BlockSpec((tm * ROW_SUB, LANES), lambda i: (i, 0)), full(wsu), full(wsd)],
        out_specs=row,
        out_shape=jax.ShapeDtypeStruct((T, D_MODEL), BF16),
        compiler_params=pltpu.CompilerParams(dimension_semantics=("parallel",),
                                             vmem_limit_bytes=VMEM_LIMIT),
        name="shared",
    )(h1p, wsu, wsd)


def _combine_kernel(w_ref, h1_ref, sh_ref, yg_ref, g2_ref, b2_ref, *rest):
    o_ref = rest[-1]
    tm = h1_ref.shape[0]
    h1 = h1_ref[...]
    ffn = sh_ref[...].astype(F32)

    w = w_ref[...]
    acc = [None] * (2 * ROW_SUB)
    for k in range(TOP_K):
        wk = w[:, k:k + 1]
        chunks = _unpack_rows(yg_ref.at[k], tm)
        for c in range(2 * ROW_SUB):
            acc[c] = wk * chunks[c] if acc[c] is None else acc[c] + wk * chunks[c]
    routed = jnp.concatenate(acc, axis=1)
    z = DEEPNORM_ALPHA * h1 + (routed + ffn)
    mu = jnp.mean(z, axis=-1, keepdims=True)
    zc = z - mu
    var = jnp.mean(zc * zc, axis=-1, keepdims=True)
    o_ref[...] = zc * lax.rsqrt(var + LN_EPS) * g2_ref[...] + b2_ref[...]


def _combine(w_tok, h1, sh, yg, g2, b2, in_first, out_first, out_rows, out_prev):
    n_tok = yg.shape[1] // ROW_SUB
    tm = COMBINE_TM
    fi, fo = in_first // tm, out_first // tm
    full = lambda a: pl.BlockSpec(a.shape, lambda i: (0, 0))
    row = pl.BlockSpec((tm, D_MODEL), lambda i: (fi + i, 0))
    in_specs = [pl.BlockSpec((tm, TOP_K), lambda i: (fi + i, 0)), row, row,
                pl.BlockSpec((TOP_K, tm * ROW_SUB, LANES), lambda i: (0, i, 0)),
                full(g2), full(b2)]
    args = [w_tok, h1, sh, yg, g2, b2]
    aliases = {}
    if out_prev is not None:
        in_specs.append(pl.BlockSpec(memory_space=pl.ANY))
        args.append(out_prev)
        aliases = {len(args) - 1: 0}
    return pl.pallas_call(
        _combine_kernel,
        grid=(n_tok // tm,),
        in_specs=in_specs,
        out_specs=pl.BlockSpec((tm, D_MODEL), lambda i: (fo + i, 0)),
        out_shape=jax.ShapeDtypeStruct((out_rows, D_MODEL), F32),
        input_output_aliases=aliases,
        compiler_params=pltpu.CompilerParams(dimension_semantics=("parallel",),
                                             vmem_limit_bytes=VMEM_LIMIT),
        name="combine",
    )(*args)


def _prep_weights(w_in, w_uq, w_ukv, seq):
    z = lambda n: jnp.zeros((D_MODEL, n), F32)
    kr = w_in[:, 512:544]
    qs = w_in[:, 544:1056] * (LOG2E / math.sqrt(SWA_HEAD_DIM))
    ks0, ks1 = w_in[:, 1056:1120], w_in[:, 1120:1184]
    half = MLA_ROPE // 2
    w1 = jnp.concatenate([
        w_in[:, 0:512], qs,
        ks0, ks1, ks1, ks0,
        z(64), kr, z(32),
        z(64), -kr[:, half:], kr[:, :half], z(32)], axis=1).astype(BF16)
    wvst = w_in[:, 1184:1312].T.astype(BF16)
    wg = w_in[:, 1312:3360].astype(BF16)

    pad3 = lambda a, lo, hi: jnp.pad(a, ((0, 0), (0, 0), (lo, hi)))
    wq3 = w_uq.reshape(MLA_Q_LORA, MLA_HEADS, MLA_NOPE + MLA_ROPE)
    rope = wq3[:, :, MLA_NOPE:]
    rot = jnp.concatenate([-rope[:, :, half:], rope[:, :, :half]], axis=2)
    wq2 = jnp.concatenate([pad3(wq3, 0, 32).reshape(MLA_Q_LORA, -1),
                           pad3(rot, MLA_NOPE, 32).reshape(MLA_Q_LORA, -1)], axis=1).astype(BF16)
    wkv3 = w_ukv.reshape(MLA_KV_LORA, MLA_HEADS, MLA_NOPE + MLA_V)
    wk = pad3(wkv3[:, :, :MLA_NOPE], 0, 64).reshape(MLA_KV_LORA, -1).astype(BF16)
    wvt = wkv3[:, :, MLA_NOPE:].reshape(MLA_KV_LORA, -1).T.astype(BF16)

    inv_freq = ROPE_THETA ** (-jnp.arange(0, MLA_ROPE, 2, dtype=F32) / MLA_ROPE)
    ang = jnp.arange(seq, dtype=F32)[:, None] * inv_freq[None, :]
    cos, sin = jnp.cos(ang), jnp.sin(ang)
    one, zero = jnp.ones((seq, 64), F32), jnp.zeros((seq, 64), F32)
    z32 = jnp.zeros((seq, 32), F32)
    scale = LOG2E / math.sqrt(MLA_NOPE + MLA_ROPE)
    tabs = (jnp.concatenate([one, cos, cos, z32], axis=1) * scale,
            jnp.concatenate([zero, sin, sin, z32], axis=1) * scale,
            jnp.concatenate([zero, cos, cos, z32], axis=1),
            jnp.concatenate([zero, sin, sin, z32], axis=1))
    return w1, wvst, wg, wq2, wk, wvt, tabs


def kernel(x, w_in, mla_q_norm, mla_kv_norm, w_uq, w_ukv, attn_sinks, w_o_mla, w_o_swa, w_out,
           ln1_g, ln1_b, w_router, router_bias, w_exp_up, w_exp_down, w_sh_up, w_sh_down,
           ln2_g, ln2_b):
    batch, seq, _ = x.shape
    T = batch * seq
    x2 = x.reshape(T, D_MODEL)
    w1, wvst, wg, wq2, wk, wvt, tabs = _prep_weights(w_in, w_uq, w_ukv, seq)
    q, k, vt, qs, ks, vst = _proj(x2, w1, wvst, wq2, wk, wvt, mla_q_norm.reshape(1, -1),
                                  mla_kv_norm.reshape(1, -1), tabs, seq)
    o_mla = _mla(q, k, vt, batch, seq)
    o_swa = _swa(attn_sinks.astype(F32), qs, ks, vst, batch, seq)

    tri = (lax.broadcasted_iota(jnp.int32, (ROUTE_TN, ROUTE_TN), 0)
           < lax.broadcasted_iota(jnp.int32, (ROUTE_TN, ROUTE_TN), 1)).astype(BF16)
    post_weights = (w_o_mla.astype(BF16), w_o_swa.astype(BF16), wg, w_out.astype(BF16),
                    ln1_g.reshape(1, -1), ln1_b.reshape(1, -1), w_router.T.astype(BF16),
                    router_bias.reshape(-1, 1).astype(F32), tri)
    wsu, wsd = w_sh_up.astype(BF16), w_sh_down.astype(BF16)
    g2, b2 = ln2_g.reshape(1, -1), ln2_b.reshape(1, -1)

    Tp = T // MOE_PARTS
    n_rows = (Tp * TOP_K // ROW_BLK + N_EXPERTS) * ROW_BLK
    out = None
    for part in range(MOE_PARTS):
        h1, h1p, e_t, w_t, r_t, cnt = _post(x2, o_mla, o_swa, *post_weights, part)
        counts = cnt[:, 0]
        pad = (counts + ROW_BLK - 1) // ROW_BLK * ROW_BLK
        pad_start = jnp.cumsum(pad) - pad
        dest = _dest(pad_start.astype(jnp.int32), e_t, r_t)
        idx = (dest.reshape(TOP_K, Tp // SC_CHUNK, SC_CHUNK, 1) * ROW_SUB
               + jnp.arange(ROW_SUB, dtype=jnp.int32))
        idx = idx.transpose(1, 0, 2, 3).reshape(Tp // SC_CHUNK, TOP_K, SC_CHUNK * ROW_SUB)
        xs = _sc_dispatch(h1p, idx, n_rows)
        sh = _shared(h1p, wsu, wsd)
        ys = _experts((pad_start // ROW_BLK).astype(jnp.int32), (pad // ROW_BLK).astype(jnp.int32),
                      counts.astype(jnp.int32), xs, w_exp_up, w_exp_down, sh)
        Tc = Tp // COMBINE_PARTS
        n_chunks = Tc // SC_CHUNK
        w_tok = w_t.T
        for cpart in range(COMBINE_PARTS):
            yg = _sc_gather(ys, idx[cpart * n_chunks:(cpart + 1) * n_chunks], Tc)
            out = _combine(w_tok, h1, sh, yg, g2, b2, cpart * Tc, part * Tp + cpart * Tc, T, out)
    return out.reshape(batch, seq, D_MODEL)
```

```python
import math

import jax
import jax.numpy as jnp
from jax import lax
from jax.experimental import pallas as pl
from jax.experimental.pallas import tpu as pltpu
from jax.experimental.pallas import tpu_sc as plsc

D_MODEL = 1024
MLA_HEADS = 8
MLA_Q_LORA = 256
MLA_KV_LORA = 256
MLA_NOPE = 64
MLA_ROPE = 32
MLA_V = 64
VT_ROWS = 80
ROPE_THETA = 10000.0
SWA_HEADS = 8
SWA_KV_HEADS = 2
SWA_GROUP = SWA_HEADS // SWA_KV_HEADS
SWA_HEAD_DIM = 64
SWA_WINDOW = 128
SWA_TQ = 256
N_EXPERTS = 256
TOP_K = 8
N_GROUPS = 8
GROUP_SIZE = N_EXPERTS // N_GROUPS
TOPK_GROUPS = 4
EXPERT_HIDDEN = 256
SHARED_HIDDEN = 256
ROUTED_SCALE = 2.5
DEEPNORM_ALPHA = 2.0 ** 0.25
LN_EPS = 1e-5
RMS_EPS = 1e-6

LANES = 128
ROW_WORDS = D_MODEL // 2
ROW_SUB = ROW_WORDS // LANES
VMEM_LIMIT = 48 * 1024 * 1024

PROJ_TM = 512
MLA_TQ = 256
MLA_TK = 512
MLA_HPS = 8
POST_TM = 512
ROUTE_TN = 256
ROW_BLK = 256
EXPERT_SLOTS = 4
EXPERT_SPLIT = 2
EXPERTS_PER_STEP = 1
SHARED_TM = 512
COMBINE_TM = 256
SC_CHUNK = 32
MOE_PARTS = 1
COMBINE_PARTS = 4

BF16 = jnp.bfloat16
F32 = jnp.float32
NEG_INF = float("-inf")
LOG2E = math.log2(math.e)


def _sigmoid(v):
    return 1.0 / (1.0 + jnp.exp(-v))


def _dot(a, b):
    return jnp.dot(a, b, preferred_element_type=F32)


def _dot_nt(a, b):
    return lax.dot_general(a, b, (((1,), (1,)), ((), ())), preferred_element_type=F32)


def _pack_rows(y, out_ref, first=0):
    rows = y.shape[0]
    for j in range(ROW_SUB):
        a = y[:, j * LANES:(j + 1) * LANES].astype(BF16).astype(F32)
        b = y[:, ROW_WORDS + j * LANES:ROW_WORDS + (j + 1) * LANES].astype(BF16).astype(F32)
        ua = pltpu.bitcast(a, jnp.uint32) >> 16
        ub = pltpu.bitcast(b, jnp.uint32)
        out_ref[pl.ds(first * ROW_SUB + j, rows, stride=ROW_SUB), :] = ua | ub


def _unpack_rows(ref, rows, first=0, n_valid=None):
    lo, hi = [], []
    if n_valid is not None:
        live = first + lax.broadcasted_iota(jnp.int32, (rows, LANES), 0) < n_valid
    for j in range(ROW_SUB):
        u = ref[pl.ds(first * ROW_SUB + j, rows, stride=ROW_SUB), :]
        if n_valid is not None:
            u = jnp.where(live, u, jnp.uint32(0))
        lo.append(pltpu.bitcast(u << 16, F32))
        hi.append(pltpu.bitcast(u & jnp.uint32(0xFFFF0000), F32))
    return lo + hi


def _proj_kernel(x_ref, w1_ref, wvst_ref, wq_ref, wk_ref, wvt_ref, gq_ref, gkv_ref,
                 cq_ref, sq_ref, ck_ref, sk_ref,
                 q_ref, k_ref, vt_ref, qs_ref, ks_ref, vst_ref):
    xb = x_ref[...].astype(BF16)
    p = _dot(xb, w1_ref[...])

    def rms(c, g):
        return c * lax.rsqrt(jnp.mean(c * c, axis=-1, keepdims=True) + RMS_EPS) * g

    cqn = rms(p[:, 0:256], gq_ref[...]).astype(BF16)
    ckvn = rms(p[:, 256:512], gkv_ref[...]).astype(BF16)
    qs_ref[...] = p[:, 512:1024].astype(BF16)
    k01 = p[:, 1024:1152].astype(BF16)
    k10 = p[:, 1152:1280].astype(BF16)
    lane_lo = lax.broadcasted_iota(jnp.int32, k01.shape, 1) < SWA_HEAD_DIM
    zero = jnp.zeros_like(k01)
    ks_ref[:, 0 * LANES:1 * LANES] = jnp.where(lane_lo, k01, zero)
    ks_ref[:, 1 * LANES:2 * LANES] = jnp.where(lane_lo, zero, k10)
    ks_ref[:, 2 * LANES:3 * LANES] = jnp.where(lane_lo, k10, zero)
    ks_ref[:, 3 * LANES:4 * LANES] = jnp.where(lane_lo, zero, k01)
    ones = jnp.ones((VT_ROWS - MLA_V, xb.shape[0]), BF16)
    vst = _dot_nt(wvst_ref[...], xb).astype(BF16)
    for h in range(SWA_KV_HEADS):
        vst_ref[h * VT_ROWS:h * VT_ROWS + SWA_HEAD_DIM, :] = vst[h * SWA_HEAD_DIM:(h + 1) * SWA_HEAD_DIM, :]
        vst_ref[h * VT_ROWS + SWA_HEAD_DIM:(h + 1) * VT_ROWS, :] = ones
    kr = p[:, 1280:1408] * ck_ref[...] + p[:, 1408:1536] * sk_ref[...]
    qq = _dot(cqn, wq_ref[...])
    kn = _dot(ckvn, wk_ref[...])
    cq = cq_ref[...]
    sq = sq_ref[...]
    for h in range(MLA_HEADS):
        a = qq[:, h * LANES:(h + 1) * LANES]
        b = qq[:, 1024 + h * LANES:1024 + (h + 1) * LANES]
        q_ref[:, h * LANES:(h + 1) * LANES] = (a * cq + b * sq).astype(BF16)
        k_ref[:, h * LANES:(h + 1) * LANES] = (kn[:, h * LANES:(h + 1) * LANES] + kr).astype(BF16)
    vt = _dot_nt(wvt_ref[...], ckvn).astype(BF16)
    for h in range(MLA_HEADS):
        vt_ref[h * VT_ROWS:h * VT_ROWS + MLA_V, :] = vt[h * MLA_V:(h + 1) * MLA_V, :]
        vt_ref[h * VT_ROWS + MLA_V:(h + 1) * VT_ROWS, :] = ones


def _proj(x2, w1, wvst, wq2, wk, wvt, gq, gkv, tabs, seq):
    T = x2.shape[0]
    tm = PROJ_TM
    nper = seq // tm
    full = lambda shape: pl.BlockSpec(shape, lambda i: (0, 0))
    tab = pl.BlockSpec((tm, LANES), lambda i: (i % nper, 0))
    row = lambda n: pl.BlockSpec((tm, n), lambda i: (i, 0))
    col = lambda n: pl.BlockSpec((n, tm), lambda i: (0, i))
    return pl.pallas_call(
        _proj_kernel,
        grid=(T // tm,),
        in_specs=[row(D_MODEL), full(w1.shape), full(wvst.shape), full(wq2.shape), full(wk.shape),
                  full(wvt.shape), full(gq.shape), full(gkv.shape), tab, tab, tab, tab],
        out_specs=[row(1024), row(1024), col(MLA_HEADS * VT_ROWS), row(512), row(512),
                   col(SWA_KV_HEADS * VT_ROWS)],
        out_shape=[jax.ShapeDtypeStruct((T, 1024), BF16), jax.ShapeDtypeStruct((T, 1024), BF16),
                   jax.ShapeDtypeStruct((MLA_HEADS * VT_ROWS, T), BF16), jax.ShapeDtypeStruct((T, 512), BF16),
                   jax.ShapeDtypeStruct((T, 512), BF16),
                   jax.ShapeDtypeStruct((SWA_KV_HEADS * VT_ROWS, T), BF16)],
        compiler_params=pltpu.CompilerParams(dimension_semantics=("parallel",),
                                             vmem_limit_bytes=VMEM_LIMIT),
        name="proj",
    )(x2, w1, wvst, wq2, wk, wvt, gq, gkv, *tabs)


def _mla_kernel(q_ref, k_ref, vt_ref, o_ref, *acc_scr):
    tq = q_ref.shape[0]
    qi = pl.program_id(2)
    for acc in acc_scr:
        acc[...] = jnp.zeros(acc.shape, F32)

    def step(ks, tk, maxes, masked):
        scores = [_dot_nt(k_ref[pl.ds(ks, tk), h * LANES:(h + 1) * LANES],
                          q_ref[:, h * LANES:(h + 1) * LANES]) for h in range(MLA_HPS)]
        new_maxes, probs, alphas = [], [], []
        for h in range(MLA_HPS):
            s = scores[h]
            if masked:
                key = ks + lax.broadcasted_iota(jnp.int32, s.shape, 0)
                qry = qi * tq + lax.broadcasted_iota(jnp.int32, s.shape, 1)
                s = jnp.where(key <= qry, s, NEG_INF)
            m_new = jnp.maximum(maxes[h], jnp.max(s, axis=0, keepdims=True))
            alphas.append(jnp.exp2(maxes[h] - m_new))
            probs.append(jnp.exp2(s - m_new).astype(BF16))
            new_maxes.append(m_new)
        for h in range(MLA_HPS):
            pv = _dot(vt_ref[h * VT_ROWS:(h + 1) * VT_ROWS, pl.ds(ks, tk)], probs[h])
            acc_scr[h][...] = acc_scr[h][...] * alphas[h] + pv
        return tuple(new_maxes)

    init = tuple(jnp.full((1, tq), NEG_INF, F32) for _ in range(MLA_HPS))
    big = MLA_TK
    n_big = (qi * tq) // big
    maxes = lax.fori_loop(
        0, n_big, lambda kc, mx: step(pl.multiple_of(kc * big, big), big, mx, False), init)
    rest = pl.multiple_of(n_big * big, tq)
    maxes = lax.cond(
        rest < qi * tq,
        lambda mx: step(rest, 2 * tq, mx, True),
        lambda mx: step(rest, tq, mx, True),
        maxes)
    for h2 in range(MLA_HPS // 2):
        out_t = jnp.concatenate(
            [acc_scr[2 * h2 + g][0:MLA_V, :] / acc_scr[2 * h2 + g][MLA_V:MLA_V + 1, :] for g in range(2)],
            axis=0)
        o_ref[:, h2 * LANES:(h2 + 1) * LANES] = out_t.T.astype(BF16)


def _mla(q, k, vt, batch, seq):
    T = q.shape[0]
    tq = MLA_TQ
    nq = seq // tq
    hps = MLA_HPS
    return pl.pallas_call(
        _mla_kernel,
        grid=(batch, MLA_HEADS // hps, nq),
        in_specs=[pl.BlockSpec((tq, hps * LANES), lambda b, j, i: (b * nq + i, j)),
                  pl.BlockSpec((seq, hps * LANES), lambda b, j, i: (b, j)),
                  pl.BlockSpec((hps * VT_ROWS, seq), lambda b, j, i: (j, b))],
        out_specs=pl.BlockSpec((tq, hps * MLA_V), lambda b, j, i: (b * nq + i, j)),
        out_shape=jax.ShapeDtypeStruct((T, MLA_HEADS * MLA_V), BF16),
        scratch_shapes=[pltpu.VMEM((VT_ROWS, tq), F32) for _ in range(hps)],
        compiler_params=pltpu.CompilerParams(
            dimension_semantics=("parallel", "parallel", "arbitrary"), vmem_limit_bytes=VMEM_LIMIT),
        name="mla",
    )(q, k, vt)


def _swa_kernel(sink_ref, q_ref, kc_ref, kp_ref, vtc_ref, vtp_ref, bias_ref, o_ref):
    scores = []
    for head in range(SWA_HEADS):
        pair, g, kvh = head // 2, head % 2, head // SWA_GROUP
        col = (2 * kvh + g) * LANES
        band = jnp.concatenate([kp_ref[:, col:col + LANES], kc_ref[:, col:col + LANES]], axis=0)
        scores.append(_dot_nt(band, q_ref[:, pair * LANES:(pair + 1) * LANES]))
    probs, sink_terms = [], []
    for head in range(SWA_HEADS):
        s = scores[head] + bias_ref[0, head]
        sink = sink_ref[head] * LOG2E
        m = jnp.maximum(jnp.max(s, axis=0, keepdims=True), sink)
        probs.append(jnp.exp2(s - m).astype(BF16))
        sink_terms.append(jnp.exp2(sink - m))
    outs = []
    for head in range(SWA_HEADS):
        rows = slice((head // SWA_GROUP) * VT_ROWS, (head // SWA_GROUP + 1) * VT_ROWS)
        v_band = jnp.concatenate([vtp_ref[rows, :], vtc_ref[rows, :]], axis=1)
        pv = _dot(v_band, probs[head])
        outs.append(pv[0:SWA_HEAD_DIM, :] / (pv[SWA_HEAD_DIM:SWA_HEAD_DIM + 1, :] + sink_terms[head]))
    o_ref[...] = jnp.concatenate(outs, axis=0).T.astype(BF16)


def _swa_bias():
    W = SWA_WINDOW
    j = jnp.arange(W + SWA_TQ, dtype=jnp.int32)[:, None]
    i = jnp.arange(SWA_TQ, dtype=jnp.int32)[None, :]
    dist = i + W - j
    valid = (dist >= 0) & (dist < W)
    slopes = 2.0 ** (-8.0 * jnp.arange(1, SWA_HEADS + 1, dtype=F32) / SWA_HEADS)
    pen = -(slopes[:, None, None] * dist.astype(F32)[None]) * LOG2E
    general = jnp.where(valid[None], pen, NEG_INF)
    first = jnp.where((valid & (j >= W))[None], pen, NEG_INF)
    return jnp.stack([first, general])


def _swa(sinks, qs, ks, vst, batch, seq):
    T = qs.shape[0]
    W, tq = SWA_WINDOW, SWA_TQ
    nb = seq // tq
    wpb = tq // W
    before = lambda b, i: b * nb * wpb + jnp.maximum(wpb * i - 1, 0)
    cur = lambda n: pl.BlockSpec((tq, n), lambda b, i: (b * nb + i, 0))
    prev = lambda n: pl.BlockSpec((W, n), lambda b, i: (before(b, i), 0))
    vt_cur = pl.BlockSpec((SWA_KV_HEADS * VT_ROWS, tq), lambda b, i: (0, b * nb + i))
    vt_prev = pl.BlockSpec((SWA_KV_HEADS * VT_ROWS, W), lambda b, i: (0, before(b, i)))
    bias = pl.BlockSpec((1, SWA_HEADS, W + tq, tq), lambda b, i: (jnp.minimum(i, 1), 0, 0, 0))
    return pl.pallas_call(
        _swa_kernel,
        grid=(batch, nb),
        in_specs=[pl.BlockSpec(memory_space=pltpu.SMEM), cur(512), cur(512), prev(512),
                  vt_cur, vt_prev, bias],
        out_specs=cur(512),
        out_shape=jax.ShapeDtypeStruct((T, 512), BF16),
        compiler_params=pltpu.CompilerParams(dimension_semantics=("parallel", "parallel"),
                                             vmem_limit_bytes=VMEM_LIMIT),
        name="swa",
    )(sinks, qs, ks, ks, vst, vst, _swa_bias())


def _post_kernel(x_ref, om_ref, os_ref, wom_ref, wos_ref, wg_ref, wout_ref, g1_ref, b1_ref,
                 wr_ref, rb_ref, tri_ref,
                 h1_ref, h1p_ref, e_ref, w_ref, r_ref, cnt_ref, carry_scr):
    tm = x_ref.shape[0]
    step = pl.program_id(0)

    @pl.when(step == 0)
    def _():
        carry_scr[...] = jnp.zeros(carry_scr.shape, F32)

    x = x_ref[...]
    ya = _dot(om_ref[...], wom_ref[...])
    yb = _dot(os_ref[...], wos_ref[...])
    gates = _dot(x.astype(BF16), wg_ref[...])
    merged = _sigmoid(gates[:, :D_MODEL]) * ya + _sigmoid(gates[:, D_MODEL:]) * yb
    mix = _dot(merged.astype(BF16), wout_ref[...])
    z = DEEPNORM_ALPHA * x + mix
    mu = jnp.mean(z, axis=-1, keepdims=True)
    zc = z - mu
    var = jnp.mean(zc * zc, axis=-1, keepdims=True)
    h1 = zc * lax.rsqrt(var + LN_EPS) * g1_ref[...] + b1_ref[...]
    h1_ref[...] = h1
    _pack_rows(h1, h1p_ref)

    all_scores = _sigmoid(_dot_nt(wr_ref[...], h1.astype(BF16)))
    carry = carry_scr[...]
    for c in range(tm // ROUTE_TN):
        cols = slice(c * ROUTE_TN, (c + 1) * ROUTE_TN)
        idxs, weights, ranks, carry = _route(all_scores[:, cols], rb_ref[...], tri_ref[...], carry)
        e_ref[:, cols] = idxs
        w_ref[cols, :] = weights.T
        r_ref[:, cols] = ranks
    carry_scr[...] = carry
    cnt_ref[...] = carry.astype(jnp.int32)


def _route(scores, bias, tri, carry):
    tn = scores.shape[1]
    choice = scores + bias
    row = lax.broadcasted_iota(jnp.int32, (N_EXPERTS, tn), 0)
    grow = lax.broadcasted_iota(jnp.int32, (GROUP_SIZE, tn), 0)
    gscore = []
    for g in range(N_GROUPS):
        blk = choice[g * GROUP_SIZE:(g + 1) * GROUP_SIZE, :]
        m1 = jnp.max(blk, axis=0, keepdims=True)
        i1 = jnp.min(jnp.where(blk == m1, grow, GROUP_SIZE), axis=0, keepdims=True)
        m2 = jnp.max(jnp.where(grow == i1, NEG_INF, blk), axis=0, keepdims=True)
        gscore.append(m1 + m2)
    gsc = jnp.concatenate(gscore, axis=0)
    gidx = lax.broadcasted_iota(jnp.int32, (N_GROUPS, tn), 0)
    grank = jnp.zeros((N_GROUPS, tn), jnp.int32)
    for g in range(N_GROUPS):
        sg = gsc[g:g + 1, :]
        beats = (sg > gsc) | ((sg == gsc) & (gidx > g))
        grank = grank + beats.astype(jnp.int32)
    gsel = (grank < TOPK_GROUPS).astype(F32)
    emask = jnp.concatenate(
        [jnp.broadcast_to(gsel[g:g + 1, :], (GROUP_SIZE, tn)) for g in range(N_GROUPS)], axis=0)
    work = jnp.where(emask > 0.0, choice, NEG_INF)
    eligible = work
    idxs, svals = [], []
    for _k in range(TOP_K):
        m = jnp.max(work, axis=0, keepdims=True)
        idx = jnp.min(jnp.where(work == m, row, N_EXPERTS), axis=0, keepdims=True)
        hit = row == idx
        svals.append(jnp.sum(jnp.where(hit, scores, 0.0), axis=0, keepdims=True))
        work = jnp.where(hit, NEG_INF, work)
        idxs.append(idx)
    sel = jnp.where(work != eligible, 1.0, 0.0)
    ssum = svals[0]
    for sv in svals[1:]:
        ssum = ssum + sv
    weights = jnp.concatenate([sv / ssum * ROUTED_SCALE for sv in svals], axis=0)

    rank = _dot(sel.astype(BF16), tri) + carry[:, 0:1]
    ranks = jnp.concatenate(
        [jnp.sum(jnp.where(row == idx, rank, 0.0), axis=0, keepdims=True) for idx in idxs],
        axis=0).astype(jnp.int32)
    carry = carry + jnp.sum(sel, axis=1, keepdims=True)
    return jnp.concatenate(idxs, axis=0), weights, ranks, carry


def _post(x2, o_mla, o_swa, wom, wos, wg, wout, g1, b1, wr_t, rbias, tri, part):
    T = x2.shape[0] // MOE_PARTS
    tm = POST_TM
    first = part * (T // tm)
    full = lambda a: pl.BlockSpec(a.shape, lambda i: (0, 0))
    row_in = lambda n: pl.BlockSpec((tm, n), lambda i: (first + i, 0))
    row = lambda n: pl.BlockSpec((tm, n), lambda i: (i, 0))
    col = pl.BlockSpec((TOP_K, tm), lambda i: (0, i))
    return pl.pallas_call(
        _post_kernel,
        grid=(T // tm,),
        in_specs=[row_in(D_MODEL), row_in(512), row_in(512), full(wom), full(wos), full(wg), full(wout),
                  full(g1), full(b1), full(wr_t), full(rbias), full(tri)],
        out_specs=[row(D_MODEL), pl.BlockSpec((tm * ROW_SUB, LANES), lambda i: (i, 0)), col, row(TOP_K), col,
                   pl.BlockSpec((N_EXPERTS, LANES), lambda i: (0, 0))],
        out_shape=[jax.ShapeDtypeStruct((T, D_MODEL), F32),
                   jax.ShapeDtypeStruct((T * ROW_SUB, LANES), jnp.uint32),
                   jax.ShapeDtypeStruct((TOP_K, T), jnp.int32),
                   jax.ShapeDtypeStruct((T, TOP_K), F32),
                   jax.ShapeDtypeStruct((TOP_K, T), jnp.int32),
                   jax.ShapeDtypeStruct((N_EXPERTS, LANES), jnp.int32)],
        scratch_shapes=[pltpu.VMEM((N_EXPERTS, LANES), F32)],
        compiler_params=pltpu.CompilerParams(dimension_semantics=("arbitrary",),
                                             vmem_limit_bytes=VMEM_LIMIT),
        name="post",
    )(x2, o_mla, o_swa, wom, wos, wg, wout, g1, b1, wr_t, rbias, tri)


def _dest_kernel(start_ref, e_ref, r_ref, o_ref):
    e = e_ref[...]
    base = jnp.zeros(e.shape, jnp.int32)
    for j in range(N_EXPERTS):
        base = jnp.where(e == j, start_ref[j], base)
    o_ref[...] = base + r_ref[...]


def _dest(seg_start, e_t, r_t):
    T = e_t.shape[1]
    tn = min(T, 4096)
    col = pl.BlockSpec((TOP_K, tn), lambda i: (0, i))
    return pl.pallas_call(
        _dest_kernel,
        grid=(T // tn,),
        in_specs=[pl.BlockSpec(memory_space=pltpu.SMEM), col, col],
        out_specs=col,
        out_shape=jax.ShapeDtypeStruct((TOP_K, T), jnp.int32),
        compiler_params=pltpu.CompilerParams(dimension_semantics=("parallel",)),
        name="dest",
    )(seg_start, e_t, r_t)


def _sc_worker_chunks(n_tokens):
    info = plsc.get_sparse_core_info()
    n_workers = info.num_cores * info.num_subcores
    per_worker = n_tokens // SC_CHUNK // n_workers
    assert per_worker * n_workers * SC_CHUNK == n_tokens
    first = (lax.axis_index("s") * info.num_cores + lax.axis_index("c")) * per_worker
    return first, per_worker


def _sc_dispatch(h1p, idx, n_rows):
    n_tokens = h1p.shape[0] // ROW_SUB
    R = SC_CHUNK * ROW_SUB

    def body(h_hbm, idx_hbm, xs_hbm, idx_v, rows_v, sem):
        first, per_worker = _sc_worker_chunks(n_tokens)

        @pl.loop(0, per_worker)
        def _(ci):
            c = first + ci
            pltpu.sync_copy(idx_hbm.at[:, c], idx_v)
            pltpu.sync_copy(h_hbm.at[pl.ds(c * R, R)], rows_v)
            copies = [pltpu.make_async_copy(rows_v, xs_hbm.at[idx_v.at[k]], sem) for k in range(TOP_K)]
            for cp in copies:
                cp.start()
            for cp in copies:
                cp.wait()

    run = pl.kernel(
        body, out_type=jax.ShapeDtypeStruct((n_rows * ROW_SUB, LANES), jnp.uint32),
        mesh=plsc.VectorSubcoreMesh(core_axis_name="c", subcore_axis_name="s"),
        scratch_types=[pltpu.VMEM((TOP_K, R), jnp.int32), pltpu.VMEM((R, LANES), jnp.uint32),
                       pltpu.SemaphoreType.DMA],
        name="sc_dispatch")
    return run(h1p, idx)


def _sc_gather(ys, idx, n_tokens):
    R = SC_CHUNK * ROW_SUB
    half = TOP_K // 2

    def body(ys_hbm, idx_hbm, yg_hbm, idx_v, buf, gsem, wsem):
        first, per_worker = _sc_worker_chunks(n_tokens)

        @pl.loop(0, per_worker)
        def _(ci):
            c = first + ci
            pltpu.sync_copy(idx_hbm.at[:, c], idx_v)
            for k0 in (0, half):
                gathers = [pltpu.make_async_copy(ys_hbm.at[idx_v.at[k0 + k]], buf.at[k], gsem)
                           for k in range(half)]
                for cp in gathers:
                    cp.start()
                for cp in gathers:
                    cp.wait()
                writes = [pltpu.make_async_copy(buf.at[k], yg_hbm.at[k0 + k, pl.ds(c * R, R)], wsem)
                          for k in range(half)]
                for cp in writes:
                    cp.start()
                for cp in writes:
                    cp.wait()

    run = pl.kernel(
        body, out_type=jax.ShapeDtypeStruct((TOP_K, n_tokens * ROW_SUB, LANES), jnp.uint32),
        mesh=plsc.VectorSubcoreMesh(core_axis_name="c", subcore_axis_name="s"),
        scratch_types=[pltpu.VMEM((TOP_K, R), jnp.int32), pltpu.VMEM((half, R, LANES), jnp.uint32),
                       pltpu.SemaphoreType.DMA, pltpu.SemaphoreType.DMA],
        name="sc_gather")
    return run(ys, idx)


def _experts_kernel(first_ref, nblk_ref, cnt_ref, xs_hbm, wup_ref, wdn_ref, after_ref, ys_hbm,
                    wup_bf, wdn_bf, xbuf, ybuf, in_sem, out_sem):
    del after_ref
    step = pl.program_id(0)
    total = first_ref[N_EXPERTS - 1] + nblk_ref[N_EXPERTS - 1]
    rb = ROW_BLK * ROW_SUB
    depth = EXPERT_SLOTS

    def slot_of(b):
        return lax.rem(b, depth)

    def rows_of(b):
        return pl.ds(pl.multiple_of(b * rb, rb), rb)

    def in_copy(b):
        return pltpu.make_async_copy(xs_hbm.at[rows_of(b)], xbuf.at[slot_of(b)], in_sem.at[slot_of(b)])

    def out_copy(b):
        return pltpu.make_async_copy(ybuf.at[slot_of(b)], ys_hbm.at[rows_of(b)], out_sem.at[slot_of(b)])

    @pl.when(step == 0)
    def _():
        for b in range(depth - 1):
            @pl.when(b < total)
            def _():
                in_copy(b).start()

    for j in range(EXPERTS_PER_STEP):
        _run_expert(step * EXPERTS_PER_STEP + j, j, first_ref, nblk_ref, cnt_ref, wup_ref, wdn_ref,
                    wup_bf, wdn_bf, xbuf, ybuf, in_copy, out_copy, slot_of, total)

    @pl.when(step == N_EXPERTS // EXPERTS_PER_STEP - 1)
    def _():
        for back in range(depth, 0, -1):
            @pl.when(total - back >= 0)
            def _():
                out_copy(total - back).wait()


def _run_expert(e, j, first_ref, nblk_ref, cnt_ref, wup_ref, wdn_ref, wup_bf, wdn_bf, xbuf, ybuf,
                in_copy, out_copy, slot_of, total):
    n = nblk_ref[e]
    b0 = first_ref[e]
    count = cnt_ref[e]
    depth = EXPERT_SLOTS

    @pl.when(n > 0)
    def _():
        wup_bf[...] = wup_ref[j].astype(BF16)
        wdn_bf[...] = wdn_ref[j].astype(BF16)

        def body(b, carry):
            slot = slot_of(b)
            in_copy(b).wait()

            @pl.when(b + depth - 1 < total)
            def _():
                in_copy(b + depth - 1).start()

            @pl.when(b >= depth)
            def _():
                out_copy(b - depth).wait()

            sub = ROW_BLK // EXPERT_SPLIT
            n_valid = count - (b - b0) * ROW_BLK
            xbs = [jnp.concatenate(
                [c.astype(BF16) for c in _unpack_rows(xbuf.at[slot], sub, first=i * sub, n_valid=n_valid)],
                axis=1) for i in range(EXPERT_SPLIT)]
            gus = [_dot(xb, wup_bf[...]) for xb in xbs]
            hids = [(gu[:, :EXPERT_HIDDEN] * _sigmoid(gu[:, :EXPERT_HIDDEN])
                     * gu[:, EXPERT_HIDDEN:]).astype(BF16) for gu in gus]
            ys = [_dot(hid, wdn_bf[...]) for hid in hids]
            for i in range(EXPERT_SPLIT):
                _pack_rows(ys[i], ybuf.at[slot], first=i * sub)
            out_copy(b).start()
            return carry

        lax.fori_loop(b0, b0 + n, body, 0)


def _experts(first_blk, n_blk, counts, xs, w_exp_up, w_exp_down, run_after):
    rb = ROW_BLK * ROW_SUB
    eps = EXPERTS_PER_STEP
    grid_spec = pltpu.PrefetchScalarGridSpec(
        num_scalar_prefetch=3,
        grid=(N_EXPERTS // eps,),
        in_specs=[pl.BlockSpec(memory_space=pl.ANY),
                  pl.BlockSpec((eps, D_MODEL, 2 * EXPERT_HIDDEN), lambda s, fb, nb, ct: (s, 0, 0)),
                  pl.BlockSpec((eps, EXPERT_HIDDEN, D_MODEL), lambda s, fb, nb, ct: (s, 0, 0)),
                  pl.BlockSpec(memory_space=pl.ANY)],
        out_specs=pl.BlockSpec(memory_space=pl.ANY),
        scratch_shapes=[pltpu.VMEM((D_MODEL, 2 * EXPERT_HIDDEN), BF16),
                        pltpu.VMEM((EXPERT_HIDDEN, D_MODEL), BF16),
                        pltpu.VMEM((EXPERT_SLOTS, rb, LANES), jnp.uint32),
                        pltpu.VMEM((EXPERT_SLOTS, rb, LANES), jnp.uint32),
                        pltpu.SemaphoreType.DMA((EXPERT_SLOTS,)),
                        pltpu.SemaphoreType.DMA((EXPERT_SLOTS,))],
    )
    return pl.pallas_call(
        _experts_kernel,
        grid_spec=grid_spec,
        out_shape=jax.ShapeDtypeStruct(xs.shape, jnp.uint32),
        compiler_params=pltpu.CompilerParams(dimension_semantics=("arbitrary",),
                                             vmem_limit_bytes=VMEM_LIMIT),
        name="experts",
    )(first_blk, n_blk, counts, xs, w_exp_up, w_exp_down, run_after)


def _shared_kernel(h1p_ref, wsu_ref, wsd_ref, o_ref):
    tm = o_ref.shape[0]
    hb = jnp.concatenate([c.astype(BF16) for c in _unpack_rows(h1p_ref, tm)], axis=1)
    gu = _dot(hb, wsu_ref[...])
    g = gu[:, :SHARED_HIDDEN]
    hid = g * _sigmoid(g) * gu[:, SHARED_HIDDEN:]
    o_ref[...] = _dot(hid.astype(BF16), wsd_ref[...]).astype(BF16)


def _shared(h1p, wsu, wsd):
    T = h1p.shape[0] // ROW_SUB
    tm = SHARED_TM
    full = lambda a: pl.BlockSpec(a.shape, lambda i: (0, 0))
    row = pl.BlockSpec((tm, D_MODEL), lambda i: (i, 0))
    return pl.pallas_call(
        _shared_kernel,
        grid=(T // tm,),
        in_specs=[pl.BlockSpec((tm * ROW_SUB, LANES), lambda i: (i, 0)), full(wsu), full(wsd)],
        out_specs=row,
        out_shape=jax.ShapeDtypeStruct((T, D_MODEL), BF16),
        compiler_params=pltpu.CompilerParams(dimension_semantics=("parallel",),
                                             vmem_limit_bytes=VMEM_LIMIT),
        name="shared",
    )(h1p, wsu, wsd)


def _combine_kernel(w_ref, h1_ref, sh_ref, yg_ref, g2_ref, b2_ref, *rest):
    o_ref = rest[-1]
    tm = h1_ref.shape[0]
    h1 = h1_ref[...]
    ffn = sh_ref[...].astype(F32)

    w = w_ref[...]
    acc = [None] * (2 * ROW_SUB)
    for k in range(TOP_K):
        wk = w[:, k:k + 1]
        chunks = _unpack_rows(yg_ref.at[k], tm)
        for c in range(2 * ROW_SUB):
            acc[c] = wk * chunks[c] if acc[c] is None else acc[c] + wk * chunks[c]
    routed = jnp.concatenate(acc, axis=1)
    z = DEEPNORM_ALPHA * h1 + (routed + ffn)
    mu = jnp.mean(z, axis=-1, keepdims=True)
    zc = z - mu
    var = jnp.mean(zc * zc, axis=-1, keepdims=True)
    o_ref[...] = zc * lax.rsqrt(var + LN_EPS) * g2_ref[...] + b2_ref[...]


def _combine(w_tok, h1, sh, yg, g2, b2, in_first, out_first, out_rows, out_prev):
    n_tok = yg.shape[1] // ROW_SUB
    tm = COMBINE_TM
    fi, fo = in_first // tm, out_first // tm
    full = lambda a: pl.BlockSpec(a.shape, lambda i: (0, 0))
    row = pl.BlockSpec((tm, D_MODEL), lambda i: (fi + i, 0))
    in_specs = [pl.BlockSpec((tm, TOP_K), lambda i: (fi + i, 0)), row, row,
                pl.BlockSpec((TOP_K, tm * ROW_SUB, LANES), lambda i: (0, i, 0)),
                full(g2), full(b2)]
    args = [w_tok, h1, sh, yg, g2, b2]
    aliases = {}
    if out_prev is not None:
        in_specs.append(pl.BlockSpec(memory_space=pl.ANY))
        args.append(out_prev)
        aliases = {len(args) - 1: 0}
    return pl.pallas_call(
        _combine_kernel,
        grid=(n_tok // tm,),
        in_specs=in_specs,
        out_specs=pl.BlockSpec((tm, D_MODEL), lambda i: (fo + i, 0)),
        out_shape=jax.ShapeDtypeStruct((out_rows, D_MODEL), F32),
        input_output_aliases=aliases,
        compiler_params=pltpu.CompilerParams(dimension_semantics=("parallel",),
                                             vmem_limit_bytes=VMEM_LIMIT),
        name="combine",
    )(*args)


def _prep_weights(w_in, w_uq, w_ukv, seq):
    z = lambda n: jnp.zeros((D_MODEL, n), F32)
    kr = w_in[:, 512:544]
    qs = w_in[:, 544:1056] * (LOG2E / math.sqrt(SWA_HEAD_DIM))
    ks0, ks1 = w_in[:, 1056:1120], w_in[:, 1120:1184]
    half = MLA_ROPE // 2
    w1 = jnp.concatenate([
        w_in[:, 0:512], qs,
        ks0, ks1, ks1, ks0,
        z(64), kr, z(32),
        z(64), -kr[:, half:], kr[:, :half], z(32)], axis=1).astype(BF16)
    wvst = w_in[:, 1184:1312].T.astype(BF16)
    wg = w_in[:, 1312:3360].astype(BF16)

    pad3 = lambda a, lo, hi: jnp.pad(a, ((0, 0), (0, 0), (lo, hi)))
    wq3 = w_uq.reshape(MLA_Q_LORA, MLA_HEADS, MLA_NOPE + MLA_ROPE)
    rope = wq3[:, :, MLA_NOPE:]
    rot = jnp.concatenate([-rope[:, :, half:], rope[:, :, :half]], axis=2)
    wq2 = jnp.concatenate([pad3(wq3, 0, 32).reshape(MLA_Q_LORA, -1),
                           pad3(rot, MLA_NOPE, 32).reshape(MLA_Q_LORA, -1)], axis=1).astype(BF16)
    wkv3 = w_ukv.reshape(MLA_KV_LORA, MLA_HEADS, MLA_NOPE + MLA_V)
    wk = pad3(wkv3[:, :, :MLA_NOPE], 0, 64).reshape(MLA_KV_LORA, -1).astype(BF16)
    wvt = wkv3[:, :, MLA_NOPE:].reshape(MLA_KV_LORA, -1).T.astype(BF16)

    inv_freq = ROPE_THETA ** (-jnp.arange(0, MLA_ROPE, 2, dtype=F32) / MLA_ROPE)
    ang = jnp.arange(seq, dtype=F32)[:, None] * inv_freq[None, :]
    cos, sin = jnp.cos(ang), jnp.sin(ang)
    one, zero = jnp.ones((seq, 64), F32), jnp.zeros((seq, 64), F32)
    z32 = jnp.zeros((seq, 32), F32)
    scale = LOG2E / math.sqrt(MLA_NOPE + MLA_ROPE)
    tabs = (jnp.concatenate([one, cos, cos, z32], axis=1) * scale,
            jnp.concatenate([zero, sin, sin, z32], axis=1) * scale,
            jnp.concatenate([zero, cos, cos, z32], axis=1),
            jnp.concatenate([zero, sin, sin, z32], axis=1))
    return w1, wvst, wg, wq2, wk, wvt, tabs


def kernel(x, w_in, mla_q_norm, mla_kv_norm, w_uq, w_ukv, attn_sinks, w_o_mla, w_o_swa, w_out,
           ln1_g, ln1_b, w_router, router_bias, w_exp_up, w_exp_down, w_sh_up, w_sh_down,
           ln2_g, ln2_b):
    batch, seq, _ = x.shape
    T = batch * seq
    x2 = x.reshape(T, D_MODEL)
    w1, wvst, wg, wq2, wk, wvt, tabs = _prep_weights(w_in, w_uq, w_ukv, seq)
    q, k, vt, qs, ks, vst = _proj(x2, w1, wvst, wq2, wk, wvt, mla_q_norm.reshape(1, -1),
                                  mla_kv_norm.reshape(1, -1), tabs, seq)
    o_mla = _mla(q, k, vt, batch, seq)
    o_swa = _swa(attn_sinks.astype(F32), qs, ks, vst, batch, seq)

    tri = (lax.broadcasted_iota(jnp.int32, (ROUTE_TN, ROUTE_TN), 0)
           < lax.broadcasted_iota(jnp.int32, (ROUTE_TN, ROUTE_TN), 1)).astype(BF16)
    post_weights = (w_o_mla.astype(BF16), w_o_swa.astype(BF16), wg, w_out.astype(BF16),
                    ln1_g.reshape(1, -1), ln1_b.reshape(1, -1), w_router.T.astype(BF16),
                    router_bias.reshape(-1, 1).astype(F32), tri)
    wsu, wsd = w_sh_up.astype(BF16), w_sh_down.astype(BF16)
    g2, b2 = ln2_g.reshape(1, -1), ln2_b.reshape(1, -1)

    Tp = T // MOE_PARTS
    n_rows = (Tp * TOP_K // ROW_BLK + N_EXPERTS) * ROW_BLK
    out = None
    for part in range(MOE_PARTS):
        h1, h1p, e_t, w_tok, r_t, cnt = _post(x2, o_mla, o_swa, *post_weights, part)
        counts = cnt[:, 0]
        pad = (counts + ROW_BLK - 1) // ROW_BLK * ROW_BLK
        pad_start = jnp.cumsum(pad) - pad
        dest = _dest(pad_start.astype(jnp.int32), e_t, r_t)
        idx = (dest.reshape(TOP_K, Tp // SC_CHUNK, SC_CHUNK, 1) * ROW_SUB
               + jnp.arange(ROW_SUB, dtype=jnp.int32)).reshape(TOP_K, Tp // SC_CHUNK, SC_CHUNK * ROW_SUB)
        xs = _sc_dispatch(h1p, idx, n_rows)
        sh = _shared(h1p, wsu, wsd)
        ys = _experts((pad_start // ROW_BLK).astype(jnp.int32), (pad // ROW_BLK).astype(jnp.int32),
                      counts.astype(jnp.int32), xs, w_exp_up, w_exp_down, sh)
        Tc = Tp // COMBINE_PARTS
        n_chunks = Tc // SC_CHUNK
        for cpart in range(COMBINE_PARTS):
            yg = _sc_gather(ys, idx[:, cpart * n_chunks:(cpart + 1) * n_chunks], Tc)
            out = _combine(w_tok, h1, sh, yg, g2, b2, cpart * Tc, part * Tp + cpart * Tc, T, out)
    return out.reshape(batch, seq, D_MODEL)
```

```python
import math

import jax
import jax.numpy as jnp
from jax import lax
from jax.experimental import pallas as pl
from jax.experimental.pallas import tpu as pltpu
from jax.experimental.pallas import tpu_sc as plsc

D_MODEL = 1024
MLA_HEADS = 8
MLA_Q_LORA = 256
MLA_KV_LORA = 256
MLA_NOPE = 64
MLA_ROPE = 32
MLA_V = 64
VT_ROWS = 80
ROPE_THETA = 10000.0
SWA_HEADS = 8
SWA_KV_HEADS = 2
SWA_GROUP = SWA_HEADS // SWA_KV_HEADS
SWA_HEAD_DIM = 64
SWA_WINDOW = 128
SWA_TQ = 256
N_EXPERTS = 256
TOP_K = 8
N_GROUPS = 8
GROUP_SIZE = N_EXPERTS // N_GROUPS
TOPK_GROUPS = 4
EXPERT_HIDDEN = 256
SHARED_HIDDEN = 256
ROUTED_SCALE = 2.5
DEEPNORM_ALPHA = 2.0 ** 0.25
LN_EPS = 1e-5
RMS_EPS = 1e-6

LANES = 128
ROW_WORDS = D_MODEL // 2
ROW_SUB = ROW_WORDS // LANES
VMEM_LIMIT = 48 * 1024 * 1024

PROJ_TM = 512
MLA_TQ = 256
MLA_TK = 512
MLA_HPS = 8
POST_TM = 512
ROUTE_TN = 256
ROW_BLK = 256
EXPERT_SLOTS = 4
EXPERT_SPLIT = 2
SHARED_TM = 512
COMBINE_TM = 256
SC_CHUNK = 32
COMBINE_PARTS = 4

BF16 = jnp.bfloat16
F32 = jnp.float32
NEG_INF = float("-inf")
LOG2E = math.log2(math.e)


def _sigmoid(v):
    return 1.0 / (1.0 + jnp.exp(-v))


def _dot(a, b):
    return jnp.dot(a, b, preferred_element_type=F32)


def _dot_nt(a, b):
    return lax.dot_general(a, b, (((1,), (1,)), ((), ())), preferred_element_type=F32)


def _pack_rows(y, out_ref, first=0):
    rows = y.shape[0]
    for j in range(ROW_SUB):
        a = y[:, j * LANES:(j + 1) * LANES].astype(BF16).astype(F32)
        b = y[:, ROW_WORDS + j * LANES:ROW_WORDS + (j + 1) * LANES].astype(BF16).astype(F32)
        ua = pltpu.bitcast(a, jnp.uint32) >> 16
        ub = pltpu.bitcast(b, jnp.uint32)
        out_ref[pl.ds(first * ROW_SUB + j, rows, stride=ROW_SUB), :] = ua | ub


def _unpack_rows(ref, rows, first=0, n_valid=None):
    lo, hi = [], []
    if n_valid is not None:
        live = first + lax.broadcasted_iota(jnp.int32, (rows, LANES), 0) < n_valid
    for j in range(ROW_SUB):
        u = ref[pl.ds(first * ROW_SUB + j, rows, stride=ROW_SUB), :]
        if n_valid is not None:
            u = jnp.where(live, u, jnp.uint32(0))
        lo.append(pltpu.bitcast(u << 16, F32))
        hi.append(pltpu.bitcast(u & jnp.uint32(0xFFFF0000), F32))
    return lo + hi


def _proj_kernel(x_ref, w1_ref, wvst_ref, wq_ref, wk_ref, wvt_ref, gq_ref, gkv_ref,
                 cq_ref, sq_ref, ck_ref, sk_ref,
                 q_ref, k_ref, vt_ref, qs_ref, ks_ref, vst_ref):
    xb = x_ref[...].astype(BF16)
    p = _dot(xb, w1_ref[...])

    def rms(c, g):
        return c * lax.rsqrt(jnp.mean(c * c, axis=-1, keepdims=True) + RMS_EPS) * g

    cqn = rms(p[:, 0:256], gq_ref[...]).astype(BF16)
    ckvn = rms(p[:, 256:512], gkv_ref[...]).astype(BF16)
    qs_ref[...] = p[:, 512:1024].astype(BF16)
    k01 = p[:, 1024:1152].astype(BF16)
    k10 = p[:, 1152:1280].astype(BF16)
    lane_lo = lax.broadcasted_iota(jnp.int32, k01.shape, 1) < SWA_HEAD_DIM
    zero = jnp.zeros_like(k01)
    ks_ref[:, 0 * LANES:1 * LANES] = jnp.where(lane_lo, k01, zero)
    ks_ref[:, 1 * LANES:2 * LANES] = jnp.where(lane_lo, zero, k10)
    ks_ref[:, 2 * LANES:3 * LANES] = jnp.where(lane_lo, k10, zero)
    ks_ref[:, 3 * LANES:4 * LANES] = jnp.where(lane_lo, zero, k01)
    ones = jnp.ones((VT_ROWS - MLA_V, xb.shape[0]), BF16)
    vst = _dot_nt(wvst_ref[...], xb).astype(BF16)
    for h in range(SWA_KV_HEADS):
        vst_ref[h * VT_ROWS:h * VT_ROWS + SWA_HEAD_DIM, :] = vst[h * SWA_HEAD_DIM:(h + 1) * SWA_HEAD_DIM, :]
        vst_ref[h * VT_ROWS + SWA_HEAD_DIM:(h + 1) * VT_ROWS, :] = ones
    kr = p[:, 1280:1408] * ck_ref[...] + p[:, 1408:1536] * sk_ref[...]
    qq = _dot(cqn, wq_ref[...])
    kn = _dot(ckvn, wk_ref[...])
    cq = cq_ref[...]
    sq = sq_ref[...]
    for h in range(MLA_HEADS):
        a = qq[:, h * LANES:(h + 1) * LANES]
        b = qq[:, 1024 + h * LANES:1024 + (h + 1) * LANES]
        q_ref[:, h * LANES:(h + 1) * LANES] = (a * cq + b * sq).astype(BF16)
        k_ref[:, h * LANES:(h + 1) * LANES] = (kn[:, h * LANES:(h + 1) * LANES] + kr).astype(BF16)
    vt = _dot_nt(wvt_ref[...], ckvn).astype(BF16)
    for h in range(MLA_HEADS):
        vt_ref[h * VT_ROWS:h * VT_ROWS + MLA_V, :] = vt[h * MLA_V:(h + 1) * MLA_V, :]
        vt_ref[h * VT_ROWS + MLA_V:(h + 1) * VT_ROWS, :] = ones


def _proj(x2, w1, wvst, wq2, wk, wvt, gq, gkv, tabs, seq):
    T = x2.shape[0]
    tm = PROJ_TM
    nper = seq // tm
    full = lambda shape: pl.BlockSpec(shape, lambda i: (0, 0))
    tab = pl.BlockSpec((tm, LANES), lambda i: (i % nper, 0))
    row = lambda n: pl.BlockSpec((tm, n), lambda i: (i, 0))
    col = lambda n: pl.BlockSpec((n, tm), lambda i: (0, i))
    return pl.pallas_call(
        _proj_kernel,
        grid=(T // tm,),
        in_specs=[row(D_MODEL), full(w1.shape), full(wvst.shape), full(wq2.shape), full(wk.shape),
                  full(wvt.shape), full(gq.shape), full(gkv.shape), tab, tab, tab, tab],
        out_specs=[row(1024), row(1024), col(MLA_HEADS * VT_ROWS), row(512), row(512),
                   col(SWA_KV_HEADS * VT_ROWS)],
        out_shape=[jax.ShapeDtypeStruct((T, 1024), BF16), jax.ShapeDtypeStruct((T, 1024), BF16),
                   jax.ShapeDtypeStruct((MLA_HEADS * VT_ROWS, T), BF16), jax.ShapeDtypeStruct((T, 512), BF16),
                   jax.ShapeDtypeStruct((T, 512), BF16),
                   jax.ShapeDtypeStruct((SWA_KV_HEADS * VT_ROWS, T), BF16)],
        compiler_params=pltpu.CompilerParams(dimension_semantics=("parallel",),
                                             vmem_limit_bytes=VMEM_LIMIT),
        name="proj",
    )(x2, w1, wvst, wq2, wk, wvt, gq, gkv, *tabs)


def _mla_kernel(q_ref, k_ref, vt_ref, o_ref, *acc_scr):
    tq = q_ref.shape[0]
    qi = pl.program_id(2)
    for acc in acc_scr:
        acc[...] = jnp.zeros(acc.shape, F32)

    def step(ks, tk, maxes, masked):
        scores = [_dot_nt(k_ref[pl.ds(ks, tk), h * LANES:(h + 1) * LANES],
                          q_ref[:, h * LANES:(h + 1) * LANES]) for h in range(MLA_HPS)]
        new_maxes, probs, alphas = [], [], []
        for h in range(MLA_HPS):
            s = scores[h]
            if masked:
                key = ks + lax.broadcasted_iota(jnp.int32, s.shape, 0)
                qry = qi * tq + lax.broadcasted_iota(jnp.int32, s.shape, 1)
                s = jnp.where(key <= qry, s, NEG_INF)
            m_new = jnp.maximum(maxes[h], jnp.max(s, axis=0, keepdims=True))
            alphas.append(jnp.exp2(maxes[h] - m_new))
            probs.append(jnp.exp2(s - m_new).astype(BF16))
            new_maxes.append(m_new)
        for h in range(MLA_HPS):
            pv = _dot(vt_ref[h * VT_ROWS:(h + 1) * VT_ROWS, pl.ds(ks, tk)], probs[h])
            acc_scr[h][...] = acc_scr[h][...] * alphas[h] + pv
        return tuple(new_maxes)

    init = tuple(jnp.full((1, tq), NEG_INF, F32) for _ in range(MLA_HPS))
    big = MLA_TK
    n_big = (qi * tq) // big
    maxes = lax.fori_loop(
        0, n_big, lambda kc, mx: step(pl.multiple_of(kc * big, big), big, mx, False), init)
    rest = pl.multiple_of(n_big * big, tq)
    maxes = lax.cond(
        rest < qi * tq,
        lambda mx: step(rest, 2 * tq, mx, True),
        lambda mx: step(rest, tq, mx, True),
        maxes)
    for h2 in range(MLA_HPS // 2):
        out_t = jnp.concatenate(
            [acc_scr[2 * h2 + g][0:MLA_V, :] / acc_scr[2 * h2 + g][MLA_V:MLA_V + 1, :] for g in range(2)],
            axis=0)
        o_ref[:, h2 * LANES:(h2 + 1) * LANES] = out_t.T.astype(BF16)


def _mla(q, k, vt, batch, seq):
    T = q.shape[0]
    tq = MLA_TQ
    nq = seq // tq
    hps = MLA_HPS
    return pl.pallas_call(
        _mla_kernel,
        grid=(batch, MLA_HEADS // hps, nq),
        in_specs=[pl.BlockSpec((tq, hps * LANES), lambda b, j, i: (b * nq + i, j)),
                  pl.BlockSpec((seq, hps * LANES), lambda b, j, i: (b, j)),
                  pl.BlockSpec((hps * VT_ROWS, seq), lambda b, j, i: (j, b))],
        out_specs=pl.BlockSpec((tq, hps * MLA_V), lambda b, j, i: (b * nq + i, j)),
        out_shape=jax.ShapeDtypeStruct((T, MLA_HEADS * MLA_V), BF16),
        scratch_shapes=[pltpu.VMEM((VT_ROWS, tq), F32) for _ in range(hps)],
        compiler_params=pltpu.CompilerParams(
            dimension_semantics=("parallel", "parallel", "arbitrary"), vmem_limit_bytes=VMEM_LIMIT),
        name="mla",
    )(q, k, vt)


def _swa_kernel(sink_ref, q_ref, kc_ref, kp_ref, vtc_ref, vtp_ref, bias_ref, o_ref):
    scores = []
    for head in range(SWA_HEADS):
        pair, g, kvh = head // 2, head % 2, head // SWA_GROUP
        col = (2 * kvh + g) * LANES
        band = jnp.concatenate([kp_ref[:, col:col + LANES], kc_ref[:, col:col + LANES]], axis=0)
        scores.append(_dot_nt(band, q_ref[:, pair * LANES:(pair + 1) * LANES]))
    probs, sink_terms = [], []
    for head in range(SWA_HEADS):
        s = scores[head] + bias_ref[0, head]
        sink = sink_ref[head] * LOG2E
        m = jnp.maximum(jnp.max(s, axis=0, keepdims=True), sink)
        probs.append(jnp.exp2(s - m).astype(BF16))
        sink_terms.append(jnp.exp2(sink - m))
    outs = []
    for head in range(SWA_HEADS):
        rows = slice((head // SWA_GROUP) * VT_ROWS, (head // SWA_GROUP + 1) * VT_ROWS)
        v_band = jnp.concatenate([vtp_ref[rows, :], vtc_ref[rows, :]], axis=1)
        pv = _dot(v_band, probs[head])
        outs.append(pv[0:SWA_HEAD_DIM, :] / (pv[SWA_HEAD_DIM:SWA_HEAD_DIM + 1, :] + sink_terms[head]))
    o_ref[...] = jnp.concatenate(outs, axis=0).T.astype(BF16)


def _swa_bias():
    W = SWA_WINDOW
    j = jnp.arange(W + SWA_TQ, dtype=jnp.int32)[:, None]
    i = jnp.arange(SWA_TQ, dtype=jnp.int32)[None, :]
    dist = i + W - j
    valid = (dist >= 0) & (dist < W)
    slopes = 2.0 ** (-8.0 * jnp.arange(1, SWA_HEADS + 1, dtype=F32) / SWA_HEADS)
    pen = -(slopes[:, None, None] * dist.astype(F32)[None]) * LOG2E
    general = jnp.where(valid[None], pen, NEG_INF)
    first = jnp.where((valid & (j >= W))[None], pen, NEG_INF)
    return jnp.stack([first, general])


def _swa(sinks, qs, ks, vst, batch, seq):
    T = qs.shape[0]
    W, tq = SWA_WINDOW, SWA_TQ
    nb = seq // tq
    wpb = tq // W
    before = lambda b, i: b * nb * wpb + jnp.maximum(wpb * i - 1, 0)
    cur = lambda n: pl.BlockSpec((tq, n), lambda b, i: (b * nb + i, 0))
    prev = lambda n: pl.BlockSpec((W, n), lambda b, i: (before(b, i), 0))
    vt_cur = pl.BlockSpec((SWA_KV_HEADS * VT_ROWS, tq), lambda b, i: (0, b * nb + i))
    vt_prev = pl.BlockSpec((SWA_KV_HEADS * VT_ROWS, W), lambda b, i: (0, before(b, i)))
    bias = pl.BlockSpec((1, SWA_HEADS, W + tq, tq), lambda b, i: (jnp.minimum(i, 1), 0, 0, 0))
    return pl.pallas_call(
        _swa_kernel,
        grid=(batch, nb),
        in_specs=[pl.BlockSpec(memory_space=pltpu.SMEM), cur(512), cur(512), prev(512),
                  vt_cur, vt_prev, bias],
        out_specs=cur(512),
        out_shape=jax.ShapeDtypeStruct((T, 512), BF16),
        compiler_params=pltpu.CompilerParams(dimension_semantics=("parallel", "parallel"),
                                             vmem_limit_bytes=VMEM_LIMIT),
        name="swa",
    )(sinks, qs, ks, ks, vst, vst, _swa_bias())


def _post_kernel(x_ref, om_ref, os_ref, wom_ref, wos_ref, wg_ref, wout_ref, g1_ref, b1_ref,
                 wr_ref, rb_ref, tri_ref,
                 h1_ref, h1p_ref, e_ref, w_ref, r_ref, cnt_ref, carry_scr):
    tm = x_ref.shape[0]
    step = pl.program_id(0)

    @pl.when(step == 0)
    def _():
        carry_scr[...] = jnp.zeros(carry_scr.shape, F32)

    x = x_ref[...]
    ya = _dot(om_ref[...], wom_ref[...])
    yb = _dot(os_ref[...], wos_ref[...])
    gates = _dot(x.astype(BF16), wg_ref[...])
    merged = _sigmoid(gates[:, :D_MODEL]) * ya + _sigmoid(gates[:, D_MODEL:]) * yb
    mix = _dot(merged.astype(BF16), wout_ref[...])
    z = DEEPNORM_ALPHA * x + mix
    mu = jnp.mean(z, axis=-1, keepdims=True)
    zc = z - mu
    var = jnp.mean(zc * zc, axis=-1, keepdims=True)
    h1 = zc * lax.rsqrt(var + LN_EPS) * g1_ref[...] + b1_ref[...]
    h1_ref[...] = h1
    _pack_rows(h1, h1p_ref)

    all_scores = _sigmoid(_dot_nt(wr_ref[...], h1.astype(BF16)))
    carry = carry_scr[...]
    for c in range(tm // ROUTE_TN):
        cols = slice(c * ROUTE_TN, (c + 1) * ROUTE_TN)
        idxs, weights, ranks, carry = _route(all_scores[:, cols], rb_ref[...], tri_ref[...], carry)
        e_ref[:, cols] = idxs
        w_ref[:, cols] = weights
        r_ref[:, cols] = ranks
    carry_scr[...] = carry
    cnt_ref[...] = carry.astype(jnp.int32)


def _route(scores, bias, tri, carry):
    tn = scores.shape[1]
    choice = scores + bias
    row = lax.broadcasted_iota(jnp.int32, (N_EXPERTS, tn), 0)
    grow = lax.broadcasted_iota(jnp.int32, (GROUP_SIZE, tn), 0)
    gscore = []
    for g in range(N_GROUPS):
        blk = choice[g * GROUP_SIZE:(g + 1) * GROUP_SIZE, :]
        m1 = jnp.max(blk, axis=0, keepdims=True)
        i1 = jnp.min(jnp.where(blk == m1, grow, GROUP_SIZE), axis=0, keepdims=True)
        m2 = jnp.max(jnp.where(grow == i1, NEG_INF, blk), axis=0, keepdims=True)
        gscore.append(m1 + m2)
    gsc = jnp.concatenate(gscore, axis=0)
    gidx = lax.broadcasted_iota(jnp.int32, (N_GROUPS, tn), 0)
    grank = jnp.zeros((N_GROUPS, tn), jnp.int32)
    for g in range(N_GROUPS):
        sg = gsc[g:g + 1, :]
        beats = (sg > gsc) | ((sg == gsc) & (gidx > g))
        grank = grank + beats.astype(jnp.int32)
    gsel = (grank < TOPK_GROUPS).astype(F32)
    emask = jnp.concatenate(
        [jnp.broadcast_to(gsel[g:g + 1, :], (GROUP_SIZE, tn)) for g in range(N_GROUPS)], axis=0)
    work = jnp.where(emask > 0.0, choice, NEG_INF)
    eligible = work
    idxs, svals = [], []
    for _k in range(TOP_K):
        m = jnp.max(work, axis=0, keepdims=True)
        idx = jnp.min(jnp.where(work == m, row, N_EXPERTS), axis=0, keepdims=True)
        hit = row == idx
        svals.append(jnp.sum(jnp.where(hit, scores, 0.0), axis=0, keepdims=True))
        work = jnp.where(hit, NEG_INF, work)
        idxs.append(idx)
    sel = jnp.where(work != eligible, 1.0, 0.0)
    ssum = svals[0]
    for sv in svals[1:]:
        ssum = ssum + sv
    weights = jnp.concatenate([sv / ssum * ROUTED_SCALE for sv in svals], axis=0)

    rank = _dot(sel.astype(BF16), tri) + carry[:, 0:1]
    ranks = jnp.concatenate(
        [jnp.sum(jnp.where(row == idx, rank, 0.0), axis=0, keepdims=True) for idx in idxs],
        axis=0).astype(jnp.int32)
    carry = carry + jnp.sum(sel, axis=1, keepdims=True)
    return jnp.concatenate(idxs, axis=0), weights, ranks, carry


def _post(x2, o_mla, o_swa, wom, wos, wg, wout, g1, b1, wr_t, rbias, tri):
    T = x2.shape[0]
    tm = POST_TM
    full = lambda a: pl.BlockSpec(a.shape, lambda i: (0, 0))
    row = lambda n: pl.BlockSpec((tm, n), lambda i: (i, 0))
    col = pl.BlockSpec((TOP_K, tm), lambda i: (0, i))
    return pl.pallas_call(
        _post_kernel,
        grid=(T // tm,),
        in_specs=[row(D_MODEL), row(512), row(512), full(wom), full(wos), full(wg), full(wout),
                  full(g1), full(b1), full(wr_t), full(rbias), full(tri)],
        out_specs=[row(D_MODEL), pl.BlockSpec((tm * ROW_SUB, LANES), lambda i: (i, 0)), col, col, col,
                   pl.BlockSpec((N_EXPERTS, LANES), lambda i: (0, 0))],
        out_shape=[jax.ShapeDtypeStruct((T, D_MODEL), F32),
                   jax.ShapeDtypeStruct((T * ROW_SUB, LANES), jnp.uint32),
                   jax.ShapeDtypeStruct((TOP_K, T), jnp.int32),
                   jax.ShapeDtypeStruct((TOP_K, T), F32),
                   jax.ShapeDtypeStruct((TOP_K, T), jnp.int32),
                   jax.ShapeDtypeStruct((N_EXPERTS, LANES), jnp.int32)],
        scratch_shapes=[pltpu.VMEM((N_EXPERTS, LANES), F32)],
        compiler_params=pltpu.CompilerParams(dimension_semantics=("arbitrary",),
                                             vmem_limit_bytes=VMEM_LIMIT),
        name="post",
    )(x2, o_mla, o_swa, wom, wos, wg, wout, g1, b1, wr_t, rbias, tri)


def _dest_kernel(start_ref, e_ref, r_ref, o_ref):
    e = e_ref[...]
    base = jnp.zeros(e.shape, jnp.int32)
    for j in range(N_EXPERTS):
        base = jnp.where(e == j, start_ref[j], base)
    o_ref[...] = base + r_ref[...]


def _dest(seg_start, e_t, r_t):
    T = e_t.shape[1]
    tn = min(T, 4096)
    col = pl.BlockSpec((TOP_K, tn), lambda i: (0, i))
    return pl.pallas_call(
        _dest_kernel,
        grid=(T // tn,),
        in_specs=[pl.BlockSpec(memory_space=pltpu.SMEM), col, col],
        out_specs=col,
        out_shape=jax.ShapeDtypeStruct((TOP_K, T), jnp.int32),
        compiler_params=pltpu.CompilerParams(dimension_semantics=("parallel",)),
        name="dest",
    )(seg_start, e_t, r_t)


def _sc_worker_chunks(n_tokens):
    info = plsc.get_sparse_core_info()
    n_workers = info.num_cores * info.num_subcores
    per_worker = n_tokens // SC_CHUNK // n_workers
    assert per_worker * n_workers * SC_CHUNK == n_tokens
    first = (lax.axis_index("s") * info.num_cores + lax.axis_index("c")) * per_worker
    return first, per_worker


def _sc_dispatch(h1p, idx, n_rows):
    n_tokens = h1p.shape[0] // ROW_SUB
    R = SC_CHUNK * ROW_SUB

    def body(h_hbm, idx_hbm, xs_hbm, idx_v, rows_v, sem):
        first, per_worker = _sc_worker_chunks(n_tokens)

        @pl.loop(0, per_worker)
        def _(ci):
            c = first + ci
            pltpu.sync_copy(idx_hbm.at[c], idx_v)
            pltpu.sync_copy(h_hbm.at[pl.ds(c * R, R)], rows_v)
            copies = [pltpu.make_async_copy(rows_v, xs_hbm.at[idx_v.at[k]], sem) for k in range(TOP_K)]
            for cp in copies:
                cp.start()
            for cp in copies:
                cp.wait()

    run = pl.kernel(
        body, out_type=jax.ShapeDtypeStruct((n_rows * ROW_SUB, LANES), jnp.uint32),
        mesh=plsc.VectorSubcoreMesh(core_axis_name="c", subcore_axis_name="s"),
        scratch_types=[pltpu.VMEM((TOP_K, R), jnp.int32), pltpu.VMEM((R, LANES), jnp.uint32),
                       pltpu.SemaphoreType.DMA],
        name="sc_dispatch")
    return run(h1p, idx)


def _sc_gather(ys, idx, n_tokens):
    R = SC_CHUNK * ROW_SUB
    half = TOP_K // 2

    def body(ys_hbm, idx_hbm, yg_hbm, idx_v, buf, gsem, wsem):
        first, per_worker = _sc_worker_chunks(n_tokens)

        @pl.loop(0, per_worker)
        def _(ci):
            c = first + ci
            pltpu.sync_copy(idx_hbm.at[c], idx_v)
            for k0 in (0, half):
                gathers = [pltpu.make_async_copy(ys_hbm.at[idx_v.at[k0 + k]], buf.at[k], gsem)
                           for k in range(half)]
                for cp in gathers:
                    cp.start()
                for cp in gathers:
                    cp.wait()
                writes = [pltpu.make_async_copy(buf.at[k], yg_hbm.at[k0 + k, pl.ds(c * R, R)], wsem)
                          for k in range(half)]
                for cp in writes:
                    cp.start()
                for cp in writes:
                    cp.wait()

    run = pl.kernel(
        body, out_type=jax.ShapeDtypeStruct((TOP_K, n_tokens * ROW_SUB, LANES), jnp.uint32),
        mesh=plsc.VectorSubcoreMesh(core_axis_name="c", subcore_axis_name="s"),
        scratch_types=[pltpu.VMEM((TOP_K, R), jnp.int32), pltpu.VMEM((half, R, LANES), jnp.uint32),
                       pltpu.SemaphoreType.DMA, pltpu.SemaphoreType.DMA],
        name="sc_gather")
    return run(ys, idx)


def _experts_kernel(first_ref, nblk_ref, cnt_ref, xs_hbm, wup_ref, wdn_ref, after_ref, ys_hbm,
                    wup_bf, wdn_bf, xbuf, ybuf, in_sem, out_sem):
    del after_ref
    e = pl.program_id(0)
    n = nblk_ref[e]
    b0 = first_ref[e]
    count = cnt_ref[e]
    total = first_ref[N_EXPERTS - 1] + nblk_ref[N_EXPERTS - 1]
    rb = ROW_BLK * ROW_SUB
    depth = EXPERT_SLOTS

    def slot_of(b):
        return lax.rem(b, depth)

    def rows_of(b):
        return pl.ds(pl.multiple_of(b * rb, rb), rb)

    def in_copy(b):
        return pltpu.make_async_copy(xs_hbm.at[rows_of(b)], xbuf.at[slot_of(b)], in_sem.at[slot_of(b)])

    def out_copy(b):
        return pltpu.make_async_copy(ybuf.at[slot_of(b)], ys_hbm.at[rows_of(b)], out_sem.at[slot_of(b)])

    @pl.when(e == 0)
    def _():
        for b in range(depth - 1):
            @pl.when(b < total)
            def _():
                in_copy(b).start()

    @pl.when(n > 0)
    def _():
        wup_bf[...] = wup_ref[0].astype(BF16)
        wdn_bf[...] = wdn_ref[0].astype(BF16)

        def body(b, carry):
            slot = slot_of(b)
            in_copy(b).wait()

            @pl.when(b + depth - 1 < total)
            def _():
                in_copy(b + depth - 1).start()

            @pl.when(b >= depth)
            def _():
                out_copy(b - depth).wait()

            sub = ROW_BLK // EXPERT_SPLIT
            n_valid = count - (b - b0) * ROW_BLK
            xbs = [jnp.concatenate(
                [c.astype(BF16) for c in _unpack_rows(xbuf.at[slot], sub, first=i * sub, n_valid=n_valid)],
                axis=1) for i in range(EXPERT_SPLIT)]
            gus = [_dot(xb, wup_bf[...]) for xb in xbs]
            hids = [(gu[:, :EXPERT_HIDDEN] * _sigmoid(gu[:, :EXPERT_HIDDEN])
                     * gu[:, EXPERT_HIDDEN:]).astype(BF16) for gu in gus]
            ys = [_dot(hid, wdn_bf[...]) for hid in hids]
            for i in range(EXPERT_SPLIT):
                _pack_rows(ys[i], ybuf.at[slot], first=i * sub)
            out_copy(b).start()
            return carry

        lax.fori_loop(b0, b0 + n, body, 0)

    @pl.when(e == N_EXPERTS - 1)
    def _():
        for back in range(depth, 0, -1):
            @pl.when(total - back >= 0)
            def _():
                out_copy(total - back).wait()


def _experts(first_blk, n_blk, counts, xs, w_exp_up, w_exp_down, run_after):
    rb = ROW_BLK * ROW_SUB
    grid_spec = pltpu.PrefetchScalarGridSpec(
        num_scalar_prefetch=3,
        grid=(N_EXPERTS,),
        in_specs=[pl.BlockSpec(memory_space=pl.ANY),
                  pl.BlockSpec((1, D_MODEL, 2 * EXPERT_HIDDEN), lambda e, fb, nb, ct: (e, 0, 0)),
                  pl.BlockSpec((1, EXPERT_HIDDEN, D_MODEL), lambda e, fb, nb, ct: (e, 0, 0)),
                  pl.BlockSpec(memory_space=pl.ANY)],
        out_specs=pl.BlockSpec(memory_space=pl.ANY),
        scratch_shapes=[pltpu.VMEM((D_MODEL, 2 * EXPERT_HIDDEN), BF16),
                        pltpu.VMEM((EXPERT_HIDDEN, D_MODEL), BF16),
                        pltpu.VMEM((EXPERT_SLOTS, rb, LANES), jnp.uint32),
                        pltpu.VMEM((EXPERT_SLOTS, rb, LANES), jnp.uint32),
                        pltpu.SemaphoreType.DMA((EXPERT_SLOTS,)),
                        pltpu.SemaphoreType.DMA((EXPERT_SLOTS,))],
    )
    return pl.pallas_call(
        _experts_kernel,
        grid_spec=grid_spec,
        out_shape=jax.ShapeDtypeStruct(xs.shape, jnp.uint32),
        compiler_params=pltpu.CompilerParams(dimension_semantics=("arbitrary",),
                                             vmem_limit_bytes=VMEM_LIMIT),
        name="experts",
    )(first_blk, n_blk, counts, xs, w_exp_up, w_exp_down, run_after)


def _shared_kernel(h1p_ref, wsu_ref, wsd_ref, o_ref):
    tm = o_ref.shape[0]
    hb = jnp.concatenate([c.astype(BF16) for c in _unpack_rows(h1p_ref, tm)], axis=1)
    gu = _dot(hb, wsu_ref[...])
    g = gu[:, :SHARED_HIDDEN]
    hid = g * _sigmoid(g) * gu[:, SHARED_HIDDEN:]
    o_ref[...] = _dot(hid.astype(BF16), wsd_ref[...]).astype(BF16)


def _shared(h1p, wsu, wsd):
    T = h1p.shape[0] // ROW_SUB
    tm = SHARED_TM
    full = lambda a: pl.BlockSpec(a.shape, lambda i: (0, 0))
    row = pl.BlockSpec((tm, D_MODEL), lambda i: (i, 0))
    return pl.pallas_call(
        _shared_kernel,
        grid=(T // tm,),
        in_specs=[pl.BlockSpec((tm * ROW_SUB, LANES), lambda i: (i, 0)), full(wsu), full(wsd)],
        out_specs=row,
        out_shape=jax.ShapeDtypeStruct((T, D_MODEL), BF16),
        compiler_params=pltpu.CompilerParams(dimension_semantics=("parallel",),
                                             vmem_limit_bytes=VMEM_LIMIT),
        name="shared",
    )(h1p, wsu, wsd)


def _combine_kernel(w_ref, h1_ref, sh_ref, yg_ref, g2_ref, b2_ref, *rest):
    o_ref = rest[-1]
    tm = h1_ref.shape[0]
    h1 = h1_ref[...]
    ffn = sh_ref[...].astype(F32)

    w = w_ref[...]
    acc = [None] * (2 * ROW_SUB)
    for k in range(TOP_K):
        wk = w[:, k:k + 1]
        chunks = _unpack_rows(yg_ref.at[k], tm)
        for c in range(2 * ROW_SUB):
            acc[c] = wk * chunks[c] if acc[c] is None else acc[c] + wk * chunks[c]
    routed = jnp.concatenate(acc, axis=1)
    z = DEEPNORM_ALPHA * h1 + (routed + ffn)
    mu = jnp.mean(z, axis=-1, keepdims=True)
    zc = z - mu
    var = jnp.mean(zc * zc, axis=-1, keepdims=True)
    o_ref[...] = zc * lax.rsqrt(var + LN_EPS) * g2_ref[...] + b2_ref[...]


def _combine(w_tok, h1, sh, yg, g2, b2, first_token, out_prev):
    n_tok = yg.shape[1] // ROW_SUB
    tm = COMBINE_TM
    first = first_token // tm
    full = lambda a: pl.BlockSpec(a.shape, lambda i: (0, 0))
    row = pl.BlockSpec((tm, D_MODEL), lambda i: (first + i, 0))
    in_specs = [pl.BlockSpec((tm, TOP_K), lambda i: (first + i, 0)), row, row,
                pl.BlockSpec((TOP_K, tm * ROW_SUB, LANES), lambda i: (0, i, 0)),
                full(g2), full(b2)]
    args = [w_tok, h1, sh, yg, g2, b2]
    aliases = {}
    if out_prev is not None:
        in_specs.append(pl.BlockSpec(memory_space=pl.ANY))
        args.append(out_prev)
        aliases = {len(args) - 1: 0}
    return pl.pallas_call(
        _combine_kernel,
        grid=(n_tok // tm,),
        in_specs=in_specs,
        out_specs=row,
        out_shape=jax.ShapeDtypeStruct(h1.shape, F32),
        input_output_aliases=aliases,
        compiler_params=pltpu.CompilerParams(dimension_semantics=("parallel",),
                                             vmem_limit_bytes=VMEM_LIMIT),
        name="combine",
    )(*args)


def _prep_weights(w_in, w_uq, w_ukv, seq):
    z = lambda n: jnp.zeros((D_MODEL, n), F32)
    kr = w_in[:, 512:544]
    qs = w_in[:, 544:1056] * (LOG2E / math.sqrt(SWA_HEAD_DIM))
    ks0, ks1 = w_in[:, 1056:1120], w_in[:, 1120:1184]
    half = MLA_ROPE // 2
    w1 = jnp.concatenate([
        w_in[:, 0:512], qs,
        ks0, ks1, ks1, ks0,
        z(64), kr, z(32),
        z(64), -kr[:, half:], kr[:, :half], z(32)], axis=1).astype(BF16)
    wvst = w_in[:, 1184:1312].T.astype(BF16)
    wg = w_in[:, 1312:3360].astype(BF16)

    pad3 = lambda a, lo, hi: jnp.pad(a, ((0, 0), (0, 0), (lo, hi)))
    wq3 = w_uq.reshape(MLA_Q_LORA, MLA_HEADS, MLA_NOPE + MLA_ROPE)
    rope = wq3[:, :, MLA_NOPE:]
    rot = jnp.concatenate([-rope[:, :, half:], rope[:, :, :half]], axis=2)
    wq2 = jnp.concatenate([pad3(wq3, 0, 32).reshape(MLA_Q_LORA, -1),
                           pad3(rot, MLA_NOPE, 32).reshape(MLA_Q_LORA, -1)], axis=1).astype(BF16)
    wkv3 = w_ukv.reshape(MLA_KV_LORA, MLA_HEADS, MLA_NOPE + MLA_V)
    wk = pad3(wkv3[:, :, :MLA_NOPE], 0, 64).reshape(MLA_KV_LORA, -1).astype(BF16)
    wvt = wkv3[:, :, MLA_NOPE:].reshape(MLA_KV_LORA, -1).T.astype(BF16)

    inv_freq = ROPE_THETA ** (-jnp.arange(0, MLA_ROPE, 2, dtype=F32) / MLA_ROPE)
    ang = jnp.arange(seq, dtype=F32)[:, None] * inv_freq[None, :]
    cos, sin = jnp.cos(ang), jnp.sin(ang)
    one, zero = jnp.ones((seq, 64), F32), jnp.zeros((seq, 64), F32)
    z32 = jnp.zeros((seq, 32), F32)
    scale = LOG2E / math.sqrt(MLA_NOPE + MLA_ROPE)
    tabs = (jnp.concatenate([one, cos, cos, z32], axis=1) * scale,
            jnp.concatenate([zero, sin, sin, z32], axis=1) * scale,
            jnp.concatenate([zero, cos, cos, z32], axis=1),
            jnp.concatenate([zero, sin, sin, z32], axis=1))
    return w1, wvst, wg, wq2, wk, wvt, tabs


def kernel(x, w_in, mla_q_norm, mla_kv_norm, w_uq, w_ukv, attn_sinks, w_o_mla, w_o_swa, w_out,
           ln1_g, ln1_b, w_router, router_bias, w_exp_up, w_exp_down, w_sh_up, w_sh_down,
           ln2_g, ln2_b):
    batch, seq, _ = x.shape
    T = batch * seq
    x2 = x.reshape(T, D_MODEL)
    w1, wvst, wg, wq2, wk, wvt, tabs = _prep_weights(w_in, w_uq, w_ukv, seq)
    q, k, vt, qs, ks, vst = _proj(x2, w1, wvst, wq2, wk, wvt, mla_q_norm.reshape(1, -1),
                                  mla_kv_norm.reshape(1, -1), tabs, seq)
    o_mla = _mla(q, k, vt, batch, seq)
    o_swa = _swa(attn_sinks.astype(F32), qs, ks, vst, batch, seq)

    tri = (lax.broadcasted_iota(jnp.int32, (ROUTE_TN, ROUTE_TN), 0)
           < lax.broadcasted_iota(jnp.int32, (ROUTE_TN, ROUTE_TN), 1)).astype(BF16)
    h1, h1p, e_t, w_t, r_t, cnt = _post(
        x2, o_mla, o_swa, w_o_mla.astype(BF16), w_o_swa.astype(BF16), wg, w_out.astype(BF16),
        ln1_g.reshape(1, -1), ln1_b.reshape(1, -1), w_router.T.astype(BF16),
        router_bias.reshape(-1, 1).astype(F32), tri)

    counts = cnt[:, 0]
    pad = (counts + ROW_BLK - 1) // ROW_BLK * ROW_BLK
    pad_start = jnp.cumsum(pad) - pad
    n_rows = (T * TOP_K // ROW_BLK + N_EXPERTS) * ROW_BLK
    dest = _dest(pad_start.astype(jnp.int32), e_t, r_t)
    idx = (dest.reshape(TOP_K, T // SC_CHUNK, SC_CHUNK, 1) * ROW_SUB
           + jnp.arange(ROW_SUB, dtype=jnp.int32))
    idx = idx.transpose(1, 0, 2, 3).reshape(T // SC_CHUNK, TOP_K, SC_CHUNK * ROW_SUB)
    xs = _sc_dispatch(h1p, idx, n_rows)
    sh = _shared(h1p, w_sh_up.astype(BF16), w_sh_down.astype(BF16))
    ys = _experts((pad_start // ROW_BLK).astype(jnp.int32), (pad // ROW_BLK).astype(jnp.int32),
                  counts.astype(jnp.int32), xs, w_exp_up, w_exp_down, sh)

    Tc = T // COMBINE_PARTS
    n_chunks = Tc // SC_CHUNK
    w_tok = w_t.T
    g2, b2 = ln2_g.reshape(1, -1), ln2_b.reshape(1, -1)
    out = None
    for cpart in range(COMBINE_PARTS):
        yg = _sc_gather(ys, idx[cpart * n_chunks:(cpart + 1) * n_chunks], Tc)
        out = _combine(w_tok, h1, sh, yg, g2, b2, cpart * Tc, out)
    return out.reshape(batch, seq, D_MODEL)
```

```python
import math

import jax
import jax.numpy as jnp
from jax import lax
from jax.experimental import pallas as pl
from jax.experimental.pallas import tpu as pltpu
from jax.experimental.pallas import tpu_sc as plsc

D_MODEL = 1024
MLA_HEADS = 8
MLA_Q_LORA = 256
MLA_KV_LORA = 256
MLA_NOPE = 64
MLA_ROPE = 32
MLA_V = 64
VT_ROWS = 80
ROPE_THETA = 10000.0
SWA_HEADS = 8
SWA_KV_HEADS = 2
SWA_GROUP = SWA_HEADS // SWA_KV_HEADS
SWA_HEAD_DIM = 64
SWA_WINDOW = 128
SWA_TQ = 256
N_EXPERTS = 256
TOP_K = 8
N_GROUPS = 8
GROUP_SIZE = N_EXPERTS // N_GROUPS
TOPK_GROUPS = 4
EXPERT_HIDDEN = 256
SHARED_HIDDEN = 256
ROUTED_SCALE = 2.5
DEEPNORM_ALPHA = 2.0 ** 0.25
LN_EPS = 1e-5
RMS_EPS = 1e-6

LANES = 128
ROW_WORDS = D_MODEL // 2
ROW_SUB = ROW_WORDS // LANES
VMEM_LIMIT = 48 * 1024 * 1024

PROJ_TM = 512
MLA_TQ = 256
MLA_TK = 512
MLA_HPS = 8
POST_TM = 512
ROUTE_TN = 256
ROW_BLK = 256
EXPERT_SLOTS = 4
EXPERT_SPLIT = 2
SHARED_TM = 512
COMBINE_TM = 256
SC_CHUNK = 32
COMBINE_PARTS = 4

BF16 = jnp.bfloat16
F32 = jnp.float32
NEG_INF = float("-inf")
LOG2E = math.log2(math.e)


def _sigmoid(v):
    return 1.0 / (1.0 + jnp.exp(-v))


def _dot(a, b):
    return jnp.dot(a, b, preferred_element_type=F32)


def _dot_nt(a, b):
    return lax.dot_general(a, b, (((1,), (1,)), ((), ())), preferred_element_type=F32)


def _pack_rows(y, out_ref, first=0):
    rows = y.shape[0]
    for j in range(ROW_SUB):
        a = y[:, j * LANES:(j + 1) * LANES].astype(BF16).astype(F32)
        b = y[:, ROW_WORDS + j * LANES:ROW_WORDS + (j + 1) * LANES].astype(BF16).astype(F32)
        ua = pltpu.bitcast(a, jnp.uint32) >> 16
        ub = pltpu.bitcast(b, jnp.uint32)
        out_ref[pl.ds(first * ROW_SUB + j, rows, stride=ROW_SUB), :] = ua | ub


def _unpack_rows(ref, rows, first=0, n_valid=None):
    lo, hi = [], []
    if n_valid is not None:
        live = first + lax.broadcasted_iota(jnp.int32, (rows, LANES), 0) < n_valid
    for j in range(ROW_SUB):
        u = ref[pl.ds(first * ROW_SUB + j, rows, stride=ROW_SUB), :]
        if n_valid is not None:
            u = jnp.where(live, u, jnp.uint32(0))
        lo.append(pltpu.bitcast(u << 16, F32))
        hi.append(pltpu.bitcast(u & jnp.uint32(0xFFFF0000), F32))
    return lo + hi


def _proj_kernel(x_ref, w1_ref, wvst_ref, wq_ref, wk_ref, wvt_ref, gq_ref, gkv_ref,
                 cq_ref, sq_ref, ck_ref, sk_ref,
                 q_ref, k_ref, vt_ref, qs_ref, ks_ref, vst_ref):
    xb = x_ref[...].astype(BF16)
    p = _dot(xb, w1_ref[...])

    def rms(c, g):
        return c * lax.rsqrt(jnp.mean(c * c, axis=-1, keepdims=True) + RMS_EPS) * g

    cqn = rms(p[:, 0:256], gq_ref[...]).astype(BF16)
    ckvn = rms(p[:, 256:512], gkv_ref[...]).astype(BF16)
    qs_ref[...] = p[:, 512:1024].astype(BF16)
    k01 = p[:, 1024:1152].astype(BF16)
    k10 = p[:, 1152:1280].astype(BF16)
    lane_lo = lax.broadcasted_iota(jnp.int32, k01.shape, 1) < SWA_HEAD_DIM
    zero = jnp.zeros_like(k01)
    ks_ref[:, 0 * LANES:1 * LANES] = jnp.where(lane_lo, k01, zero)
    ks_ref[:, 1 * LANES:2 * LANES] = jnp.where(lane_lo, zero, k10)
    ks_ref[:, 2 * LANES:3 * LANES] = jnp.where(lane_lo, k10, zero)
    ks_ref[:, 3 * LANES:4 * LANES] = jnp.where(lane_lo, zero, k01)
    ones = jnp.ones((VT_ROWS - MLA_V, xb.shape[0]), BF16)
    vst = _dot_nt(wvst_ref[...], xb).astype(BF16)
    for h in range(SWA_KV_HEADS):
        vst_ref[h * VT_ROWS:h * VT_ROWS + SWA_HEAD_DIM, :] = vst[h * SWA_HEAD_DIM:(h + 1) * SWA_HEAD_DIM, :]
        vst_ref[h * VT_ROWS + SWA_HEAD_DIM:(h + 1) * VT_ROWS, :] = ones
    kr = p[:, 1280:1408] * ck_ref[...] + p[:, 1408:1536] * sk_ref[...]
    qq = _dot(cqn, wq_ref[...])
    kn = _dot(ckvn, wk_ref[...])
    cq = cq_ref[...]
    sq = sq_ref[...]
    for h in range(MLA_HEADS):
        a = qq[:, h * LANES:(h + 1) * LANES]
        b = qq[:, 1024 + h * LANES:1024 + (h + 1) * LANES]
        q_ref[:, h * LANES:(h + 1) * LANES] = (a * cq + b * sq).astype(BF16)
        k_ref[:, h * LANES:(h + 1) * LANES] = (kn[:, h * LANES:(h + 1) * LANES] + kr).astype(BF16)
    vt = _dot_nt(wvt_ref[...], ckvn).astype(BF16)
    for h in range(MLA_HEADS):
        vt_ref[h * VT_ROWS:h * VT_ROWS + MLA_V, :] = vt[h * MLA_V:(h + 1) * MLA_V, :]
        vt_ref[h * VT_ROWS + MLA_V:(h + 1) * VT_ROWS, :] = ones


def _proj(x2, w1, wvst, wq2, wk, wvt, gq, gkv, tabs, seq):
    T = x2.shape[0]
    tm = PROJ_TM
    nper = seq // tm
    full = lambda shape: pl.BlockSpec(shape, lambda i: (0, 0))
    tab = pl.BlockSpec((tm, LANES), lambda i: (i % nper, 0))
    row = lambda n: pl.BlockSpec((tm, n), lambda i: (i, 0))
    col = lambda n: pl.BlockSpec((n, tm), lambda i: (0, i))
    return pl.pallas_call(
        _proj_kernel,
        grid=(T // tm,),
        in_specs=[row(D_MODEL), full(w1.shape), full(wvst.shape), full(wq2.shape), full(wk.shape),
                  full(wvt.shape), full(gq.shape), full(gkv.shape), tab, tab, tab, tab],
        out_specs=[row(1024), row(1024), col(MLA_HEADS * VT_ROWS), row(512), row(512),
                   col(SWA_KV_HEADS * VT_ROWS)],
        out_shape=[jax.ShapeDtypeStruct((T, 1024), BF16), jax.ShapeDtypeStruct((T, 1024), BF16),
                   jax.ShapeDtypeStruct((MLA_HEADS * VT_ROWS, T), BF16), jax.ShapeDtypeStruct((T, 512), BF16),
                   jax.ShapeDtypeStruct((T, 512), BF16),
                   jax.ShapeDtypeStruct((SWA_KV_HEADS * VT_ROWS, T), BF16)],
        compiler_params=pltpu.CompilerParams(dimension_semantics=("parallel",),
                                             vmem_limit_bytes=VMEM_LIMIT),
        name="proj",
    )(x2, w1, wvst, wq2, wk, wvt, gq, gkv, *tabs)


def _mla_kernel(q_ref, k_ref, vt_ref, o_ref, *acc_scr):
    tq = q_ref.shape[0]
    qi = pl.program_id(2)
    for acc in acc_scr:
        acc[...] = jnp.zeros(acc.shape, F32)

    def step(ks, tk, maxes, masked):
        scores = [_dot_nt(k_ref[pl.ds(ks, tk), h * LANES:(h + 1) * LANES],
                          q_ref[:, h * LANES:(h + 1) * LANES]) for h in range(MLA_HPS)]
        new_maxes, probs, alphas = [], [], []
        for h in range(MLA_HPS):
            s = scores[h]
            if masked:
                key = ks + lax.broadcasted_iota(jnp.int32, s.shape, 0)
                qry = qi * tq + lax.broadcasted_iota(jnp.int32, s.shape, 1)
                s = jnp.where(key <= qry, s, NEG_INF)
            m_new = jnp.maximum(maxes[h], jnp.max(s, axis=0, keepdims=True))
            alphas.append(jnp.exp2(maxes[h] - m_new))
            probs.append(jnp.exp2(s - m_new).astype(BF16))
            new_maxes.append(m_new)
        for h in range(MLA_HPS):
            pv = _dot(vt_ref[h * VT_ROWS:(h + 1) * VT_ROWS, pl.ds(ks, tk)], probs[h])
            acc_scr[h][...] = acc_scr[h][...] * alphas[h] + pv
        return tuple(new_maxes)

    init = tuple(jnp.full((1, tq), NEG_INF, F32) for _ in range(MLA_HPS))
    big = MLA_TK
    n_big = (qi * tq) // big
    maxes = lax.fori_loop(
        0, n_big, lambda kc, mx: step(pl.multiple_of(kc * big, big), big, mx, False), init)
    rest = pl.multiple_of(n_big * big, tq)
    maxes = lax.cond(
        rest < qi * tq,
        lambda mx: step(rest, 2 * tq, mx, True),
        lambda mx: step(rest, tq, mx, True),
        maxes)
    for h2 in range(MLA_HPS // 2):
        out_t = jnp.concatenate(
            [acc_scr[2 * h2 + g][0:MLA_V, :] / acc_scr[2 * h2 + g][MLA_V:MLA_V + 1, :] for g in range(2)],
            axis=0)
        o_ref[:, h2 * LANES:(h2 + 1) * LANES] = out_t.T.astype(BF16)


def _mla(q, k, vt, batch, seq):
    T = q.shape[0]
    tq = MLA_TQ
    nq = seq // tq
    hps = MLA_HPS
    return pl.pallas_call(
        _mla_kernel,
        grid=(batch, MLA_HEADS // hps, nq),
        in_specs=[pl.BlockSpec((tq, hps * LANES), lambda b, j, i: (b * nq + i, j)),
                  pl.BlockSpec((seq, hps * LANES), lambda b, j, i: (b, j)),
                  pl.BlockSpec((hps * VT_ROWS, seq), lambda b, j, i: (j, b))],
        out_specs=pl.BlockSpec((tq, hps * MLA_V), lambda b, j, i: (b * nq + i, j)),
        out_shape=jax.ShapeDtypeStruct((T, MLA_HEADS * MLA_V), BF16),
        scratch_shapes=[pltpu.VMEM((VT_ROWS, tq), F32) for _ in range(hps)],
        compiler_params=pltpu.CompilerParams(
            dimension_semantics=("parallel", "parallel", "arbitrary"), vmem_limit_bytes=VMEM_LIMIT),
        name="mla",
    )(q, k, vt)


def _swa_kernel(sink_ref, q_ref, kc_ref, kp_ref, vtc_ref, vtp_ref, bias_ref, o_ref):
    scores = []
    for head in range(SWA_HEADS):
        pair, g, kvh = head // 2, head % 2, head // SWA_GROUP
        col = (2 * kvh + g) * LANES
        band = jnp.concatenate([kp_ref[:, col:col + LANES], kc_ref[:, col:col + LANES]], axis=0)
        scores.append(_dot_nt(band, q_ref[:, pair * LANES:(pair + 1) * LANES]))
    probs, sink_terms = [], []
    for head in range(SWA_HEADS):
        s = scores[head] + bias_ref[0, head]
        sink = sink_ref[head] * LOG2E
        m = jnp.maximum(jnp.max(s, axis=0, keepdims=True), sink)
        probs.append(jnp.exp2(s - m).astype(BF16))
        sink_terms.append(jnp.exp2(sink - m))
    outs = []
    for head in range(SWA_HEADS):
        rows = slice((head // SWA_GROUP) * VT_ROWS, (head // SWA_GROUP + 1) * VT_ROWS)
        v_band = jnp.concatenate([vtp_ref[rows, :], vtc_ref[rows, :]], axis=1)
        pv = _dot(v_band, probs[head])
        outs.append(pv[0:SWA_HEAD_DIM, :] / (pv[SWA_HEAD_DIM:SWA_HEAD_DIM + 1, :] + sink_terms[head]))
    o_ref[...] = jnp.concatenate(outs, axis=0).T.astype(BF16)


def _swa_bias():
    W = SWA_WINDOW
    j = jnp.arange(W + SWA_TQ, dtype=jnp.int32)[:, None]
    i = jnp.arange(SWA_TQ, dtype=jnp.int32)[None, :]
    dist = i + W - j
    valid = (dist >= 0) & (dist < W)
    slopes = 2.0 ** (-8.0 * jnp.arange(1, SWA_HEADS + 1, dtype=F32) / SWA_HEADS)
    pen = -(slopes[:, None, None] * dist.astype(F32)[None]) * LOG2E
    general = jnp.where(valid[None], pen, NEG_INF)
    first = jnp.where((valid & (j >= W))[None], pen, NEG_INF)
    return jnp.stack([first, general])


def _swa(sinks, qs, ks, vst, batch, seq):
    T = qs.shape[0]
    W, tq = SWA_WINDOW, SWA_TQ
    nb = seq // tq
    wpb = tq // W
    before = lambda b, i: b * nb * wpb + jnp.maximum(wpb * i - 1, 0)
    cur = lambda n: pl.BlockSpec((tq, n), lambda b, i: (b * nb + i, 0))
    prev = lambda n: pl.BlockSpec((W, n), lambda b, i: (before(b, i), 0))
    vt_cur = pl.BlockSpec((SWA_KV_HEADS * VT_ROWS, tq), lambda b, i: (0, b * nb + i))
    vt_prev = pl.BlockSpec((SWA_KV_HEADS * VT_ROWS, W), lambda b, i: (0, before(b, i)))
    bias = pl.BlockSpec((1, SWA_HEADS, W + tq, tq), lambda b, i: (jnp.minimum(i, 1), 0, 0, 0))
    return pl.pallas_call(
        _swa_kernel,
        grid=(batch, nb),
        in_specs=[pl.BlockSpec(memory_space=pltpu.SMEM), cur(512), cur(512), prev(512),
                  vt_cur, vt_prev, bias],
        out_specs=cur(512),
        out_shape=jax.ShapeDtypeStruct((T, 512), BF16),
        compiler_params=pltpu.CompilerParams(dimension_semantics=("parallel", "parallel"),
                                             vmem_limit_bytes=VMEM_LIMIT),
        name="swa",
    )(sinks, qs, ks, ks, vst, vst, _swa_bias())


def _post_kernel(x_ref, om_ref, os_ref, wom_ref, wos_ref, wg_ref, wout_ref, g1_ref, b1_ref,
                 wr_ref, rb_ref, tri_ref,
                 h1_ref, h1p_ref, e_ref, w_ref, r_ref, cnt_ref, carry_scr):
    tm = x_ref.shape[0]
    step = pl.program_id(0)

    @pl.when(step == 0)
    def _():
        carry_scr[...] = jnp.zeros(carry_scr.shape, F32)

    x = x_ref[...]
    ya = _dot(om_ref[...], wom_ref[...])
    yb = _dot(os_ref[...], wos_ref[...])
    gates = _dot(x.astype(BF16), wg_ref[...])
    merged = _sigmoid(gates[:, :D_MODEL]) * ya + _sigmoid(gates[:, D_MODEL:]) * yb
    mix = _dot(merged.astype(BF16), wout_ref[...])
    z = DEEPNORM_ALPHA * x + mix
    mu = jnp.mean(z, axis=-1, keepdims=True)
    zc = z - mu
    var = jnp.mean(zc * zc, axis=-1, keepdims=True)
    h1 = zc * lax.rsqrt(var + LN_EPS) * g1_ref[...] + b1_ref[...]
    h1_ref[...] = h1
    _pack_rows(h1, h1p_ref)

    all_scores = _sigmoid(_dot_nt(wr_ref[...], h1.astype(BF16)))
    carry = carry_scr[...]
    for c in range(tm // ROUTE_TN):
        cols = slice(c * ROUTE_TN, (c + 1) * ROUTE_TN)
        idxs, weights, ranks, carry = _route(all_scores[:, cols], rb_ref[...], tri_ref[...], carry)
        e_ref[:, cols] = idxs
        w_ref[:, cols] = weights
        r_ref[:, cols] = ranks
    carry_scr[...] = carry
    cnt_ref[...] = carry.astype(jnp.int32)


def _route(scores, bias, tri, carry):
    tn = scores.shape[1]
    choice = scores + bias
    row = lax.broadcasted_iota(jnp.int32, (N_EXPERTS, tn), 0)
    grow = lax.broadcasted_iota(jnp.int32, (GROUP_SIZE, tn), 0)
    gscore = []
    for g in range(N_GROUPS):
        blk = choice[g * GROUP_SIZE:(g + 1) * GROUP_SIZE, :]
        m1 = jnp.max(blk, axis=0, keepdims=True)
        i1 = jnp.min(jnp.where(blk == m1, grow, GROUP_SIZE), axis=0, keepdims=True)
        m2 = jnp.max(jnp.where(grow == i1, NEG_INF, blk), axis=0, keepdims=True)
        gscore.append(m1 + m2)
    gsc = jnp.concatenate(gscore, axis=0)
    gidx = lax.broadcasted_iota(jnp.int32, (N_GROUPS, tn), 0)
    grank = jnp.zeros((N_GROUPS, tn), jnp.int32)
    for g in range(N_GROUPS):
        sg = gsc[g:g + 1, :]
        beats = (sg > gsc) | ((sg == gsc) & (gidx > g))
        grank = grank + beats.astype(jnp.int32)
    gsel = (grank < TOPK_GROUPS).astype(F32)
    emask = jnp.concatenate(
        [jnp.broadcast_to(gsel[g:g + 1, :], (GROUP_SIZE, tn)) for g in range(N_GROUPS)], axis=0)
    work = jnp.where(emask > 0.0, choice, NEG_INF)
    eligible = work
    idxs, svals = [], []
    for _k in range(TOP_K):
        m = jnp.max(work, axis=0, keepdims=True)
        idx = jnp.min(jnp.where(work == m, row, N_EXPERTS), axis=0, keepdims=True)
        hit = row == idx
        svals.append(jnp.sum(jnp.where(hit, scores, 0.0), axis=0, keepdims=True))
        work = jnp.where(hit, NEG_INF, work)
        idxs.append(idx)
    sel = jnp.where(work != eligible, 1.0, 0.0)
    ssum = svals[0]
    for sv in svals[1:]:
        ssum = ssum + sv
    weights = jnp.concatenate([sv / ssum * ROUTED_SCALE for sv in svals], axis=0)

    rank = _dot(sel.astype(BF16), tri) + carry[:, 0:1]
    ranks = jnp.concatenate(
        [jnp.sum(jnp.where(row == idx, rank, 0.0), axis=0, keepdims=True) for idx in idxs],
        axis=0).astype(jnp.int32)
    carry = carry + jnp.sum(sel, axis=1, keepdims=True)
    return jnp.concatenate(idxs, axis=0), weights, ranks, carry


def _post(x2, o_mla, o_swa, wom, wos, wg, wout, g1, b1, wr_t, rbias, tri):
    T = x2.shape[0]
    tm = POST_TM
    full = lambda a: pl.BlockSpec(a.shape, lambda i: (0, 0))
    row = lambda n: pl.BlockSpec((tm, n), lambda i: (i, 0))
    col = pl.BlockSpec((TOP_K, tm), lambda i: (0, i))
    return pl.pallas_call(
        _post_kernel,
        grid=(T // tm,),
        in_specs=[row(D_MODEL), row(512), row(512), full(wom), full(wos), full(wg), full(wout),
                  full(g1), full(b1), full(wr_t), full(rbias), full(tri)],
        out_specs=[row(D_MODEL), pl.BlockSpec((tm * ROW_SUB, LANES), lambda i: (i, 0)), col, col, col,
                   pl.BlockSpec((N_EXPERTS, LANES), lambda i: (0, 0))],
        out_shape=[jax.ShapeDtypeStruct((T, D_MODEL), F32),
                   jax.ShapeDtypeStruct((T * ROW_SUB, LANES), jnp.uint32),
                   jax.ShapeDtypeStruct((TOP_K, T), jnp.int32),
                   jax.ShapeDtypeStruct((TOP_K, T), F32),
                   jax.ShapeDtypeStruct((TOP_K, T), jnp.int32),
                   jax.ShapeDtypeStruct((N_EXPERTS, LANES), jnp.int32)],
        scratch_shapes=[pltpu.VMEM((N_EXPERTS, LANES), F32)],
        compiler_params=pltpu.CompilerParams(dimension_semantics=("arbitrary",),
                                             vmem_limit_bytes=VMEM_LIMIT),
        name="post",
    )(x2, o_mla, o_swa, wom, wos, wg, wout, g1, b1, wr_t, rbias, tri)


def _dest_kernel(start_ref, e_ref, r_ref, o_ref):
    e = e_ref[...]
    base = jnp.zeros(e.shape, jnp.int32)
    for j in range(N_EXPERTS):
        base = jnp.where(e == j, start_ref[j], base)
    o_ref[...] = base + r_ref[...]


def _dest(seg_start, e_t, r_t):
    T = e_t.shape[1]
    tn = min(T, 4096)
    col = pl.BlockSpec((TOP_K, tn), lambda i: (0, i))
    return pl.pallas_call(
        _dest_kernel,
        grid=(T // tn,),
        in_specs=[pl.BlockSpec(memory_space=pltpu.SMEM), col, col],
        out_specs=col,
        out_shape=jax.ShapeDtypeStruct((TOP_K, T), jnp.int32),
        compiler_params=pltpu.CompilerParams(dimension_semantics=("parallel",)),
        name="dest",
    )(seg_start, e_t, r_t)


def _sc_worker_chunks(n_tokens):
    info = plsc.get_sparse_core_info()
    n_workers = info.num_cores * info.num_subcores
    per_worker = n_tokens // SC_CHUNK // n_workers
    assert per_worker * n_workers * SC_CHUNK == n_tokens
    first = (lax.axis_index("s") * info.num_cores + lax.axis_index("c")) * per_worker
    return first, per_worker


def _sc_dispatch(h1p, idx, n_rows):
    n_tokens = h1p.shape[0] // ROW_SUB
    R = SC_CHUNK * ROW_SUB

    def body(h_hbm, idx_hbm, xs_hbm, idx_v, rows_v, load_sem, scatter_sem):
        first, per_worker = _sc_worker_chunks(n_tokens)
        assert per_worker % 2 == 0

        def loads(c, slot):
            return (pltpu.make_async_copy(idx_hbm.at[c], idx_v.at[slot], load_sem.at[slot]),
                    pltpu.make_async_copy(h_hbm.at[pl.ds(c * R, R)], rows_v.at[slot], load_sem.at[slot]))

        for cp in loads(first, 0):
            cp.start()

        @pl.loop(0, per_worker, step=2)
        def _(ci):
            for slot in range(2):
                c = first + ci + slot
                for cp in loads(c, slot):
                    cp.wait()

                @pl.when(ci + slot + 1 < per_worker)
                def _():
                    for cp in loads(c + 1, 1 - slot):
                        cp.start()

                copies = [pltpu.make_async_copy(rows_v.at[slot], xs_hbm.at[idx_v.at[slot, k]], scatter_sem)
                          for k in range(TOP_K)]
                for cp in copies:
                    cp.start()
                for cp in copies:
                    cp.wait()

    run = pl.kernel(
        body, out_type=jax.ShapeDtypeStruct((n_rows * ROW_SUB, LANES), jnp.uint32),
        mesh=plsc.VectorSubcoreMesh(core_axis_name="c", subcore_axis_name="s"),
        scratch_types=[pltpu.VMEM((2, TOP_K, R), jnp.int32), pltpu.VMEM((2, R, LANES), jnp.uint32),
                       pltpu.SemaphoreType.DMA((2,)), pltpu.SemaphoreType.DMA],
        name="sc_dispatch")
    return run(h1p, idx)


def _sc_gather(ys, idx, n_tokens):
    R = SC_CHUNK * ROW_SUB
    half = TOP_K // 2

    def body(ys_hbm, idx_hbm, yg_hbm, idx_v, buf, gsem, wsem):
        first, per_worker = _sc_worker_chunks(n_tokens)

        @pl.loop(0, per_worker)
        def _(ci):
            c = first + ci
            pltpu.sync_copy(idx_hbm.at[c], idx_v)
            for k0 in (0, half):
                gathers = [pltpu.make_async_copy(ys_hbm.at[idx_v.at[k0 + k]], buf.at[k], gsem)
                           for k in range(half)]
                for cp in gathers:
                    cp.start()
                for cp in gathers:
                    cp.wait()
                writes = [pltpu.make_async_copy(buf.at[k], yg_hbm.at[k0 + k, pl.ds(c * R, R)], wsem)
                          for k in range(half)]
                for cp in writes:
                    cp.start()
                for cp in writes:
                    cp.wait()

    run = pl.kernel(
        body, out_type=jax.ShapeDtypeStruct((TOP_K, n_tokens * ROW_SUB, LANES), jnp.uint32),
        mesh=plsc.VectorSubcoreMesh(core_axis_name="c", subcore_axis_name="s"),
        scratch_types=[pltpu.VMEM((TOP_K, R), jnp.int32), pltpu.VMEM((half, R, LANES), jnp.uint32),
                       pltpu.SemaphoreType.DMA, pltpu.SemaphoreType.DMA],
        name="sc_gather")
    return run(ys, idx)


def _experts_kernel(first_ref, nblk_ref, cnt_ref, xs_hbm, wup_ref, wdn_ref, after_ref, ys_hbm,
                    wup_bf, wdn_bf, xbuf, ybuf, in_sem, out_sem):
    del after_ref
    e = pl.program_id(0)
    n = nblk_ref[e]
    b0 = first_ref[e]
    count = cnt_ref[e]
    total = first_ref[N_EXPERTS - 1] + nblk_ref[N_EXPERTS - 1]
    rb = ROW_BLK * ROW_SUB
    depth = EXPERT_SLOTS

    def slot_of(b):
        return lax.rem(b, depth)

    def rows_of(b):
        return pl.ds(pl.multiple_of(b * rb, rb), rb)

    def in_copy(b):
        return pltpu.make_async_copy(xs_hbm.at[rows_of(b)], xbuf.at[slot_of(b)], in_sem.at[slot_of(b)])

    def out_copy(b):
        return pltpu.make_async_copy(ybuf.at[slot_of(b)], ys_hbm.at[rows_of(b)], out_sem.at[slot_of(b)])

    @pl.when(e == 0)
    def _():
        for b in range(depth - 1):
            @pl.when(b < total)
            def _():
                in_copy(b).start()

    @pl.when(n > 0)
    def _():
        wup_bf[...] = wup_ref[0].astype(BF16)
        wdn_bf[...] = wdn_ref[0].astype(BF16)

        def body(b, carry):
            slot = slot_of(b)
            in_copy(b).wait()

            @pl.when(b + depth - 1 < total)
            def _():
                in_copy(b + depth - 1).start()

            @pl.when(b >= depth)
            def _():
                out_copy(b - depth).wait()

            sub = ROW_BLK // EXPERT_SPLIT
            n_valid = count - (b - b0) * ROW_BLK
            xbs = [jnp.concatenate(
                [c.astype(BF16) for c in _unpack_rows(xbuf.at[slot], sub, first=i * sub, n_valid=n_valid)],
                axis=1) for i in range(EXPERT_SPLIT)]
            gus = [_dot(xb, wup_bf[...]) for xb in xbs]
            hids = [(gu[:, :EXPERT_HIDDEN] * _sigmoid(gu[:, :EXPERT_HIDDEN])
                     * gu[:, EXPERT_HIDDEN:]).astype(BF16) for gu in gus]
            ys = [_dot(hid, wdn_bf[...]) for hid in hids]
            for i in range(EXPERT_SPLIT):
                _pack_rows(ys[i], ybuf.at[slot], first=i * sub)
            out_copy(b).start()
            return carry

        lax.fori_loop(b0, b0 + n, body, 0)

    @pl.when(e == N_EXPERTS - 1)
    def _():
        for back in range(depth, 0, -1):
            @pl.when(total - back >= 0)
            def _():
                out_copy(total - back).wait()


def _experts(first_blk, n_blk, counts, xs, w_exp_up, w_exp_down, run_after):
    rb = ROW_BLK * ROW_SUB
    grid_spec = pltpu.PrefetchScalarGridSpec(
        num_scalar_prefetch=3,
        grid=(N_EXPERTS,),
        in_specs=[pl.BlockSpec(memory_space=pl.ANY),
                  pl.BlockSpec((1, D_MODEL, 2 * EXPERT_HIDDEN), lambda e, fb, nb, ct: (e, 0, 0)),
                  pl.BlockSpec((1, EXPERT_HIDDEN, D_MODEL), lambda e, fb, nb, ct: (e, 0, 0)),
                  pl.BlockSpec(memory_space=pl.ANY)],
        out_specs=pl.BlockSpec(memory_space=pl.ANY),
        scratch_shapes=[pltpu.VMEM((D_MODEL, 2 * EXPERT_HIDDEN), BF16),
                        pltpu.VMEM((EXPERT_HIDDEN, D_MODEL), BF16),
                        pltpu.VMEM((EXPERT_SLOTS, rb, LANES), jnp.uint32),
                        pltpu.VMEM((EXPERT_SLOTS, rb, LANES), jnp.uint32),
                        pltpu.SemaphoreType.DMA((EXPERT_SLOTS,)),
                        pltpu.SemaphoreType.DMA((EXPERT_SLOTS,))],
    )
    return pl.pallas_call(
        _experts_kernel,
        grid_spec=grid_spec,
        out_shape=jax.ShapeDtypeStruct(xs.shape, jnp.uint32),
        compiler_params=pltpu.CompilerParams(dimension_semantics=("arbitrary",),
                                             vmem_limit_bytes=VMEM_LIMIT),
        name="experts",
    )(first_blk, n_blk, counts, xs, w_exp_up, w_exp_down, run_after)


def _shared_kernel(h1p_ref, wsu_ref, wsd_ref, o_ref):
    tm = o_ref.shape[0]
    hb = jnp.concatenate([c.astype(BF16) for c in _unpack_rows(h1p_ref, tm)], axis=1)
    gu = _dot(hb, wsu_ref[...])
    g = gu[:, :SHARED_HIDDEN]
    hid = g * _sigmoid(g) * gu[:, SHARED_HIDDEN:]
    o_ref[...] = _dot(hid.astype(BF16), wsd_ref[...]).astype(BF16)


def _shared(h1p, wsu, wsd):
    T = h1p.shape[0] // ROW_SUB
    tm = SHARED_TM
    full = lambda a: pl.BlockSpec(a.shape, lambda i: (0, 0))
    row = pl.BlockSpec((tm, D_MODEL), lambda i: (i, 0))
    return pl.pallas_call(
        _shared_kernel,
        grid=(T // tm,),
        in_specs=[pl.BlockSpec((tm * ROW_SUB, LANES), lambda i: (i, 0)), full(wsu), full(wsd)],
        out_specs=row,
        out_shape=jax.ShapeDtypeStruct((T, D_MODEL), BF16),
        compiler_params=pltpu.CompilerParams(dimension_semantics=("parallel",),
                                             vmem_limit_bytes=VMEM_LIMIT),
        name="shared",
    )(h1p, wsu, wsd)


def _combine_kernel(w_ref, h1_ref, sh_ref, yg_ref, g2_ref, b2_ref, *rest):
    o_ref = rest[-1]
    tm = h1_ref.shape[0]
    h1 = h1_ref[...]
    ffn = sh_ref[...].astype(F32)

    w = w_ref[...]
    acc = [None] * (2 * ROW_SUB)
    for k in range(TOP_K):
        wk = w[:, k:k + 1]
        chunks = _unpack_rows(yg_ref.at[k], tm)
        for c in range(2 * ROW_SUB):
            acc[c] = wk * chunks[c] if acc[c] is None else acc[c] + wk * chunks[c]
    routed = jnp.concatenate(acc, axis=1)
    z = DEEPNORM_ALPHA * h1 + (routed + ffn)
    mu = jnp.mean(z, axis=-1, keepdims=True)
    zc = z - mu
    var = jnp.mean(zc * zc, axis=-1, keepdims=True)
    o_ref[...] = zc * lax.rsqrt(var + LN_EPS) * g2_ref[...] + b2_ref[...]


def _combine(w_tok, h1, sh, yg, g2, b2, first_token, out_prev):
    n_tok = yg.shape[1] // ROW_SUB
    tm = COMBINE_TM
    first = first_token // tm
    full = lambda a: pl.BlockSpec(a.shape, lambda i: (0, 0))
    row = pl.BlockSpec((tm, D_MODEL), lambda i: (first + i, 0))
    in_specs = [pl.BlockSpec((tm, TOP_K), lambda i: (first + i, 0)), row, row,
                pl.BlockSpec((TOP_K, tm * ROW_SUB, LANES), lambda i: (0, i, 0)),
                full(g2), full(b2)]
    args = [w_tok, h1, sh, yg, g2, b2]
    aliases = {}
    if out_prev is not None:
        in_specs.append(pl.BlockSpec(memory_space=pl.ANY))
        args.append(out_prev)
        aliases = {len(args) - 1: 0}
    return pl.pallas_call(
        _combine_kernel,
        grid=(n_tok // tm,),
        in_specs=in_specs,
        out_specs=row,
        out_shape=jax.ShapeDtypeStruct(h1.shape, F32),
        input_output_aliases=aliases,
        compiler_params=pltpu.CompilerParams(dimension_semantics=("parallel",),
                                             vmem_limit_bytes=VMEM_LIMIT),
        name="combine",
    )(*args)


def _prep_weights(w_in, w_uq, w_ukv, seq):
    z = lambda n: jnp.zeros((D_MODEL, n), F32)
    kr = w_in[:, 512:544]
    qs = w_in[:, 544:1056] * (LOG2E / math.sqrt(SWA_HEAD_DIM))
    ks0, ks1 = w_in[:, 1056:1120], w_in[:, 1120:1184]
    half = MLA_ROPE // 2
    w1 = jnp.concatenate([
        w_in[:, 0:512], qs,
        ks0, ks1, ks1, ks0,
        z(64), kr, z(32),
        z(64), -kr[:, half:], kr[:, :half], z(32)], axis=1).astype(BF16)
    wvst = w_in[:, 1184:1312].T.astype(BF16)
    wg = w_in[:, 1312:3360].astype(BF16)

    pad3 = lambda a, lo, hi: jnp.pad(a, ((0, 0), (0, 0), (lo, hi)))
    wq3 = w_uq.reshape(MLA_Q_LORA, MLA_HEADS, MLA_NOPE + MLA_ROPE)
    rope = wq3[:, :, MLA_NOPE:]
    rot = jnp.concatenate([-rope[:, :, half:], rope[:, :, :half]], axis=2)
    wq2 = jnp.concatenate([pad3(wq3, 0, 32).reshape(MLA_Q_LORA, -1),
                           pad3(rot, MLA_NOPE, 32).reshape(MLA_Q_LORA, -1)], axis=1).astype(BF16)
    wkv3 = w_ukv.reshape(MLA_KV_LORA, MLA_HEADS, MLA_NOPE + MLA_V)
    wk = pad3(wkv3[:, :, :MLA_NOPE], 0, 64).reshape(MLA_KV_LORA, -1).astype(BF16)
    wvt = wkv3[:, :, MLA_NOPE:].reshape(MLA_KV_LORA, -1).T.astype(BF16)

    inv_freq = ROPE_THETA ** (-jnp.arange(0, MLA_ROPE, 2, dtype=F32) / MLA_ROPE)
    ang = jnp.arange(seq, dtype=F32)[:, None] * inv_freq[None, :]
    cos, sin = jnp.cos(ang), jnp.sin(ang)
    one, zero = jnp.ones((seq, 64), F32), jnp.zeros((seq, 64), F32)
    z32 = jnp.zeros((seq, 32), F32)
    scale = LOG2E / math.sqrt(MLA_NOPE + MLA_ROPE)
    tabs = (jnp.concatenate([one, cos, cos, z32], axis=1) * scale,
            jnp.concatenate([zero, sin, sin, z32], axis=1) * scale,
            jnp.concatenate([zero, cos, cos, z32], axis=1),
            jnp.concatenate([zero, sin, sin, z32], axis=1))
    return w1, wvst, wg, wq2, wk, wvt, tabs


def kernel(x, w_in, mla_q_norm, mla_kv_norm, w_uq, w_ukv, attn_sinks, w_o_mla, w_o_swa, w_out,
           ln1_g, ln1_b, w_router, router_bias, w_exp_up, w_exp_down, w_sh_up, w_sh_down,
           ln2_g, ln2_b):
    batch, seq, _ = x.shape
    T = batch * seq
    x2 = x.reshape(T, D_MODEL)
    w1, wvst, wg, wq2, wk, wvt, tabs = _prep_weights(w_in, w_uq, w_ukv, seq)
    q, k, vt, qs, ks, vst = _proj(x2, w1, wvst, wq2, wk, wvt, mla_q_norm.reshape(1, -1),
                                  mla_kv_norm.reshape(1, -1), tabs, seq)
    o_mla = _mla(q, k, vt, batch, seq)
    o_swa = _swa(attn_sinks.astype(F32), qs, ks, vst, batch, seq)

    tri = (lax.broadcasted_iota(jnp.int32, (ROUTE_TN, ROUTE_TN), 0)
           < lax.broadcasted_iota(jnp.int32, (ROUTE_TN, ROUTE_TN), 1)).astype(BF16)
    h1, h1p, e_t, w_t, r_t, cnt = _post(
        x2, o_mla, o_swa, w_o_mla.astype(BF16), w_o_swa.astype(BF16), wg, w_out.astype(BF16),
        ln1_g.reshape(1, -1), ln1_b.reshape(1, -1), w_router.T.astype(BF16),
        router_bias.reshape(-1, 1).astype(F32), tri)

    counts = cnt[:, 0]
    pad = (counts + ROW_BLK - 1) // ROW_BLK * ROW_BLK
    pad_start = jnp.cumsum(pad) - pad
    n_rows = (T * TOP_K // ROW_BLK + N_EXPERTS) * ROW_BLK
    dest = _dest(pad_start.astype(jnp.int32), e_t, r_t)
    idx = (dest.reshape(TOP_K, T // SC_CHUNK, SC_CHUNK, 1) * ROW_SUB
           + jnp.arange(ROW_SUB, dtype=jnp.int32))
    idx = idx.transpose(1, 0, 2, 3).reshape(T // SC_CHUNK, TOP_K, SC_CHUNK * ROW_SUB)
    xs = _sc_dispatch(h1p, idx, n_rows)
    sh = _shared(h1p, w_sh_up.astype(BF16), w_sh_down.astype(BF16))
    ys = _experts((pad_start // ROW_BLK).astype(jnp.int32), (pad // ROW_BLK).astype(jnp.int32),
                  counts.astype(jnp.int32), xs, w_exp_up, w_exp_down, sh)

    Tc = T // COMBINE_PARTS
    n_chunks = Tc // SC_CHUNK
    w_tok = w_t.T
    g2, b2 = ln2_g.reshape(1, -1), ln2_b.reshape(1, -1)
    out = None
    for cpart in range(COMBINE_PARTS):
        yg = _sc_gather(ys, idx[cpart * n_chunks:(cpart + 1) * n_chunks], Tc)
        out = _combine(w_tok, h1, sh, yg, g2, b2, cpart * Tc, out)
    return out.reshape(batch, seq, D_MODEL)
```

```python
import math

import jax
import jax.numpy as jnp
from jax import lax
from jax.experimental import pallas as pl
from jax.experimental.pallas import tpu as pltpu
from jax.experimental.pallas import tpu_sc as plsc

D_MODEL = 1024
MLA_HEADS = 8
MLA_Q_LORA = 256
MLA_KV_LORA = 256
MLA_NOPE = 64
MLA_ROPE = 32
MLA_V = 64
VT_ROWS = 80
ROPE_THETA = 10000.0
SWA_HEADS = 8
SWA_KV_HEADS = 2
SWA_GROUP = SWA_HEADS // SWA_KV_HEADS
SWA_HEAD_DIM = 64
SWA_WINDOW = 128
SWA_TQ = 256
N_EXPERTS = 256
TOP_K = 8
N_GROUPS = 8
GROUP_SIZE = N_EXPERTS // N_GROUPS
TOPK_GROUPS = 4
EXPERT_HIDDEN = 256
SHARED_HIDDEN = 256
ROUTED_SCALE = 2.5
DEEPNORM_ALPHA = 2.0 ** 0.25
LN_EPS = 1e-5
RMS_EPS = 1e-6

LANES = 128
ROW_WORDS = D_MODEL // 2
ROW_SUB = ROW_WORDS // LANES
VMEM_LIMIT = 48 * 1024 * 1024

PROJ_TM = 512
MLA_TQ = 256
MLA_TK = 512
MLA_HPS = 8
POST_TM = 512
ROUTE_TN = 256
ROW_BLK = 256
EXPERT_SLOTS = 4
EXPERT_SPLIT = 2
SHARED_TM = 512
COMBINE_TM = 256
SC_CHUNK = 32
COMBINE_PARTS = 4

BF16 = jnp.bfloat16
F32 = jnp.float32
NEG_INF = float("-inf")
LOG2E = math.log2(math.e)


def _sigmoid(v):
    return 1.0 / (1.0 + jnp.exp(-v))


def _dot(a, b):
    return jnp.dot(a, b, preferred_element_type=F32)


def _dot_nt(a, b):
    return lax.dot_general(a, b, (((1,), (1,)), ((), ())), preferred_element_type=F32)


def _pack_rows(y, out_ref, first=0):
    rows = y.shape[0]
    for j in range(ROW_SUB):
        a = y[:, j * LANES:(j + 1) * LANES].astype(BF16).astype(F32)
        b = y[:, ROW_WORDS + j * LANES:ROW_WORDS + (j + 1) * LANES].astype(BF16).astype(F32)
        ua = pltpu.bitcast(a, jnp.uint32) >> 16
        ub = pltpu.bitcast(b, jnp.uint32)
        out_ref[pl.ds(first * ROW_SUB + j, rows, stride=ROW_SUB), :] = ua | ub


def _unpack_rows(ref, rows, first=0, n_valid=None):
    lo, hi = [], []
    if n_valid is not None:
        live = first + lax.broadcasted_iota(jnp.int32, (rows, LANES), 0) < n_valid
    for j in range(ROW_SUB):
        u = ref[pl.ds(first * ROW_SUB + j, rows, stride=ROW_SUB), :]
        if n_valid is not None:
            u = jnp.where(live, u, jnp.uint32(0))
        lo.append(pltpu.bitcast(u << 16, F32))
        hi.append(pltpu.bitcast(u & jnp.uint32(0xFFFF0000), F32))
    return lo + hi


def _proj_kernel(x_ref, w1_ref, wvst_ref, wq_ref, wk_ref, wvt_ref, gq_ref, gkv_ref,
                 cq_ref, sq_ref, ck_ref, sk_ref,
                 q_ref, k_ref, vt_ref, qs_ref, ks_ref, vst_ref):
    xb = x_ref[...].astype(BF16)
    p = _dot(xb, w1_ref[...])

    def rms(c, g):
        return c * lax.rsqrt(jnp.mean(c * c, axis=-1, keepdims=True) + RMS_EPS) * g

    cqn = rms(p[:, 0:256], gq_ref[...]).astype(BF16)
    ckvn = rms(p[:, 256:512], gkv_ref[...]).astype(BF16)
    qs_ref[...] = p[:, 512:1024].astype(BF16)
    k01 = p[:, 1024:1152].astype(BF16)
    k10 = p[:, 1152:1280].astype(BF16)
    lane_lo = lax.broadcasted_iota(jnp.int32, k01.shape, 1) < SWA_HEAD_DIM
    zero = jnp.zeros_like(k01)
    ks_ref[:, 0 * LANES:1 * LANES] = jnp.where(lane_lo, k01, zero)
    ks_ref[:, 1 * LANES:2 * LANES] = jnp.where(lane_lo, zero, k10)
    ks_ref[:, 2 * LANES:3 * LANES] = jnp.where(lane_lo, k10, zero)
    ks_ref[:, 3 * LANES:4 * LANES] = jnp.where(lane_lo, zero, k01)
    ones = jnp.ones((VT_ROWS - MLA_V, xb.shape[0]), BF16)
    vst = _dot_nt(wvst_ref[...], xb).astype(BF16)
    for h in range(SWA_KV_HEADS):
        vst_ref[h * VT_ROWS:h * VT_ROWS + SWA_HEAD_DIM, :] = vst[h * SWA_HEAD_DIM:(h + 1) * SWA_HEAD_DIM, :]
        vst_ref[h * VT_ROWS + SWA_HEAD_DIM:(h + 1) * VT_ROWS, :] = ones
    kr = p[:, 1280:1408] * ck_ref[...] + p[:, 1408:1536] * sk_ref[...]
    qq = _dot(cqn, wq_ref[...])
    kn = _dot(ckvn, wk_ref[...])
    cq = cq_ref[...]
    sq = sq_ref[...]
    for h in range(MLA_HEADS):
        a = qq[:, h * LANES:(h + 1) * LANES]
        b = qq[:, 1024 + h * LANES:1024 + (h + 1) * LANES]
        q_ref[:, h * LANES:(h + 1) * LANES] = (a * cq + b * sq).astype(BF16)
        k_ref[:, h * LANES:(h + 1) * LANES] = (kn[:, h * LANES:(h + 1) * LANES] + kr).astype(BF16)
    vt = _dot_nt(wvt_ref[...], ckvn).astype(BF16)
    for h in range(MLA_HEADS):
        vt_ref[h * VT_ROWS:h * VT_ROWS + MLA_V, :] = vt[h * MLA_V:(h + 1) * MLA_V, :]
        vt_ref[h * VT_ROWS + MLA_V:(h + 1) * VT_ROWS, :] = ones


def _proj(x2, w1, wvst, wq2, wk, wvt, gq, gkv, tabs, seq):
    T = x2.shape[0]
    tm = PROJ_TM
    nper = seq // tm
    full = lambda shape: pl.BlockSpec(shape, lambda i: (0, 0))
    tab = pl.BlockSpec((tm, LANES), lambda i: (i % nper, 0))
    row = lambda n: pl.BlockSpec((tm, n), lambda i: (i, 0))
    col = lambda n: pl.BlockSpec((n, tm), lambda i: (0, i))
    return pl.pallas_call(
        _proj_kernel,
        grid=(T // tm,),
        in_specs=[row(D_MODEL), full(w1.shape), full(wvst.shape), full(wq2.shape), full(wk.shape),
                  full(wvt.shape), full(gq.shape), full(gkv.shape), tab, tab, tab, tab],
        out_specs=[row(1024), row(1024), col(MLA_HEADS * VT_ROWS), row(512), row(512),
                   col(SWA_KV_HEADS * VT_ROWS)],
        out_shape=[jax.ShapeDtypeStruct((T, 1024), BF16), jax.ShapeDtypeStruct((T, 1024), BF16),
                   jax.ShapeDtypeStruct((MLA_HEADS * VT_ROWS, T), BF16), jax.ShapeDtypeStruct((T, 512), BF16),
                   jax.ShapeDtypeStruct((T, 512), BF16),
                   jax.ShapeDtypeStruct((SWA_KV_HEADS * VT_ROWS, T), BF16)],
        compiler_params=pltpu.CompilerParams(dimension_semantics=("parallel",),
                                             vmem_limit_bytes=VMEM_LIMIT),
        name="proj",
    )(x2, w1, wvst, wq2, wk, wvt, gq, gkv, *tabs)


def _mla_kernel(q_ref, k_ref, vt_ref, o_ref, *acc_scr):
    tq = q_ref.shape[0]
    qi = pl.program_id(2)
    for acc in acc_scr:
        acc[...] = jnp.zeros(acc.shape, F32)

    def step(ks, tk, maxes, masked):
        scores = [_dot_nt(k_ref[pl.ds(ks, tk), h * LANES:(h + 1) * LANES],
                          q_ref[:, h * LANES:(h + 1) * LANES]) for h in range(MLA_HPS)]
        new_maxes, probs, alphas = [], [], []
        for h in range(MLA_HPS):
            s = scores[h]
            if masked:
                key = ks + lax.broadcasted_iota(jnp.int32, s.shape, 0)
                qry = qi * tq + lax.broadcasted_iota(jnp.int32, s.shape, 1)
                s = jnp.where(key <= qry, s, NEG_INF)
            m_new = jnp.maximum(maxes[h], jnp.max(s, axis=0, keepdims=True))
            alphas.append(jnp.exp2(maxes[h] - m_new))
            probs.append(jnp.exp2(s - m_new).astype(BF16))
            new_maxes.append(m_new)
        for h in range(MLA_HPS):
            pv = _dot(vt_ref[h * VT_ROWS:(h + 1) * VT_ROWS, pl.ds(ks, tk)], probs[h])
            acc_scr[h][...] = acc_scr[h][...] * alphas[h] + pv
        return tuple(new_maxes)

    init = tuple(jnp.full((1, tq), NEG_INF, F32) for _ in range(MLA_HPS))
    big = MLA_TK
    n_big = (qi * tq) // big
    maxes = lax.fori_loop(
        0, n_big, lambda kc, mx: step(pl.multiple_of(kc * big, big), big, mx, False), init)
    rest = pl.multiple_of(n_big * big, tq)
    maxes = lax.cond(
        rest < qi * tq,
        lambda mx: step(rest, 2 * tq, mx, True),
        lambda mx: step(rest, tq, mx, True),
        maxes)
    for h2 in range(MLA_HPS // 2):
        out_t = jnp.concatenate(
            [acc_scr[2 * h2 + g][0:MLA_V, :] / acc_scr[2 * h2 + g][MLA_V:MLA_V + 1, :] for g in range(2)],
            axis=0)
        o_ref[:, h2 * LANES:(h2 + 1) * LANES] = out_t.T.astype(BF16)


def _mla(q, k, vt, batch, seq):
    T = q.shape[0]
    tq = MLA_TQ
    nq = seq // tq
    hps = MLA_HPS
    return pl.pallas_call(
        _mla_kernel,
        grid=(batch, MLA_HEADS // hps, nq),
        in_specs=[pl.BlockSpec((tq, hps * LANES), lambda b, j, i: (b * nq + i, j)),
                  pl.BlockSpec((seq, hps * LANES), lambda b, j, i: (b, j)),
                  pl.BlockSpec((hps * VT_ROWS, seq), lambda b, j, i: (j, b))],
        out_specs=pl.BlockSpec((tq, hps * MLA_V), lambda b, j, i: (b * nq + i, j)),
        out_shape=jax.ShapeDtypeStruct((T, MLA_HEADS * MLA_V), BF16),
        scratch_shapes=[pltpu.VMEM((VT_ROWS, tq), F32) for _ in range(hps)],
        compiler_params=pltpu.CompilerParams(
            dimension_semantics=("parallel", "parallel", "arbitrary"), vmem_limit_bytes=VMEM_LIMIT),
        name="mla",
    )(q, k, vt)


def _swa_kernel(sink_ref, q_ref, kc_ref, kp_ref, vtc_ref, vtp_ref, bias_ref, o_ref):
    scores = []
    for head in range(SWA_HEADS):
        pair, g, kvh = head // 2, head % 2, head // SWA_GROUP
        col = (2 * kvh + g) * LANES
        band = jnp.concatenate([kp_ref[:, col:col + LANES], kc_ref[:, col:col + LANES]], axis=0)
        scores.append(_dot_nt(band, q_ref[:, pair * LANES:(pair + 1) * LANES]))
    probs, sink_terms = [], []
    for head in range(SWA_HEADS):
        s = scores[head] + bias_ref[0, head]
        sink = sink_ref[head] * LOG2E
        m = jnp.maximum(jnp.max(s, axis=0, keepdims=True), sink)
        probs.append(jnp.exp2(s - m).astype(BF16))
        sink_terms.append(jnp.exp2(sink - m))
    outs = []
    for head in range(SWA_HEADS):
        rows = slice((head // SWA_GROUP) * VT_ROWS, (head // SWA_GROUP + 1) * VT_ROWS)
        v_band = jnp.concatenate([vtp_ref[rows, :], vtc_ref[rows, :]], axis=1)
        pv = _dot(v_band, probs[head])
        outs.append(pv[0:SWA_HEAD_DIM, :] / (pv[SWA_HEAD_DIM:SWA_HEAD_DIM + 1, :] + sink_terms[head]))
    o_ref[...] = jnp.concatenate(outs, axis=0).T.astype(BF16)


def _swa_bias():
    W = SWA_WINDOW
    j = jnp.arange(W + SWA_TQ, dtype=jnp.int32)[:, None]
    i = jnp.arange(SWA_TQ, dtype=jnp.int32)[None, :]
    dist = i + W - j
    valid = (dist >= 0) & (dist < W)
    slopes = 2.0 ** (-8.0 * jnp.arange(1, SWA_HEADS + 1, dtype=F32) / SWA_HEADS)
    pen = -(slopes[:, None, None] * dist.astype(F32)[None]) * LOG2E
    general = jnp.where(valid[None], pen, NEG_INF)
    first = jnp.where((valid & (j >= W))[None], pen, NEG_INF)
    return jnp.stack([first, general])


def _swa(sinks, qs, ks, vst, batch, seq):
    T = qs.shape[0]
    W, tq = SWA_WINDOW, SWA_TQ
    nb = seq // tq
    wpb = tq // W
    before = lambda b, i: b * nb * wpb + jnp.maximum(wpb * i - 1, 0)
    cur = lambda n: pl.BlockSpec((tq, n), lambda b, i: (b * nb + i, 0))
    prev = lambda n: pl.BlockSpec((W, n), lambda b, i: (before(b, i), 0))
    vt_cur = pl.BlockSpec((SWA_KV_HEADS * VT_ROWS, tq), lambda b, i: (0, b * nb + i))
    vt_prev = pl.BlockSpec((SWA_KV_HEADS * VT_ROWS, W), lambda b, i: (0, before(b, i)))
    bias = pl.BlockSpec((1, SWA_HEADS, W + tq, tq), lambda b, i: (jnp.minimum(i, 1), 0, 0, 0))
    return pl.pallas_call(
        _swa_kernel,
        grid=(batch, nb),
        in_specs=[pl.BlockSpec(memory_space=pltpu.SMEM), cur(512), cur(512), prev(512),
                  vt_cur, vt_prev, bias],
        out_specs=cur(512),
        out_shape=jax.ShapeDtypeStruct((T, 512), BF16),
        compiler_params=pltpu.CompilerParams(dimension_semantics=("parallel", "parallel"),
                                             vmem_limit_bytes=VMEM_LIMIT),
        name="swa",
    )(sinks, qs, ks, ks, vst, vst, _swa_bias())


def _post_kernel(x_ref, om_ref, os_ref, wom_ref, wos_ref, wg_ref, wout_ref, g1_ref, b1_ref,
                 wr_ref, rb_ref, tri_ref,
                 h1_ref, h1p_ref, e_ref, w_ref, r_ref, cnt_ref, carry_scr):
    tm = x_ref.shape[0]
    step = pl.program_id(0)

    @pl.when(step == 0)
    def _():
        carry_scr[...] = jnp.zeros(carry_scr.shape, F32)

    x = x_ref[...]
    ya = _dot(om_ref[...], wom_ref[...])
    yb = _dot(os_ref[...], wos_ref[...])
    gates = _dot(x.astype(BF16), wg_ref[...])
    merged = _sigmoid(gates[:, :D_MODEL]) * ya + _sigmoid(gates[:, D_MODEL:]) * yb
    mix = _dot(merged.astype(BF16), wout_ref[...])
    z = DEEPNORM_ALPHA * x + mix
    mu = jnp.mean(z, axis=-1, keepdims=True)
    zc = z - mu
    var = jnp.mean(zc * zc, axis=-1, keepdims=True)
    h1 = zc * lax.rsqrt(var + LN_EPS) * g1_ref[...] + b1_ref[...]
    h1_ref[...] = h1
    _pack_rows(h1, h1p_ref)

    all_scores = _sigmoid(_dot_nt(wr_ref[...], h1.astype(BF16)))
    carry = carry_scr[...]
    for c in range(tm // ROUTE_TN):
        cols = slice(c * ROUTE_TN, (c + 1) * ROUTE_TN)
        idxs, weights, ranks, carry = _route(all_scores[:, cols], rb_ref[...], tri_ref[...], carry)
        e_ref[:, cols] = idxs
        w_ref[:, cols] = weights
        r_ref[:, cols] = ranks
    carry_scr[...] = carry
    cnt_ref[...] = carry.astype(jnp.int32)


def _route(scores, bias, tri, carry):
    tn = scores.shape[1]
    choice = scores + bias
    row = lax.broadcasted_iota(jnp.int32, (N_EXPERTS, tn), 0)
    grow = lax.broadcasted_iota(jnp.int32, (GROUP_SIZE, tn), 0)
    gscore = []
    for g in range(N_GROUPS):
        blk = choice[g * GROUP_SIZE:(g + 1) * GROUP_SIZE, :]
        m1 = jnp.max(blk, axis=0, keepdims=True)
        i1 = jnp.min(jnp.where(blk == m1, grow, GROUP_SIZE), axis=0, keepdims=True)
        m2 = jnp.max(jnp.where(grow == i1, NEG_INF, blk), axis=0, keepdims=True)
        gscore.append(m1 + m2)
    gsc = jnp.concatenate(gscore, axis=0)
    gidx = lax.broadcasted_iota(jnp.int32, (N_GROUPS, tn), 0)
    grank = jnp.zeros((N_GROUPS, tn), jnp.int32)
    for g in range(N_GROUPS):
        sg = gsc[g:g + 1, :]
        beats = (sg > gsc) | ((sg == gsc) & (gidx > g))
        grank = grank + beats.astype(jnp.int32)
    gsel = (grank < TOPK_GROUPS).astype(F32)
    emask = jnp.concatenate(
        [jnp.broadcast_to(gsel[g:g + 1, :], (GROUP_SIZE, tn)) for g in range(N_GROUPS)], axis=0)
    work = jnp.where(emask > 0.0, choice, NEG_INF)
    eligible = work
    idxs, svals = [], []
    for _k in range(TOP_K):
        m = jnp.max(work, axis=0, keepdims=True)
        idx = jnp.min(jnp.where(work == m, row, N_EXPERTS), axis=0, keepdims=True)
        hit = row == idx
        svals.append(jnp.sum(jnp.where(hit, scores, 0.0), axis=0, keepdims=True))
        work = jnp.where(hit, NEG_INF, work)
        idxs.append(idx)
    sel = jnp.where(work != eligible, 1.0, 0.0)
    ssum = svals[0]
    for sv in svals[1:]:
        ssum = ssum + sv
    weights = jnp.concatenate([sv / ssum * ROUTED_SCALE for sv in svals], axis=0)

    rank = _dot(sel.astype(BF16), tri) + carry[:, 0:1]
    ranks = jnp.concatenate(
        [jnp.sum(jnp.where(row == idx, rank, 0.0), axis=0, keepdims=True) for idx in idxs],
        axis=0).astype(jnp.int32)
    carry = carry + jnp.sum(sel, axis=1, keepdims=True)
    return jnp.concatenate(idxs, axis=0), weights, ranks, carry


def _post(x2, o_mla, o_swa, wom, wos, wg, wout, g1, b1, wr_t, rbias, tri):
    T = x2.shape[0]
    tm = POST_TM
    full = lambda a: pl.BlockSpec(a.shape, lambda i: (0, 0))
    row = lambda n: pl.BlockSpec((tm, n), lambda i: (i, 0))
    col = pl.BlockSpec((TOP_K, tm), lambda i: (0, i))
    return pl.pallas_call(
        _post_kernel,
        grid=(T // tm,),
        in_specs=[row(D_MODEL), row(512), row(512), full(wom), full(wos), full(wg), full(wout),
                  full(g1), full(b1), full(wr_t), full(rbias), full(tri)],
        out_specs=[row(D_MODEL), pl.BlockSpec((tm * ROW_SUB, LANES), lambda i: (i, 0)), col, col, col,
                   pl.BlockSpec((N_EXPERTS, LANES), lambda i: (0, 0))],
        out_shape=[jax.ShapeDtypeStruct((T, D_MODEL), F32),
                   jax.ShapeDtypeStruct((T * ROW_SUB, LANES), jnp.uint32),
                   jax.ShapeDtypeStruct((TOP_K, T), jnp.int32),
                   jax.ShapeDtypeStruct((TOP_K, T), F32),
                   jax.ShapeDtypeStruct((TOP_K, T), jnp.int32),
                   jax.ShapeDtypeStruct((N_EXPERTS, LANES), jnp.int32)],
        scratch_shapes=[pltpu.VMEM((N_EXPERTS, LANES), F32)],
        compiler_params=pltpu.CompilerParams(dimension_semantics=("arbitrary",),
                                             vmem_limit_bytes=VMEM_LIMIT),
        name="post",
    )(x2, o_mla, o_swa, wom, wos, wg, wout, g1, b1, wr_t, rbias, tri)


def _dest_kernel(start_ref, e_ref, r_ref, o_ref):
    e = e_ref[...]
    base = jnp.zeros(e.shape, jnp.int32)
    for j in range(N_EXPERTS):
        base = jnp.where(e == j, start_ref[j], base)
    o_ref[...] = base + r_ref[...]


def _dest(seg_start, e_t, r_t):
    T = e_t.shape[1]
    tn = min(T, 4096)
    col = pl.BlockSpec((TOP_K, tn), lambda i: (0, i))
    return pl.pallas_call(
        _dest_kernel,
        grid=(T // tn,),
        in_specs=[pl.BlockSpec(memory_space=pltpu.SMEM), col, col],
        out_specs=col,
        out_shape=jax.ShapeDtypeStruct((TOP_K, T), jnp.int32),
        compiler_params=pltpu.CompilerParams(dimension_semantics=("parallel",)),
        name="dest",
    )(seg_start, e_t, r_t)


def _sc_worker_chunks(n_tokens):
    info = plsc.get_sparse_core_info()
    n_workers = info.num_cores * info.num_subcores
    per_worker = n_tokens // SC_CHUNK // n_workers
    assert per_worker * n_workers * SC_CHUNK == n_tokens
    first = (lax.axis_index("s") * info.num_cores + lax.axis_index("c")) * per_worker
    return first, per_worker


def _sc_dispatch(h1p, idx, n_rows):
    n_tokens = h1p.shape[0] // ROW_SUB
    R = SC_CHUNK * ROW_SUB

    def body(h_hbm, idx_hbm, xs_hbm, idx_v, rows_v, load_sem, scatter_sem):
        first, per_worker = _sc_worker_chunks(n_tokens)
        assert per_worker % 2 == 0

        def loads(c, slot):
            return (pltpu.make_async_copy(idx_hbm.at[c], idx_v.at[slot], load_sem.at[slot]),
                    pltpu.make_async_copy(h_hbm.at[pl.ds(c * R, R)], rows_v.at[slot], load_sem.at[slot]))

        for cp in loads(first, 0):
            cp.start()

        @pl.loop(0, per_worker, step=2)
        def _(ci):
            for slot in range(2):
                c = first + ci + slot
                for cp in loads(c, slot):
                    cp.wait()

                @pl.when(ci + slot + 1 < per_worker)
                def _():
                    for cp in loads(c + 1, 1 - slot):
                        cp.start()

                copies = [pltpu.make_async_copy(rows_v.at[slot], xs_hbm.at[idx_v.at[slot, k]], scatter_sem)
                          for k in range(TOP_K)]
                for cp in copies:
                    cp.start()
                for cp in copies:
                    cp.wait()

    run = pl.kernel(
        body, out_type=jax.ShapeDtypeStruct((n_rows * ROW_SUB, LANES), jnp.uint32),
        mesh=plsc.VectorSubcoreMesh(core_axis_name="c", subcore_axis_name="s"),
        scratch_types=[pltpu.VMEM((2, TOP_K, R), jnp.int32), pltpu.VMEM((2, R, LANES), jnp.uint32),
                       pltpu.SemaphoreType.DMA((2,)), pltpu.SemaphoreType.DMA],
        name="sc_dispatch")
    return run(h1p, idx)


def _sc_gather(ys, idx, n_tokens):
    R = SC_CHUNK * ROW_SUB
    n_groups = TOP_K // 2

    def body(ys_hbm, idx_hbm, yg_hbm, idx_v, buf, isem, gsem, wsem):
        first, per_worker = _sc_worker_chunks(n_tokens)
        assert per_worker % 2 == 0

        def idx_load(c, islot):
            return pltpu.make_async_copy(idx_hbm.at[c], idx_v.at[islot], isem.at[islot])

        def gathers(islot, q):
            s = q % 2
            return [pltpu.make_async_copy(ys_hbm.at[idx_v.at[islot, 2 * q + j]], buf.at[s, j], gsem.at[s])
                    for j in range(2)]

        def writes(c, q):
            s = q % 2
            return [pltpu.make_async_copy(buf.at[s, j], yg_hbm.at[2 * q + j, pl.ds(c * R, R)], wsem.at[s])
                    for j in range(2)]

        idx_load(first, 0).start()

        @pl.loop(0, per_worker, step=2)
        def _(ci):
            for islot in range(2):
                c = first + ci + islot
                idx_load(c, islot).wait()
                for q in range(n_groups):
                    if q >= 2:
                        for cp in writes(c, q - 2):
                            cp.wait()
                    else:
                        @pl.when(ci + islot > 0)
                        def _():
                            for cp in writes(c - 1, q + 2):
                                cp.wait()
                    for cp in gathers(islot, q):
                        cp.start()
                    if q >= 1:
                        for cp in gathers(islot, q - 1):
                            cp.wait()
                        for cp in writes(c, q - 1):
                            cp.start()
                    else:
                        @pl.when(ci + islot > 0)
                        def _():
                            for cp in gathers(1 - islot, n_groups - 1):
                                cp.wait()
                            for cp in writes(c - 1, n_groups - 1):
                                cp.start()

                        @pl.when(ci + islot + 1 < per_worker)
                        def _():
                            idx_load(c + 1, 1 - islot).start()

        last = first + per_worker - 1
        for cp in gathers(1, n_groups - 1):
            cp.wait()
        for cp in writes(last, n_groups - 1):
            cp.start()
        for q in (n_groups - 2, n_groups - 1):
            for cp in writes(last, q):
                cp.wait()

    run = pl.kernel(
        body, out_type=jax.ShapeDtypeStruct((TOP_K, n_tokens * ROW_SUB, LANES), jnp.uint32),
        mesh=plsc.VectorSubcoreMesh(core_axis_name="c", subcore_axis_name="s"),
        scratch_types=[pltpu.VMEM((2, TOP_K, R), jnp.int32), pltpu.VMEM((2, 2, R, LANES), jnp.uint32),
                       pltpu.SemaphoreType.DMA((2,)), pltpu.SemaphoreType.DMA((2,)),
                       pltpu.SemaphoreType.DMA((2,))],
        name="sc_gather")
    return run(ys, idx)


def _experts_kernel(first_ref, nblk_ref, cnt_ref, xs_hbm, wup_ref, wdn_ref, after_ref, ys_hbm,
                    wup_bf, wdn_bf, xbuf, ybuf, in_sem, out_sem):
    del after_ref
    e = pl.program_id(0)
    n = nblk_ref[e]
    b0 = first_ref[e]
    count = cnt_ref[e]
    total = first_ref[N_EXPERTS - 1] + nblk_ref[N_EXPERTS - 1]
    rb = ROW_BLK * ROW_SUB
    depth = EXPERT_SLOTS

    def slot_of(b):
        return lax.rem(b, depth)

    def rows_of(b):
        return pl.ds(pl.multiple_of(b * rb, rb), rb)

    def in_copy(b):
        return pltpu.make_async_copy(xs_hbm.at[rows_of(b)], xbuf.at[slot_of(b)], in_sem.at[slot_of(b)])

    def out_copy(b):
        return pltpu.make_async_copy(ybuf.at[slot_of(b)], ys_hbm.at[rows_of(b)], out_sem.at[slot_of(b)])

    @pl.when(e == 0)
    def _():
        for b in range(depth - 1):
            @pl.when(b < total)
            def _():
                in_copy(b).start()

    @pl.when(n > 0)
    def _():
        wup_bf[...] = wup_ref[0].astype(BF16)
        wdn_bf[...] = wdn_ref[0].astype(BF16)

        def body(b, carry):
            slot = slot_of(b)
            in_copy(b).wait()

            @pl.when(b + depth - 1 < total)
            def _():
                in_copy(b + depth - 1).start()

            @pl.when(b >= depth)
            def _():
                out_copy(b - depth).wait()

            sub = ROW_BLK // EXPERT_SPLIT
            n_valid = count - (b - b0) * ROW_BLK
            xbs = [jnp.concatenate(
                [c.astype(BF16) for c in _unpack_rows(xbuf.at[slot], sub, first=i * sub, n_valid=n_valid)],
                axis=1) for i in range(EXPERT_SPLIT)]
            gus = [_dot(xb, wup_bf[...]) for xb in xbs]
            hids = [(gu[:, :EXPERT_HIDDEN] * _sigmoid(gu[:, :EXPERT_HIDDEN])
                     * gu[:, EXPERT_HIDDEN:]).astype(BF16) for gu in gus]
            ys = [_dot(hid, wdn_bf[...]) for hid in hids]
            for i in range(EXPERT_SPLIT):
                _pack_rows(ys[i], ybuf.at[slot], first=i * sub)
            out_copy(b).start()
            return carry

        lax.fori_loop(b0, b0 + n, body, 0)

    @pl.when(e == N_EXPERTS - 1)
    def _():
        for back in range(depth, 0, -1):
            @pl.when(total - back >= 0)
            def _():
                out_copy(total - back).wait()


def _experts(first_blk, n_blk, counts, xs, w_exp_up, w_exp_down, run_after):
    rb = ROW_BLK * ROW_SUB
    grid_spec = pltpu.PrefetchScalarGridSpec(
        num_scalar_prefetch=3,
        grid=(N_EXPERTS,),
        in_specs=[pl.BlockSpec(memory_space=pl.ANY),
                  pl.BlockSpec((1, D_MODEL, 2 * EXPERT_HIDDEN), lambda e, fb, nb, ct: (e, 0, 0)),
                  pl.BlockSpec((1, EXPERT_HIDDEN, D_MODEL), lambda e, fb, nb, ct: (e, 0, 0)),
                  pl.BlockSpec(memory_space=pl.ANY)],
        out_specs=pl.BlockSpec(memory_space=pl.ANY),
        scratch_shapes=[pltpu.VMEM((D_MODEL, 2 * EXPERT_HIDDEN), BF16),
                        pltpu.VMEM((EXPERT_HIDDEN, D_MODEL), BF16),
                        pltpu.VMEM((EXPERT_SLOTS, rb, LANES), jnp.uint32),
                        pltpu.VMEM((EXPERT_SLOTS, rb, LANES), jnp.uint32),
                        pltpu.SemaphoreType.DMA((EXPERT_SLOTS,)),
                        pltpu.SemaphoreType.DMA((EXPERT_SLOTS,))],
    )
    return pl.pallas_call(
        _experts_kernel,
        grid_spec=grid_spec,
        out_shape=jax.ShapeDtypeStruct(xs.shape, jnp.uint32),
        compiler_params=pltpu.CompilerParams(dimension_semantics=("arbitrary",),
                                             vmem_limit_bytes=VMEM_LIMIT),
        name="experts",
    )(first_blk, n_blk, counts, xs, w_exp_up, w_exp_down, run_after)


def _shared_kernel(h1p_ref, wsu_ref, wsd_ref, o_ref):
    tm = o_ref.shape[0]
    hb = jnp.concatenate([c.astype(BF16) for c in _unpack_rows(h1p_ref, tm)], axis=1)
    gu = _dot(hb, wsu_ref[...])
    g = gu[:, :SHARED_HIDDEN]
    hid = g * _sigmoid(g) * gu[:, SHARED_HIDDEN:]
    o_ref[...] = _dot(hid.astype(BF16), wsd_ref[...]).astype(BF16)


def _shared(h1p, wsu, wsd):
    T = h1p.shape[0] // ROW_SUB
    tm = SHARED_TM
    full = lambda a: pl.BlockSpec(a.shape, lambda i: (0, 0))
    row = pl.BlockSpec((tm, D_MODEL), lambda i: (i, 0))
    return pl.pallas_call(
        _shared_kernel,
        grid=(T // tm,),
        in_specs=[pl.BlockSpec((tm * ROW_SUB, LANES), lambda i: (i, 0)), full(wsu), full(wsd)],
        out_specs=row,
        out_shape=jax.ShapeDtypeStruct((T, D_MODEL), BF16),
        compiler_params=pltpu.CompilerParams(dimension_semantics=("parallel",),
                                             vmem_limit_bytes=VMEM_LIMIT),
        name="shared",
    )(h1p, wsu, wsd)


def _combine_kernel(w_ref, h1_ref, sh_ref, yg_ref, g2_ref, b2_ref, *rest):
    o_ref = rest[-1]
    tm = h1_ref.shape[0]
    h1 = h1_ref[...]
    ffn = sh_ref[...].astype(F32)

    w = w_ref[...]
    acc = [None] * (2 * ROW_SUB)
    for k in range(TOP_K):
        wk = w[:, k:k + 1]
        chunks = _unpack_rows(yg_ref.at[k], tm)
        for c in range(2 * ROW_SUB):
            acc[c] = wk * chunks[c] if acc[c] is None else acc[c] + wk * chunks[c]
    routed = jnp.concatenate(acc, axis=1)
    z = DEEPNORM_ALPHA * h1 + (routed + ffn)
    mu = jnp.mean(z, axis=-1, keepdims=True)
    zc = z - mu
    var = jnp.mean(zc * zc, axis=-1, keepdims=True)
    o_ref[...] = zc * lax.rsqrt(var + LN_EPS) * g2_ref[...] + b2_ref[...]


def _combine(w_tok, h1, sh, yg, g2, b2, first_token, out_prev):
    n_tok = yg.shape[1] // ROW_SUB
    tm = COMBINE_TM
    first = first_token // tm
    full = lambda a: pl.BlockSpec(a.shape, lambda i: (0, 0))
    row = pl.BlockSpec((tm, D_MODEL), lambda i: (first + i, 0))
    in_specs = [pl.BlockSpec((tm, TOP_K), lambda i: (first + i, 0)), row, row,
                pl.BlockSpec((TOP_K, tm * ROW_SUB, LANES), lambda i: (0, i, 0)),
                full(g2), full(b2)]
    args = [w_tok, h1, sh, yg, g2, b2]
    aliases = {}
    if out_prev is not None:
        in_specs.append(pl.BlockSpec(memory_space=pl.ANY))
        args.append(out_prev)
        aliases = {len(args) - 1: 0}
    return pl.pallas_call(
        _combine_kernel,
        grid=(n_tok // tm,),
        in_specs=in_specs,
        out_specs=row,
        out_shape=jax.ShapeDtypeStruct(h1.shape, F32),
        input_output_aliases=aliases,
        compiler_params=pltpu.CompilerParams(dimension_semantics=("parallel",),
                                             vmem_limit_bytes=VMEM_LIMIT),
        name="combine",
    )(*args)


def _prep_weights(w_in, w_uq, w_ukv, seq):
    z = lambda n: jnp.zeros((D_MODEL, n), F32)
    kr = w_in[:, 512:544]
    qs = w_in[:, 544:1056] * (LOG2E / math.sqrt(SWA_HEAD_DIM))
    ks0, ks1 = w_in[:, 1056:1120], w_in[:, 1120:1184]
    half = MLA_ROPE // 2
    w1 = jnp.concatenate([
        w_in[:, 0:512], qs,
        ks0, ks1, ks1, ks0,
        z(64), kr, z(32),
        z(64), -kr[:, half:], kr[:, :half], z(32)], axis=1).astype(BF16)
    wvst = w_in[:, 1184:1312].T.astype(BF16)
    wg = w_in[:, 1312:3360].astype(BF16)

    pad3 = lambda a, lo, hi: jnp.pad(a, ((0, 0), (0, 0), (lo, hi)))
    wq3 = w_uq.reshape(MLA_Q_LORA, MLA_HEADS, MLA_NOPE + MLA_ROPE)
    rope = wq3[:, :, MLA_NOPE:]
    rot = jnp.concatenate([-rope[:, :, half:], rope[:, :, :half]], axis=2)
    wq2 = jnp.concatenate([pad3(wq3, 0, 32).reshape(MLA_Q_LORA, -1),
                           pad3(rot, MLA_NOPE, 32).reshape(MLA_Q_LORA, -1)], axis=1).astype(BF16)
    wkv3 = w_ukv.reshape(MLA_KV_LORA, MLA_HEADS, MLA_NOPE + MLA_V)
    wk = pad3(wkv3[:, :, :MLA_NOPE], 0, 64).reshape(MLA_KV_LORA, -1).astype(BF16)
    wvt = wkv3[:, :, MLA_NOPE:].reshape(MLA_KV_LORA, -1).T.astype(BF16)

    inv_freq = ROPE_THETA ** (-jnp.arange(0, MLA_ROPE, 2, dtype=F32) / MLA_ROPE)
    ang = jnp.arange(seq, dtype=F32)[:, None] * inv_freq[None, :]
    cos, sin = jnp.cos(ang), jnp.sin(ang)
    one, zero = jnp.ones((seq, 64), F32), jnp.zeros((seq, 64), F32)
    z32 = jnp.zeros((seq, 32), F32)
    scale = LOG2E / math.sqrt(MLA_NOPE + MLA_ROPE)
    tabs = (jnp.concatenate([one, cos, cos, z32], axis=1) * scale,
            jnp.concatenate([zero, sin, sin, z32], axis=1) * scale,
            jnp.concatenate([zero, cos, cos, z32], axis=1),
            jnp.concatenate([zero, sin, sin, z32], axis=1))
    return w1, wvst, wg, wq2, wk, wvt, tabs


def kernel(x, w_in, mla_q_norm, mla_kv_norm, w_uq, w_ukv, attn_sinks, w_o_mla, w_o_swa, w_out,
           ln1_g, ln1_b, w_router, router_bias, w_exp_up, w_exp_down, w_sh_up, w_sh_down,
           ln2_g, ln2_b):
    batch, seq, _ = x.shape
    T = batch * seq
    x2 = x.reshape(T, D_MODEL)
    w1, wvst, wg, wq2, wk, wvt, tabs = _prep_weights(w_in, w_uq, w_ukv, seq)
    q, k, vt, qs, ks, vst = _proj(x2, w1, wvst, wq2, wk, wvt, mla_q_norm.reshape(1, -1),
                                  mla_kv_norm.reshape(1, -1), tabs, seq)
    o_mla = _mla(q, k, vt, batch, seq)
    o_swa = _swa(attn_sinks.astype(F32), qs, ks, vst, batch, seq)

    tri = (lax.broadcasted_iota(jnp.int32, (ROUTE_TN, ROUTE_TN), 0)
           < lax.broadcasted_iota(jnp.int32, (ROUTE_TN, ROUTE_TN), 1)).astype(BF16)
    h1, h1p, e_t, w_t, r_t, cnt = _post(
        x2, o_mla, o_swa, w_o_mla.astype(BF16), w_o_swa.astype(BF16), wg, w_out.astype(BF16),
        ln1_g.reshape(1, -1), ln1_b.reshape(1, -1), w_router.T.astype(BF16),
        router_bias.reshape(-1, 1).astype(F32), tri)

    counts = cnt[:, 0]
    pad = (counts + ROW_BLK - 1) // ROW_BLK * ROW_BLK
    pad_start = jnp.cumsum(pad) - pad
    n_rows = (T * TOP_K // ROW_BLK + N_EXPERTS) * ROW_BLK
    dest = _dest(pad_start.astype(jnp.int32), e_t, r_t)
    idx = (dest.reshape(TOP_K, T // SC_CHUNK, SC_CHUNK, 1) * ROW_SUB
           + jnp.arange(ROW_SUB, dtype=jnp.int32))
    idx = idx.transpose(1, 0, 2, 3).reshape(T // SC_CHUNK, TOP_K, SC_CHUNK * ROW_SUB)
    xs = _sc_dispatch(h1p, idx, n_rows)
    sh = _shared(h1p, w_sh_up.astype(BF16), w_sh_down.astype(BF16))
    ys = _experts((pad_start // ROW_BLK).astype(jnp.int32), (pad // ROW_BLK).astype(jnp.int32),
                  counts.astype(jnp.int32), xs, w_exp_up, w_exp_down, sh)

    Tc = T // COMBINE_PARTS
    n_chunks = Tc // SC_CHUNK
    w_tok = w_t.T
    g2, b2 = ln2_g.reshape(1, -1), ln2_b.reshape(1, -1)
    out = None
    for cpart in range(COMBINE_PARTS):
        yg = _sc_gather(ys, idx[cpart * n_chunks:(cpart + 1) * n_chunks], Tc)
        out = _combine(w_tok, h1, sh, yg, g2, b2, cpart * Tc, out)
    return out.reshape(batch, seq, D_MODEL)
```

```python
import math

import jax
import jax.numpy as jnp
from jax import lax
from jax.experimental import pallas as pl
from jax.experimental.pallas import tpu as pltpu
from jax.experimental.pallas import tpu_sc as plsc

D_MODEL = 1024
MLA_HEADS = 8
MLA_Q_LORA = 256
MLA_KV_LORA = 256
MLA_NOPE = 64
MLA_ROPE = 32
MLA_V = 64
VT_ROWS = 80
ROPE_THETA = 10000.0
SWA_HEADS = 8
SWA_KV_HEADS = 2
SWA_GROUP = SWA_HEADS // SWA_KV_HEADS
SWA_HEAD_DIM = 64
SWA_WINDOW = 128
SWA_TQ = 256
N_EXPERTS = 256
TOP_K = 8
N_GROUPS = 8
GROUP_SIZE = N_EXPERTS // N_GROUPS
TOPK_GROUPS = 4
EXPERT_HIDDEN = 256
SHARED_HIDDEN = 256
ROUTED_SCALE = 2.5
DEEPNORM_ALPHA = 2.0 ** 0.25
LN_EPS = 1e-5
RMS_EPS = 1e-6

LANES = 128
ROW_WORDS = D_MODEL // 2
ROW_SUB = ROW_WORDS // LANES
VMEM_LIMIT = 48 * 1024 * 1024

PROJ_TM = 512
MLA_TQ = 256
MLA_TK = 512
MLA_HPS = 8
POST_TM = 512
ROUTE_TN = 256
ROW_BLK = 256
EXPERT_SLOTS = 4
EXPERT_SPLIT = 2
SHARED_TM = 512
COMBINE_TM = 256
SC_CHUNK = 32
COMBINE_PARTS = 4

BF16 = jnp.bfloat16
F32 = jnp.float32
NEG_INF = float("-inf")
LOG2E = math.log2(math.e)


def _sigmoid(v):
    return 1.0 / (1.0 + jnp.exp(-v))


def _dot(a, b):
    return jnp.dot(a, b, preferred_element_type=F32)


def _dot_nt(a, b):
    return lax.dot_general(a, b, (((1,), (1,)), ((), ())), preferred_element_type=F32)


def _pack_rows(y, out_ref, first=0):
    rows = y.shape[0]
    for j in range(ROW_SUB):
        a = y[:, j * LANES:(j + 1) * LANES].astype(BF16).astype(F32)
        b = y[:, ROW_WORDS + j * LANES:ROW_WORDS + (j + 1) * LANES].astype(BF16).astype(F32)
        ua = pltpu.bitcast(a, jnp.uint32) >> 16
        ub = pltpu.bitcast(b, jnp.uint32)
        out_ref[pl.ds(first * ROW_SUB + j, rows, stride=ROW_SUB), :] = ua | ub


def _unpack_rows(ref, rows, first=0, n_valid=None):
    lo, hi = [], []
    if n_valid is not None:
        live = first + lax.broadcasted_iota(jnp.int32, (rows, LANES), 0) < n_valid
    for j in range(ROW_SUB):
        u = ref[pl.ds(first * ROW_SUB + j, rows, stride=ROW_SUB), :]
        if n_valid is not None:
            u = jnp.where(live, u, jnp.uint32(0))
        lo.append(pltpu.bitcast(u << 16, F32))
        hi.append(pltpu.bitcast(u & jnp.uint32(0xFFFF0000), F32))
    return lo + hi


def _proj_kernel(x_ref, w1_ref, wvst_ref, wq_ref, wk_ref, wvt_ref, gq_ref, gkv_ref,
                 cq_ref, sq_ref, ck_ref, sk_ref,
                 q_ref, k_ref, vt_ref, qs_ref, ks_ref, vst_ref):
    xb = x_ref[...].astype(BF16)
    p = _dot(xb, w1_ref[...])

    def rms(c, g):
        return c * lax.rsqrt(jnp.mean(c * c, axis=-1, keepdims=True) + RMS_EPS) * g

    cqn = rms(p[:, 0:256], gq_ref[...]).astype(BF16)
    ckvn = rms(p[:, 256:512], gkv_ref[...]).astype(BF16)
    qs_ref[...] = p[:, 512:1024].astype(BF16)
    k01 = p[:, 1024:1152].astype(BF16)
    k10 = p[:, 1152:1280].astype(BF16)
    lane_lo = lax.broadcasted_iota(jnp.int32, k01.shape, 1) < SWA_HEAD_DIM
    zero = jnp.zeros_like(k01)
    ks_ref[:, 0 * LANES:1 * LANES] = jnp.where(lane_lo, k01, zero)
    ks_ref[:, 1 * LANES:2 * LANES] = jnp.where(lane_lo, zero, k10)
    ks_ref[:, 2 * LANES:3 * LANES] = jnp.where(lane_lo, k10, zero)
    ks_ref[:, 3 * LANES:4 * LANES] = jnp.where(lane_lo, zero, k01)
    ones = jnp.ones((VT_ROWS - MLA_V, xb.shape[0]), BF16)
    vst = _dot_nt(wvst_ref[...], xb).astype(BF16)
    for h in range(SWA_KV_HEADS):
        vst_ref[h * VT_ROWS:h * VT_ROWS + SWA_HEAD_DIM, :] = vst[h * SWA_HEAD_DIM:(h + 1) * SWA_HEAD_DIM, :]
        vst_ref[h * VT_ROWS + SWA_HEAD_DIM:(h + 1) * VT_ROWS, :] = ones
    kr = p[:, 1280:1408] * ck_ref[...] + p[:, 1408:1536] * sk_ref[...]
    qq = _dot(cqn, wq_ref[...])
    kn = _dot(ckvn, wk_ref[...])
    cq = cq_ref[...]
    sq = sq_ref[...]
    for h in range(MLA_HEADS):
        a = qq[:, h * LANES:(h + 1) * LANES]
        b = qq[:, 1024 + h * LANES:1024 + (h + 1) * LANES]
        q_ref[:, h * LANES:(h + 1) * LANES] = (a * cq + b * sq).astype(BF16)
        k_ref[:, h * LANES:(h + 1) * LANES] = (kn[:, h * LANES:(h + 1) * LANES] + kr).astype(BF16)
    vt = _dot_nt(wvt_ref[...], ckvn).astype(BF16)
    for h in range(MLA_HEADS):
        vt_ref[h * VT_ROWS:h * VT_ROWS + MLA_V, :] = vt[h * MLA_V:(h + 1) * MLA_V, :]
        vt_ref[h * VT_ROWS + MLA_V:(h + 1) * VT_ROWS, :] = ones


def _proj(x2, w1, wvst, wq2, wk, wvt, gq, gkv, tabs, seq):
    T = x2.shape[0]
    tm = PROJ_TM
    nper = seq // tm
    full = lambda shape: pl.BlockSpec(shape, lambda i: (0, 0))
    tab = pl.BlockSpec((tm, LANES), lambda i: (i % nper, 0))
    row = lambda n: pl.BlockSpec((tm, n), lambda i: (i, 0))
    col = lambda n: pl.BlockSpec((n, tm), lambda i: (0, i))
    return pl.pallas_call(
        _proj_kernel,
        grid=(T // tm,),
        in_specs=[row(D_MODEL), full(w1.shape), full(wvst.shape), full(wq2.shape), full(wk.shape),
                  full(wvt.shape), full(gq.shape), full(gkv.shape), tab, tab, tab, tab],
        out_specs=[row(1024), row(1024), col(MLA_HEADS * VT_ROWS), row(512), row(512),
                   col(SWA_KV_HEADS * VT_ROWS)],
        out_shape=[jax.ShapeDtypeStruct((T, 1024), BF16), jax.ShapeDtypeStruct((T, 1024), BF16),
                   jax.ShapeDtypeStruct((MLA_HEADS * VT_ROWS, T), BF16), jax.ShapeDtypeStruct((T, 512), BF16),
                   jax.ShapeDtypeStruct((T, 512), BF16),
                   jax.ShapeDtypeStruct((SWA_KV_HEADS * VT_ROWS, T), BF16)],
        compiler_params=pltpu.CompilerParams(dimension_semantics=("parallel",),
                                             vmem_limit_bytes=VMEM_LIMIT),
        name="proj",
    )(x2, w1, wvst, wq2, wk, wvt, gq, gkv, *tabs)


def _mla_kernel(q_ref, k_ref, vt_ref, o_ref, *acc_scr):
    tq = q_ref.shape[0]
    qi = pl.program_id(2)
    for acc in acc_scr:
        acc[...] = jnp.zeros(acc.shape, F32)

    def step(ks, tk, maxes, masked):
        scores = [_dot_nt(k_ref[pl.ds(ks, tk), h * LANES:(h + 1) * LANES],
                          q_ref[:, h * LANES:(h + 1) * LANES]) for h in range(MLA_HPS)]
        new_maxes, probs, alphas = [], [], []
        for h in range(MLA_HPS):
            s = scores[h]
            if masked:
                key = ks + lax.broadcasted_iota(jnp.int32, s.shape, 0)
                qry = qi * tq + lax.broadcasted_iota(jnp.int32, s.shape, 1)
                s = jnp.where(key <= qry, s, NEG_INF)
            m_new = jnp.maximum(maxes[h], jnp.max(s, axis=0, keepdims=True))
            alphas.append(jnp.exp2(maxes[h] - m_new))
            probs.append(jnp.exp2(s - m_new).astype(BF16))
            new_maxes.append(m_new)
        for h in range(MLA_HPS):
            pv = _dot(vt_ref[h * VT_ROWS:(h + 1) * VT_ROWS, pl.ds(ks, tk)], probs[h])
            acc_scr[h][...] = acc_scr[h][...] * alphas[h] + pv
        return tuple(new_maxes)

    init = tuple(jnp.full((1, tq), NEG_INF, F32) for _ in range(MLA_HPS))
    big = MLA_TK
    n_big = (qi * tq) // big
    maxes = lax.fori_loop(
        0, n_big, lambda kc, mx: step(pl.multiple_of(kc * big, big), big, mx, False), init)
    rest = pl.multiple_of(n_big * big, tq)
    maxes = lax.cond(
        rest < qi * tq,
        lambda mx: step(rest, 2 * tq, mx, True),
        lambda mx: step(rest, tq, mx, True),
        maxes)
    for h2 in range(MLA_HPS // 2):
        out_t = jnp.concatenate(
            [acc_scr[2 * h2 + g][0:MLA_V, :] / acc_scr[2 * h2 + g][MLA_V:MLA_V + 1, :] for g in range(2)],
            axis=0)
        o_ref[:, h2 * LANES:(h2 + 1) * LANES] = out_t.T.astype(BF16)


def _mla(q, k, vt, batch, seq):
    T = q.shape[0]
    tq = MLA_TQ
    nq = seq // tq
    hps = MLA_HPS
    return pl.pallas_call(
        _mla_kernel,
        grid=(batch, MLA_HEADS // hps, nq),
        in_specs=[pl.BlockSpec((tq, hps * LANES), lambda b, j, i: (b * nq + i, j)),
                  pl.BlockSpec((seq, hps * LANES), lambda b, j, i: (b, j)),
                  pl.BlockSpec((hps * VT_ROWS, seq), lambda b, j, i: (j, b))],
        out_specs=pl.BlockSpec((tq, hps * MLA_V), lambda b, j, i: (b * nq + i, j)),
        out_shape=jax.ShapeDtypeStruct((T, MLA_HEADS * MLA_V), BF16),
        scratch_shapes=[pltpu.VMEM((VT_ROWS, tq), F32) for _ in range(hps)],
        compiler_params=pltpu.CompilerParams(
            dimension_semantics=("parallel", "parallel", "arbitrary"), vmem_limit_bytes=VMEM_LIMIT),
        name="mla",
    )(q, k, vt)


def _swa_kernel(sink_ref, q_ref, kc_ref, kp_ref, vtc_ref, vtp_ref, bias_ref, o_ref):
    scores = []
    for head in range(SWA_HEADS):
        pair, g, kvh = head // 2, head % 2, head // SWA_GROUP
        col = (2 * kvh + g) * LANES
        band = jnp.concatenate([kp_ref[:, col:col + LANES], kc_ref[:, col:col + LANES]], axis=0)
        scores.append(_dot_nt(band, q_ref[:, pair * LANES:(pair + 1) * LANES]))
    probs, sink_terms = [], []
    for head in range(SWA_HEADS):
        s = scores[head] + bias_ref[0, head]
        sink = sink_ref[head] * LOG2E
        m = jnp.maximum(jnp.max(s, axis=0, keepdims=True), sink)
        probs.append(jnp.exp2(s - m).astype(BF16))
        sink_terms.append(jnp.exp2(sink - m))
    outs = []
    for head in range(SWA_HEADS):
        rows = slice((head // SWA_GROUP) * VT_ROWS, (head // SWA_GROUP + 1) * VT_ROWS)
        v_band = jnp.concatenate([vtp_ref[rows, :], vtc_ref[rows, :]], axis=1)
        pv = _dot(v_band, probs[head])
        outs.append(pv[0:SWA_HEAD_DIM, :] / (pv[SWA_HEAD_DIM:SWA_HEAD_DIM + 1, :] + sink_terms[head]))
    o_ref[...] = jnp.concatenate(outs, axis=0).T.astype(BF16)


def _swa_bias():
    W = SWA_WINDOW
    j = jnp.arange(W + SWA_TQ, dtype=jnp.int32)[:, None]
    i = jnp.arange(SWA_TQ, dtype=jnp.int32)[None, :]
    dist = i + W - j
    valid = (dist >= 0) & (dist < W)
    slopes = 2.0 ** (-8.0 * jnp.arange(1, SWA_HEADS + 1, dtype=F32) / SWA_HEADS)
    pen = -(slopes[:, None, None] * dist.astype(F32)[None]) * LOG2E
    general = jnp.where(valid[None], pen, NEG_INF)
    first = jnp.where((valid & (j >= W))[None], pen, NEG_INF)
    return jnp.stack([first, general])


def _swa(sinks, qs, ks, vst, batch, seq):
    T = qs.shape[0]
    W, tq = SWA_WINDOW, SWA_TQ
    nb = seq // tq
    wpb = tq // W
    before = lambda b, i: b * nb * wpb + jnp.maximum(wpb * i - 1, 0)
    cur = lambda n: pl.BlockSpec((tq, n), lambda b, i: (b * nb + i, 0))
    prev = lambda n: pl.BlockSpec((W, n), lambda b, i: (before(b, i), 0))
    vt_cur = pl.BlockSpec((SWA_KV_HEADS * VT_ROWS, tq), lambda b, i: (0, b * nb + i))
    vt_prev = pl.BlockSpec((SWA_KV_HEADS * VT_ROWS, W), lambda b, i: (0, before(b, i)))
    bias = pl.BlockSpec((1, SWA_HEADS, W + tq, tq), lambda b, i: (jnp.minimum(i, 1), 0, 0, 0))
    return pl.pallas_call(
        _swa_kernel,
        grid=(batch, nb),
        in_specs=[pl.BlockSpec(memory_space=pltpu.SMEM), cur(512), cur(512), prev(512),
                  vt_cur, vt_prev, bias],
        out_specs=cur(512),
        out_shape=jax.ShapeDtypeStruct((T, 512), BF16),
        compiler_params=pltpu.CompilerParams(dimension_semantics=("parallel", "parallel"),
                                             vmem_limit_bytes=VMEM_LIMIT),
        name="swa",
    )(sinks, qs, ks, ks, vst, vst, _swa_bias())


def _post_kernel(x_ref, om_ref, os_ref, wom_ref, wos_ref, wg_ref, wout_ref, g1_ref, b1_ref,
                 wr_ref, rb_ref, tri_ref,
                 h1_ref, h1p_ref, e_ref, w_ref, r_ref, cnt_ref, carry_scr):
    tm = x_ref.shape[0]
    step = pl.program_id(0)

    @pl.when(step == 0)
    def _():
        carry_scr[...] = jnp.zeros(carry_scr.shape, F32)

    x = x_ref[...]
    ya = _dot(om_ref[...], wom_ref[...])
    yb = _dot(os_ref[...], wos_ref[...])
    gates = _dot(x.astype(BF16), wg_ref[...])
    merged = _sigmoid(gates[:, :D_MODEL]) * ya + _sigmoid(gates[:, D_MODEL:]) * yb
    mix = _dot(merged.astype(BF16), wout_ref[...])
    z = DEEPNORM_ALPHA * x + mix
    mu = jnp.mean(z, axis=-1, keepdims=True)
    zc = z - mu
    var = jnp.mean(zc * zc, axis=-1, keepdims=True)
    h1 = zc * lax.rsqrt(var + LN_EPS) * g1_ref[...] + b1_ref[...]
    h1_ref[...] = h1
    _pack_rows(h1, h1p_ref)

    all_scores = _sigmoid(_dot_nt(wr_ref[...], h1.astype(BF16)))
    carry = carry_scr[...]
    for c in range(tm // ROUTE_TN):
        cols = slice(c * ROUTE_TN, (c + 1) * ROUTE_TN)
        idxs, weights, ranks, carry = _route(all_scores[:, cols], rb_ref[...], tri_ref[...], carry)
        e_ref[:, cols] = idxs
        w_ref[:, cols] = weights
        r_ref[:, cols] = ranks
    carry_scr[...] = carry
    cnt_ref[...] = carry.astype(jnp.int32)


def _route(scores, bias, tri, carry):
    tn = scores.shape[1]
    choice = scores + bias
    row = lax.broadcasted_iota(jnp.int32, (N_EXPERTS, tn), 0)
    grow = lax.broadcasted_iota(jnp.int32, (GROUP_SIZE, tn), 0)
    gscore = []
    for g in range(N_GROUPS):
        blk = choice[g * GROUP_SIZE:(g + 1) * GROUP_SIZE, :]
        m1 = jnp.max(blk, axis=0, keepdims=True)
        i1 = jnp.min(jnp.where(blk == m1, grow, GROUP_SIZE), axis=0, keepdims=True)
        m2 = jnp.max(jnp.where(grow == i1, NEG_INF, blk), axis=0, keepdims=True)
        gscore.append(m1 + m2)
    gsc = jnp.concatenate(gscore, axis=0)
    gidx = lax.broadcasted_iota(jnp.int32, (N_GROUPS, tn), 0)
    grank = jnp.zeros((N_GROUPS, tn), jnp.int32)
    for g in range(N_GROUPS):
        sg = gsc[g:g + 1, :]
        beats = (sg > gsc) | ((sg == gsc) & (gidx > g))
        grank = grank + beats.astype(jnp.int32)
    gsel = (grank < TOPK_GROUPS).astype(F32)
    emask = jnp.concatenate(
        [jnp.broadcast_to(gsel[g:g + 1, :], (GROUP_SIZE, tn)) for g in range(N_GROUPS)], axis=0)
    work = jnp.where(emask > 0.0, choice, NEG_INF)
    eligible = work
    idxs, svals = [], []
    for _k in range(TOP_K):
        m = jnp.max(work, axis=0, keepdims=True)
        idx = jnp.min(jnp.where(work == m, row, N_EXPERTS), axis=0, keepdims=True)
        hit = row == idx
        svals.append(jnp.sum(jnp.where(hit, scores, 0.0), axis=0, keepdims=True))
        work = jnp.where(hit, NEG_INF, work)
        idxs.append(idx)
    sel = jnp.where(work != eligible, 1.0, 0.0)
    ssum = svals[0]
    for sv in svals[1:]:
        ssum = ssum + sv
    weights = jnp.concatenate([sv / ssum * ROUTED_SCALE for sv in svals], axis=0)

    rank = _dot(sel.astype(BF16), tri) + carry[:, 0:1]
    ranks = jnp.concatenate(
        [jnp.sum(jnp.where(row == idx, rank, 0.0), axis=0, keepdims=True) for idx in idxs],
        axis=0).astype(jnp.int32)
    carry = carry + jnp.sum(sel, axis=1, keepdims=True)
    return jnp.concatenate(idxs, axis=0), weights, ranks, carry


def _post(x2, o_mla, o_swa, wom, wos, wg, wout, g1, b1, wr_t, rbias, tri):
    T = x2.shape[0]
    tm = POST_TM
    full = lambda a: pl.BlockSpec(a.shape, lambda i: (0, 0))
    row = lambda n: pl.BlockSpec((tm, n), lambda i: (i, 0))
    col = pl.BlockSpec((TOP_K, tm), lambda i: (0, i))
    return pl.pallas_call(
        _post_kernel,
        grid=(T // tm,),
        in_specs=[row(D_MODEL), row(512), row(512), full(wom), full(wos), full(wg), full(wout),
                  full(g1), full(b1), full(wr_t), full(rbias), full(tri)],
        out_specs=[row(D_MODEL), pl.BlockSpec((tm * ROW_SUB, LANES), lambda i: (i, 0)), col, col, col,
                   pl.BlockSpec((N_EXPERTS, LANES), lambda i: (0, 0))],
        out_shape=[jax.ShapeDtypeStruct((T, D_MODEL), F32),
                   jax.ShapeDtypeStruct((T * ROW_SUB, LANES), jnp.uint32),
                   jax.ShapeDtypeStruct((TOP_K, T), jnp.int32),
                   jax.ShapeDtypeStruct((TOP_K, T), F32),
                   jax.ShapeDtypeStruct((TOP_K, T), jnp.int32),
                   jax.ShapeDtypeStruct((N_EXPERTS, LANES), jnp.int32)],
        scratch_shapes=[pltpu.VMEM((N_EXPERTS, LANES), F32)],
        compiler_params=pltpu.CompilerParams(dimension_semantics=("arbitrary",),
                                             vmem_limit_bytes=VMEM_LIMIT),
        name="post",
    )(x2, o_mla, o_swa, wom, wos, wg, wout, g1, b1, wr_t, rbias, tri)


def _dest_kernel(start_ref, e_ref, r_ref, o_ref):
    e = e_ref[...]
    base = jnp.zeros(e.shape, jnp.int32)
    for j in range(N_EXPERTS):
        base = jnp.where(e == j, start_ref[j], base)
    o_ref[...] = base + r_ref[...]


def _dest(seg_start, e_t, r_t):
    T = e_t.shape[1]
    tn = min(T, 4096)
    col = pl.BlockSpec((TOP_K, tn), lambda i: (0, i))
    return pl.pallas_call(
        _dest_kernel,
        grid=(T // tn,),
        in_specs=[pl.BlockSpec(memory_space=pltpu.SMEM), col, col],
        out_specs=col,
        out_shape=jax.ShapeDtypeStruct((TOP_K, T), jnp.int32),
        compiler_params=pltpu.CompilerParams(dimension_semantics=("parallel",)),
        name="dest",
    )(seg_start, e_t, r_t)


def _sc_worker_chunks(n_tokens):
    info = plsc.get_sparse_core_info()
    n_workers = info.num_cores * info.num_subcores
    per_worker = n_tokens // SC_CHUNK // n_workers
    assert per_worker * n_workers * SC_CHUNK == n_tokens
    first = (lax.axis_index("s") * info.num_cores + lax.axis_index("c")) * per_worker
    return first, per_worker


def _sc_dispatch(h1p, idx, n_rows):
    n_tokens = h1p.shape[0] // ROW_SUB
    R = SC_CHUNK * ROW_SUB

    def body(h_hbm, idx_hbm, xs_hbm, idx_v, rows_v, load_sem, scatter_sem):
        first, per_worker = _sc_worker_chunks(n_tokens)
        assert per_worker % 2 == 0

        def loads(c, slot):
            return (pltpu.make_async_copy(idx_hbm.at[c], idx_v.at[slot], load_sem.at[slot]),
                    pltpu.make_async_copy(h_hbm.at[pl.ds(c * R, R)], rows_v.at[slot], load_sem.at[slot]))

        def scatters(slot):
            return [pltpu.make_async_copy(rows_v.at[slot], xs_hbm.at[idx_v.at[slot, k]], scatter_sem.at[slot])
                    for k in range(TOP_K)]

        for cp in loads(first, 0):
            cp.start()

        @pl.loop(0, per_worker, step=2)
        def _(ci):
            for slot in range(2):
                c = first + ci + slot
                for cp in loads(c, slot):
                    cp.wait()

                @pl.when(ci + slot > 0)
                def _():
                    for cp in scatters(1 - slot):
                        cp.wait()

                @pl.when(ci + slot + 1 < per_worker)
                def _():
                    for cp in loads(c + 1, 1 - slot):
                        cp.start()

                for cp in scatters(slot):
                    cp.start()

        for cp in scatters(1):
            cp.wait()

    run = pl.kernel(
        body, out_type=jax.ShapeDtypeStruct((n_rows * ROW_SUB, LANES), jnp.uint32),
        mesh=plsc.VectorSubcoreMesh(core_axis_name="c", subcore_axis_name="s"),
        scratch_types=[pltpu.VMEM((2, TOP_K, R), jnp.int32), pltpu.VMEM((2, R, LANES), jnp.uint32),
                       pltpu.SemaphoreType.DMA((2,)), pltpu.SemaphoreType.DMA((2,))],
        name="sc_dispatch")
    return run(h1p, idx)


def _sc_gather(ys, idx, n_tokens):
    R = SC_CHUNK * ROW_SUB
    n_groups = TOP_K // 2

    def body(ys_hbm, idx_hbm, yg_hbm, idx_v, buf, isem, gsem, wsem):
        first, per_worker = _sc_worker_chunks(n_tokens)
        assert per_worker % 2 == 0

        def idx_load(c, islot):
            return pltpu.make_async_copy(idx_hbm.at[c], idx_v.at[islot], isem.at[islot])

        def gathers(islot, q):
            s = q % 2
            return [pltpu.make_async_copy(ys_hbm.at[idx_v.at[islot, 2 * q + j]], buf.at[s, j], gsem.at[s])
                    for j in range(2)]

        def writes(c, q):
            s = q % 2
            return [pltpu.make_async_copy(buf.at[s, j], yg_hbm.at[2 * q + j, pl.ds(c * R, R)], wsem.at[s])
                    for j in range(2)]

        idx_load(first, 0).start()

        @pl.loop(0, per_worker, step=2)
        def _(ci):
            for islot in range(2):
                c = first + ci + islot
                idx_load(c, islot).wait()
                for q in range(n_groups):
                    if q >= 2:
                        for cp in writes(c, q - 2):
                            cp.wait()
                    else:
                        @pl.when(ci + islot > 0)
                        def _():
                            for cp in writes(c - 1, q + 2):
                                cp.wait()
                    for cp in gathers(islot, q):
                        cp.start()
                    if q >= 1:
                        for cp in gathers(islot, q - 1):
                            cp.wait()
                        for cp in writes(c, q - 1):
                            cp.start()
                    else:
                        @pl.when(ci + islot > 0)
                        def _():
                            for cp in gathers(1 - islot, n_groups - 1):
                                cp.wait()
                            for cp in writes(c - 1, n_groups - 1):
                                cp.start()

                        @pl.when(ci + islot + 1 < per_worker)
                        def _():
                            idx_load(c + 1, 1 - islot).start()

        last = first + per_worker - 1
        for cp in gathers(1, n_groups - 1):
            cp.wait()
        for cp in writes(last, n_groups - 1):
            cp.start()
        for q in (n_groups - 2, n_groups - 1):
            for cp in writes(last, q):
                cp.wait()

    run = pl.kernel(
        body, out_type=jax.ShapeDtypeStruct((TOP_K, n_tokens * ROW_SUB, LANES), jnp.uint32),
        mesh=plsc.VectorSubcoreMesh(core_axis_name="c", subcore_axis_name="s"),
        scratch_types=[pltpu.VMEM((2, TOP_K, R), jnp.int32), pltpu.VMEM((2, 2, R, LANES), jnp.uint32),
                       pltpu.SemaphoreType.DMA((2,)), pltpu.SemaphoreType.DMA((2,)),
                       pltpu.SemaphoreType.DMA((2,))],
        name="sc_gather")
    return run(ys, idx)


def _experts_kernel(first_ref, nblk_ref, cnt_ref, xs_hbm, wup_ref, wdn_ref, after_ref, ys_hbm,
                    wup_bf, wdn_bf, xbuf, ybuf, in_sem, out_sem):
    del after_ref
    e = pl.program_id(0)
    n = nblk_ref[e]
    b0 = first_ref[e]
    count = cnt_ref[e]
    total = first_ref[N_EXPERTS - 1] + nblk_ref[N_EXPERTS - 1]
    rb = ROW_BLK * ROW_SUB
    depth = EXPERT_SLOTS

    def slot_of(b):
        return lax.rem(b, depth)

    def rows_of(b):
        return pl.ds(pl.multiple_of(b * rb, rb), rb)

    def in_copy(b):
        return pltpu.make_async_copy(xs_hbm.at[rows_of(b)], xbuf.at[slot_of(b)], in_sem.at[slot_of(b)])

    def out_copy(b):
        return pltpu.make_async_copy(ybuf.at[slot_of(b)], ys_hbm.at[rows_of(b)], out_sem.at[slot_of(b)])

    @pl.when(e == 0)
    def _():
        for b in range(depth - 1):
            @pl.when(b < total)
            def _():
                in_copy(b).start()

    @pl.when(n > 0)
    def _():
        wup_bf[...] = wup_ref[0].astype(BF16)
        wdn_bf[...] = wdn_ref[0].astype(BF16)

        def body(b, carry):
            slot = slot_of(b)
            in_copy(b).wait()

            @pl.when(b + depth - 1 < total)
            def _():
                in_copy(b + depth - 1).start()

            @pl.when(b >= depth)
            def _():
                out_copy(b - depth).wait()

            sub = ROW_BLK // EXPERT_SPLIT
            n_valid = count - (b - b0) * ROW_BLK
            xbs = [jnp.concatenate(
                [c.astype(BF16) for c in _unpack_rows(xbuf.at[slot], sub, first=i * sub, n_valid=n_valid)],
                axis=1) for i in range(EXPERT_SPLIT)]
            gus = [_dot(xb, wup_bf[...]) for xb in xbs]
            hids = [(gu[:, :EXPERT_HIDDEN] * _sigmoid(gu[:, :EXPERT_HIDDEN])
                     * gu[:, EXPERT_HIDDEN:]).astype(BF16) for gu in gus]
            ys = [_dot(hid, wdn_bf[...]) for hid in hids]
            for i in range(EXPERT_SPLIT):
                _pack_rows(ys[i], ybuf.at[slot], first=i * sub)
            out_copy(b).start()
            return carry

        lax.fori_loop(b0, b0 + n, body, 0)

    @pl.when(e == N_EXPERTS - 1)
    def _():
        for back in range(depth, 0, -1):
            @pl.when(total - back >= 0)
            def _():
                out_copy(total - back).wait()


def _experts(first_blk, n_blk, counts, xs, w_exp_up, w_exp_down, run_after):
    rb = ROW_BLK * ROW_SUB
    grid_spec = pltpu.PrefetchScalarGridSpec(
        num_scalar_prefetch=3,
        grid=(N_EXPERTS,),
        in_specs=[pl.BlockSpec(memory_space=pl.ANY),
                  pl.BlockSpec((1, D_MODEL, 2 * EXPERT_HIDDEN), lambda e, fb, nb, ct: (e, 0, 0)),
                  pl.BlockSpec((1, EXPERT_HIDDEN, D_MODEL), lambda e, fb, nb, ct: (e, 0, 0)),
                  pl.BlockSpec(memory_space=pl.ANY)],
        out_specs=pl.BlockSpec(memory_space=pl.ANY),
        scratch_shapes=[pltpu.VMEM((D_MODEL, 2 * EXPERT_HIDDEN), BF16),
                        pltpu.VMEM((EXPERT_HIDDEN, D_MODEL), BF16),
                        pltpu.VMEM((EXPERT_SLOTS, rb, LANES), jnp.uint32),
                        pltpu.VMEM((EXPERT_SLOTS, rb, LANES), jnp.uint32),
                        pltpu.SemaphoreType.DMA((EXPERT_SLOTS,)),
                        pltpu.SemaphoreType.DMA((EXPERT_SLOTS,))],
    )
    return pl.pallas_call(
        _experts_kernel,
        grid_spec=grid_spec,
        out_shape=jax.ShapeDtypeStruct(xs.shape, jnp.uint32),
        compiler_params=pltpu.CompilerParams(dimension_semantics=("arbitrary",),
                                             vmem_limit_bytes=VMEM_LIMIT),
        name="experts",
    )(first_blk, n_blk, counts, xs, w_exp_up, w_exp_down, run_after)


def _shared_kernel(h1p_ref, wsu_ref, wsd_ref, o_ref):
    tm = o_ref.shape[0]
    hb = jnp.concatenate([c.astype(BF16) for c in _unpack_rows(h1p_ref, tm)], axis=1)
    gu = _dot(hb, wsu_ref[...])
    g = gu[:, :SHARED_HIDDEN]
    hid = g * _sigmoid(g) * gu[:, SHARED_HIDDEN:]
    o_ref[...] = _dot(hid.astype(BF16), wsd_ref[...]).astype(BF16)


def _shared(h1p, wsu, wsd):
    T = h1p.shape[0] // ROW_SUB
    tm = SHARED_TM
    full = lambda a: pl.BlockSpec(a.shape, lambda i: (0, 0))
    row = pl.BlockSpec((tm, D_MODEL), lambda i: (i, 0))
    return pl.pallas_call(
        _shared_kernel,
        grid=(T // tm,),
        in_specs=[pl.BlockSpec((tm * ROW_SUB, LANES), lambda i: (i, 0)), full(wsu), full(wsd)],
        out_specs=row,
        out_shape=jax.ShapeDtypeStruct((T, D_MODEL), BF16),
        compiler_params=pltpu.CompilerParams(dimension_semantics=("parallel",),
                                             vmem_limit_bytes=VMEM_LIMIT),
        name="shared",
    )(h1p, wsu, wsd)


def _combine_kernel(w_ref, h1_ref, sh_ref, yg_ref, g2_ref, b2_ref, *rest):
    o_ref = rest[-1]
    tm = h1_ref.shape[0]
    h1 = h1_ref[...]
    ffn = sh_ref[...].astype(F32)

    w = w_ref[...]
    acc = [None] * (2 * ROW_SUB)
    for k in range(TOP_K):
        wk = w[:, k:k + 1]
        chunks = _unpack_rows(yg_ref.at[k], tm)
        for c in range(2 * ROW_SUB):
            acc[c] = wk * chunks[c] if acc[c] is None else acc[c] + wk * chunks[c]
    routed = jnp.concatenate(acc, axis=1)
    z = DEEPNORM_ALPHA * h1 + (routed + ffn)
    mu = jnp.mean(z, axis=-1, keepdims=True)
    zc = z - mu
    var = jnp.mean(zc * zc, axis=-1, keepdims=True)
    o_ref[...] = zc * lax.rsqrt(var + LN_EPS) * g2_ref[...] + b2_ref[...]


def _combine(w_tok, h1, sh, yg, g2, b2, first_token, out_prev):
    n_tok = yg.shape[1] // ROW_SUB
    tm = COMBINE_TM
    first = first_token // tm
    full = lambda a: pl.BlockSpec(a.shape, lambda i: (0, 0))
    row = pl.BlockSpec((tm, D_MODEL), lambda i: (first + i, 0))
    in_specs = [pl.BlockSpec((tm, TOP_K), lambda i: (first + i, 0)), row, row,
                pl.BlockSpec((TOP_K, tm * ROW_SUB, LANES), lambda i: (0, i, 0)),
                full(g2), full(b2)]
    args = [w_tok, h1, sh, yg, g2, b2]
    aliases = {}
    if out_prev is not None:
        in_specs.append(pl.BlockSpec(memory_space=pl.ANY))
        args.append(out_prev)
        aliases = {len(args) - 1: 0}
    return pl.pallas_call(
        _combine_kernel,
        grid=(n_tok // tm,),
        in_specs=in_specs,
        out_specs=row,
        out_shape=jax.ShapeDtypeStruct(h1.shape, F32),
        input_output_aliases=aliases,
        compiler_params=pltpu.CompilerParams(dimension_semantics=("parallel",),
                                             vmem_limit_bytes=VMEM_LIMIT),
        name="combine",
    )(*args)


def _prep_weights(w_in, w_uq, w_ukv, seq):
    z = lambda n: jnp.zeros((D_MODEL, n), F32)
    kr = w_in[:, 512:544]
    qs = w_in[:, 544:1056] * (LOG2E / math.sqrt(SWA_HEAD_DIM))
    ks0, ks1 = w_in[:, 1056:1120], w_in[:, 1120:1184]
    half = MLA_ROPE // 2
    w1 = jnp.concatenate([
        w_in[:, 0:512], qs,
        ks0, ks1, ks1, ks0,
        z(64), kr, z(32),
        z(64), -kr[:, half:], kr[:, :half], z(32)], axis=1).astype(BF16)
    wvst = w_in[:, 1184:1312].T.astype(BF16)
    wg = w_in[:, 1312:3360].astype(BF16)

    pad3 = lambda a, lo, hi: jnp.pad(a, ((0, 0), (0, 0), (lo, hi)))
    wq3 = w_uq.reshape(MLA_Q_LORA, MLA_HEADS, MLA_NOPE + MLA_ROPE)
    rope = wq3[:, :, MLA_NOPE:]
    rot = jnp.concatenate([-rope[:, :, half:], rope[:, :, :half]], axis=2)
    wq2 = jnp.concatenate([pad3(wq3, 0, 32).reshape(MLA_Q_LORA, -1),
                           pad3(rot, MLA_NOPE, 32).reshape(MLA_Q_LORA, -1)], axis=1).astype(BF16)
    wkv3 = w_ukv.reshape(MLA_KV_LORA, MLA_HEADS, MLA_NOPE + MLA_V)
    wk = pad3(wkv3[:, :, :MLA_NOPE], 0, 64).reshape(MLA_KV_LORA, -1).astype(BF16)
    wvt = wkv3[:, :, MLA_NOPE:].reshape(MLA_KV_LORA, -1).T.astype(BF16)

    inv_freq = ROPE_THETA ** (-jnp.arange(0, MLA_ROPE, 2, dtype=F32) / MLA_ROPE)
    ang = jnp.arange(seq, dtype=F32)[:, None] * inv_freq[None, :]
    cos, sin = jnp.cos(ang), jnp.sin(ang)
    one, zero = jnp.ones((seq, 64), F32), jnp.zeros((seq, 64), F32)
    z32 = jnp.zeros((seq, 32), F32)
    scale = LOG2E / math.sqrt(MLA_NOPE + MLA_ROPE)
    tabs = (jnp.concatenate([one, cos, cos, z32], axis=1) * scale,
            jnp.concatenate([zero, sin, sin, z32], axis=1) * scale,
            jnp.concatenate([zero, cos, cos, z32], axis=1),
            jnp.concatenate([zero, sin, sin, z32], axis=1))
    return w1, wvst, wg, wq2, wk, wvt, tabs


def kernel(x, w_in, mla_q_norm, mla_kv_norm, w_uq, w_ukv, attn_sinks, w_o_mla, w_o_swa, w_out,
           ln1_g, ln1_b, w_router, router_bias, w_exp_up, w_exp_down, w_sh_up, w_sh_down,
           ln2_g, ln2_b):
    batch, seq, _ = x.shape
    T = batch * seq
    x2 = x.reshape(T, D_MODEL)
    w1, wvst, wg, wq2, wk, wvt, tabs = _prep_weights(w_in, w_uq, w_ukv, seq)
    q, k, vt, qs, ks, vst = _proj(x2, w1, wvst, wq2, wk, wvt, mla_q_norm.reshape(1, -1),
                                  mla_kv_norm.reshape(1, -1), tabs, seq)
    o_mla = _mla(q, k, vt, batch, seq)
    o_swa = _swa(attn_sinks.astype(F32), qs, ks, vst, batch, seq)

    tri = (lax.broadcasted_iota(jnp.int32, (ROUTE_TN, ROUTE_TN), 0)
           < lax.broadcasted_iota(jnp.int32, (ROUTE_TN, ROUTE_TN), 1)).astype(BF16)
    h1, h1p, e_t, w_t, r_t, cnt = _post(
        x2, o_mla, o_swa, w_o_mla.astype(BF16), w_o_swa.astype(BF16), wg, w_out.astype(BF16),
        ln1_g.reshape(1, -1), ln1_b.reshape(1, -1), w_router.T.astype(BF16),
        router_bias.reshape(-1, 1).astype(F32), tri)

    counts = cnt[:, 0]
    pad = (counts + ROW_BLK - 1) // ROW_BLK * ROW_BLK
    pad_start = jnp.cumsum(pad) - pad
    n_rows = (T * TOP_K // ROW_BLK + N_EXPERTS) * ROW_BLK
    dest = _dest(pad_start.astype(jnp.int32), e_t, r_t)
    idx = (dest.reshape(TOP_K, T // SC_CHUNK, SC_CHUNK, 1) * ROW_SUB
           + jnp.arange(ROW_SUB, dtype=jnp.int32))
    idx = idx.transpose(1, 0, 2, 3).reshape(T // SC_CHUNK, TOP_K, SC_CHUNK * ROW_SUB)
    xs = _sc_dispatch(h1p, idx, n_rows)
    sh = _shared(h1p, w_sh_up.astype(BF16), w_sh_down.astype(BF16))
    ys = _experts((pad_start // ROW_BLK).astype(jnp.int32), (pad // ROW_BLK).astype(jnp.int32),
                  counts.astype(jnp.int32), xs, w_exp_up, w_exp_down, sh)

    Tc = T // COMBINE_PARTS
    n_chunks = Tc // SC_CHUNK
    w_tok = w_t.T
    g2, b2 = ln2_g.reshape(1, -1), ln2_b.reshape(1, -1)
    out = None
    for cpart in range(COMBINE_PARTS):
        yg = _sc_gather(ys, idx[cpart * n_chunks:(cpart + 1) * n_chunks], Tc)
        out = _combine(w_tok, h1, sh, yg, g2, b2, cpart * Tc, out)
    return out.reshape(batch, seq, D_MODEL)
```

```python
import math

import jax
import jax.numpy as jnp
from jax import lax
from jax.experimental import pallas as pl
from jax.experimental.pallas import tpu as pltpu
from jax.experimental.pallas import tpu_sc as plsc

D_MODEL = 1024
MLA_HEADS = 8
MLA_Q_LORA = 256
MLA_KV_LORA = 256
MLA_NOPE = 64
MLA_ROPE = 32
MLA_V = 64
VT_ROWS = 80
ROPE_THETA = 10000.0
SWA_HEADS = 8
SWA_KV_HEADS = 2
SWA_GROUP = SWA_HEADS // SWA_KV_HEADS
SWA_HEAD_DIM = 64
SWA_WINDOW = 128
SWA_TQ = 256
N_EXPERTS = 256
TOP_K = 8
N_GROUPS = 8
GROUP_SIZE = N_EXPERTS // N_GROUPS
TOPK_GROUPS = 4
EXPERT_HIDDEN = 256
SHARED_HIDDEN = 256
ROUTED_SCALE = 2.5
DEEPNORM_ALPHA = 2.0 ** 0.25
LN_EPS = 1e-5
RMS_EPS = 1e-6

LANES = 128
ROW_WORDS = D_MODEL // 2
ROW_SUB = ROW_WORDS // LANES
VMEM_LIMIT = 48 * 1024 * 1024

PROJ_TM = 512
MLA_TQ = 256
MLA_TK = 512
MLA_HPS = 8
POST_TM = 512
ROUTE_TN = 256
ROW_BLK = 256
EXPERT_SLOTS = 4
EXPERT_SPLIT = 2
SHARED_TM = 512
COMBINE_TM = 256
SC_CHUNK = 32
COMBINE_PARTS = 8

BF16 = jnp.bfloat16
F32 = jnp.float32
NEG_INF = float("-inf")
LOG2E = math.log2(math.e)


def _sigmoid(v):
    return 1.0 / (1.0 + jnp.exp(-v))


def _dot(a, b):
    return jnp.dot(a, b, preferred_element_type=F32)


def _dot_nt(a, b):
    return lax.dot_general(a, b, (((1,), (1,)), ((), ())), preferred_element_type=F32)


def _pack_rows(y, out_ref, first=0):
    rows = y.shape[0]
    for j in range(ROW_SUB):
        a = y[:, j * LANES:(j + 1) * LANES].astype(BF16).astype(F32)
        b = y[:, ROW_WORDS + j * LANES:ROW_WORDS + (j + 1) * LANES].astype(BF16).astype(F32)
        ua = pltpu.bitcast(a, jnp.uint32) >> 16
        ub = pltpu.bitcast(b, jnp.uint32)
        out_ref[pl.ds(first * ROW_SUB + j, rows, stride=ROW_SUB), :] = ua | ub


def _unpack_rows(ref, rows, first=0, n_valid=None):
    lo, hi = [], []
    if n_valid is not None:
        live = first + lax.broadcasted_iota(jnp.int32, (rows, LANES), 0) < n_valid
    for j in range(ROW_SUB):
        u = ref[pl.ds(first * ROW_SUB + j, rows, stride=ROW_SUB), :]
        if n_valid is not None:
            u = jnp.where(live, u, jnp.uint32(0))
        lo.append(pltpu.bitcast(u << 16, F32))
        hi.append(pltpu.bitcast(u & jnp.uint32(0xFFFF0000), F32))
    return lo + hi


def _proj_kernel(x_ref, w1_ref, wvst_ref, wq_ref, wk_ref, wvt_ref, gq_ref, gkv_ref,
                 cq_ref, sq_ref, ck_ref, sk_ref,
                 q_ref, k_ref, vt_ref, qs_ref, ks_ref, vst_ref):
    xb = x_ref[...].astype(BF16)
    p = _dot(xb, w1_ref[...])

    def rms(c, g):
        return c * lax.rsqrt(jnp.mean(c * c, axis=-1, keepdims=True) + RMS_EPS) * g

    cqn = rms(p[:, 0:256], gq_ref[...]).astype(BF16)
    ckvn = rms(p[:, 256:512], gkv_ref[...]).astype(BF16)
    qs_ref[...] = p[:, 512:1024].astype(BF16)
    k01 = p[:, 1024:1152].astype(BF16)
    k10 = p[:, 1152:1280].astype(BF16)
    lane_lo = lax.broadcasted_iota(jnp.int32, k01.shape, 1) < SWA_HEAD_DIM
    zero = jnp.zeros_like(k01)
    ks_ref[:, 0 * LANES:1 * LANES] = jnp.where(lane_lo, k01, zero)
    ks_ref[:, 1 * LANES:2 * LANES] = jnp.where(lane_lo, zero, k10)
    ks_ref[:, 2 * LANES:3 * LANES] = jnp.where(lane_lo, k10, zero)
    ks_ref[:, 3 * LANES:4 * LANES] = jnp.where(lane_lo, zero, k01)
    ones = jnp.ones((VT_ROWS - MLA_V, xb.shape[0]), BF16)
    vst = _dot_nt(wvst_ref[...], xb).astype(BF16)
    for h in range(SWA_KV_HEADS):
        vst_ref[h * VT_ROWS:h * VT_ROWS + SWA_HEAD_DIM, :] = vst[h * SWA_HEAD_DIM:(h + 1) * SWA_HEAD_DIM, :]
        vst_ref[h * VT_ROWS + SWA_HEAD_DIM:(h + 1) * VT_ROWS, :] = ones
    kr = p[:, 1280:1408] * ck_ref[...] + p[:, 1408:1536] * sk_ref[...]
    qq = _dot(cqn, wq_ref[...])
    kn = _dot(ckvn, wk_ref[...])
    cq = cq_ref[...]
    sq = sq_ref[...]
    for h in range(MLA_HEADS):
        a = qq[:, h * LANES:(h + 1) * LANES]
        b = qq[:, 1024 + h * LANES:1024 + (h + 1) * LANES]
        q_ref[:, h * LANES:(h + 1) * LANES] = (a * cq + b * sq).astype(BF16)
        k_ref[:, h * LANES:(h + 1) * LANES] = (kn[:, h * LANES:(h + 1) * LANES] + kr).astype(BF16)
    vt = _dot_nt(wvt_ref[...], ckvn).astype(BF16)
    for h in range(MLA_HEADS):
        vt_ref[h * VT_ROWS:h * VT_ROWS + MLA_V, :] = vt[h * MLA_V:(h + 1) * MLA_V, :]
        vt_ref[h * VT_ROWS + MLA_V:(h + 1) * VT_ROWS, :] = ones


def _proj(x2, w1, wvst, wq2, wk, wvt, gq, gkv, tabs, seq):
    T = x2.shape[0]
    tm = PROJ_TM
    nper = seq // tm
    full = lambda shape: pl.BlockSpec(shape, lambda i: (0, 0))
    tab = pl.BlockSpec((tm, LANES), lambda i: (i % nper, 0))
    row = lambda n: pl.BlockSpec((tm, n), lambda i: (i, 0))
    col = lambda n: pl.BlockSpec((n, tm), lambda i: (0, i))
    return pl.pallas_call(
        _proj_kernel,
        grid=(T // tm,),
        in_specs=[row(D_MODEL), full(w1.shape), full(wvst.shape), full(wq2.shape), full(wk.shape),
                  full(wvt.shape), full(gq.shape), full(gkv.shape), tab, tab, tab, tab],
        out_specs=[row(1024), row(1024), col(MLA_HEADS * VT_ROWS), row(512), row(512),
                   col(SWA_KV_HEADS * VT_ROWS)],
        out_shape=[jax.ShapeDtypeStruct((T, 1024), BF16), jax.ShapeDtypeStruct((T, 1024), BF16),
                   jax.ShapeDtypeStruct((MLA_HEADS * VT_ROWS, T), BF16), jax.ShapeDtypeStruct((T, 512), BF16),
                   jax.ShapeDtypeStruct((T, 512), BF16),
                   jax.ShapeDtypeStruct((SWA_KV_HEADS * VT_ROWS, T), BF16)],
        compiler_params=pltpu.CompilerParams(dimension_semantics=("parallel",),
                                             vmem_limit_bytes=VMEM_LIMIT),
        name="proj",
    )(x2, w1, wvst, wq2, wk, wvt, gq, gkv, *tabs)


def _mla_kernel(q_ref, k_ref, vt_ref, o_ref, *acc_scr):
    tq = q_ref.shape[0]
    qi = pl.program_id(2)
    for acc in acc_scr:
        acc[...] = jnp.zeros(acc.shape, F32)

    def step(ks, tk, maxes, masked):
        scores = [_dot_nt(k_ref[pl.ds(ks, tk), h * LANES:(h + 1) * LANES],
                          q_ref[:, h * LANES:(h + 1) * LANES]) for h in range(MLA_HPS)]
        new_maxes, probs, alphas = [], [], []
        for h in range(MLA_HPS):
            s = scores[h]
            if masked:
                key = ks + lax.broadcasted_iota(jnp.int32, s.shape, 0)
                qry = qi * tq + lax.broadcasted_iota(jnp.int32, s.shape, 1)
                s = jnp.where(key <= qry, s, NEG_INF)
            m_new = jnp.maximum(maxes[h], jnp.max(s, axis=0, keepdims=True))
            alphas.append(jnp.exp2(maxes[h] - m_new))
            probs.append(jnp.exp2(s - m_new).astype(BF16))
            new_maxes.append(m_new)
        for h in range(MLA_HPS):
            pv = _dot(vt_ref[h * VT_ROWS:(h + 1) * VT_ROWS, pl.ds(ks, tk)], probs[h])
            acc_scr[h][...] = acc_scr[h][...] * alphas[h] + pv
        return tuple(new_maxes)

    init = tuple(jnp.full((1, tq), NEG_INF, F32) for _ in range(MLA_HPS))
    big = MLA_TK
    n_big = (qi * tq) // big
    maxes = lax.fori_loop(
        0, n_big, lambda kc, mx: step(pl.multiple_of(kc * big, big), big, mx, False), init)
    rest = pl.multiple_of(n_big * big, tq)
    maxes = lax.cond(
        rest < qi * tq,
        lambda mx: step(rest, 2 * tq, mx, True),
        lambda mx: step(rest, tq, mx, True),
        maxes)
    for h2 in range(MLA_HPS // 2):
        out_t = jnp.concatenate(
            [acc_scr[2 * h2 + g][0:MLA_V, :] / acc_scr[2 * h2 + g][MLA_V:MLA_V + 1, :] for g in range(2)],
            axis=0)
        o_ref[:, h2 * LANES:(h2 + 1) * LANES] = out_t.T.astype(BF16)


def _mla(q, k, vt, batch, seq):
    T = q.shape[0]
    tq = MLA_TQ
    nq = seq // tq
    hps = MLA_HPS
    return pl.pallas_call(
        _mla_kernel,
        grid=(batch, MLA_HEADS // hps, nq),
        in_specs=[pl.BlockSpec((tq, hps * LANES), lambda b, j, i: (b * nq + i, j)),
                  pl.BlockSpec((seq, hps * LANES), lambda b, j, i: (b, j)),
                  pl.BlockSpec((hps * VT_ROWS, seq), lambda b, j, i: (j, b))],
        out_specs=pl.BlockSpec((tq, hps * MLA_V), lambda b, j, i: (b * nq + i, j)),
        out_shape=jax.ShapeDtypeStruct((T, MLA_HEADS * MLA_V), BF16),
        scratch_shapes=[pltpu.VMEM((VT_ROWS, tq), F32) for _ in range(hps)],
        compiler_params=pltpu.CompilerParams(
            dimension_semantics=("parallel", "parallel", "arbitrary"), vmem_limit_bytes=VMEM_LIMIT),
        name="mla",
    )(q, k, vt)


def _swa_kernel(sink_ref, q_ref, kc_ref, kp_ref, vtc_ref, vtp_ref, bias_ref, o_ref):
    scores = []
    for head in range(SWA_HEADS):
        pair, g, kvh = head // 2, head % 2, head // SWA_GROUP
        col = (2 * kvh + g) * LANES
        band = jnp.concatenate([kp_ref[:, col:col + LANES], kc_ref[:, col:col + LANES]], axis=0)
        scores.append(_dot_nt(band, q_ref[:, pair * LANES:(pair + 1) * LANES]))
    probs, sink_terms = [], []
    for head in range(SWA_HEADS):
        s = scores[head] + bias_ref[0, head]
        sink = sink_ref[head] * LOG2E
        m = jnp.maximum(jnp.max(s, axis=0, keepdims=True), sink)
        probs.append(jnp.exp2(s - m).astype(BF16))
        sink_terms.append(jnp.exp2(sink - m))
    outs = []
    for head in range(SWA_HEADS):
        rows = slice((head // SWA_GROUP) * VT_ROWS, (head // SWA_GROUP + 1) * VT_ROWS)
        v_band = jnp.concatenate([vtp_ref[rows, :], vtc_ref[rows, :]], axis=1)
        pv = _dot(v_band, probs[head])
        outs.append(pv[0:SWA_HEAD_DIM, :] / (pv[SWA_HEAD_DIM:SWA_HEAD_DIM + 1, :] + sink_terms[head]))
    o_ref[...] = jnp.concatenate(outs, axis=0).T.astype(BF16)


def _swa_bias():
    W = SWA_WINDOW
    j = jnp.arange(W + SWA_TQ, dtype=jnp.int32)[:, None]
    i = jnp.arange(SWA_TQ, dtype=jnp.int32)[None, :]
    dist = i + W - j
    valid = (dist >= 0) & (dist < W)
    slopes = 2.0 ** (-8.0 * jnp.arange(1, SWA_HEADS + 1, dtype=F32) / SWA_HEADS)
    pen = -(slopes[:, None, None] * dist.astype(F32)[None]) * LOG2E
    general = jnp.where(valid[None], pen, NEG_INF)
    first = jnp.where((valid & (j >= W))[None], pen, NEG_INF)
    return jnp.stack([first, general])


def _swa(sinks, qs, ks, vst, batch, seq):
    T = qs.shape[0]
    W, tq = SWA_WINDOW, SWA_TQ
    nb = seq // tq
    wpb = tq // W
    before = lambda b, i: b * nb * wpb + jnp.maximum(wpb * i - 1, 0)
    cur = lambda n: pl.BlockSpec((tq, n), lambda b, i: (b * nb + i, 0))
    prev = lambda n: pl.BlockSpec((W, n), lambda b, i: (before(b, i), 0))
    vt_cur = pl.BlockSpec((SWA_KV_HEADS * VT_ROWS, tq), lambda b, i: (0, b * nb + i))
    vt_prev = pl.BlockSpec((SWA_KV_HEADS * VT_ROWS, W), lambda b, i: (0, before(b, i)))
    bias = pl.BlockSpec((1, SWA_HEADS, W + tq, tq), lambda b, i: (jnp.minimum(i, 1), 0, 0, 0))
    return pl.pallas_call(
        _swa_kernel,
        grid=(batch, nb),
        in_specs=[pl.BlockSpec(memory_space=pltpu.SMEM), cur(512), cur(512), prev(512),
                  vt_cur, vt_prev, bias],
        out_specs=cur(512),
        out_shape=jax.ShapeDtypeStruct((T, 512), BF16),
        compiler_params=pltpu.CompilerParams(dimension_semantics=("parallel", "parallel"),
                                             vmem_limit_bytes=VMEM_LIMIT),
        name="swa",
    )(sinks, qs, ks, ks, vst, vst, _swa_bias())


def _post_kernel(x_ref, om_ref, os_ref, wom_ref, wos_ref, wg_ref, wout_ref, g1_ref, b1_ref,
                 wr_ref, rb_ref, tri_ref,
                 h1_ref, h1p_ref, e_ref, w_ref, r_ref, cnt_ref, carry_scr):
    tm = x_ref.shape[0]
    step = pl.program_id(0)

    @pl.when(step == 0)
    def _():
        carry_scr[...] = jnp.zeros(carry_scr.shape, F32)

    x = x_ref[...]
    ya = _dot(om_ref[...], wom_ref[...])
    yb = _dot(os_ref[...], wos_ref[...])
    gates = _dot(x.astype(BF16), wg_ref[...])
    merged = _sigmoid(gates[:, :D_MODEL]) * ya + _sigmoid(gates[:, D_MODEL:]) * yb
    mix = _dot(merged.astype(BF16), wout_ref[...])
    z = DEEPNORM_ALPHA * x + mix
    mu = jnp.mean(z, axis=-1, keepdims=True)
    zc = z - mu
    var = jnp.mean(zc * zc, axis=-1, keepdims=True)
    h1 = zc * lax.rsqrt(var + LN_EPS) * g1_ref[...] + b1_ref[...]
    h1_ref[...] = h1
    _pack_rows(h1, h1p_ref)

    all_scores = _sigmoid(_dot_nt(wr_ref[...], h1.astype(BF16)))
    carry = carry_scr[...]
    for c in range(tm // ROUTE_TN):
        cols = slice(c * ROUTE_TN, (c + 1) * ROUTE_TN)
        idxs, weights, ranks, carry = _route(all_scores[:, cols], rb_ref[...], tri_ref[...], carry)
        e_ref[:, cols] = idxs
        w_ref[:, cols] = weights
        r_ref[:, cols] = ranks
    carry_scr[...] = carry
    cnt_ref[...] = carry.astype(jnp.int32)


def _route(scores, bias, tri, carry):
    tn = scores.shape[1]
    choice = scores + bias
    row = lax.broadcasted_iota(jnp.int32, (N_EXPERTS, tn), 0)
    grow = lax.broadcasted_iota(jnp.int32, (GROUP_SIZE, tn), 0)
    gscore = []
    for g in range(N_GROUPS):
        blk = choice[g * GROUP_SIZE:(g + 1) * GROUP_SIZE, :]
        m1 = jnp.max(blk, axis=0, keepdims=True)
        i1 = jnp.min(jnp.where(blk == m1, grow, GROUP_SIZE), axis=0, keepdims=True)
        m2 = jnp.max(jnp.where(grow == i1, NEG_INF, blk), axis=0, keepdims=True)
        gscore.append(m1 + m2)
    gsc = jnp.concatenate(gscore, axis=0)
    gidx = lax.broadcasted_iota(jnp.int32, (N_GROUPS, tn), 0)
    grank = jnp.zeros((N_GROUPS, tn), jnp.int32)
    for g in range(N_GROUPS):
        sg = gsc[g:g + 1, :]
        beats = (sg > gsc) | ((sg == gsc) & (gidx > g))
        grank = grank + beats.astype(jnp.int32)
    gsel = (grank < TOPK_GROUPS).astype(F32)
    emask = jnp.concatenate(
        [jnp.broadcast_to(gsel[g:g + 1, :], (GROUP_SIZE, tn)) for g in range(N_GROUPS)], axis=0)
    work = jnp.where(emask > 0.0, choice, NEG_INF)
    eligible = work
    idxs, svals = [], []
    for _k in range(TOP_K):
        m = jnp.max(work, axis=0, keepdims=True)
        idx = jnp.min(jnp.where(work == m, row, N_EXPERTS), axis=0, keepdims=True)
        hit = row == idx
        svals.append(jnp.sum(jnp.where(hit, scores, 0.0), axis=0, keepdims=True))
        work = jnp.where(hit, NEG_INF, work)
        idxs.append(idx)
    sel = jnp.where(work != eligible, 1.0, 0.0)
    ssum = svals[0]
    for sv in svals[1:]:
        ssum = ssum + sv
    weights = jnp.concatenate([sv / ssum * ROUTED_SCALE for sv in svals], axis=0)

    rank = _dot(sel.astype(BF16), tri) + carry[:, 0:1]
    ranks = jnp.concatenate(
        [jnp.sum(jnp.where(row == idx, rank, 0.0), axis=0, keepdims=True) for idx in idxs],
        axis=0).astype(jnp.int32)
    carry = carry + jnp.sum(sel, axis=1, keepdims=True)
    return jnp.concatenate(idxs, axis=0), weights, ranks, carry


def _post(x2, o_mla, o_swa, wom, wos, wg, wout, g1, b1, wr_t, rbias, tri):
    T = x2.shape[0]
    tm = POST_TM
    full = lambda a: pl.BlockSpec(a.shape, lambda i: (0, 0))
    row = lambda n: pl.BlockSpec((tm, n), lambda i: (i, 0))
    col = pl.BlockSpec((TOP_K, tm), lambda i: (0, i))
    return pl.pallas_call(
        _post_kernel,
        grid=(T // tm,),
        in_specs=[row(D_MODEL), row(512), row(512), full(wom), full(wos), full(wg), full(wout),
                  full(g1), full(b1), full(wr_t), full(rbias), full(tri)],
        out_specs=[row(D_MODEL), pl.BlockSpec((tm * ROW_SUB, LANES), lambda i: (i, 0)), col, col, col,
                   pl.BlockSpec((N_EXPERTS, LANES), lambda i: (0, 0))],
        out_shape=[jax.ShapeDtypeStruct((T, D_MODEL), F32),
                   jax.ShapeDtypeStruct((T * ROW_SUB, LANES), jnp.uint32),
                   jax.ShapeDtypeStruct((TOP_K, T), jnp.int32),
                   jax.ShapeDtypeStruct((TOP_K, T), F32),
                   jax.ShapeDtypeStruct((TOP_K, T), jnp.int32),
                   jax.ShapeDtypeStruct((N_EXPERTS, LANES), jnp.int32)],
        scratch_shapes=[pltpu.VMEM((N_EXPERTS, LANES), F32)],
        compiler_params=pltpu.CompilerParams(dimension_semantics=("arbitrary",),
                                             vmem_limit_bytes=VMEM_LIMIT),
        name="post",
    )(x2, o_mla, o_swa, wom, wos, wg, wout, g1, b1, wr_t, rbias, tri)


def _dest_kernel(start_ref, e_ref, r_ref, o_ref):
    e = e_ref[...]
    base = jnp.zeros(e.shape, jnp.int32)
    for j in range(N_EXPERTS):
        base = jnp.where(e == j, start_ref[j], base)
    o_ref[...] = base + r_ref[...]


def _dest(seg_start, e_t, r_t):
    T = e_t.shape[1]
    tn = min(T, 4096)
    col = pl.BlockSpec((TOP_K, tn), lambda i: (0, i))
    return pl.pallas_call(
        _dest_kernel,
        grid=(T // tn,),
        in_specs=[pl.BlockSpec(memory_space=pltpu.SMEM), col, col],
        out_specs=col,
        out_shape=jax.ShapeDtypeStruct((TOP_K, T), jnp.int32),
        compiler_params=pltpu.CompilerParams(dimension_semantics=("parallel",)),
        name="dest",
    )(seg_start, e_t, r_t)


def _sc_worker_chunks(n_tokens):
    info = plsc.get_sparse_core_info()
    n_workers = info.num_cores * info.num_subcores
    per_worker = n_tokens // SC_CHUNK // n_workers
    assert per_worker * n_workers * SC_CHUNK == n_tokens
    first = (lax.axis_index("s") * info.num_cores + lax.axis_index("c")) * per_worker
    return first, per_worker


def _sc_dispatch(h1p, idx, n_rows):
    n_tokens = h1p.shape[0] // ROW_SUB
    R = SC_CHUNK * ROW_SUB

    def body(h_hbm, idx_hbm, xs_hbm, idx_v, rows_v, load_sem, scatter_sem):
        first, per_worker = _sc_worker_chunks(n_tokens)
        assert per_worker % 2 == 0

        def loads(c, slot):
            return (pltpu.make_async_copy(idx_hbm.at[c], idx_v.at[slot], load_sem.at[slot]),
                    pltpu.make_async_copy(h_hbm.at[pl.ds(c * R, R)], rows_v.at[slot], load_sem.at[slot]))

        def scatters(slot):
            return [pltpu.make_async_copy(rows_v.at[slot], xs_hbm.at[idx_v.at[slot, k]], scatter_sem.at[slot])
                    for k in range(TOP_K)]

        for cp in loads(first, 0):
            cp.start()

        @pl.loop(0, per_worker, step=2)
        def _(ci):
            for slot in range(2):
                c = first + ci + slot
                for cp in loads(c, slot):
                    cp.wait()

                @pl.when(ci + slot > 0)
                def _():
                    for cp in scatters(1 - slot):
                        cp.wait()

                @pl.when(ci + slot + 1 < per_worker)
                def _():
                    for cp in loads(c + 1, 1 - slot):
                        cp.start()

                for cp in scatters(slot):
                    cp.start()

        for cp in scatters(1):
            cp.wait()

    run = pl.kernel(
        body, out_type=jax.ShapeDtypeStruct((n_rows * ROW_SUB, LANES), jnp.uint32),
        mesh=plsc.VectorSubcoreMesh(core_axis_name="c", subcore_axis_name="s"),
        scratch_types=[pltpu.VMEM((2, TOP_K, R), jnp.int32), pltpu.VMEM((2, R, LANES), jnp.uint32),
                       pltpu.SemaphoreType.DMA((2,)), pltpu.SemaphoreType.DMA((2,))],
        name="sc_dispatch")
    return run(h1p, idx)


def _sc_gather(ys, idx, n_tokens):
    R = SC_CHUNK * ROW_SUB
    n_groups = TOP_K // 2

    def body(ys_hbm, idx_hbm, yg_hbm, idx_v, buf, isem, gsem, wsem):
        first, per_worker = _sc_worker_chunks(n_tokens)
        assert per_worker % 2 == 0

        def idx_load(c, islot):
            return pltpu.make_async_copy(idx_hbm.at[c], idx_v.at[islot], isem.at[islot])

        def gathers(islot, q):
            s = q % 2
            return [pltpu.make_async_copy(ys_hbm.at[idx_v.at[islot, 2 * q + j]], buf.at[s, j], gsem.at[s])
                    for j in range(2)]

        def writes(c, q):
            s = q % 2
            return [pltpu.make_async_copy(buf.at[s, j], yg_hbm.at[2 * q + j, pl.ds(c * R, R)], wsem.at[s])
                    for j in range(2)]

        idx_load(first, 0).start()

        @pl.loop(0, per_worker, step=2)
        def _(ci):
            for islot in range(2):
                c = first + ci + islot
                idx_load(c, islot).wait()
                for q in range(n_groups):
                    if q >= 2:
                        for cp in writes(c, q - 2):
                            cp.wait()
                    else:
                        @pl.when(ci + islot > 0)
                        def _():
                            for cp in writes(c - 1, q + 2):
                                cp.wait()
                    for cp in gathers(islot, q):
                        cp.start()
                    if q >= 1:
                        for cp in gathers(islot, q - 1):
                            cp.wait()
                        for cp in writes(c, q - 1):
                            cp.start()
                    else:
                        @pl.when(ci + islot > 0)
                        def _():
                            for cp in gathers(1 - islot, n_groups - 1):
                                cp.wait()
                            for cp in writes(c - 1, n_groups - 1):
                                cp.start()

                        @pl.when(ci + islot + 1 < per_worker)
                        def _():
                            idx_load(c + 1, 1 - islot).start()

        last = first + per_worker - 1
        for cp in gathers(1, n_groups - 1):
            cp.wait()
        for cp in writes(last, n_groups - 1):
            cp.start()
        for q in (n_groups - 2, n_groups - 1):
            for cp in writes(last, q):
                cp.wait()

    run = pl.kernel(
        body, out_type=jax.ShapeDtypeStruct((TOP_K, n_tokens * ROW_SUB, LANES), jnp.uint32),
        mesh=plsc.VectorSubcoreMesh(core_axis_name="c", subcore_axis_name="s"),
        scratch_types=[pltpu.VMEM((2, TOP_K, R), jnp.int32), pltpu.VMEM((2, 2, R, LANES), jnp.uint32),
                       pltpu.SemaphoreType.DMA((2,)), pltpu.SemaphoreType.DMA((2,)),
                       pltpu.SemaphoreType.DMA((2,))],
        name="sc_gather")
    return run(ys, idx)


def _experts_kernel(first_ref, nblk_ref, cnt_ref, xs_hbm, wup_ref, wdn_ref, after_ref, ys_hbm,
                    wup_bf, wdn_bf, xbuf, ybuf, in_sem, out_sem):
    del after_ref
    e = pl.program_id(0)
    n = nblk_ref[e]
    b0 = first_ref[e]
    count = cnt_ref[e]
    total = first_ref[N_EXPERTS - 1] + nblk_ref[N_EXPERTS - 1]
    rb = ROW_BLK * ROW_SUB
    depth = EXPERT_SLOTS

    def slot_of(b):
        return lax.rem(b, depth)

    def rows_of(b):
        return pl.ds(pl.multiple_of(b * rb, rb), rb)

    def in_copy(b):
        return pltpu.make_async_copy(xs_hbm.at[rows_of(b)], xbuf.at[slot_of(b)], in_sem.at[slot_of(b)])

    def out_copy(b):
        return pltpu.make_async_copy(ybuf.at[slot_of(b)], ys_hbm.at[rows_of(b)], out_sem.at[slot_of(b)])

    @pl.when(e == 0)
    def _():
        for b in range(depth - 1):
            @pl.when(b < total)
            def _():
                in_copy(b).start()

    @pl.when(n > 0)
    def _():
        wup_bf[...] = wup_ref[0].astype(BF16)
        wdn_bf[...] = wdn_ref[0].astype(BF16)

        def body(b, carry):
            slot = slot_of(b)
            in_copy(b).wait()

            @pl.when(b + depth - 1 < total)
            def _():
                in_copy(b + depth - 1).start()

            @pl.when(b >= depth)
            def _():
                out_copy(b - depth).wait()

            sub = ROW_BLK // EXPERT_SPLIT
            n_valid = count - (b - b0) * ROW_BLK
            xbs = [jnp.concatenate(
                [c.astype(BF16) for c in _unpack_rows(xbuf.at[slot], sub, first=i * sub, n_valid=n_valid)],
                axis=1) for i in range(EXPERT_SPLIT)]
            gus = [_dot(xb, wup_bf[...]) for xb in xbs]
            hids = [(gu[:, :EXPERT_HIDDEN] * _sigmoid(gu[:, :EXPERT_HIDDEN])
                     * gu[:, EXPERT_HIDDEN:]).astype(BF16) for gu in gus]
            ys = [_dot(hid, wdn_bf[...]) for hid in hids]
            for i in range(EXPERT_SPLIT):
                _pack_rows(ys[i], ybuf.at[slot], first=i * sub)
            out_copy(b).start()
            return carry

        lax.fori_loop(b0, b0 + n, body, 0)

    @pl.when(e == N_EXPERTS - 1)
    def _():
        for back in range(depth, 0, -1):
            @pl.when(total - back >= 0)
            def _():
                out_copy(total - back).wait()


def _experts(first_blk, n_blk, counts, xs, w_exp_up, w_exp_down, run_after):
    rb = ROW_BLK * ROW_SUB
    grid_spec = pltpu.PrefetchScalarGridSpec(
        num_scalar_prefetch=3,
        grid=(N_EXPERTS,),
        in_specs=[pl.BlockSpec(memory_space=pl.ANY),
                  pl.BlockSpec((1, D_MODEL, 2 * EXPERT_HIDDEN), lambda e, fb, nb, ct: (e, 0, 0)),
                  pl.BlockSpec((1, EXPERT_HIDDEN, D_MODEL), lambda e, fb, nb, ct: (e, 0, 0)),
                  pl.BlockSpec(memory_space=pl.ANY)],
        out_specs=pl.BlockSpec(memory_space=pl.ANY),
        scratch_shapes=[pltpu.VMEM((D_MODEL, 2 * EXPERT_HIDDEN), BF16),
                        pltpu.VMEM((EXPERT_HIDDEN, D_MODEL), BF16),
                        pltpu.VMEM((EXPERT_SLOTS, rb, LANES), jnp.uint32),
                        pltpu.VMEM((EXPERT_SLOTS, rb, LANES), jnp.uint32),
                        pltpu.SemaphoreType.DMA((EXPERT_SLOTS,)),
                        pltpu.SemaphoreType.DMA((EXPERT_SLOTS,))],
    )
    return pl.pallas_call(
        _experts_kernel,
        grid_spec=grid_spec,
        out_shape=jax.ShapeDtypeStruct(xs.shape, jnp.uint32),
        compiler_params=pltpu.CompilerParams(dimension_semantics=("arbitrary",),
                                             vmem_limit_bytes=VMEM_LIMIT),
        name="experts",
    )(first_blk, n_blk, counts, xs, w_exp_up, w_exp_down, run_after)


def _shared_kernel(h1p_ref, wsu_ref, wsd_ref, o_ref):
    tm = o_ref.shape[0]
    hb = jnp.concatenate([c.astype(BF16) for c in _unpack_rows(h1p_ref, tm)], axis=1)
    gu = _dot(hb, wsu_ref[...])
    g = gu[:, :SHARED_HIDDEN]
    hid = g * _sigmoid(g) * gu[:, SHARED_HIDDEN:]
    o_ref[...] = _dot(hid.astype(BF16), wsd_ref[...]).astype(BF16)


def _shared(h1p, wsu, wsd):
    T = h1p.shape[0] // ROW_SUB
    tm = SHARED_TM
    full = lambda a: pl.BlockSpec(a.shape, lambda i: (0, 0))
    row = pl.BlockSpec((tm, D_MODEL), lambda i: (i, 0))
    return pl.pallas_call(
        _shared_kernel,
        grid=(T // tm,),
        in_specs=[pl.BlockSpec((tm * ROW_SUB, LANES), lambda i: (i, 0)), full(wsu), full(wsd)],
        out_specs=row,
        out_shape=jax.ShapeDtypeStruct((T, D_MODEL), BF16),
        compiler_params=pltpu.CompilerParams(dimension_semantics=("parallel",),
                                             vmem_limit_bytes=VMEM_LIMIT),
        name="shared",
    )(h1p, wsu, wsd)


def _combine_kernel(w_ref, h1_ref, sh_ref, yg_ref, g2_ref, b2_ref, *rest):
    o_ref = rest[-1]
    tm = h1_ref.shape[0]
    h1 = h1_ref[...]
    ffn = sh_ref[...].astype(F32)

    w = w_ref[...]
    acc = [None] * (2 * ROW_SUB)
    for k in range(TOP_K):
        wk = w[:, k:k + 1]
        chunks = _unpack_rows(yg_ref.at[k], tm)
        for c in range(2 * ROW_SUB):
            acc[c] = wk * chunks[c] if acc[c] is None else acc[c] + wk * chunks[c]
    routed = jnp.concatenate(acc, axis=1)
    z = DEEPNORM_ALPHA * h1 + (routed + ffn)
    mu = jnp.mean(z, axis=-1, keepdims=True)
    zc = z - mu
    var = jnp.mean(zc * zc, axis=-1, keepdims=True)
    o_ref[...] = zc * lax.rsqrt(var + LN_EPS) * g2_ref[...] + b2_ref[...]


def _combine(w_tok, h1, sh, yg, g2, b2, first_token, out_prev):
    n_tok = yg.shape[1] // ROW_SUB
    tm = COMBINE_TM
    first = first_token // tm
    full = lambda a: pl.BlockSpec(a.shape, lambda i: (0, 0))
    row = pl.BlockSpec((tm, D_MODEL), lambda i: (first + i, 0))
    in_specs = [pl.BlockSpec((tm, TOP_K), lambda i: (first + i, 0)), row, row,
                pl.BlockSpec((TOP_K, tm * ROW_SUB, LANES), lambda i: (0, i, 0)),
                full(g2), full(b2)]
    args = [w_tok, h1, sh, yg, g2, b2]
    aliases = {}
    if out_prev is not None:
        in_specs.append(pl.BlockSpec(memory_space=pl.ANY))
        args.append(out_prev)
        aliases = {len(args) - 1: 0}
    return pl.pallas_call(
        _combine_kernel,
        grid=(n_tok // tm,),
        in_specs=in_specs,
        out_specs=row,
        out_shape=jax.ShapeDtypeStruct(h1.shape, F32),
        input_output_aliases=aliases,
        compiler_params=pltpu.CompilerParams(dimension_semantics=("parallel",),
                                             vmem_limit_bytes=VMEM_LIMIT),
        name="combine",
    )(*args)


def _prep_weights(w_in, w_uq, w_ukv, seq):
    z = lambda n: jnp.zeros((D_MODEL, n), F32)
    kr = w_in[:, 512:544]
    qs = w_in[:, 544:1056] * (LOG2E / math.sqrt(SWA_HEAD_DIM))
    ks0, ks1 = w_in[:, 1056:1120], w_in[:, 1120:1184]
    half = MLA_ROPE // 2
    w1 = jnp.concatenate([
        w_in[:, 0:512], qs,
        ks0, ks1, ks1, ks0,
        z(64), kr, z(32),
        z(64), -kr[:, half:], kr[:, :half], z(32)], axis=1).astype(BF16)
    wvst = w_in[:, 1184:1312].T.astype(BF16)
    wg = w_in[:, 1312:3360].astype(BF16)

    pad3 = lambda a, lo, hi: jnp.pad(a, ((0, 0), (0, 0), (lo, hi)))
    wq3 = w_uq.reshape(MLA_Q_LORA, MLA_HEADS, MLA_NOPE + MLA_ROPE)
    rope = wq3[:, :, MLA_NOPE:]
    rot = jnp.concatenate([-rope[:, :, half:], rope[:, :, :half]], axis=2)
    wq2 = jnp.concatenate([pad3(wq3, 0, 32).reshape(MLA_Q_LORA, -1),
                           pad3(rot, MLA_NOPE, 32).reshape(MLA_Q_LORA, -1)], axis=1).astype(BF16)
    wkv3 = w_ukv.reshape(MLA_KV_LORA, MLA_HEADS, MLA_NOPE + MLA_V)
    wk = pad3(wkv3[:, :, :MLA_NOPE], 0, 64).reshape(MLA_KV_LORA, -1).astype(BF16)
    wvt = wkv3[:, :, MLA_NOPE:].reshape(MLA_KV_LORA, -1).T.astype(BF16)

    inv_freq = ROPE_THETA ** (-jnp.arange(0, MLA_ROPE, 2, dtype=F32) / MLA_ROPE)
    ang = jnp.arange(seq, dtype=F32)[:, None] * inv_freq[None, :]
    cos, sin = jnp.cos(ang), jnp.sin(ang)
    one, zero = jnp.ones((seq, 64), F32), jnp.zeros((seq, 64), F32)
    z32 = jnp.zeros((seq, 32), F32)
    scale = LOG2E / math.sqrt(MLA_NOPE + MLA_ROPE)
    tabs = (jnp.concatenate([one, cos, cos, z32], axis=1) * scale,
            jnp.concatenate([zero, sin, sin, z32], axis=1) * scale,
            jnp.concatenate([zero, cos, cos, z32], axis=1),
            jnp.concatenate([zero, sin, sin, z32], axis=1))
    return w1, wvst, wg, wq2, wk, wvt, tabs


def kernel(x, w_in, mla_q_norm, mla_kv_norm, w_uq, w_ukv, attn_sinks, w_o_mla, w_o_swa, w_out,
           ln1_g, ln1_b, w_router, router_bias, w_exp_up, w_exp_down, w_sh_up, w_sh_down,
           ln2_g, ln2_b):
    batch, seq, _ = x.shape
    T = batch * seq
    x2 = x.reshape(T, D_MODEL)
    w1, wvst, wg, wq2, wk, wvt, tabs = _prep_weights(w_in, w_uq, w_ukv, seq)
    q, k, vt, qs, ks, vst = _proj(x2, w1, wvst, wq2, wk, wvt, mla_q_norm.reshape(1, -1),
                                  mla_kv_norm.reshape(1, -1), tabs, seq)
    o_mla = _mla(q, k, vt, batch, seq)
    o_swa = _swa(attn_sinks.astype(F32), qs, ks, vst, batch, seq)

    tri = (lax.broadcasted_iota(jnp.int32, (ROUTE_TN, ROUTE_TN), 0)
           < lax.broadcasted_iota(jnp.int32, (ROUTE_TN, ROUTE_TN), 1)).astype(BF16)
    h1, h1p, e_t, w_t, r_t, cnt = _post(
        x2, o_mla, o_swa, w_o_mla.astype(BF16), w_o_swa.astype(BF16), wg, w_out.astype(BF16),
        ln1_g.reshape(1, -1), ln1_b.reshape(1, -1), w_router.T.astype(BF16),
        router_bias.reshape(-1, 1).astype(F32), tri)

    counts = cnt[:, 0]
    pad = (counts + ROW_BLK - 1) // ROW_BLK * ROW_BLK
    pad_start = jnp.cumsum(pad) - pad
    n_rows = (T * TOP_K // ROW_BLK + N_EXPERTS) * ROW_BLK
    dest = _dest(pad_start.astype(jnp.int32), e_t, r_t)
    idx = (dest.reshape(TOP_K, T // SC_CHUNK, SC_CHUNK, 1) * ROW_SUB
           + jnp.arange(ROW_SUB, dtype=jnp.int32))
    idx = idx.transpose(1, 0, 2, 3).reshape(T // SC_CHUNK, TOP_K, SC_CHUNK * ROW_SUB)
    xs = _sc_dispatch(h1p, idx, n_rows)
    sh = _shared(h1p, w_sh_up.astype(BF16), w_sh_down.astype(BF16))
    ys = _experts((pad_start // ROW_BLK).astype(jnp.int32), (pad // ROW_BLK).astype(jnp.int32),
                  counts.astype(jnp.int32), xs, w_exp_up, w_exp_down, sh)

    Tc = T // COMBINE_PARTS
    n_chunks = Tc // SC_CHUNK
    w_tok = w_t.T
    g2, b2 = ln2_g.reshape(1, -1), ln2_b.reshape(1, -1)
    out = None
    for cpart in range(COMBINE_PARTS):
        yg = _sc_gather(ys, idx[cpart * n_chunks:(cpart + 1) * n_chunks], Tc)
        out = _combine(w_tok, h1, sh, yg, g2, b2, cpart * Tc, out)
    return out.reshape(batch, seq, D_MODEL)
```

```python
import math

import jax
import jax.numpy as jnp
from jax import lax
from jax.experimental import pallas as pl
from jax.experimental.pallas import tpu as pltpu
from jax.experimental.pallas import tpu_sc as plsc

D_MODEL = 1024
MLA_HEADS = 8
MLA_Q_LORA = 256
MLA_KV_LORA = 256
MLA_NOPE = 64
MLA_ROPE = 32
MLA_V = 64
VT_ROWS = 80
ROPE_THETA = 10000.0
SWA_HEADS = 8
SWA_KV_HEADS = 2
SWA_GROUP = SWA_HEADS // SWA_KV_HEADS
SWA_HEAD_DIM = 64
SWA_WINDOW = 128
SWA_TQ = 256
N_EXPERTS = 256
TOP_K = 8
N_GROUPS = 8
GROUP_SIZE = N_EXPERTS // N_GROUPS
TOPK_GROUPS = 4
EXPERT_HIDDEN = 256
SHARED_HIDDEN = 256
ROUTED_SCALE = 2.5
DEEPNORM_ALPHA = 2.0 ** 0.25
LN_EPS = 1e-5
RMS_EPS = 1e-6

LANES = 128
ROW_WORDS = D_MODEL // 2
ROW_SUB = ROW_WORDS // LANES
VMEM_LIMIT = 48 * 1024 * 1024

PROJ_TM = 512
MLA_TQ = 256
MLA_TK = 512
MLA_HPS = 8
POST_TM = 512
ROUTE_TN = 256
ROW_BLK = 256
EXPERT_SLOTS = 4
EXPERT_SPLIT = 2
SHARED_TM = 512
COMBINE_TM = 256
SC_CHUNK = 32
COMBINE_PARTS = 4

BF16 = jnp.bfloat16
F32 = jnp.float32
NEG_INF = float("-inf")
LOG2E = math.log2(math.e)


def _sigmoid(v):
    return 1.0 / (1.0 + jnp.exp(-v))


def _dot(a, b):
    return jnp.dot(a, b, preferred_element_type=F32)


def _dot_nt(a, b):
    return lax.dot_general(a, b, (((1,), (1,)), ((), ())), preferred_element_type=F32)


def _pack_rows(y, out_ref, first=0):
    rows = y.shape[0]
    for j in range(ROW_SUB):
        a = y[:, j * LANES:(j + 1) * LANES].astype(BF16).astype(F32)
        b = y[:, ROW_WORDS + j * LANES:ROW_WORDS + (j + 1) * LANES].astype(BF16).astype(F32)
        ua = pltpu.bitcast(a, jnp.uint32) >> 16
        ub = pltpu.bitcast(b, jnp.uint32)
        out_ref[pl.ds(first * ROW_SUB + j, rows, stride=ROW_SUB), :] = ua | ub


def _unpack_rows(ref, rows, first=0, n_valid=None):
    lo, hi = [], []
    if n_valid is not None:
        live = first + lax.broadcasted_iota(jnp.int32, (rows, LANES), 0) < n_valid
    for j in range(ROW_SUB):
        u = ref[pl.ds(first * ROW_SUB + j, rows, stride=ROW_SUB), :]
        if n_valid is not None:
            u = jnp.where(live, u, jnp.uint32(0))
        lo.append(pltpu.bitcast(u << 16, F32))
        hi.append(pltpu.bitcast(u & jnp.uint32(0xFFFF0000), F32))
    return lo + hi


def _proj_kernel(x_ref, w1_ref, wvst_ref, wq_ref, wk_ref, wvt_ref, gq_ref, gkv_ref,
                 cq_ref, sq_ref, ck_ref, sk_ref,
                 q_ref, k_ref, vt_ref, qs_ref, ks_ref, vst_ref):
    xb = x_ref[...].astype(BF16)
    p = _dot(xb, w1_ref[...])

    def rms(c, g):
        return c * lax.rsqrt(jnp.mean(c * c, axis=-1, keepdims=True) + RMS_EPS) * g

    cqn = rms(p[:, 0:256], gq_ref[...]).astype(BF16)
    ckvn = rms(p[:, 256:512], gkv_ref[...]).astype(BF16)
    qs_ref[...] = p[:, 512:1024].astype(BF16)
    k01 = p[:, 1024:1152].astype(BF16)
    k10 = p[:, 1152:1280].astype(BF16)
    lane_lo = lax.broadcasted_iota(jnp.int32, k01.shape, 1) < SWA_HEAD_DIM
    zero = jnp.zeros_like(k01)
    ks_ref[:, 0 * LANES:1 * LANES] = jnp.where(lane_lo, k01, zero)
    ks_ref[:, 1 * LANES:2 * LANES] = jnp.where(lane_lo, zero, k10)
    ks_ref[:, 2 * LANES:3 * LANES] = jnp.where(lane_lo, k10, zero)
    ks_ref[:, 3 * LANES:4 * LANES] = jnp.where(lane_lo, zero, k01)
    ones = jnp.ones((VT_ROWS - MLA_V, xb.shape[0]), BF16)
    vst = _dot_nt(wvst_ref[...], xb).astype(BF16)
    for h in range(SWA_KV_HEADS):
        vst_ref[h * VT_ROWS:h * VT_ROWS + SWA_HEAD_DIM, :] = vst[h * SWA_HEAD_DIM:(h + 1) * SWA_HEAD_DIM, :]
        vst_ref[h * VT_ROWS + SWA_HEAD_DIM:(h + 1) * VT_ROWS, :] = ones
    kr = p[:, 1280:1408] * ck_ref[...] + p[:, 1408:1536] * sk_ref[...]
    qq = _dot(cqn, wq_ref[...])
    kn = _dot(ckvn, wk_ref[...])
    cq = cq_ref[...]
    sq = sq_ref[...]
    for h in range(MLA_HEADS):
        a = qq[:, h * LANES:(h + 1) * LANES]
        b = qq[:, 1024 + h * LANES:1024 + (h + 1) * LANES]
        q_ref[:, h * LANES:(h + 1) * LANES] = (a * cq + b * sq).astype(BF16)
        k_ref[:, h * LANES:(h + 1) * LANES] = (kn[:, h * LANES:(h + 1) * LANES] + kr).astype(BF16)
    vt = _dot_nt(wvt_ref[...], ckvn).astype(BF16)
    for h in range(MLA_HEADS):
        vt_ref[h * VT_ROWS:h * VT_ROWS + MLA_V, :] = vt[h * MLA_V:(h + 1) * MLA_V, :]
        vt_ref[h * VT_ROWS + MLA_V:(h + 1) * VT_ROWS, :] = ones


def _proj(x2, w1, wvst, wq2, wk, wvt, gq, gkv, tabs, seq):
    T = x2.shape[0]
    tm = PROJ_TM
    nper = seq // tm
    full = lambda shape: pl.BlockSpec(shape, lambda i: (0, 0))
    tab = pl.BlockSpec((tm, LANES), lambda i: (i % nper, 0))
    row = lambda n: pl.BlockSpec((tm, n), lambda i: (i, 0))
    col = lambda n: pl.BlockSpec((n, tm), lambda i: (0, i))
    return pl.pallas_call(
        _proj_kernel,
        grid=(T // tm,),
        in_specs=[row(D_MODEL), full(w1.shape), full(wvst.shape), full(wq2.shape), full(wk.shape),
                  full(wvt.shape), full(gq.shape), full(gkv.shape), tab, tab, tab, tab],
        out_specs=[row(1024), row(1024), col(MLA_HEADS * VT_ROWS), row(512), row(512),
                   col(SWA_KV_HEADS * VT_ROWS)],
        out_shape=[jax.ShapeDtypeStruct((T, 1024), BF16), jax.ShapeDtypeStruct((T, 1024), BF16),
                   jax.ShapeDtypeStruct((MLA_HEADS * VT_ROWS, T), BF16), jax.ShapeDtypeStruct((T, 512), BF16),
                   jax.ShapeDtypeStruct((T, 512), BF16),
                   jax.ShapeDtypeStruct((SWA_KV_HEADS * VT_ROWS, T), BF16)],
        compiler_params=pltpu.CompilerParams(dimension_semantics=("parallel",),
                                             vmem_limit_bytes=VMEM_LIMIT),
        name="proj",
    )(x2, w1, wvst, wq2, wk, wvt, gq, gkv, *tabs)


def _mla_kernel(q_ref, k_ref, vt_ref, o_ref, *acc_scr):
    tq = q_ref.shape[0]
    qi = pl.program_id(2)
    for acc in acc_scr:
        acc[...] = jnp.zeros(acc.shape, F32)

    def step(ks, tk, maxes, masked):
        scores = [_dot_nt(k_ref[pl.ds(ks, tk), h * LANES:(h + 1) * LANES],
                          q_ref[:, h * LANES:(h + 1) * LANES]) for h in range(MLA_HPS)]
        new_maxes, probs, alphas = [], [], []
        for h in range(MLA_HPS):
            s = scores[h]
            if masked:
                key = ks + lax.broadcasted_iota(jnp.int32, s.shape, 0)
                qry = qi * tq + lax.broadcasted_iota(jnp.int32, s.shape, 1)
                s = jnp.where(key <= qry, s, NEG_INF)
            m_new = jnp.maximum(maxes[h], jnp.max(s, axis=0, keepdims=True))
            alphas.append(jnp.exp2(maxes[h] - m_new))
            probs.append(jnp.exp2(s - m_new).astype(BF16))
            new_maxes.append(m_new)
        for h in range(MLA_HPS):
            pv = _dot(vt_ref[h * VT_ROWS:(h + 1) * VT_ROWS, pl.ds(ks, tk)], probs[h])
            acc_scr[h][...] = acc_scr[h][...] * alphas[h] + pv
        return tuple(new_maxes)

    init = tuple(jnp.full((1, tq), NEG_INF, F32) for _ in range(MLA_HPS))
    big = MLA_TK
    n_big = (qi * tq) // big
    maxes = lax.fori_loop(
        0, n_big, lambda kc, mx: step(pl.multiple_of(kc * big, big), big, mx, False), init)
    rest = pl.multiple_of(n_big * big, tq)
    maxes = lax.cond(
        rest < qi * tq,
        lambda mx: step(rest, 2 * tq, mx, True),
        lambda mx: step(rest, tq, mx, True),
        maxes)
    for h2 in range(MLA_HPS // 2):
        out_t = jnp.concatenate(
            [acc_scr[2 * h2 + g][0:MLA_V, :] / acc_scr[2 * h2 + g][MLA_V:MLA_V + 1, :] for g in range(2)],
            axis=0)
        o_ref[:, h2 * LANES:(h2 + 1) * LANES] = out_t.T.astype(BF16)


def _mla(q, k, vt, batch, seq):
    T = q.shape[0]
    tq = MLA_TQ
    nq = seq // tq
    hps = MLA_HPS
    return pl.pallas_call(
        _mla_kernel,
        grid=(batch, MLA_HEADS // hps, nq),
        in_specs=[pl.BlockSpec((tq, hps * LANES), lambda b, j, i: (b * nq + i, j)),
                  pl.BlockSpec((seq, hps * LANES), lambda b, j, i: (b, j)),
                  pl.BlockSpec((hps * VT_ROWS, seq), lambda b, j, i: (j, b))],
        out_specs=pl.BlockSpec((tq, hps * MLA_V), lambda b, j, i: (b * nq + i, j)),
        out_shape=jax.ShapeDtypeStruct((T, MLA_HEADS * MLA_V), BF16),
        scratch_shapes=[pltpu.VMEM((VT_ROWS, tq), F32) for _ in range(hps)],
        compiler_params=pltpu.CompilerParams(
            dimension_semantics=("parallel", "parallel", "arbitrary"), vmem_limit_bytes=VMEM_LIMIT),
        name="mla",
    )(q, k, vt)


def _swa_kernel(sink_ref, q_ref, kc_ref, kp_ref, vtc_ref, vtp_ref, bias_ref, o_ref):
    scores = []
    for head in range(SWA_HEADS):
        pair, g, kvh = head // 2, head % 2, head // SWA_GROUP
        col = (2 * kvh + g) * LANES
        band = jnp.concatenate([kp_ref[:, col:col + LANES], kc_ref[:, col:col + LANES]], axis=0)
        scores.append(_dot_nt(band, q_ref[:, pair * LANES:(pair + 1) * LANES]))
    probs, sink_terms = [], []
    for head in range(SWA_HEADS):
        s = scores[head] + bias_ref[0, head]
        sink = sink_ref[head] * LOG2E
        m = jnp.maximum(jnp.max(s, axis=0, keepdims=True), sink)
        probs.append(jnp.exp2(s - m).astype(BF16))
        sink_terms.append(jnp.exp2(sink - m))
    outs = []
    for head in range(SWA_HEADS):
        rows = slice((head // SWA_GROUP) * VT_ROWS, (head // SWA_GROUP + 1) * VT_ROWS)
        v_band = jnp.concatenate([vtp_ref[rows, :], vtc_ref[rows, :]], axis=1)
        pv = _dot(v_band, probs[head])
        outs.append(pv[0:SWA_HEAD_DIM, :] / (pv[SWA_HEAD_DIM:SWA_HEAD_DIM + 1, :] + sink_terms[head]))
    o_ref[...] = jnp.concatenate(outs, axis=0).T.astype(BF16)


def _swa_bias():
    W = SWA_WINDOW
    j = jnp.arange(W + SWA_TQ, dtype=jnp.int32)[:, None]
    i = jnp.arange(SWA_TQ, dtype=jnp.int32)[None, :]
    dist = i + W - j
    valid = (dist >= 0) & (dist < W)
    slopes = 2.0 ** (-8.0 * jnp.arange(1, SWA_HEADS + 1, dtype=F32) / SWA_HEADS)
    pen = -(slopes[:, None, None] * dist.astype(F32)[None]) * LOG2E
    general = jnp.where(valid[None], pen, NEG_INF)
    first = jnp.where((valid & (j >= W))[None], pen, NEG_INF)
    return jnp.stack([first, general])


def _swa(sinks, qs, ks, vst, batch, seq):
    T = qs.shape[0]
    W, tq = SWA_WINDOW, SWA_TQ
    nb = seq // tq
    wpb = tq // W
    before = lambda b, i: b * nb * wpb + jnp.maximum(wpb * i - 1, 0)
    cur = lambda n: pl.BlockSpec((tq, n), lambda b, i: (b * nb + i, 0))
    prev = lambda n: pl.BlockSpec((W, n), lambda b, i: (before(b, i), 0))
    vt_cur = pl.BlockSpec((SWA_KV_HEADS * VT_ROWS, tq), lambda b, i: (0, b * nb + i))
    vt_prev = pl.BlockSpec((SWA_KV_HEADS * VT_ROWS, W), lambda b, i: (0, before(b, i)))
    bias = pl.BlockSpec((1, SWA_HEADS, W + tq, tq), lambda b, i: (jnp.minimum(i, 1), 0, 0, 0))
    return pl.pallas_call(
        _swa_kernel,
        grid=(batch, nb),
        in_specs=[pl.BlockSpec(memory_space=pltpu.SMEM), cur(512), cur(512), prev(512),
                  vt_cur, vt_prev, bias],
        out_specs=cur(512),
        out_shape=jax.ShapeDtypeStruct((T, 512), BF16),
        compiler_params=pltpu.CompilerParams(dimension_semantics=("parallel", "parallel"),
                                             vmem_limit_bytes=VMEM_LIMIT),
        name="swa",
    )(sinks, qs, ks, ks, vst, vst, _swa_bias())


def _post_kernel(x_ref, om_ref, os_ref, wom_ref, wos_ref, wg_ref, wout_ref, g1_ref, b1_ref,
                 wr_ref, rb_ref, tri_ref, wsu_ref, wsd_ref,
                 h1_ref, h1p_ref, e_ref, w_ref, r_ref, cnt_ref, sh_ref, carry_scr):
    tm = x_ref.shape[0]
    step = pl.program_id(0)

    @pl.when(step == 0)
    def _():
        carry_scr[...] = jnp.zeros(carry_scr.shape, F32)

    x = x_ref[...]
    ya = _dot(om_ref[...], wom_ref[...])
    yb = _dot(os_ref[...], wos_ref[...])
    gates = _dot(x.astype(BF16), wg_ref[...])
    merged = _sigmoid(gates[:, :D_MODEL]) * ya + _sigmoid(gates[:, D_MODEL:]) * yb
    mix = _dot(merged.astype(BF16), wout_ref[...])
    z = DEEPNORM_ALPHA * x + mix
    mu = jnp.mean(z, axis=-1, keepdims=True)
    zc = z - mu
    var = jnp.mean(zc * zc, axis=-1, keepdims=True)
    h1 = zc * lax.rsqrt(var + LN_EPS) * g1_ref[...] + b1_ref[...]
    h1_ref[...] = h1
    _pack_rows(h1, h1p_ref)
    hb = h1.astype(BF16)

    gu = _dot(hb, wsu_ref[...])
    g = gu[:, :SHARED_HIDDEN]
    hid = g * _sigmoid(g) * gu[:, SHARED_HIDDEN:]
    sh_ref[...] = _dot(hid.astype(BF16), wsd_ref[...]).astype(BF16)

    all_scores = _sigmoid(_dot_nt(wr_ref[...], hb))
    carry = carry_scr[...]
    for c in range(tm // ROUTE_TN):
        cols = slice(c * ROUTE_TN, (c + 1) * ROUTE_TN)
        idxs, weights, ranks, carry = _route(all_scores[:, cols], rb_ref[...], tri_ref[...], carry)
        e_ref[:, cols] = idxs
        w_ref[:, cols] = weights
        r_ref[:, cols] = ranks
    carry_scr[...] = carry
    cnt_ref[...] = carry.astype(jnp.int32)


def _route(scores, bias, tri, carry):
    tn = scores.shape[1]
    choice = scores + bias
    row = lax.broadcasted_iota(jnp.int32, (N_EXPERTS, tn), 0)
    grow = lax.broadcasted_iota(jnp.int32, (GROUP_SIZE, tn), 0)
    gscore = []
    for g in range(N_GROUPS):
        blk = choice[g * GROUP_SIZE:(g + 1) * GROUP_SIZE, :]
        m1 = jnp.max(blk, axis=0, keepdims=True)
        i1 = jnp.min(jnp.where(blk == m1, grow, GROUP_SIZE), axis=0, keepdims=True)
        m2 = jnp.max(jnp.where(grow == i1, NEG_INF, blk), axis=0, keepdims=True)
        gscore.append(m1 + m2)
    gsc = jnp.concatenate(gscore, axis=0)
    gidx = lax.broadcasted_iota(jnp.int32, (N_GROUPS, tn), 0)
    grank = jnp.zeros((N_GROUPS, tn), jnp.int32)
    for g in range(N_GROUPS):
        sg = gsc[g:g + 1, :]
        beats = (sg > gsc) | ((sg == gsc) & (gidx > g))
        grank = grank + beats.astype(jnp.int32)
    gsel = (grank < TOPK_GROUPS).astype(F32)
    emask = jnp.concatenate(
        [jnp.broadcast_to(gsel[g:g + 1, :], (GROUP_SIZE, tn)) for g in range(N_GROUPS)], axis=0)
    work = jnp.where(emask > 0.0, choice, NEG_INF)
    eligible = work
    idxs, svals = [], []
    for _k in range(TOP_K):
        m = jnp.max(work, axis=0, keepdims=True)
        idx = jnp.min(jnp.where(work == m, row, N_EXPERTS), axis=0, keepdims=True)
        hit = row == idx
        svals.append(jnp.sum(jnp.where(hit, scores, 0.0), axis=0, keepdims=True))
        work = jnp.where(hit, NEG_INF, work)
        idxs.append(idx)
    sel = jnp.where(work != eligible, 1.0, 0.0)
    ssum = svals[0]
    for sv in svals[1:]:
        ssum = ssum + sv
    weights = jnp.concatenate([sv / ssum * ROUTED_SCALE for sv in svals], axis=0)

    rank = _dot(sel.astype(BF16), tri) + carry[:, 0:1]
    ranks = jnp.concatenate(
        [jnp.sum(jnp.where(row == idx, rank, 0.0), axis=0, keepdims=True) for idx in idxs],
        axis=0).astype(jnp.int32)
    carry = carry + jnp.sum(sel, axis=1, keepdims=True)
    return jnp.concatenate(idxs, axis=0), weights, ranks, carry


def _post(x2, o_mla, o_swa, wom, wos, wg, wout, g1, b1, wr_t, rbias, tri, wsu, wsd):
    T = x2.shape[0]
    tm = POST_TM
    full = lambda a: pl.BlockSpec(a.shape, lambda i: (0, 0))
    row = lambda n: pl.BlockSpec((tm, n), lambda i: (i, 0))
    col = pl.BlockSpec((TOP_K, tm), lambda i: (0, i))
    return pl.pallas_call(
        _post_kernel,
        grid=(T // tm,),
        in_specs=[row(D_MODEL), row(512), row(512), full(wom), full(wos), full(wg), full(wout),
                  full(g1), full(b1), full(wr_t), full(rbias), full(tri), full(wsu), full(wsd)],
        out_specs=[row(D_MODEL), pl.BlockSpec((tm * ROW_SUB, LANES), lambda i: (i, 0)), col, col, col,
                   pl.BlockSpec((N_EXPERTS, LANES), lambda i: (0, 0)), row(D_MODEL)],
        out_shape=[jax.ShapeDtypeStruct((T, D_MODEL), F32),
                   jax.ShapeDtypeStruct((T * ROW_SUB, LANES), jnp.uint32),
                   jax.ShapeDtypeStruct((TOP_K, T), jnp.int32),
                   jax.ShapeDtypeStruct((TOP_K, T), F32),
                   jax.ShapeDtypeStruct((TOP_K, T), jnp.int32),
                   jax.ShapeDtypeStruct((N_EXPERTS, LANES), jnp.int32),
                   jax.ShapeDtypeStruct((T, D_MODEL), BF16)],
        scratch_shapes=[pltpu.VMEM((N_EXPERTS, LANES), F32)],
        compiler_params=pltpu.CompilerParams(dimension_semantics=("arbitrary",),
                                             vmem_limit_bytes=VMEM_LIMIT),
        name="post",
    )(x2, o_mla, o_swa, wom, wos, wg, wout, g1, b1, wr_t, rbias, tri, wsu, wsd)


def _dest_kernel(start_ref, e_ref, r_ref, o_ref):
    e = e_ref[...]
    base = jnp.zeros(e.shape, jnp.int32)
    for j in range(N_EXPERTS):
        base = jnp.where(e == j, start_ref[j], base)
    o_ref[...] = base + r_ref[...]


def _dest(seg_start, e_t, r_t):
    T = e_t.shape[1]
    tn = min(T, 4096)
    col = pl.BlockSpec((TOP_K, tn), lambda i: (0, i))
    return pl.pallas_call(
        _dest_kernel,
        grid=(T // tn,),
        in_specs=[pl.BlockSpec(memory_space=pltpu.SMEM), col, col],
        out_specs=col,
        out_shape=jax.ShapeDtypeStruct((TOP_K, T), jnp.int32),
        compiler_params=pltpu.CompilerParams(dimension_semantics=("parallel",)),
        name="dest",
    )(seg_start, e_t, r_t)


def _sc_worker_chunks(n_tokens):
    info = plsc.get_sparse_core_info()
    n_workers = info.num_cores * info.num_subcores
    per_worker = n_tokens // SC_CHUNK // n_workers
    assert per_worker * n_workers * SC_CHUNK == n_tokens
    first = (lax.axis_index("s") * info.num_cores + lax.axis_index("c")) * per_worker
    return first, per_worker


def _sc_dispatch(h1p, idx, n_rows):
    n_tokens = h1p.shape[0] // ROW_SUB
    R = SC_CHUNK * ROW_SUB

    def body(h_hbm, idx_hbm, xs_hbm, idx_v, rows_v, load_sem, scatter_sem):
        first, per_worker = _sc_worker_chunks(n_tokens)
        assert per_worker % 2 == 0

        def loads(c, slot):
            return (pltpu.make_async_copy(idx_hbm.at[c], idx_v.at[slot], load_sem.at[slot]),
                    pltpu.make_async_copy(h_hbm.at[pl.ds(c * R, R)], rows_v.at[slot], load_sem.at[slot]))

        def scatters(slot):
            return [pltpu.make_async_copy(rows_v.at[slot], xs_hbm.at[idx_v.at[slot, k]], scatter_sem.at[slot])
                    for k in range(TOP_K)]

        for cp in loads(first, 0):
            cp.start()

        @pl.loop(0, per_worker, step=2)
        def _(ci):
            for slot in range(2):
                c = first + ci + slot
                for cp in loads(c, slot):
                    cp.wait()

                @pl.when(ci + slot > 0)
                def _():
                    for cp in scatters(1 - slot):
                        cp.wait()

                @pl.when(ci + slot + 1 < per_worker)
                def _():
                    for cp in loads(c + 1, 1 - slot):
                        cp.start()

                for cp in scatters(slot):
                    cp.start()

        for cp in scatters(1):
            cp.wait()

    run = pl.kernel(
        body, out_type=jax.ShapeDtypeStruct((n_rows * ROW_SUB, LANES), jnp.uint32),
        mesh=plsc.VectorSubcoreMesh(core_axis_name="c", subcore_axis_name="s"),
        scratch_types=[pltpu.VMEM((2, TOP_K, R), jnp.int32), pltpu.VMEM((2, R, LANES), jnp.uint32),
                       pltpu.SemaphoreType.DMA((2,)), pltpu.SemaphoreType.DMA((2,))],
        name="sc_dispatch")
    return run(h1p, idx)


def _sc_gather(ys, idx, n_tokens):
    R = SC_CHUNK * ROW_SUB
    n_groups = TOP_K // 2

    def body(ys_hbm, idx_hbm, yg_hbm, idx_v, buf, isem, gsem, wsem):
        first, per_worker = _sc_worker_chunks(n_tokens)
        assert per_worker % 2 == 0

        def idx_load(c, islot):
            return pltpu.make_async_copy(idx_hbm.at[c], idx_v.at[islot], isem.at[islot])

        def gathers(islot, q):
            s = q % 2
            return [pltpu.make_async_copy(ys_hbm.at[idx_v.at[islot, 2 * q + j]], buf.at[s, j], gsem.at[s])
                    for j in range(2)]

        def writes(c, q):
            s = q % 2
            return [pltpu.make_async_copy(buf.at[s, j], yg_hbm.at[2 * q + j, pl.ds(c * R, R)], wsem.at[s])
                    for j in range(2)]

        idx_load(first, 0).start()

        @pl.loop(0, per_worker, step=2)
        def _(ci):
            for islot in range(2):
                c = first + ci + islot
                idx_load(c, islot).wait()
                for q in range(n_groups):
                    if q >= 2:
                        for cp in writes(c, q - 2):
                            cp.wait()
                    else:
                        @pl.when(ci + islot > 0)
                        def _():
                            for cp in writes(c - 1, q + 2):
                                cp.wait()
                    for cp in gathers(islot, q):
                        cp.start()
                    if q >= 1:
                        for cp in gathers(islot, q - 1):
                            cp.wait()
                        for cp in writes(c, q - 1):
                            cp.start()
                    else:
                        @pl.when(ci + islot > 0)
                        def _():
                            for cp in gathers(1 - islot, n_groups - 1):
                                cp.wait()
                            for cp in writes(c - 1, n_groups - 1):
                                cp.start()

                        @pl.when(ci + islot + 1 < per_worker)
                        def _():
                            idx_load(c + 1, 1 - islot).start()

        last = first + per_worker - 1
        for cp in gathers(1, n_groups - 1):
            cp.wait()
        for cp in writes(last, n_groups - 1):
            cp.start()
        for q in (n_groups - 2, n_groups - 1):
            for cp in writes(last, q):
                cp.wait()

    run = pl.kernel(
        body, out_type=jax.ShapeDtypeStruct((TOP_K, n_tokens * ROW_SUB, LANES), jnp.uint32),
        mesh=plsc.VectorSubcoreMesh(core_axis_name="c", subcore_axis_name="s"),
        scratch_types=[pltpu.VMEM((2, TOP_K, R), jnp.int32), pltpu.VMEM((2, 2, R, LANES), jnp.uint32),
                       pltpu.SemaphoreType.DMA((2,)), pltpu.SemaphoreType.DMA((2,)),
                       pltpu.SemaphoreType.DMA((2,))],
        name="sc_gather")
    return run(ys, idx)


def _experts_kernel(first_ref, nblk_ref, cnt_ref, xs_hbm, wup_ref, wdn_ref, after_ref, ys_hbm,
                    wup_bf, wdn_bf, xbuf, ybuf, in_sem, out_sem):
    del after_ref
    e = pl.program_id(0)
    n = nblk_ref[e]
    b0 = first_ref[e]
    count = cnt_ref[e]
    total = first_ref[N_EXPERTS - 1] + nblk_ref[N_EXPERTS - 1]
    rb = ROW_BLK * ROW_SUB
    depth = EXPERT_SLOTS

    def slot_of(b):
        return lax.rem(b, depth)

    def rows_of(b):
        return pl.ds(pl.multiple_of(b * rb, rb), rb)

    def in_copy(b):
        return pltpu.make_async_copy(xs_hbm.at[rows_of(b)], xbuf.at[slot_of(b)], in_sem.at[slot_of(b)])

    def out_copy(b):
        return pltpu.make_async_copy(ybuf.at[slot_of(b)], ys_hbm.at[rows_of(b)], out_sem.at[slot_of(b)])

    @pl.when(e == 0)
    def _():
        for b in range(depth - 1):
            @pl.when(b < total)
            def _():
                in_copy(b).start()

    @pl.when(n > 0)
    def _():
        wup_bf[...] = wup_ref[0].astype(BF16)
        wdn_bf[...] = wdn_ref[0].astype(BF16)

        def body(b, carry):
            slot = slot_of(b)
            in_copy(b).wait()

            @pl.when(b + depth - 1 < total)
            def _():
                in_copy(b + depth - 1).start()

            @pl.when(b >= depth)
            def _():
                out_copy(b - depth).wait()

            sub = ROW_BLK // EXPERT_SPLIT
            n_valid = count - (b - b0) * ROW_BLK
            xbs = [jnp.concatenate(
                [c.astype(BF16) for c in _unpack_rows(xbuf.at[slot], sub, first=i * sub, n_valid=n_valid)],
                axis=1) for i in range(EXPERT_SPLIT)]
            gus = [_dot(xb, wup_bf[...]) for xb in xbs]
            hids = [(gu[:, :EXPERT_HIDDEN] * _sigmoid(gu[:, :EXPERT_HIDDEN])
                     * gu[:, EXPERT_HIDDEN:]).astype(BF16) for gu in gus]
            ys = [_dot(hid, wdn_bf[...]) for hid in hids]
            for i in range(EXPERT_SPLIT):
                _pack_rows(ys[i], ybuf.at[slot], first=i * sub)
            out_copy(b).start()
            return carry

        lax.fori_loop(b0, b0 + n, body, 0)

    @pl.when(e == N_EXPERTS - 1)
    def _():
        for back in range(depth, 0, -1):
            @pl.when(total - back >= 0)
            def _():
                out_copy(total - back).wait()


def _experts(first_blk, n_blk, counts, xs, w_exp_up, w_exp_down, run_after):
    rb = ROW_BLK * ROW_SUB
    grid_spec = pltpu.PrefetchScalarGridSpec(
        num_scalar_prefetch=3,
        grid=(N_EXPERTS,),
        in_specs=[pl.BlockSpec(memory_space=pl.ANY),
                  pl.BlockSpec((1, D_MODEL, 2 * EXPERT_HIDDEN), lambda e, fb, nb, ct: (e, 0, 0)),
                  pl.BlockSpec((1, EXPERT_HIDDEN, D_MODEL), lambda e, fb, nb, ct: (e, 0, 0)),
                  pl.BlockSpec(memory_space=pl.ANY)],
        out_specs=pl.BlockSpec(memory_space=pl.ANY),
        scratch_shapes=[pltpu.VMEM((D_MODEL, 2 * EXPERT_HIDDEN), BF16),
                        pltpu.VMEM((EXPERT_HIDDEN, D_MODEL), BF16),
                        pltpu.VMEM((EXPERT_SLOTS, rb, LANES), jnp.uint32),
                        pltpu.VMEM((EXPERT_SLOTS, rb, LANES), jnp.uint32),
                        pltpu.SemaphoreType.DMA((EXPERT_SLOTS,)),
                        pltpu.SemaphoreType.DMA((EXPERT_SLOTS,))],
    )
    return pl.pallas_call(
        _experts_kernel,
        grid_spec=grid_spec,
        out_shape=jax.ShapeDtypeStruct(xs.shape, jnp.uint32),
        compiler_params=pltpu.CompilerParams(dimension_semantics=("arbitrary",),
                                             vmem_limit_bytes=VMEM_LIMIT),
        name="experts",
    )(first_blk, n_blk, counts, xs, w_exp_up, w_exp_down, run_after)


def _shared_kernel(h1p_ref, wsu_ref, wsd_ref, o_ref):
    tm = o_ref.shape[0]
    hb = jnp.concatenate([c.astype(BF16) for c in _unpack_rows(h1p_ref, tm)], axis=1)
    gu = _dot(hb, wsu_ref[...])
    g = gu[:, :SHARED_HIDDEN]
    hid = g * _sigmoid(g) * gu[:, SHARED_HIDDEN:]
    o_ref[...] = _dot(hid.astype(BF16), wsd_ref[...]).astype(BF16)


def _shared(h1p, wsu, wsd):
    T = h1p.shape[0] // ROW_SUB
    tm = SHARED_TM
    full = lambda a: pl.BlockSpec(a.shape, lambda i: (0, 0))
    row = pl.BlockSpec((tm, D_MODEL), lambda i: (i, 0))
    return pl.pallas_call(
        _shared_kernel,
        grid=(T // tm,),
        in_specs=[pl.BlockSpec((tm * ROW_SUB, LANES), lambda i: (i, 0)), full(wsu), full(wsd)],
        out_specs=row,
        out_shape=jax.ShapeDtypeStruct((T, D_MODEL), BF16),
        compiler_params=pltpu.CompilerParams(dimension_semantics=("parallel",),
                                             vmem_limit_bytes=VMEM_LIMIT),
        name="shared",
    )(h1p, wsu, wsd)


def _combine_kernel(w_ref, h1_ref, sh_ref, yg_ref, g2_ref, b2_ref, *rest):
    o_ref = rest[-1]
    tm = h1_ref.shape[0]
    h1 = h1_ref[...]
    ffn = sh_ref[...].astype(F32)

    w = w_ref[...]
    acc = [None] * (2 * ROW_SUB)
    for k in range(TOP_K):
        wk = w[:, k:k + 1]
        chunks = _unpack_rows(yg_ref.at[k], tm)
        for c in range(2 * ROW_SUB):
            acc[c] = wk * chunks[c] if acc[c] is None else acc[c] + wk * chunks[c]
    routed = jnp.concatenate(acc, axis=1)
    z = DEEPNORM_ALPHA * h1 + (routed + ffn)
    mu = jnp.mean(z, axis=-1, keepdims=True)
    zc = z - mu
    var = jnp.mean(zc * zc, axis=-1, keepdims=True)
    o_ref[...] = zc * lax.rsqrt(var + LN_EPS) * g2_ref[...] + b2_ref[...]


def _combine(w_tok, h1, sh, yg, g2, b2, first_token, out_prev):
    n_tok = yg.shape[1] // ROW_SUB
    tm = COMBINE_TM
    first = first_token // tm
    full = lambda a: pl.BlockSpec(a.shape, lambda i: (0, 0))
    row = pl.BlockSpec((tm, D_MODEL), lambda i: (first + i, 0))
    in_specs = [pl.BlockSpec((tm, TOP_K), lambda i: (first + i, 0)), row, row,
                pl.BlockSpec((TOP_K, tm * ROW_SUB, LANES), lambda i: (0, i, 0)),
                full(g2), full(b2)]
    args = [w_tok, h1, sh, yg, g2, b2]
    aliases = {}
    if out_prev is not None:
        in_specs.append(pl.BlockSpec(memory_space=pl.ANY))
        args.append(out_prev)
        aliases = {len(args) - 1: 0}
    return pl.pallas_call(
        _combine_kernel,
        grid=(n_tok // tm,),
        in_specs=in_specs,
        out_specs=row,
        out_shape=jax.ShapeDtypeStruct(h1.shape, F32),
        input_output_aliases=aliases,
        compiler_params=pltpu.CompilerParams(dimension_semantics=("parallel",),
                                             vmem_limit_bytes=VMEM_LIMIT),
        name="combine",
    )(*args)


def _prep_weights(w_in, w_uq, w_ukv, seq):
    z = lambda n: jnp.zeros((D_MODEL, n), F32)
    kr = w_in[:, 512:544]
    qs = w_in[:, 544:1056] * (LOG2E / math.sqrt(SWA_HEAD_DIM))
    ks0, ks1 = w_in[:, 1056:1120], w_in[:, 1120:1184]
    half = MLA_ROPE // 2
    w1 = jnp.concatenate([
        w_in[:, 0:512], qs,
        ks0, ks1, ks1, ks0,
        z(64), kr, z(32),
        z(64), -kr[:, half:], kr[:, :half], z(32)], axis=1).astype(BF16)
    wvst = w_in[:, 1184:1312].T.astype(BF16)
    wg = w_in[:, 1312:3360].astype(BF16)

    pad3 = lambda a, lo, hi: jnp.pad(a, ((0, 0), (0, 0), (lo, hi)))
    wq3 = w_uq.reshape(MLA_Q_LORA, MLA_HEADS, MLA_NOPE + MLA_ROPE)
    rope = wq3[:, :, MLA_NOPE:]
    rot = jnp.concatenate([-rope[:, :, half:], rope[:, :, :half]], axis=2)
    wq2 = jnp.concatenate([pad3(wq3, 0, 32).reshape(MLA_Q_LORA, -1),
                           pad3(rot, MLA_NOPE, 32).reshape(MLA_Q_LORA, -1)], axis=1).astype(BF16)
    wkv3 = w_ukv.reshape(MLA_KV_LORA, MLA_HEADS, MLA_NOPE + MLA_V)
    wk = pad3(wkv3[:, :, :MLA_NOPE], 0, 64).reshape(MLA_KV_LORA, -1).astype(BF16)
    wvt = wkv3[:, :, MLA_NOPE:].reshape(MLA_KV_LORA, -1).T.astype(BF16)

    inv_freq = ROPE_THETA ** (-jnp.arange(0, MLA_ROPE, 2, dtype=F32) / MLA_ROPE)
    ang = jnp.arange(seq, dtype=F32)[:, None] * inv_freq[None, :]
    cos, sin = jnp.cos(ang), jnp.sin(ang)
    one, zero = jnp.ones((seq, 64), F32), jnp.zeros((seq, 64), F32)
    z32 = jnp.zeros((seq, 32), F32)
    scale = LOG2E / math.sqrt(MLA_NOPE + MLA_ROPE)
    tabs = (jnp.concatenate([one, cos, cos, z32], axis=1) * scale,
            jnp.concatenate([zero, sin, sin, z32], axis=1) * scale,
            jnp.concatenate([zero, cos, cos, z32], axis=1),
            jnp.concatenate([zero, sin, sin, z32], axis=1))
    return w1, wvst, wg, wq2, wk, wvt, tabs


def kernel(x, w_in, mla_q_norm, mla_kv_norm, w_uq, w_ukv, attn_sinks, w_o_mla, w_o_swa, w_out,
           ln1_g, ln1_b, w_router, router_bias, w_exp_up, w_exp_down, w_sh_up, w_sh_down,
           ln2_g, ln2_b):
    batch, seq, _ = x.shape
    T = batch * seq
    x2 = x.reshape(T, D_MODEL)
    w1, wvst, wg, wq2, wk, wvt, tabs = _prep_weights(w_in, w_uq, w_ukv, seq)
    q, k, vt, qs, ks, vst = _proj(x2, w1, wvst, wq2, wk, wvt, mla_q_norm.reshape(1, -1),
                                  mla_kv_norm.reshape(1, -1), tabs, seq)
    o_mla = _mla(q, k, vt, batch, seq)
    o_swa = _swa(attn_sinks.astype(F32), qs, ks, vst, batch, seq)

    tri = (lax.broadcasted_iota(jnp.int32, (ROUTE_TN, ROUTE_TN), 0)
           < lax.broadcasted_iota(jnp.int32, (ROUTE_TN, ROUTE_TN), 1)).astype(BF16)
    h1, h1p, e_t, w_t, r_t, cnt, sh = _post(
        x2, o_mla, o_swa, w_o_mla.astype(BF16), w_o_swa.astype(BF16), wg, w_out.astype(BF16),
        ln1_g.reshape(1, -1), ln1_b.reshape(1, -1), w_router.T.astype(BF16),
        router_bias.reshape(-1, 1).astype(F32), tri, w_sh_up.astype(BF16), w_sh_down.astype(BF16))

    counts = cnt[:, 0]
    pad = (counts + ROW_BLK - 1) // ROW_BLK * ROW_BLK
    pad_start = jnp.cumsum(pad) - pad
    n_rows = (T * TOP_K // ROW_BLK + N_EXPERTS) * ROW_BLK
    dest = _dest(pad_start.astype(jnp.int32), e_t, r_t)
    idx = (dest.reshape(TOP_K, T // SC_CHUNK, SC_CHUNK, 1) * ROW_SUB
           + jnp.arange(ROW_SUB, dtype=jnp.int32))
    idx = idx.transpose(1, 0, 2, 3).reshape(T // SC_CHUNK, TOP_K, SC_CHUNK * ROW_SUB)
    xs = _sc_dispatch(h1p, idx, n_rows)
    ys = _experts((pad_start // ROW_BLK).astype(jnp.int32), (pad // ROW_BLK).astype(jnp.int32),
                  counts.astype(jnp.int32), xs, w_exp_up, w_exp_down, sh)

    Tc = T // COMBINE_PARTS
    n_chunks = Tc // SC_CHUNK
    w_tok = w_t.T
    g2, b2 = ln2_g.reshape(1, -1), ln2_b.reshape(1, -1)
    out = None
    for cpart in range(COMBINE_PARTS):
        yg = _sc_gather(ys, idx[cpart * n_chunks:(cpart + 1) * n_chunks], Tc)
        out = _combine(w_tok, h1, sh, yg, g2, b2, cpart * Tc, out)
    return out.reshape(batch, seq, D_MODEL)
```
